```python
import math
import jax, jax.numpy as jnp
from jax import lax
import numpy as np

D_MODEL = 2048
BATCH = 16
SEQ = 2048
DEPTH = 4

N_MIXERS = 2
N_HGRN_LAYERS = (DEPTH + 1) // 2
N_MAMBA_LAYERS = DEPTH // 2
NORM_EPS = 1e-5

HGRN_EXPAND = 128
HGRN_HEADS = D_MODEL // HGRN_EXPAND
HGRN_DK = HGRN_EXPAND
HGRN_DV = D_MODEL // HGRN_HEADS
HGRN_FDIM = HGRN_HEADS * HGRN_DK
HGRN_IN_DIM = 2 * HGRN_FDIM + 2 * HGRN_HEADS * HGRN_DV
HGRN_CHUNK = 64

M_EXPAND = 2
M_D_INNER = M_EXPAND * D_MODEL
M_HEADDIM = 64
M_HEADS = M_D_INNER // M_HEADDIM
M_GROUPS = 8
M_HPG = M_HEADS // M_GROUPS
M_D_STATE = 128
M_CONV = 4
M_CONV_DIM = M_D_INNER + 2 * M_GROUPS * M_D_STATE
M_IN_DIM = M_D_INNER + M_CONV_DIM + M_HEADS
M_CHUNK = 128

D_FF = 5632
FFN_CONV = 3

kernel_name = 'hybrid_hgrn2_mamba2_convffn'


def rms_norm(x, w):
    xf = x.astype(jnp.float32)
    y = xf * lax.rsqrt(jnp.mean(xf * xf, axis=-1, keepdims=True) + NORM_EPS)
    return (y * w.astype(jnp.float32)).astype(x.dtype)


def causal_dwconv(x, w, b):
    K, C = w.shape
    y = lax.conv_general_dilated(x, w[:, None, :].astype(x.dtype), window_strides=(1,),
                                 padding=[(K - 1, 0)], dimension_numbers=('NWC', 'WIO', 'NWC'),
                                 feature_group_count=C)
    return y + b.astype(x.dtype)


def masked_exp(mask, logits):
    return jnp.where(mask, jnp.exp(jnp.where(mask, logits, 0.0)), 0.0)


def hgrn2_mixer(u, w_in, lb, gn_w, w_out):
    Bsz, L, _ = u.shape
    n_chunks = L // HGRN_CHUNK
    f32 = jnp.float32
    q, f, v, g = jnp.split(u @ w_in, [HGRN_FDIM, 2 * HGRN_FDIM, 2 * HGRN_FDIM + HGRN_HEADS * HGRN_DV], axis=-1)
    q = jax.nn.silu(q.astype(f32))
    f = f.astype(f32)
    lb = lb.astype(f32)
    log_f = jnp.log(lb + (1.0 - lb) * jax.nn.sigmoid(f))
    k = (1.0 - lb) * jax.nn.sigmoid(-f)
    v = v.astype(f32)

    def to_chunks(t, d):
        return t.reshape(Bsz, n_chunks, HGRN_CHUNK, HGRN_HEADS, d).transpose(1, 0, 3, 2, 4)

    causal = jnp.tril(jnp.ones((HGRN_CHUNK, HGRN_CHUNK), bool))[..., None]

    def chunk_step(S, inp):
        qc, kc, vc, gc = inp
        b = jnp.cumsum(gc, axis=2)
        diff = b[:, :, :, None, :] - b[:, :, None, :, :]
        decay = masked_exp(causal, diff)
        A = jnp.einsum('bhik,bhjk,bhijk->bhij', qc, kc, decay)
        o = jnp.einsum('bhij,bhjv->bhiv', A, vc) + jnp.einsum('bhik,bhkv->bhiv', qc * jnp.exp(b), S)
        b_last = b[:, :, -1:, :]
        S = jnp.exp(b_last[:, :, 0, :, None]) * S + jnp.einsum('bhjk,bhjv->bhkv', kc * jnp.exp(b_last - b), vc)
        return S, o

    S0 = jnp.zeros((Bsz, HGRN_HEADS, HGRN_DK, HGRN_DV), f32)
    _, o = lax.scan(chunk_step, S0, (to_chunks(q, HGRN_DK), to_chunks(k, HGRN_DK),
                                     to_chunks(v, HGRN_DV), to_chunks(log_f, HGRN_DK)))
    o = o.transpose(1, 0, 3, 2, 4).reshape(Bsz, L, HGRN_HEADS, HGRN_DV)
    g = g.astype(f32).reshape(Bsz, L, HGRN_HEADS, HGRN_DV)
    o = o * lax.rsqrt(jnp.mean(o * o, axis=-1, keepdims=True) + NORM_EPS) * gn_w.astype(f32) * jax.nn.silu(g)
    return o.reshape(Bsz, L, HGRN_HEADS * HGRN_DV).astype(u.dtype) @ w_out


def mamba2_mixer(u, w_in, conv_w, conv_b, dt_bias, A_log, D_skip, norm_w, w_out):
    Bsz, L, _ = u.shape
    nc = L // M_CHUNK
    f32 = jnp.float32
    z, xBC, dt = jnp.split(u @ w_in, [M_D_INNER, M_D_INNER + M_CONV_DIM], axis=-1)
    xBC = jax.nn.silu(causal_dwconv(xBC, conv_w, conv_b)).astype(f32)
    xs, Bm, Cm = jnp.split(xBC, [M_D_INNER, M_D_INNER + M_GROUPS * M_D_STATE], axis=-1)
    dt = jax.nn.softplus(dt.astype(f32) + dt_bias.astype(f32))
    A = -jnp.exp(A_log.astype(f32))
    xh = xs.reshape(Bsz, L, M_HEADS, M_HEADDIM)
    X = (xh * dt[..., None]).reshape(Bsz, nc, M_CHUNK, M_GROUPS, M_HPG, M_HEADDIM)
    Ad = (dt * A).reshape(Bsz, nc, M_CHUNK, M_GROUPS, M_HPG).transpose(0, 3, 4, 1, 2)
    Bc = Bm.reshape(Bsz, nc, M_CHUNK, M_GROUPS, M_D_STATE)
    Cc = Cm.reshape(Bsz, nc, M_CHUNK, M_GROUPS, M_D_STATE)
    a_cs = jnp.cumsum(Ad, axis=-1)
    causal = jnp.tril(jnp.ones((M_CHUNK, M_CHUNK), bool))
    Lmat = masked_exp(causal, a_cs[..., :, None] - a_cs[..., None, :])
    CB = jnp.einsum('bclgn,bcsgn->bcgls', Cc, Bc)
    y_diag = jnp.einsum('bcgls,bgjcls,bcsgjp->bclgjp', CB, Lmat, X)
    decay_states = jnp.exp(a_cs[..., -1:] - a_cs)
    states = jnp.einsum('bcsgn,bgjcs,bcsgjp->cbgjpn', Bc, decay_states, X)
    chunk_decay = jnp.exp(a_cs[..., -1]).transpose(3, 0, 1, 2)

    def state_step(h, inp):
        st, dec = inp
        return dec[..., None, None] * h + st, h

    h0 = jnp.zeros((Bsz, M_GROUPS, M_HPG, M_HEADDIM, M_D_STATE), f32)
    _, h_in = lax.scan(state_step, h0, (states, chunk_decay))
    y_off = jnp.einsum('bclgn,cbgjpn,bgjcl->bclgjp', Cc, h_in, jnp.exp(a_cs))
    y = (y_diag + y_off).reshape(Bsz, L, M_HEADS, M_HEADDIM) + xh * D_skip.astype(f32)[:, None]
    y = y.reshape(Bsz, L, M_D_INNER) * jax.nn.silu(z.astype(f32))
    y = y.reshape(Bsz, L, M_GROUPS, M_D_INNER // M_GROUPS)
    y = y * lax.rsqrt(jnp.mean(y * y, axis=-1, keepdims=True) + NORM_EPS)
    y = y.reshape(Bsz, L, M_D_INNER) * norm_w.astype(f32)
    return y.astype(u.dtype) @ w_out


def conv_ffn(u, w_up, conv_w, conv_b, w_down):
    h = causal_dwconv(u @ w_up, conv_w, conv_b)
    g, up = jnp.split(h, 2, axis=-1)
    return (jax.nn.silu(g) * up) @ w_down


def _fwd_setup_inputs(seed: int = 0) -> dict:
    key = jax.random.key(seed)
    ks = jax.random.split(key, 24)
    f32 = jnp.float32
    nh, nm = N_HGRN_LAYERS, N_MAMBA_LAYERS

    def dense(k, shape):
        return jax.random.normal(k, shape, f32) * shape[-2] ** -0.5

    def gain(k, shape):
        return 1.0 + 0.02 * jax.random.normal(k, shape, f32)

    def small(k, shape):
        return 0.02 * jax.random.normal(k, shape, f32)

    dt0 = jnp.exp(jax.random.uniform(ks[10], (nm, M_HEADS), f32, math.log(1e-3), math.log(1e-1)))
    return {
        'x': jax.random.normal(ks[0], (BATCH, SEQ, D_MODEL), f32),
        'mix_norm': gain(ks[1], (DEPTH, D_MODEL)),
        'ffn_norm': gain(ks[2], (DEPTH, D_MODEL)),
        'final_norm': gain(ks[3], (D_MODEL,)),
        'hgrn_w_in': dense(ks[4], (nh, D_MODEL, HGRN_IN_DIM)),
        'hgrn_lb_logits': 0.5 * jax.random.normal(ks[5], (nh, HGRN_FDIM), f32),
        'hgrn_gnorm': gain(ks[6], (nh, HGRN_DV)),
        'hgrn_w_out': dense(ks[7], (nh, HGRN_HEADS * HGRN_DV, D_MODEL)),
        'm_w_in': dense(ks[8], (nm, D_MODEL, M_IN_DIM)),
        'm_conv_w': dense(ks[9], (nm, M_CONV, M_CONV_DIM)),
        'm_conv_b': small(ks[11], (nm, M_CONV_DIM)),
        'm_dt_bias': dt0 + jnp.log(-jnp.expm1(-dt0)),
        'm_A_log': jnp.log(jax.random.uniform(ks[12], (nm, M_HEADS), f32, 1.0, 16.0)),
        'm_D': gain(ks[13], (nm, M_HEADS)),
        'm_norm': gain(ks[14], (nm, M_D_INNER)),
        'm_w_out': dense(ks[15], (nm, M_D_INNER, D_MODEL)),
        'f_w_up': dense(ks[16], (DEPTH, D_MODEL, 2 * D_FF)),
        'f_conv_w': dense(ks[17], (DEPTH, FFN_CONV, 2 * D_FF)),
        'f_conv_b': small(ks[18], (DEPTH, 2 * D_FF)),
        'f_w_down': dense(ks[19], (DEPTH, D_FF, D_MODEL)),
    }


def _fwd_reference(x, mix_norm, ffn_norm, final_norm, hgrn_w_in, hgrn_lb_logits, hgrn_gnorm, hgrn_w_out,
              m_w_in, m_conv_w, m_conv_b, m_dt_bias, m_A_log, m_D, m_norm, m_w_out,
              f_w_up, f_conv_w, f_conv_b, f_w_down):
    lb_p = jax.nn.softmax(hgrn_lb_logits.astype(jnp.float32), axis=0)
    lower_bounds = jnp.cumsum(lb_p, axis=0) - lb_p[0]
    h = x
    for i in range(DEPTH):
        u = rms_norm(h, mix_norm[i])
        j = i // N_MIXERS
        if i % N_MIXERS == 0:
            h = h + hgrn2_mixer(u, hgrn_w_in[j], lower_bounds[j], hgrn_gnorm[j], hgrn_w_out[j])
        else:
            h = h + mamba2_mixer(u, m_w_in[j], m_conv_w[j], m_conv_b[j], m_dt_bias[j], m_A_log[j],
                                 m_D[j], m_norm[j], m_w_out[j])
        h = h + conv_ffn(rms_norm(h, ffn_norm[i]), f_w_up[i], f_conv_w[i], f_conv_b[i], f_w_down[i])
    return rms_norm(h, final_norm)


import jax as _jax
import jax.numpy as _jnp

TWIN_FORMAT = 'train_step'
FWD_PARAMS = ['x', 'mix_norm', 'ffn_norm', 'final_norm', 'hgrn_w_in', 'hgrn_lb_logits', 'hgrn_gnorm', 'hgrn_w_out', 'm_w_in', 'm_conv_w', 'm_conv_b', 'm_dt_bias', 'm_A_log', 'm_D', 'm_norm', 'm_w_out', 'f_w_up', 'f_conv_w', 'f_conv_b', 'f_w_down']
TWIN_WEIGHTS = ['mix_norm', 'ffn_norm', 'final_norm', 'hgrn_w_in', 'hgrn_lb_logits', 'hgrn_gnorm', 'hgrn_w_out', 'm_w_in', 'm_conv_w', 'm_conv_b', 'm_dt_bias', 'm_A_log', 'm_D', 'm_norm', 'm_w_out', 'f_w_up', 'f_conv_w', 'f_conv_b', 'f_w_down']
TWIN_DIFF_INPUT = 'x'
TWIN_INPUTS = ['x', 'mix_norm', 'ffn_norm', 'final_norm', 'hgrn_w_in', 'hgrn_lb_logits', 'hgrn_gnorm', 'hgrn_w_out', 'm_w_in', 'm_conv_w', 'm_conv_b', 'm_dt_bias', 'm_A_log', 'm_D', 'm_norm', 'm_w_out', 'f_w_up', 'f_conv_w', 'f_conv_b', 'f_w_down', 'loss_target', 'm_mix_norm', 'm_ffn_norm', 'm_final_norm', 'm_hgrn_w_in', 'm_hgrn_lb_logits', 'm_hgrn_gnorm', 'm_hgrn_w_out', 'm_m_w_in', 'm_m_conv_w', 'm_m_conv_b', 'm_m_dt_bias', 'm_m_A_log', 'm_m_D', 'm_m_norm', 'm_m_w_out', 'm_f_w_up', 'm_f_conv_w', 'm_f_conv_b', 'm_f_w_down', 'v_mix_norm', 'v_ffn_norm', 'v_final_norm', 'v_hgrn_w_in', 'v_hgrn_lb_logits', 'v_hgrn_gnorm', 'v_hgrn_w_out', 'v_m_w_in', 'v_m_conv_w', 'v_m_conv_b', 'v_m_dt_bias', 'v_m_A_log', 'v_m_D', 'v_m_norm', 'v_m_w_out', 'v_f_w_up', 'v_f_conv_w', 'v_f_conv_b', 'v_f_w_down']
TWIN_OUTPUTS = ['loss', 'grad_x', 'grad_mix_norm', 'grad_ffn_norm', 'grad_final_norm', 'grad_hgrn_w_in', 'grad_hgrn_lb_logits', 'grad_hgrn_gnorm', 'grad_hgrn_w_out', 'grad_m_w_in', 'grad_m_conv_w', 'grad_m_conv_b', 'grad_m_dt_bias', 'grad_m_A_log', 'grad_m_D', 'grad_m_norm', 'grad_m_w_out', 'grad_f_w_up', 'grad_f_conv_w', 'grad_f_conv_b', 'grad_f_w_down', 'delta_mix_norm', 'delta_ffn_norm', 'delta_final_norm', 'delta_hgrn_w_in', 'delta_hgrn_lb_logits', 'delta_hgrn_gnorm', 'delta_hgrn_w_out', 'delta_m_w_in', 'delta_m_conv_w', 'delta_m_conv_b', 'delta_m_dt_bias', 'delta_m_A_log', 'delta_m_D', 'delta_m_norm', 'delta_m_w_out', 'delta_f_w_up', 'delta_f_conv_w', 'delta_f_conv_b', 'delta_f_w_down', 'new_m_mix_norm', 'new_m_ffn_norm', 'new_m_final_norm', 'new_m_hgrn_w_in', 'new_m_hgrn_lb_logits', 'new_m_hgrn_gnorm', 'new_m_hgrn_w_out', 'new_m_m_w_in', 'new_m_m_conv_w', 'new_m_m_conv_b', 'new_m_m_dt_bias', 'new_m_m_A_log', 'new_m_m_D', 'new_m_m_norm', 'new_m_m_w_out', 'new_m_f_w_up', 'new_m_f_conv_w', 'new_m_f_conv_b', 'new_m_f_w_down', 'new_v_mix_norm', 'new_v_ffn_norm', 'new_v_final_norm', 'new_v_hgrn_w_in', 'new_v_hgrn_lb_logits', 'new_v_hgrn_gnorm', 'new_v_hgrn_w_out', 'new_v_m_w_in', 'new_v_m_conv_w', 'new_v_m_conv_b', 'new_v_m_dt_bias', 'new_v_m_A_log', 'new_v_m_D', 'new_v_m_norm', 'new_v_m_w_out', 'new_v_f_w_up', 'new_v_f_conv_w', 'new_v_f_conv_b', 'new_v_f_w_down']
TWIN_LEAF_KINDS = {'loss': 'loss', 'grad_x': 'grad_x', 'grad_mix_norm': 'grad_w', 'grad_ffn_norm': 'grad_w', 'grad_final_norm': 'grad_w', 'grad_hgrn_w_in': 'grad_w', 'grad_hgrn_lb_logits': 'grad_w', 'grad_hgrn_gnorm': 'grad_w', 'grad_hgrn_w_out': 'grad_w', 'grad_m_w_in': 'grad_w', 'grad_m_conv_w': 'grad_w', 'grad_m_conv_b': 'grad_w', 'grad_m_dt_bias': 'grad_w', 'grad_m_A_log': 'grad_w', 'grad_m_D': 'grad_w', 'grad_m_norm': 'grad_w', 'grad_m_w_out': 'grad_w', 'grad_f_w_up': 'grad_w', 'grad_f_conv_w': 'grad_w', 'grad_f_conv_b': 'grad_w', 'grad_f_w_down': 'grad_w', 'delta_mix_norm': 'delta_w', 'delta_ffn_norm': 'delta_w', 'delta_final_norm': 'delta_w', 'delta_hgrn_w_in': 'delta_w', 'delta_hgrn_lb_logits': 'delta_w', 'delta_hgrn_gnorm': 'delta_w', 'delta_hgrn_w_out': 'delta_w', 'delta_m_w_in': 'delta_w', 'delta_m_conv_w': 'delta_w', 'delta_m_conv_b': 'delta_w', 'delta_m_dt_bias': 'delta_w', 'delta_m_A_log': 'delta_w', 'delta_m_D': 'delta_w', 'delta_m_norm': 'delta_w', 'delta_m_w_out': 'delta_w', 'delta_f_w_up': 'delta_w', 'delta_f_conv_w': 'delta_w', 'delta_f_conv_b': 'delta_w', 'delta_f_w_down': 'delta_w', 'new_m_mix_norm': 'new_m', 'new_m_ffn_norm': 'new_m', 'new_m_final_norm': 'new_m', 'new_m_hgrn_w_in': 'new_m', 'new_m_hgrn_lb_logits': 'new_m', 'new_m_hgrn_gnorm': 'new_m', 'new_m_hgrn_w_out': 'new_m', 'new_m_m_w_in': 'new_m', 'new_m_m_conv_w': 'new_m', 'new_m_m_conv_b': 'new_m', 'new_m_m_dt_bias': 'new_m', 'new_m_m_A_log': 'new_m', 'new_m_m_D': 'new_m', 'new_m_m_norm': 'new_m', 'new_m_m_w_out': 'new_m', 'new_m_f_w_up': 'new_m', 'new_m_f_conv_w': 'new_m', 'new_m_f_conv_b': 'new_m', 'new_m_f_w_down': 'new_m', 'new_v_mix_norm': 'new_v', 'new_v_ffn_norm': 'new_v', 'new_v_final_norm': 'new_v', 'new_v_hgrn_w_in': 'new_v', 'new_v_hgrn_lb_logits': 'new_v', 'new_v_hgrn_gnorm': 'new_v', 'new_v_hgrn_w_out': 'new_v', 'new_v_m_w_in': 'new_v', 'new_v_m_conv_w': 'new_v', 'new_v_m_conv_b': 'new_v', 'new_v_m_dt_bias': 'new_v', 'new_v_m_A_log': 'new_v', 'new_v_m_D': 'new_v', 'new_v_m_norm': 'new_v', 'new_v_m_w_out': 'new_v', 'new_v_f_w_up': 'new_v', 'new_v_f_conv_w': 'new_v', 'new_v_f_conv_b': 'new_v', 'new_v_f_w_down': 'new_v'}


def _forward(args):
    return _fwd_reference(*[args[k] for k in FWD_PARAMS])


def _output_shape():
    out = _jax.eval_shape(lambda: _forward(_fwd_setup_inputs(0)))
    return out.shape, out.dtype

N_MICROBATCH = 1
ADAM_LR = 0.001
ADAM_B1 = 0.9
ADAM_B2 = 0.999
ADAM_EPS = 1e-08
ADAM_WD = 0.01
ADAM_STEP = 10
PER_EXAMPLE_BATCH_AXIS = {'x': 0, 'loss_target': 0}
SHARED_INPUTS = []
_WEIGHT_DTYPES = {'mix_norm': _jnp.float32, 'ffn_norm': _jnp.float32, 'final_norm': _jnp.float32, 'hgrn_w_in': _jnp.float32, 'hgrn_lb_logits': _jnp.float32, 'hgrn_gnorm': _jnp.float32, 'hgrn_w_out': _jnp.float32, 'm_w_in': _jnp.float32, 'm_conv_w': _jnp.float32, 'm_conv_b': _jnp.float32, 'm_dt_bias': _jnp.float32, 'm_A_log': _jnp.float32, 'm_D': _jnp.float32, 'm_norm': _jnp.float32, 'm_w_out': _jnp.float32, 'f_w_up': _jnp.float32, 'f_conv_w': _jnp.float32, 'f_conv_b': _jnp.float32, 'f_w_down': _jnp.float32}
MOMENT_SCALE = {'mix_norm': 9.126716e-02, 'ffn_norm': 6.243953e-02, 'final_norm': 1.598948e+01, 'hgrn_w_in': 4.639479e-02, 'hgrn_lb_logits': 3.101802e-03, 'hgrn_gnorm': 2.603350e-01, 'hgrn_w_out': 6.389016e-02, 'm_w_in': 4.008756e-02, 'm_conv_w': 3.668125e-02, 'm_conv_b': 4.823689e-02, 'm_dt_bias': 8.640813e-02, 'm_A_log': 1.095985e-01, 'm_D': 3.345058e-01, 'm_norm': 4.220798e-02, 'm_w_out': 5.980287e-02, 'f_w_up': 2.685478e-02, 'f_conv_w': 2.683129e-02, 'f_conv_b': 2.716578e-02, 'f_w_down': 4.383866e-02}


def _to_microbatches(a, axis):
    t = _jnp.moveaxis(a, axis, 0)
    t = t.reshape((N_MICROBATCH, t.shape[0] // N_MICROBATCH) + t.shape[1:])
    return _jnp.moveaxis(t, 1, axis + 1)


def setup_inputs(seed: int = 0) -> dict:
    inp = _fwd_setup_inputs(seed)
    key = _jax.random.fold_in(_jax.random.key(seed), 7919)
    shape, _ = _output_shape()
    out = dict(inp)
    out["loss_target"] = _jax.random.normal(_jax.random.fold_in(key, 0), shape, _jnp.float32)
    for i, name in enumerate(TWIN_WEIGHTS):
        w = inp[name].astype(_jnp.float32)
        if MOMENT_SCALE is None:
            s = _jnp.sqrt(_jnp.mean(_jnp.square(w)) + 1e-30)
        else:
            s = MOMENT_SCALE[name]
        km, kv = _jax.random.split(_jax.random.fold_in(key, i + 1))
        out[name] = w
        out["m_" + name] = s * _jax.random.normal(km, w.shape, _jnp.float32)
        out["v_" + name] = (s * s) * _jax.random.uniform(kv, w.shape, _jnp.float32, 0.5, 1.5)
    if N_MICROBATCH > 1:
        for name, axis in PER_EXAMPLE_BATCH_AXIS.items():
            out[name] = _to_microbatches(out[name], axis)
    return {'x': out['x'], 'mix_norm': out['mix_norm'], 'ffn_norm': out['ffn_norm'], 'final_norm': out['final_norm'], 'hgrn_w_in': out['hgrn_w_in'], 'hgrn_lb_logits': out['hgrn_lb_logits'], 'hgrn_gnorm': out['hgrn_gnorm'], 'hgrn_w_out': out['hgrn_w_out'], 'm_w_in': out['m_w_in'], 'm_conv_w': out['m_conv_w'], 'm_conv_b': out['m_conv_b'], 'm_dt_bias': out['m_dt_bias'], 'm_A_log': out['m_A_log'], 'm_D': out['m_D'], 'm_norm': out['m_norm'], 'm_w_out': out['m_w_out'], 'f_w_up': out['f_w_up'], 'f_conv_w': out['f_conv_w'], 'f_conv_b': out['f_conv_b'], 'f_w_down': out['f_w_down'], 'loss_target': out['loss_target'], 'm_mix_norm': out['m_mix_norm'], 'm_ffn_norm': out['m_ffn_norm'], 'm_final_norm': out['m_final_norm'], 'm_hgrn_w_in': out['m_hgrn_w_in'], 'm_hgrn_lb_logits': out['m_hgrn_lb_logits'], 'm_hgrn_gnorm': out['m_hgrn_gnorm'], 'm_hgrn_w_out': out['m_hgrn_w_out'], 'm_m_w_in': out['m_m_w_in'], 'm_m_conv_w': out['m_m_conv_w'], 'm_m_conv_b': out['m_m_conv_b'], 'm_m_dt_bias': out['m_m_dt_bias'], 'm_m_A_log': out['m_m_A_log'], 'm_m_D': out['m_m_D'], 'm_m_norm': out['m_m_norm'], 'm_m_w_out': out['m_m_w_out'], 'm_f_w_up': out['m_f_w_up'], 'm_f_conv_w': out['m_f_conv_w'], 'm_f_conv_b': out['m_f_conv_b'], 'm_f_w_down': out['m_f_w_down'], 'v_mix_norm': out['v_mix_norm'], 'v_ffn_norm': out['v_ffn_norm'], 'v_final_norm': out['v_final_norm'], 'v_hgrn_w_in': out['v_hgrn_w_in'], 'v_hgrn_lb_logits': out['v_hgrn_lb_logits'], 'v_hgrn_gnorm': out['v_hgrn_gnorm'], 'v_hgrn_w_out': out['v_hgrn_w_out'], 'v_m_w_in': out['v_m_w_in'], 'v_m_conv_w': out['v_m_conv_w'], 'v_m_conv_b': out['v_m_conv_b'], 'v_m_dt_bias': out['v_m_dt_bias'], 'v_m_A_log': out['v_m_A_log'], 'v_m_D': out['v_m_D'], 'v_m_norm': out['v_m_norm'], 'v_m_w_out': out['v_m_w_out'], 'v_f_w_up': out['v_f_w_up'], 'v_f_conv_w': out['v_f_conv_w'], 'v_f_conv_b': out['v_f_conv_b'], 'v_f_w_down': out['v_f_w_down']}


def _loss(weights, diff, rest, loss_target):
    with _jax.named_scope("forward"):
        args = {**rest, TWIN_DIFF_INPUT: diff, **{k: w.astype(_WEIGHT_DTYPES[k]) for k, w in weights.items()}}
        y = _forward(args)
    with _jax.named_scope("loss_head"):
        err = _jnp.square(y.astype(_jnp.float32) - loss_target)
        return 0.5 * _jnp.sum(_jnp.mean(err, axis=-1)) if err.ndim else 0.5 * err


def _adamw(w, g, m, v):
    m = ADAM_B1 * m + (1.0 - ADAM_B1) * g
    v = ADAM_B2 * v + (1.0 - ADAM_B2) * _jnp.square(g)
    m_hat = m / (1.0 - ADAM_B1 ** ADAM_STEP)
    v_hat = v / (1.0 - ADAM_B2 ** ADAM_STEP)
    delta = -ADAM_LR * (m_hat / (_jnp.sqrt(v_hat) + ADAM_EPS) + ADAM_WD * w)
    return delta, m, v


def reference(x, mix_norm, ffn_norm, final_norm, hgrn_w_in, hgrn_lb_logits, hgrn_gnorm, hgrn_w_out, m_w_in, m_conv_w, m_conv_b, m_dt_bias, m_A_log, m_D, m_norm, m_w_out, f_w_up, f_conv_w, f_conv_b, f_w_down, loss_target, m_mix_norm, m_ffn_norm, m_final_norm, m_hgrn_w_in, m_hgrn_lb_logits, m_hgrn_gnorm, m_hgrn_w_out, m_m_w_in, m_m_conv_w, m_m_conv_b, m_m_dt_bias, m_m_A_log, m_m_D, m_m_norm, m_m_w_out, m_f_w_up, m_f_conv_w, m_f_conv_b, m_f_w_down, v_mix_norm, v_ffn_norm, v_final_norm, v_hgrn_w_in, v_hgrn_lb_logits, v_hgrn_gnorm, v_hgrn_w_out, v_m_w_in, v_m_conv_w, v_m_conv_b, v_m_dt_bias, v_m_A_log, v_m_D, v_m_norm, v_m_w_out, v_f_w_up, v_f_conv_w, v_f_conv_b, v_f_w_down):
    given = dict(x=x, mix_norm=mix_norm, ffn_norm=ffn_norm, final_norm=final_norm, hgrn_w_in=hgrn_w_in, hgrn_lb_logits=hgrn_lb_logits, hgrn_gnorm=hgrn_gnorm, hgrn_w_out=hgrn_w_out, m_w_in=m_w_in, m_conv_w=m_conv_w, m_conv_b=m_conv_b, m_dt_bias=m_dt_bias, m_A_log=m_A_log, m_D=m_D, m_norm=m_norm, m_w_out=m_w_out, f_w_up=f_w_up, f_conv_w=f_conv_w, f_conv_b=f_conv_b, f_w_down=f_w_down, loss_target=loss_target, m_mix_norm=m_mix_norm, m_ffn_norm=m_ffn_norm, m_final_norm=m_final_norm, m_hgrn_w_in=m_hgrn_w_in, m_hgrn_lb_logits=m_hgrn_lb_logits, m_hgrn_gnorm=m_hgrn_gnorm, m_hgrn_w_out=m_hgrn_w_out, m_m_w_in=m_m_w_in, m_m_conv_w=m_m_conv_w, m_m_conv_b=m_m_conv_b, m_m_dt_bias=m_m_dt_bias, m_m_A_log=m_m_A_log, m_m_D=m_m_D, m_m_norm=m_m_norm, m_m_w_out=m_m_w_out, m_f_w_up=m_f_w_up, m_f_conv_w=m_f_conv_w, m_f_conv_b=m_f_conv_b, m_f_w_down=m_f_w_down, v_mix_norm=v_mix_norm, v_ffn_norm=v_ffn_norm, v_final_norm=v_final_norm, v_hgrn_w_in=v_hgrn_w_in, v_hgrn_lb_logits=v_hgrn_lb_logits, v_hgrn_gnorm=v_hgrn_gnorm, v_hgrn_w_out=v_hgrn_w_out, v_m_w_in=v_m_w_in, v_m_conv_w=v_m_conv_w, v_m_conv_b=v_m_conv_b, v_m_dt_bias=v_m_dt_bias, v_m_A_log=v_m_A_log, v_m_D=v_m_D, v_m_norm=v_m_norm, v_m_w_out=v_m_w_out, v_f_w_up=v_f_w_up, v_f_conv_w=v_f_conv_w, v_f_conv_b=v_f_conv_b, v_f_w_down=v_f_w_down)
    weights = {n: given[n] for n in TWIN_WEIGHTS}
    shared = {n: given[n] for n in SHARED_INPUTS}
    per_example = {n: given[n] for n in ['x']}
    grad_fn = _jax.value_and_grad(_loss, argnums=(0, 1))

    def one_microbatch(ex, loss_target):
        ex = dict(ex)
        diff = ex.pop(TWIN_DIFF_INPUT)
        return grad_fn(weights, diff, {**shared, **ex}, loss_target)

    if N_MICROBATCH == 1:
        loss, (grad_w, grad_x) = one_microbatch(per_example, given["loss_target"])
    else:
        def body(carry, xs):
            loss_sum, grad_sum = carry
            l_k, (gw_k, gx_k) = one_microbatch(xs[0], xs[1])
            with _jax.named_scope("update"):
                return (loss_sum + l_k, _jax.tree.map(_jnp.add, grad_sum, gw_k)), gx_k

        init = (_jnp.zeros((), _jnp.float32), _jax.tree.map(_jnp.zeros_like, weights))
        (loss, grad_w), grad_x = _jax.lax.scan(body, init, (per_example, given["loss_target"]))
    with _jax.named_scope("update"):
        delta_w, new_m, new_v = {}, {}, {}
        for n in TWIN_WEIGHTS:
            delta_w[n], new_m[n], new_v[n] = _adamw(weights[n], grad_w[n], given["m_" + n], given["v_" + n])
    return (loss, grad_x, *[grad_w[n] for n in TWIN_WEIGHTS], *[delta_w[n] for n in TWIN_WEIGHTS],
            *[new_m[n] for n in TWIN_WEIGHTS], *[new_v[n] for n in TWIN_WEIGHTS])
```

```python
import functools
import math

import jax
import jax.numpy as jnp
from jax import lax
from jax.experimental import pallas as pl
from jax.experimental.pallas import tpu as pltpu

F32 = jnp.float32
BF16 = jnp.bfloat16
HIGHEST = lax.Precision.HIGHEST

NORM_EPS = 1e-5
ADAM_LR, ADAM_B1, ADAM_B2, ADAM_EPS, ADAM_WD, ADAM_STEP = 0.001, 0.9, 0.999, 1e-08, 0.01, 10

N_DEV = 8
LANES = 128
V7X_VMEM_BYTES = 64 * 1024 * 1024
VMEM_LIMIT = V7X_VMEM_BYTES * 3 // 4

HGRN_HEAD = 128
HGRN_CHUNK = 64
HGRN_SUB = 16
M_HEADDIM = 64
M_GROUPS = 8
M_D_STATE = 128
M_CONV = 4
M_CHUNK = 128
FFN_CONV = 3
CONV_COLS = 256


def _params(dims=None, **kw):
    return pltpu.CompilerParams(dimension_semantics=dims, vmem_limit_bytes=VMEM_LIMIT, **kw)


def _tile(dim, target, align=LANES):
    t = (min(target, dim) // align) * align
    while t >= align:
        if dim % t == 0:
            return t
        t -= align
    return dim


def _dg(a, b, ca, cb):
    return lax.dot_general(a.astype(BF16), b.astype(BF16), (((ca,), (cb,)), ((), ())), preferred_element_type=F32)


@jax.custom_vjp
def _dot_nn(a, b):
    return _dg(a, b, 1, 0)


def _dot_nn_f(a, b):
    return _dg(a, b, 1, 0), (a, b)


def _dot_nn_b(res, g):
    a, b = res
    return _dg(g, b, 1, 1), _dg(a, g, 0, 0)


_dot_nn.defvjp(_dot_nn_f, _dot_nn_b)


@jax.custom_vjp
def _dot_nt(a, b):
    return _dg(a, b, 1, 1)


def _dot_nt_f(a, b):
    return _dg(a, b, 1, 1), (a, b)


def _dot_nt_b(res, g):
    a, b = res
    return _dg(g, b, 1, 0), _dg(g, a, 0, 0)


_dot_nt.defvjp(_dot_nt_f, _dot_nt_b)


@jax.custom_vjp
def _dot_tn(a, b):
    return _dg(a, b, 0, 0)


def _dot_tn_f(a, b):
    return _dg(a, b, 0, 0), (a, b)


def _dot_tn_b(res, g):
    a, b = res
    return _dg(b, g, 1, 1), _dg(a, g, 1, 0)


_dot_tn.defvjp(_dot_tn_f, _dot_tn_b)


def _tri(n):
    return (lax.broadcasted_iota(jnp.int32, (n, n), 0) >= lax.broadcasted_iota(jnp.int32, (n, n), 1)).astype(F32)


def _silu(x):
    return x * jax.nn.sigmoid(x)


def _matmul(a, b, mode, out_dtype, *, name, add=None, tm=1024, tn=1024, tk=512):
    if mode == "nn":
        (M, K), (K2, N) = a.shape, b.shape
    elif mode == "nt":
        (M, K), (N, K2) = a.shape, b.shape
    else:
        (K, M), (K2, N) = a.shape, b.shape
    assert K == K2, (a.shape, b.shape, mode)
    tm, tn, tk = _tile(M, tm, 8 if M % LANES else LANES), _tile(N, tn), _tile(K, tk)
    nk = K // tk
    ca, cb = {"nn": (1, 0), "nt": (1, 1), "tn": (0, 0)}[mode]

    def body(*refs):
        if add is None:
            a_ref, b_ref, o_ref, acc_ref = refs
        else:
            a_ref, b_ref, add_ref, o_ref, acc_ref = refs
        k = pl.program_id(2)

        @pl.when(k == 0)
        def _():
            acc_ref[...] = jnp.zeros_like(acc_ref)

        acc_ref[...] += _dg(a_ref[...], b_ref[...], ca, cb)

        @pl.when(k == nk - 1)
        def _():
            r = acc_ref[...]
            if add is not None:
                r = r + add_ref[...]
            o_ref[...] = r.astype(o_ref.dtype)

    a_spec = pl.BlockSpec((tk, tm), lambda i, j, k: (k, i)) if mode == "tn" else pl.BlockSpec((tm, tk), lambda i, j, k: (i, k))
    b_spec = pl.BlockSpec((tn, tk), lambda i, j, k: (j, k)) if mode == "nt" else pl.BlockSpec((tk, tn), lambda i, j, k: (k, j))
    o_spec = pl.BlockSpec((tm, tn), lambda i, j, k: (i, j))
    in_specs, args = [a_spec, b_spec], [a, b]
    if add is not None:
        in_specs.append(o_spec)
        args.append(add)
    return pl.pallas_call(
        body, name=name, out_shape=jax.ShapeDtypeStruct((M, N), out_dtype), grid=(M // tm, N // tn, nk),
        in_specs=in_specs, out_specs=o_spec, scratch_shapes=[pltpu.VMEM((tm, tn), F32)],
        compiler_params=_params(("parallel", "parallel", "arbitrary")),
    )(*args)


def _rms_fwd(h, w, *, name):
    T, D = h.shape
    tr = _tile(T, 256, 8)

    def body(h_ref, w_ref, u_ref):
        x = h_ref[...]
        u_ref[...] = (x * lax.rsqrt(jnp.mean(x * x, axis=-1, keepdims=True) + NORM_EPS) * w_ref[...]).astype(u_ref.dtype)

    return pl.pallas_call(
        body, name=name, out_shape=jax.ShapeDtypeStruct((T, D), BF16), grid=(T // tr,),
        in_specs=[pl.BlockSpec((tr, D), lambda i: (i, 0)), pl.BlockSpec((1, D), lambda i: (0, 0))],
        out_specs=pl.BlockSpec((tr, D), lambda i: (i, 0)), compiler_params=_params(("parallel",)),
    )(h, w.reshape(1, D))


def _rms_bwd(h, w, du, dh_in, *, name):
    T, D = h.shape
    tr = _tile(T, 256, 8)

    def body(h_ref, w_ref, du_ref, dhin_ref, dh_ref, dw_ref):
        x = h_ref[...]
        g = du_ref[...].astype(F32)
        rstd = lax.rsqrt(jnp.mean(x * x, axis=-1, keepdims=True) + NORM_EPS)
        xhat = x * rstd
        gx = g * w_ref[...]
        dh_ref[...] = dhin_ref[...] + rstd * (gx - xhat * jnp.mean(gx * xhat, axis=-1, keepdims=True))

        @pl.when(pl.program_id(0) == 0)
        def _():
            dw_ref[...] = jnp.zeros_like(dw_ref)

        dw_ref[...] += jnp.sum(g * xhat, axis=0, keepdims=True)

    row = pl.BlockSpec((tr, D), lambda i: (i, 0))
    vec = pl.BlockSpec((1, D), lambda i: (0, 0))
    return pl.pallas_call(
        body, name=name, out_shape=(jax.ShapeDtypeStruct((T, D), F32), jax.ShapeDtypeStruct((1, D), F32)), grid=(T // tr,),
        in_specs=[row, vec, row, row], out_specs=(row, vec), compiler_params=_params(("arbitrary",)),
    )(h, w.reshape(1, D), du, dh_in)


def _loss_head(h, w, target, *, name):
    T, D = h.shape
    tr = _tile(T, 256, 8)

    def body(h_ref, w_ref, t_ref, loss_ref, dh_ref, dw_ref):
        x = h_ref[...]
        rstd = lax.rsqrt(jnp.mean(x * x, axis=-1, keepdims=True) + NORM_EPS)
        xhat = x * rstd
        err = xhat * w_ref[...] - t_ref[...]
        g = err * (1.0 / D)
        gx = g * w_ref[...]
        dh_ref[...] = rstd * (gx - xhat * jnp.mean(gx * xhat, axis=-1, keepdims=True))

        @pl.when(pl.program_id(0) == 0)
        def _():
            dw_ref[...] = jnp.zeros_like(dw_ref)
            loss_ref[...] = jnp.zeros_like(loss_ref)

        dw_ref[...] += jnp.sum(g * xhat, axis=0, keepdims=True)
        loss_ref[...] += (0.5 / D) * jnp.sum(jnp.sum(err * err, axis=-1, keepdims=True), axis=0, keepdims=True)

    row = pl.BlockSpec((tr, D), lambda i: (i, 0))
    vec = pl.BlockSpec((1, D), lambda i: (0, 0))
    one = pl.BlockSpec((1, 1), lambda i: (0, 0))
    return pl.pallas_call(
        body, name=name,
        out_shape=(jax.ShapeDtypeStruct((1, 1), F32), jax.ShapeDtypeStruct((T, D), F32), jax.ShapeDtypeStruct((1, D), F32)),
        grid=(T // tr,), in_specs=[row, vec, row], out_specs=(one, row, vec), compiler_params=_params(("arbitrary",)),
    )(h, w.reshape(1, D), target)


def _shift_down(x, s):
    if s == 0:
        return x
    rows = lax.broadcasted_iota(jnp.int32, x.shape, 0)
    return jnp.where(rows >= s, pltpu.roll(x, s, 0), 0.0)


def _shift_up(x, s):
    if s == 0:
        return x
    n = x.shape[0]
    rows = lax.broadcasted_iota(jnp.int32, x.shape, 0)
    return jnp.where(rows < n - s, pltpu.roll(x, n - s, 0), 0.0)


def _conv_pre(a, w_ref, b_ref, taps):
    pre = b_ref[0:1, :] + w_ref[taps - 1:taps, :] * a
    for k in range(taps - 1):
        pre = pre + w_ref[k:k + 1, :] * _shift_down(a, taps - 1 - k)
    return pre


def _conv_bwd(a, dpre, w_ref, taps):
    da = w_ref[taps - 1:taps, :] * dpre
    rows = []
    for k in range(taps - 1):
        da = da + w_ref[k:k + 1, :] * _shift_up(dpre, taps - 1 - k)
        rows.append(jnp.sum(dpre * _shift_down(a, taps - 1 - k), axis=0, keepdims=True))
    rows.append(jnp.sum(dpre * a, axis=0, keepdims=True))
    rows.append(jnp.sum(dpre, axis=0, keepdims=True))
    return da, rows


def _ffn_gate_fwd(a, cw, cb, Bl, L, *, name):
    T, F2 = a.shape
    F = F2 // 2
    tc = _tile(F, CONV_COLS)
    nj = F // tc

    def body(ag_ref, au_ref, wg_ref, wu_ref, bg_ref, bu_ref, y_ref):
        hg = _conv_pre(ag_ref[...].astype(F32), wg_ref, bg_ref, FFN_CONV)
        hu = _conv_pre(au_ref[...].astype(F32), wu_ref, bu_ref, FFN_CONV)
        y_ref[...] = (_silu(hg) * hu).astype(y_ref.dtype)

    blk = lambda off: pl.BlockSpec((L, tc), lambda b, j: (b, j + off))
    wblk = lambda off: pl.BlockSpec((FFN_CONV, tc), lambda b, j: (0, j + off))
    bblk = lambda off: pl.BlockSpec((1, tc), lambda b, j: (0, j + off))
    return pl.pallas_call(
        body, name=name, out_shape=jax.ShapeDtypeStruct((T, F), BF16), grid=(Bl, nj),
        in_specs=[blk(0), blk(nj), wblk(0), wblk(nj), bblk(0), bblk(nj)], out_specs=blk(0),
        compiler_params=_params(("parallel", "parallel")),
    )(a, a, cw, cw, cb.reshape(1, F2), cb.reshape(1, F2))


def _ffn_gate_bwd(a, cw, cb, dy, Bl, L, *, name):
    T, F2 = a.shape
    F = F2 // 2
    tc = _tile(F, CONV_COLS)
    nj = F // tc

    def body(ag_ref, au_ref, wg_ref, wu_ref, bg_ref, bu_ref, dy_ref, dag_ref, dau_ref, pg_ref, pu_ref):
        ag, au = ag_ref[...].astype(F32), au_ref[...].astype(F32)
        hg = _conv_pre(ag, wg_ref, bg_ref, FFN_CONV)
        hu = _conv_pre(au, wu_ref, bu_ref, FFN_CONV)
        g = dy_ref[...].astype(F32)
        s = jax.nn.sigmoid(hg)
        dhg = g * hu * (s * (1.0 + hg * (1.0 - s)))
        dhu = g * (hg * s)
        dag, rg = _conv_bwd(ag, dhg, wg_ref, FFN_CONV)
        dau, ru = _conv_bwd(au, dhu, wu_ref, FFN_CONV)
        dag_ref[...] = dag.astype(dag_ref.dtype)
        dau_ref[...] = dau.astype(dau_ref.dtype)

        @pl.when(pl.program_id(1) == 0)
        def _():
            pg_ref[...] = jnp.zeros_like(pg_ref)
            pu_ref[...] = jnp.zeros_like(pu_ref)

        for k in range(FFN_CONV + 1):
            pg_ref[k:k + 1, :] += rg[k]
            pu_ref[k:k + 1, :] += ru[k]

    blk = lambda off: pl.BlockSpec((L, tc), lambda j, b: (b, j + off))
    wblk = lambda off: pl.BlockSpec((FFN_CONV, tc), lambda j, b: (0, j + off))
    bblk = lambda off: pl.BlockSpec((1, tc), lambda j, b: (0, j + off))
    pblk = lambda off: pl.BlockSpec((8, tc), lambda j, b: (0, j + off))
    dag, dau, pg, pu = pl.pallas_call(
        body, name=name,
        out_shape=(jax.ShapeDtypeStruct((T, F), BF16), jax.ShapeDtypeStruct((T, F), BF16),
                   jax.ShapeDtypeStruct((8, F), F32), jax.ShapeDtypeStruct((8, F), F32)),
        grid=(nj, Bl),
        in_specs=[blk(0), blk(nj), wblk(0), wblk(nj), bblk(0), bblk(nj), blk(0)],
        out_specs=(blk(0), blk(0), pblk(0), pblk(0)),
        compiler_params=_params(("parallel", "arbitrary")),
    )(a, a, cw, cw, cb.reshape(1, F2), cb.reshape(1, F2), dy)
    return dag, dau, jnp.concatenate([pg, pu], axis=1)


def _mconv_fwd(xbc, cw, cb, Bl, L, *, name):
    T, W = xbc.shape
    tc = _tile(W, CONV_COLS)

    def body(a_ref, w_ref, b_ref, y_ref):
        y_ref[...] = _silu(_conv_pre(a_ref[...], w_ref, b_ref, M_CONV))

    blk = pl.BlockSpec((L, tc), lambda b, j: (b, j))
    return pl.pallas_call(
        body, name=name, out_shape=jax.ShapeDtypeStruct((T, W), F32), grid=(Bl, W // tc),
        in_specs=[blk, pl.BlockSpec((M_CONV, tc), lambda b, j: (0, j)), pl.BlockSpec((1, tc), lambda b, j: (0, j))],
        out_specs=blk, compiler_params=_params(("parallel", "parallel")),
    )(xbc, cw, cb.reshape(1, W))


def _mconv_bwd(xbc, cw, cb, dy, Bl, L, *, name):
    T, W = xbc.shape
    tc = _tile(W, CONV_COLS)

    def body(a_ref, w_ref, b_ref, dy_ref, da_ref, p_ref):
        a = a_ref[...]
        pre = _conv_pre(a, w_ref, b_ref, M_CONV)
        s = jax.nn.sigmoid(pre)
        dpre = dy_ref[...] * (s * (1.0 + pre * (1.0 - s)))
        da, rows = _conv_bwd(a, dpre, w_ref, M_CONV)
        da_ref[...] = da.astype(da_ref.dtype)

        @pl.when(pl.program_id(1) == 0)
        def _():
            p_ref[...] = jnp.zeros_like(p_ref)

        for k in range(M_CONV + 1):
            p_ref[k:k + 1, :] += rows[k]

    blk = pl.BlockSpec((L, tc), lambda j, b: (b, j))
    return pl.pallas_call(
        body, name=name, out_shape=(jax.ShapeDtypeStruct((T, W), BF16), jax.ShapeDtypeStruct((8, W), F32)),
        grid=(W // tc, Bl),
        in_specs=[blk, pl.BlockSpec((M_CONV, tc), lambda j, b: (0, j)), pl.BlockSpec((1, tc), lambda j, b: (0, j)), blk],
        out_specs=(blk, pl.BlockSpec((8, tc), lambda j, b: (0, j))),
        compiler_params=_params(("parallel", "arbitrary")),
    )(xbc, cw, cb.reshape(1, W), dy)


def _hgrn_chunk(st, qr, fr, v, gr, lb, gn):
    C = qr.shape[0]
    q = _silu(qr)
    logf = jnp.log(lb + (1.0 - lb) * jax.nn.sigmoid(fr))
    k = (1.0 - lb) * jax.nn.sigmoid(-fr)
    b = jnp.dot(_tri(C), logf, precision=HIGHEST, preferred_element_type=F32)
    o = _dot_nt(q * jnp.exp(b), st)
    keep = lax.broadcasted_iota(jnp.int32, (HGRN_SUB, HGRN_SUB), 0) >= lax.broadcasted_iota(jnp.int32, (HGRN_SUB, HGRN_SUB), 1)
    cols = lax.broadcasted_iota(jnp.int32, (HGRN_SUB, C), 1)
    parts = []
    for blk in range(C // HGRN_SUB):
        r0 = blk * HGRN_SUB
        bi, qi, ki, vi = b[r0:r0 + HGRN_SUB], q[r0:r0 + HGRN_SUB], k[r0:r0 + HGRN_SUB], v[r0:r0 + HGRN_SUB]
        decay = jnp.exp(jnp.minimum(bi[:, None, :] - bi[None, :, :], 0.0))
        a_diag = jnp.where(keep, jnp.sum(qi[:, None, :] * ki[None, :, :] * decay, axis=-1), 0.0)
        oi = _dot_nn(a_diag, vi)
        if blk > 0:
            bs = b[r0 - 1:r0]
            a_off = _dot_nt(qi * jnp.exp(bi - bs), k * jnp.exp(jnp.minimum(bs - b, 0.0)))
            oi = oi + _dot_nn(jnp.where(cols < r0, a_off, 0.0), v)
        parts.append(oi)
    o = o + jnp.concatenate(parts, axis=0)
    bl = b[C - 1:C]
    st1 = st * jnp.exp(bl) + _dot_tn(v, k * jnp.exp(bl - b))
    og = o * lax.rsqrt(jnp.mean(o * o, axis=-1, keepdims=True) + NORM_EPS) * gn * _silu(gr)
    return st1, og


def _hgrn_specs(Bl, L, H, order):
    def at(off):
        if order == "bh":
            return pl.BlockSpec((L, HGRN_HEAD), lambda b, h: (b, h + off))
        return pl.BlockSpec((L, HGRN_HEAD), lambda h, b: (b, h + off))
    return [at(0), at(H), at(2 * H), at(3 * H)]


def _hgrn_fwd(P, lb, gn, Bl, L, *, name):
    T, D4 = P.shape
    D = D4 // 4
    H = D // HGRN_HEAD
    C = HGRN_CHUNK
    NC = L // C

    def body(q_ref, f_ref, v_ref, g_ref, lb_ref, gn_ref, og_ref, st_ref):
        lbv, gnv = lb_ref[...], gn_ref[...]

        def step(c, st):
            r = pl.ds(pl.multiple_of(c * C, C), C)
            st_ref[c] = st
            st1, og = _hgrn_chunk(st, q_ref[r, :], f_ref[r, :], v_ref[r, :], g_ref[r, :], lbv, gnv)
            og_ref[r, :] = og.astype(og_ref.dtype)
            return st1

        lax.fori_loop(0, NC, step, jnp.zeros((HGRN_HEAD, HGRN_HEAD), F32))

    return pl.pallas_call(
        body, name=name,
        out_shape=(jax.ShapeDtypeStruct((T, D), BF16), jax.ShapeDtypeStruct((Bl, H, NC, HGRN_HEAD, HGRN_HEAD), F32)),
        grid=(Bl, H),
        in_specs=_hgrn_specs(Bl, L, H, "bh") + [pl.BlockSpec((1, HGRN_HEAD), lambda b, h: (0, h)), pl.BlockSpec((1, HGRN_HEAD), lambda b, h: (0, 0))],
        out_specs=(pl.BlockSpec((L, HGRN_HEAD), lambda b, h: (b, h)),
                   pl.BlockSpec((None, None, NC, HGRN_HEAD, HGRN_HEAD), lambda b, h: (b, h, 0, 0, 0))),
        compiler_params=_params(("parallel", "parallel")),
    )(P, P, P, P, lb.reshape(1, D), gn.reshape(1, HGRN_HEAD))


def _hgrn_bwd(P, lb, gn, states, dog, Bl, L, *, name):
    T, D4 = P.shape
    D = D4 // 4
    H = D // HGRN_HEAD
    C = HGRN_CHUNK
    NC = L // C

    def body(q_ref, f_ref, v_ref, g_ref, lb_ref, gn_ref, st_ref, dog_ref, dp_ref, dlb_ref, dgn_ref):
        lbv, gnv = lb_ref[...], gn_ref[...]

        @pl.when(pl.program_id(1) == 0)
        def _():
            dlb_ref[...] = jnp.zeros_like(dlb_ref)

        @pl.when((pl.program_id(0) == 0) & (pl.program_id(1) == 0))
        def _():
            dgn_ref[...] = jnp.zeros_like(dgn_ref)

        def step(i, dst):
            c = NC - 1 - i
            r = pl.ds(pl.multiple_of(c * C, C), C)
            _, vjp = jax.vjp(_hgrn_chunk, st_ref[c], q_ref[r, :], f_ref[r, :], v_ref[r, :], g_ref[r, :], lbv, gnv)
            dst0, dq, df, dv, dg, dlb, dgn = vjp((dst, dog_ref[r, :].astype(F32)))
            dp_ref[0, r, :] = dq.astype(dp_ref.dtype)
            dp_ref[1, r, :] = df.astype(dp_ref.dtype)
            dp_ref[2, r, :] = dv.astype(dp_ref.dtype)
            dp_ref[3, r, :] = dg.astype(dp_ref.dtype)
            dlb_ref[...] += dlb
            dgn_ref[...] += dgn
            return dst0

        lax.fori_loop(0, NC, step, jnp.zeros((HGRN_HEAD, HGRN_HEAD), F32))

    return pl.pallas_call(
        body, name=name,
        out_shape=(jax.ShapeDtypeStruct((4, T, D), BF16), jax.ShapeDtypeStruct((1, D), F32), jax.ShapeDtypeStruct((1, HGRN_HEAD), F32)),
        grid=(H, Bl),
        in_specs=_hgrn_specs(Bl, L, H, "hb") + [
            pl.BlockSpec((1, HGRN_HEAD), lambda h, b: (0, h)), pl.BlockSpec((1, HGRN_HEAD), lambda h, b: (0, 0)),
            pl.BlockSpec((None, None, NC, HGRN_HEAD, HGRN_HEAD), lambda h, b: (b, h, 0, 0, 0)),
            pl.BlockSpec((L, HGRN_HEAD), lambda h, b: (b, h))],
        out_specs=(pl.BlockSpec((4, L, HGRN_HEAD), lambda h, b: (0, b, h)),
                   pl.BlockSpec((1, HGRN_HEAD), lambda h, b: (0, h)), pl.BlockSpec((1, HGRN_HEAD), lambda h, b: (0, 0))),
        compiler_params=_params(("arbitrary", "arbitrary")),
    )(P, P, P, P, lb.reshape(1, D), gn.reshape(1, HGRN_HEAD), states, dog)


def _lower_bounds(logits):
    p = jax.nn.softmax(logits, axis=0)
    rows, run = [], jnp.zeros_like(p[0:1])
    for i in range(logits.shape[0]):
        run = run + p[i:i + 1]
        rows.append(run - p[0:1])
    return jnp.concatenate(rows, axis=0) if len(rows) > 1 else rows[0]


def _lb_fwd(logits, *, name):
    def body(x_ref, o_ref):
        o_ref[...] = _lower_bounds(x_ref[...])
    return pl.pallas_call(body, name=name, out_shape=jax.ShapeDtypeStruct(logits.shape, F32))(logits)


def _lb_bwd(logits, dlb, *, name):
    def body(x_ref, g_ref, o_ref):
        _, vjp = jax.vjp(_lower_bounds, x_ref[...])
        o_ref[...] = vjp(g_ref[...])[0]
    return pl.pallas_call(body, name=name, out_shape=jax.ShapeDtypeStruct(logits.shape, F32))(logits, dlb)


def _ssd_pre(dtc_raw, dtr_raw, bias_c, bias_r, alog_c, alog_r):
    C = dtc_raw.shape[0]
    dt_c = jax.nn.softplus(dtc_raw + bias_c)
    dt_r = jax.nn.softplus(dtr_raw + bias_r)
    acs_c = jnp.dot(_tri(C), dt_c * -jnp.exp(alog_c), precision=HIGHEST, preferred_element_type=F32)
    acs_r = lax.dot_general(dt_r * -jnp.exp(alog_r), _tri(C), (((1,), (1,)), ((), ())), precision=HIGHEST, preferred_element_type=F32)
    return dt_c, acs_c, acs_r


def _ssd_head(h0, xs, G, Bm, Cm, dt, acs, acs_row, dskip):
    C = xs.shape[0]
    causal = lax.broadcasted_iota(jnp.int32, (C, C), 0) >= lax.broadcasted_iota(jnp.int32, (C, C), 1)
    Lm = jnp.where(causal, jnp.exp(jnp.minimum(acs - acs_row, 0.0)), 0.0)
    X = xs * dt
    y = _dot_nn(G * Lm, X) + _dot_nt(Cm * jnp.exp(acs), h0) + dskip * xs
    alast = acs[C - 1:C]
    h1 = jnp.exp(alast) * h0 + _dot_tn(X, Bm * jnp.exp(alast - acs))
    return y, h1


def _ssd_layout(Bl, L, d_inner, order):
    G = M_GROUPS
    hpg = d_inner // M_HEADDIM // G
    gw = hpg * M_HEADDIM
    ix = (lambda b, g: (b, g)) if order == "bg" else (lambda g, b: (b, g))
    def cols(width, off):
        if order == "bg":
            return pl.BlockSpec((L, width), lambda b, g: (b, g + off))
        return pl.BlockSpec((L, width), lambda g, b: (b, g + off))
    def per_group(shape):
        n = len(shape)
        if order == "bg":
            return pl.BlockSpec((None,) + shape, lambda b, g: (g,) + (0,) * n)
        return pl.BlockSpec((None,) + shape, lambda g, b: (g,) + (0,) * n)
    if order == "bg":
        dtc = pl.BlockSpec((None, L, LANES), lambda b, g: (g, b, 0))
        dtr = pl.BlockSpec((None, 8, L), lambda b, g: (g, 0, b))
    else:
        dtc = pl.BlockSpec((None, L, LANES), lambda g, b: (g, b, 0))
        dtr = pl.BlockSpec((None, 8, L), lambda g, b: (g, 0, b))
    nb = d_inner // M_D_STATE
    return hpg, gw, cols, per_group, dtc, dtr, nb


def _ssd_fwd(xa, dtc, dtr, prm, Bl, L, d_inner, *, name):
    T = xa.shape[0]
    C = M_CHUNK
    NC = L // C
    G = M_GROUPS
    hpg, gw, cols, per_group, dtc_spec, dtr_spec, nb = _ssd_layout(Bl, L, d_inner, "bg")

    def body(xs_ref, b_ref, c_ref, dtc_ref, dtr_ref, bc_ref, br_ref, ac_ref, ar_ref, dk_ref, y_ref, hs_ref,
             dt_s, acs_s, acr_s, h_s):
        h_s[...] = jnp.zeros_like(h_s)

        def step(c, carry):
            r = pl.ds(pl.multiple_of(c * C, C), C)
            dt_c, acs_c, acs_r = _ssd_pre(dtc_ref[r, :], dtr_ref[:, r], bc_ref[...], br_ref[...], ac_ref[...], ar_ref[...])
            dt_s[...] = dt_c
            acs_s[...] = acs_c
            acr_s[...] = acs_r
            Bm, Cm = b_ref[r, :], c_ref[r, :]
            Gm = _dot_nt(Cm, Bm)
            for j in range(hpg):
                pc = slice(j * M_HEADDIM, (j + 1) * M_HEADDIM)
                h0 = h_s[j]
                hs_ref[c, j] = h0
                y, h1 = _ssd_head(h0, xs_ref[r, pc], Gm, Bm, Cm, dt_s[:, j:j + 1], acs_s[:, j:j + 1], acr_s[j:j + 1, :],
                                  dk_ref[0:1, j:j + 1])
                y_ref[r, pc] = y
                h_s[j] = h1
            return carry

        lax.fori_loop(0, NC, step, 0)

    return pl.pallas_call(
        body, name=name,
        out_shape=(jax.ShapeDtypeStruct((T, d_inner), F32), jax.ShapeDtypeStruct((Bl, G, NC, hpg, M_HEADDIM, M_D_STATE), F32)),
        grid=(Bl, G),
        in_specs=[cols(gw, 0), cols(M_D_STATE, nb), cols(M_D_STATE, nb + G), dtc_spec, dtr_spec,
                  per_group((1, LANES)), per_group((8, 1)), per_group((1, LANES)), per_group((8, 1)), per_group((1, LANES))],
        out_specs=(cols(gw, 0), pl.BlockSpec((None, None, NC, hpg, M_HEADDIM, M_D_STATE), lambda b, g: (b, g, 0, 0, 0, 0))),
        scratch_shapes=[pltpu.VMEM((C, LANES), F32), pltpu.VMEM((C, LANES), F32), pltpu.VMEM((8, C), F32),
                        pltpu.VMEM((hpg, M_HEADDIM, M_D_STATE), F32)],
        compiler_params=_params(("parallel", "parallel")),
    )(xa, xa, xa, dtc, dtr, *prm)


def _ssd_bwd(xa, dtc, dtr, prm, states, dy, Bl, L, d_inner, *, name):
    T = xa.shape[0]
    C = M_CHUNK
    NC = L // C
    G = M_GROUPS
    hpg, gw, cols, per_group, dtc_spec, dtr_spec, nb = _ssd_layout(Bl, L, d_inner, "gb")

    def body(xs_ref, b_ref, c_ref, dtc_ref, dtr_ref, bc_ref, br_ref, ac_ref, ar_ref, dk_ref, hs_ref, dy_ref,
             dxs_ref, db_ref, dc_ref, ddtc_ref, ddtr_ref, gbc_ref, gbr_ref, gac_ref, gar_ref, gdk_ref,
             dt_s, acs_s, acr_s, ddt_s, dacs_s, dacr_s, dh_s):
        dh_s[...] = jnp.zeros_like(dh_s)

        @pl.when(pl.program_id(1) == 0)
        def _():
            for ref in (gbc_ref, gbr_ref, gac_ref, gar_ref, gdk_ref):
                ref[...] = jnp.zeros_like(ref)

        def step(i, carry):
            c = NC - 1 - i
            r = pl.ds(pl.multiple_of(c * C, C), C)
            pre_in = (dtc_ref[r, :], dtr_ref[:, r], bc_ref[...], br_ref[...], ac_ref[...], ar_ref[...])
            (dt_c, acs_c, acs_r), pre_vjp = jax.vjp(_ssd_pre, *pre_in)
            dt_s[...] = dt_c
            acs_s[...] = acs_c
            acr_s[...] = acs_r
            ddt_s[...] = jnp.zeros_like(ddt_s)
            dacs_s[...] = jnp.zeros_like(dacs_s)
            dacr_s[...] = jnp.zeros_like(dacr_s)
            Bm, Cm = b_ref[r, :], c_ref[r, :]
            Gm = _dot_nt(Cm, Bm)
            dG = jnp.zeros((C, C), F32)
            dB = jnp.zeros((C, M_D_STATE), F32)
            dC = jnp.zeros((C, M_D_STATE), F32)
            for j in range(hpg):
                pc = slice(j * M_HEADDIM, (j + 1) * M_HEADDIM)
                one = slice(j, j + 1)
                _, vjp = jax.vjp(_ssd_head, hs_ref[c, j], xs_ref[r, pc], Gm, Bm, Cm, dt_s[:, one], acs_s[:, one], acr_s[one, :],
                                 dk_ref[0:1, one])
                dh0, dxs, dGj, dBj, dCj, ddt, dacs, dacr, ddk = vjp((dy_ref[r, pc], dh_s[j]))
                dh_s[j] = dh0
                dxs_ref[r, pc] = dxs
                dG, dB, dC = dG + dGj, dB + dBj, dC + dCj
                ddt_s[:, one] = ddt
                dacs_s[:, one] = dacs
                dacr_s[one, :] = dacr
                gdk_ref[0:1, one] += ddk
            db_ref[r, :] = dB + _dot_tn(dG, Cm)
            dc_ref[r, :] = dC + _dot_nn(dG, Bm)
            ddtc, ddtr, gbc, gbr, gac, gar = pre_vjp((ddt_s[...], dacs_s[...], dacr_s[...]))
            ddtc_ref[r, :] = ddtc
            ddtr_ref[:, r] = ddtr
            gbc_ref[...] += gbc
            gbr_ref[...] += gbr
            gac_ref[...] += gac
            gar_ref[...] += gar
            return carry

        lax.fori_loop(0, NC, step, 0)

    GN = G * M_D_STATE
    bc_cols = lambda off: pl.BlockSpec((L, M_D_STATE), lambda g, b: (b, g + off))
    outs = pl.pallas_call(
        body, name=name,
        out_shape=(jax.ShapeDtypeStruct((T, d_inner), F32), jax.ShapeDtypeStruct((T, GN), F32), jax.ShapeDtypeStruct((T, GN), F32),
                   jax.ShapeDtypeStruct(dtc.shape, F32), jax.ShapeDtypeStruct(dtr.shape, F32),
                   jax.ShapeDtypeStruct((G, 1, LANES), F32), jax.ShapeDtypeStruct((G, 8, 1), F32),
                   jax.ShapeDtypeStruct((G, 1, LANES), F32), jax.ShapeDtypeStruct((G, 8, 1), F32),
                   jax.ShapeDtypeStruct((G, 1, LANES), F32)),
        grid=(G, Bl),
        in_specs=[cols(gw, 0), cols(M_D_STATE, nb), cols(M_D_STATE, nb + G), dtc_spec, dtr_spec,
                  per_group((1, LANES)), per_group((8, 1)), per_group((1, LANES)), per_group((8, 1)), per_group((1, LANES)),
                  pl.BlockSpec((None, None, NC, hpg, M_HEADDIM, M_D_STATE), lambda g, b: (b, g, 0, 0, 0, 0)), cols(gw, 0)],
        out_specs=(cols(gw, 0), bc_cols(0), bc_cols(0), dtc_spec, dtr_spec,
                   per_group((1, LANES)), per_group((8, 1)), per_group((1, LANES)), per_group((8, 1)), per_group((1, LANES))),
        scratch_shapes=[pltpu.VMEM((C, LANES), F32), pltpu.VMEM((C, LANES), F32), pltpu.VMEM((8, C), F32),
                        pltpu.VMEM((C, LANES), F32), pltpu.VMEM((C, LANES), F32), pltpu.VMEM((8, C), F32),
                        pltpu.VMEM((hpg, M_HEADDIM, M_D_STATE), F32)],
        compiler_params=_params(("arbitrary", "arbitrary")),
    )(xa, xa, xa, dtc, dtr, *prm, states, dy)
    return outs


def _mgate(y, z, w):
    t = y * _silu(z)
    return t * lax.rsqrt(jnp.mean(t * t, axis=-1, keepdims=True) + NORM_EPS) * w


def _mgate_fwd(y, z, w, *, name):
    T, d_inner = y.shape
    gw = d_inner // M_GROUPS
    tr = _tile(T, 512, 8)

    def body(y_ref, z_ref, w_ref, o_ref):
        o_ref[...] = _mgate(y_ref[...], z_ref[...], w_ref[...]).astype(o_ref.dtype)

    blk = pl.BlockSpec((tr, gw), lambda i, g: (i, g))
    return pl.pallas_call(
        body, name=name, out_shape=jax.ShapeDtypeStruct((T, d_inner), BF16), grid=(T // tr, M_GROUPS),
        in_specs=[blk, blk, pl.BlockSpec((1, gw), lambda i, g: (0, g))], out_specs=blk,
        compiler_params=_params(("parallel", "parallel")),
    )(y, z, w.reshape(1, d_inner))


def _mgate_bwd(y, z, w, dyg, *, name):
    T, d_inner = y.shape
    gw = d_inner // M_GROUPS
    tr = _tile(T, 512, 8)

    def body(y_ref, z_ref, w_ref, g_ref, dy_ref, dz_ref, dw_ref):
        _, vjp = jax.vjp(_mgate, y_ref[...], z_ref[...], w_ref[...])
        dy, dz, dw = vjp(g_ref[...].astype(F32))
        dy_ref[...] = dy
        dz_ref[...] = dz.astype(dz_ref.dtype)

        @pl.when(pl.program_id(1) == 0)
        def _():
            dw_ref[...] = jnp.zeros_like(dw_ref)

        dw_ref[...] += dw

    blk = pl.BlockSpec((tr, gw), lambda g, i: (i, g))
    vec = pl.BlockSpec((1, gw), lambda g, i: (0, g))
    return pl.pallas_call(
        body, name=name,
        out_shape=(jax.ShapeDtypeStruct((T, d_inner), F32), jax.ShapeDtypeStruct((T, d_inner), BF16), jax.ShapeDtypeStruct((1, d_inner), F32)),
        grid=(M_GROUPS, T // tr), in_specs=[blk, blk, vec, blk], out_specs=(blk, blk, vec),
        compiler_params=_params(("parallel", "arbitrary")),
    )(y, z, w.reshape(1, d_inner), dyg)


def _adamw_math(w, g, m, v):
    m1 = ADAM_B1 * m + (1.0 - ADAM_B1) * g
    v1 = ADAM_B2 * v + (1.0 - ADAM_B2) * (g * g)
    m_hat = m1 / (1.0 - ADAM_B1 ** ADAM_STEP)
    v_hat = v1 / (1.0 - ADAM_B2 ** ADAM_STEP)
    delta = -ADAM_LR * (m_hat / (jnp.sqrt(v_hat) + ADAM_EPS) + ADAM_WD * w)
    return delta, m1, v1


def _adamw_sharded(land, w, m, v, *, name):
    R, n = w.shape
    tr = _tile(R, 128, 8)

    def body(l_ref, w_ref, m_ref, v_ref, g_ref, d_ref, m1_ref, v1_ref):
        g = l_ref[0].astype(F32)
        for k in range(1, N_DEV):
            g = g + l_ref[k].astype(F32)
        d, m1, v1 = _adamw_math(w_ref[...], g, m_ref[...], v_ref[...])
        g_ref[...] = g
        d_ref[...] = d
        m1_ref[...] = m1
        v1_ref[...] = v1

    blk = pl.BlockSpec((tr, n), lambda i: (i, 0))
    out = jax.ShapeDtypeStruct((R, n), F32)
    return pl.pallas_call(
        body, name=name, out_shape=(out, out, out, out), grid=(R // tr,),
        in_specs=[pl.BlockSpec((N_DEV, tr, n), lambda i: (0, i, 0)), blk, blk, blk], out_specs=(blk, blk, blk, blk),
        compiler_params=_params(("parallel",)),
    )(land, w, m, v)


def _sum_devices(parts, *, name):
    _, R, n = parts.shape
    tr = _tile(R, 512, 8)

    def body(p_ref, o_ref):
        g = p_ref[0]
        for k in range(1, N_DEV):
            g = g + p_ref[k]
        o_ref[...] = g

    return pl.pallas_call(
        body, name=name, out_shape=jax.ShapeDtypeStruct((R, n), F32), grid=(R // tr,),
        in_specs=[pl.BlockSpec((N_DEV, tr, n), lambda i: (0, i, 0))], out_specs=pl.BlockSpec((tr, n), lambda i: (i, 0)),
        compiler_params=_params(("parallel",)),
    )(parts)


def _adamw_packed(g, w, m, v, *, name):
    R, n = w.shape
    tr = _tile(R, 512, 8)

    def body(g_ref, w_ref, m_ref, v_ref, d_ref, m1_ref, v1_ref):
        d, m1, v1 = _adamw_math(w_ref[...], g_ref[...], m_ref[...], v_ref[...])
        d_ref[...] = d
        m1_ref[...] = m1
        v1_ref[...] = v1

    blk = pl.BlockSpec((tr, n), lambda i: (i, 0))
    out = jax.ShapeDtypeStruct((R, n), F32)
    return pl.pallas_call(
        body, name=name, out_shape=(out, out, out), grid=(R // tr,), in_specs=[blk] * 4, out_specs=(blk, blk, blk),
        compiler_params=_params(("parallel",)),
    )(g, w, m, v)


PACK_ALIGN = 8 * LANES


def _pack(arrs):
    parts = []
    for a in arrs:
        flat = a.reshape(-1).astype(F32)
        pad = (-flat.shape[0]) % PACK_ALIGN
        parts.append(jnp.pad(flat, (0, pad)).reshape(-1, LANES))
    return jnp.concatenate(parts, axis=0)


def _unpack(packed, shapes, lead=()):
    out, r = [], 0
    nl = len(lead)
    for s in shapes:
        n = math.prod(s)
        rows = (n + PACK_ALIGN - 1) // PACK_ALIGN * (PACK_ALIGN // LANES)
        part = lax.slice_in_dim(packed, r, r + rows, axis=nl)
        out.append(part.reshape(lead + (rows * LANES,))[..., :n].reshape(lead + tuple(s)))
        r += rows
    return out


def _exchange(src, mode, *, name):
    shape = src.shape if mode == "scatter" else (N_DEV,) + src.shape

    def body(src_ref, out_ref, send_sems, recv_sems, local_sem):
        x, y, c = lax.axis_index("x"), lax.axis_index("y"), lax.axis_index("c")
        me = 4 * x + 2 * y + c
        mine = src_ref.at[me] if mode == "scatter" else src_ref
        local = pltpu.make_async_copy(mine, out_ref.at[me], local_sem)
        local.start()
        copies = []
        for k in range(1, N_DEV):
            kx, ky, kc = (k >> 2) & 1, (k >> 1) & 1, k & 1
            px, py, pc = (1 - x if kx else x), (1 - y if ky else y), (1 - c if kc else c)
            peer = 4 * px + 2 * py + pc
            cp = pltpu.make_async_remote_copy(
                src_ref=src_ref.at[peer] if mode == "scatter" else src_ref, dst_ref=out_ref.at[me],
                send_sem=send_sems.at[k - 1], recv_sem=recv_sems.at[k - 1],
                device_id=(px, py, pc), device_id_type=pl.DeviceIdType.MESH)
            cp.start()
            copies.append((cp, pltpu.make_async_remote_copy(
                src_ref=mine, dst_ref=out_ref.at[peer], send_sem=send_sems.at[k - 1], recv_sem=recv_sems.at[k - 1],
                device_id=(px, py, pc), device_id_type=pl.DeviceIdType.MESH)))
        for cp, landing in copies:
            landing.wait_recv()
        for cp, landing in copies:
            cp.wait_send()
        local.wait()

    return pl.pallas_call(
        body, name=name, out_shape=jax.ShapeDtypeStruct(shape, src.dtype),
        in_specs=[pl.BlockSpec(memory_space=pl.ANY)], out_specs=pl.BlockSpec(memory_space=pl.ANY),
        scratch_shapes=[pltpu.SemaphoreType.DMA((N_DEV - 1,)), pltpu.SemaphoreType.DMA((N_DEV - 1,)), pltpu.SemaphoreType.DMA],
        compiler_params=pltpu.CompilerParams(has_side_effects=True),
    )(src)


def _pad_to(a, axis, size):
    pad = [(0, 0)] * a.ndim
    pad[axis] = (0, size - a.shape[axis])
    return jnp.pad(a, pad)


def _ssd_operands(dt_raw, dt_bias, a_log, dskip):
    T, heads = dt_raw.shape
    G = M_GROUPS
    hpg = heads // G
    t3 = dt_raw.reshape(T, G, hpg)
    dtc = _pad_to(t3.transpose(1, 0, 2), 2, LANES)
    dtr = _pad_to(t3.transpose(1, 2, 0), 1, 8)
    col = lambda p: _pad_to(p.reshape(G, 1, hpg), 2, LANES)
    row = lambda p: _pad_to(p.reshape(G, hpg, 1), 1, 8)
    return dtc, dtr, (col(dt_bias), row(dt_bias), col(a_log), row(a_log), col(dskip))


def _ssd_unpack(ddtc, ddtr, gbc, gbr, gac, gar, gdk, heads):
    G = M_GROUPS
    hpg = heads // G
    T = ddtc.shape[1]
    ddt = ddtc[:, :, :hpg].transpose(1, 0, 2).reshape(T, heads) + ddtr[:, :hpg, :].transpose(2, 0, 1).reshape(T, heads)
    both = lambda c, r: c[:, 0, :hpg].reshape(heads) + r[:, :hpg, 0].reshape(heads)
    return ddt, both(gbc, gbr), both(gac, gar), gdk[:, 0, :hpg].reshape(heads)


def _local_step(x, target, W):
    Bl, L, D = x.shape
    T = Bl * L
    depth = W["mix_norm"].shape[0]
    d_inner = W["m_w_out"].shape[1]
    heads = d_inner // M_HEADDIM
    conv_dim = W["m_conv_w"].shape[2]
    mm = _matmul

    lb = _lb_fwd(W["hgrn_lb_logits"], name="hgrn_lower_bounds")
    h = x.reshape(T, D)
    saved = []
    for i in range(depth):
        j = i // 2
        s = {"h": h}
        u = _rms_fwd(h, W["mix_norm"][i], name="mix_norm_fwd")
        s["u"] = u
        if i % 2 == 0:
            P = mm(u, W["hgrn_w_in"][j], "nn", F32, name="hgrn_in_fwd")
            og, st = _hgrn_fwd(P, lb[j], W["hgrn_gnorm"][j], Bl, L, name="hgrn_scan_fwd")
            h = mm(og, W["hgrn_w_out"][j], "nn", F32, add=h, name="hgrn_out_fwd")
            s.update(P=P, og=og, st=st)
        else:
            w_in = W["m_w_in"][j]
            z = mm(u, w_in[:, :d_inner], "nn", F32, name="mamba_z_fwd")
            xbc = mm(u, w_in[:, d_inner:d_inner + conv_dim], "nn", F32, name="mamba_xbc_fwd")
            dt_raw = mm(u, w_in[:, d_inner + conv_dim:], "nn", F32, name="mamba_dt_fwd")
            xa = _mconv_fwd(xbc, W["m_conv_w"][j], W["m_conv_b"][j], Bl, L, name="mamba_conv_fwd")
            dtc, dtr, prm = _ssd_operands(dt_raw, W["m_dt_bias"][j], W["m_A_log"][j], W["m_D"][j])
            y, hs = _ssd_fwd(xa, dtc, dtr, prm, Bl, L, d_inner, name="mamba_scan_fwd")
            yg = _mgate_fwd(y, z, W["m_norm"][j], name="mamba_gate_fwd")
            h = mm(yg, W["m_w_out"][j], "nn", F32, add=h, name="mamba_out_fwd")
            s.update(z=z, xbc=xbc, xa=xa, dtc=dtc, dtr=dtr, prm=prm, y=y, hs=hs, yg=yg)
        s["h2"] = h
        u2 = _rms_fwd(h, W["ffn_norm"][i], name="ffn_norm_fwd")
        a = mm(u2, W["f_w_up"][i], "nn", BF16, name="ffn_up_fwd")
        yf = _ffn_gate_fwd(a, W["f_conv_w"][i], W["f_conv_b"][i], Bl, L, name="ffn_gate_fwd")
        h = mm(yf, W["f_w_down"][i], "nn", F32, add=h, name="ffn_down_fwd")
        s.update(u2=u2, a=a, yf=yf)
        saved.append(s)

    loss, dh, d_final = _loss_head(h, W["final_norm"], target.reshape(T, D), name="loss_head")

    g = {k: [None] * W[k].shape[0] for k in W if k != "final_norm"}
    g["final_norm"] = d_final.reshape(D)
    dlb = [None] * W["hgrn_lb_logits"].shape[0]
    for i in reversed(range(depth)):
        j = i // 2
        s = saved[i]
        dyf = mm(dh, W["f_w_down"][i], "nt", BF16, name="ffn_down_bwd_x")
        g["f_w_down"][i] = mm(s["yf"], dh, "tn", BF16, name="ffn_down_bwd_w")
        dag, dau, pf = _ffn_gate_bwd(s["a"], W["f_conv_w"][i], W["f_conv_b"][i], dyf, Bl, L, name="ffn_gate_bwd")
        da = jnp.concatenate([dag, dau], axis=1)
        du2 = mm(da, W["f_w_up"][i], "nt", F32, name="ffn_up_bwd_x")
        g["f_w_up"][i] = mm(s["u2"], da, "tn", BF16, name="ffn_up_bwd_w")
        g["f_conv_w"][i], g["f_conv_b"][i] = pf[:FFN_CONV], pf[FFN_CONV]
        dh, dn = _rms_bwd(s["h2"], W["ffn_norm"][i], du2, dh, name="ffn_norm_bwd")
        g["ffn_norm"][i] = dn.reshape(D)
        if i % 2 == 0:
            dog = mm(dh, W["hgrn_w_out"][j], "nt", BF16, name="hgrn_out_bwd_x")
            g["hgrn_w_out"][j] = mm(s["og"], dh, "tn", BF16, name="hgrn_out_bwd_w")
            dP4, dlb_j, dgn = _hgrn_bwd(s["P"], lb[j], W["hgrn_gnorm"][j], s["st"], dog, Bl, L, name="hgrn_scan_bwd")
            dP = dP4.transpose(1, 0, 2).reshape(T, 4 * D)
            du = mm(dP, W["hgrn_w_in"][j], "nt", F32, name="hgrn_in_bwd_x")
            g["hgrn_w_in"][j] = mm(s["u"], dP, "tn", BF16, name="hgrn_in_bwd_w")
            dlb[j] = dlb_j
            g["hgrn_gnorm"][j] = dgn.reshape(HGRN_HEAD)
        else:
            w_in = W["m_w_in"][j]
            dyg = mm(dh, W["m_w_out"][j], "nt", BF16, name="mamba_out_bwd_x")
            g["m_w_out"][j] = mm(s["yg"], dh, "tn", BF16, name="mamba_out_bwd_w")
            dy, dz, dnw = _mgate_bwd(s["y"], s["z"], W["m_norm"][j], dyg, name="mamba_gate_bwd")
            dxs, dB, dC, ddtc, ddtr, gbc, gbr, gac, gar, gdk = _ssd_bwd(
                s["xa"], s["dtc"], s["dtr"], s["prm"], s["hs"], dy, Bl, L, d_inner, name="mamba_scan_bwd")
            ddt, g["m_dt_bias"][j], g["m_A_log"][j], g["m_D"][j] = _ssd_unpack(ddtc, ddtr, gbc, gbr, gac, gar, gdk, heads)
            dxa = jnp.concatenate([dxs, dB, dC], axis=1)
            dxbc, pm = _mconv_bwd(s["xbc"], W["m_conv_w"][j], W["m_conv_b"][j], dxa, Bl, L, name="mamba_conv_bwd")
            g["m_conv_w"][j], g["m_conv_b"][j], g["m_norm"][j] = pm[:M_CONV], pm[M_CONV], dnw.reshape(d_inner)
            ddt = ddt.astype(BF16)
            du = mm(dz, w_in[:, :d_inner], "nt", F32, name="mamba_z_bwd_x")
            du = mm(dxbc, w_in[:, d_inner:d_inner + conv_dim], "nt", F32, add=du, name="mamba_xbc_bwd_x")
            du = mm(ddt, w_in[:, d_inner + conv_dim:], "nt", F32, add=du, name="mamba_dt_bwd_x")
            g["m_w_in"][j] = jnp.concatenate([
                mm(s["u"], dz, "tn", BF16, name="mamba_z_bwd_w"), mm(s["u"], dxbc, "tn", BF16, name="mamba_xbc_bwd_w"),
                mm(s["u"], ddt, "tn", BF16, name="mamba_dt_bwd_w")], axis=1)
        dh, dn = _rms_bwd(s["h"], W["mix_norm"][i], du, dh, name="mix_norm_bwd")
        g["mix_norm"][i] = dn.reshape(D)
    g["hgrn_lb_logits"] = _lb_bwd(W["hgrn_lb_logits"], jnp.concatenate(dlb, axis=0), name="hgrn_lower_bounds_bwd")
    grads = {k: (v if not isinstance(v, list) else jnp.stack(v)) for k, v in g.items()}
    return loss, dh.reshape(Bl, L, D), grads


WEIGHTS = ("mix_norm", "ffn_norm", "final_norm", "hgrn_w_in", "hgrn_lb_logits", "hgrn_gnorm", "hgrn_w_out", "m_w_in",
           "m_conv_w", "m_conv_b", "m_dt_bias", "m_A_log", "m_D", "m_norm", "m_w_out", "f_w_up", "f_conv_w", "f_conv_b",
           "f_w_down")
COL_SHARDED = ("hgrn_w_in", "m_w_in", "f_w_up")
ROW_SHARDED = ("hgrn_w_out", "m_w_out", "f_w_down")
SMALL_SHARDED = ("m_conv_w", "m_conv_b", "m_norm", "f_conv_w")
SMALL = tuple(k for k in WEIGHTS if k not in COL_SHARDED + ROW_SHARDED)


def _join_last(g):
    nd = g.ndim
    t = g.transpose(tuple(range(1, nd - 1)) + (0, nd - 1))
    return t.reshape(t.shape[:-2] + (N_DEV * g.shape[-1],))


def _split_last(a):
    n = a.shape[-1] // N_DEV
    t = a.reshape(a.shape[:-1] + (N_DEV, n))
    nd = t.ndim
    return t.transpose((nd - 2,) + tuple(range(nd - 2)) + (nd - 1,))


def kernel(x, mix_norm, ffn_norm, final_norm, hgrn_w_in, hgrn_lb_logits, hgrn_gnorm, hgrn_w_out, m_w_in, m_conv_w, m_conv_b, m_dt_bias, m_A_log, m_D, m_norm, m_w_out, f_w_up, f_conv_w, f_conv_b, f_w_down, loss_target, m_mix_norm, m_ffn_norm, m_final_norm, m_hgrn_w_in, m_hgrn_lb_logits, m_hgrn_gnorm, m_hgrn_w_out, m_m_w_in, m_m_conv_w, m_m_conv_b, m_m_dt_bias, m_m_A_log, m_m_D, m_m_norm, m_m_w_out, m_f_w_up, m_f_conv_w, m_f_conv_b, m_f_w_down, v_mix_norm, v_ffn_norm, v_final_norm, v_hgrn_w_in, v_hgrn_lb_logits, v_hgrn_gnorm, v_hgrn_w_out, v_m_w_in, v_m_conv_w, v_m_conv_b, v_m_dt_bias, v_m_A_log, v_m_D, v_m_norm, v_m_w_out, v_f_w_up, v_f_conv_w, v_f_conv_b, v_f_w_down):
    w = dict(zip(WEIGHTS, (mix_norm, ffn_norm, final_norm, hgrn_w_in, hgrn_lb_logits, hgrn_gnorm, hgrn_w_out, m_w_in, m_conv_w, m_conv_b, m_dt_bias, m_A_log, m_D, m_norm, m_w_out, f_w_up, f_conv_w, f_conv_b, f_w_down)))
    m = dict(zip(WEIGHTS, (m_mix_norm, m_ffn_norm, m_final_norm, m_hgrn_w_in, m_hgrn_lb_logits, m_hgrn_gnorm, m_hgrn_w_out, m_m_w_in, m_m_conv_w, m_m_conv_b, m_m_dt_bias, m_m_A_log, m_m_D, m_m_norm, m_m_w_out, m_f_w_up, m_f_conv_w, m_f_conv_b, m_f_w_down)))
    v = dict(zip(WEIGHTS, (v_mix_norm, v_ffn_norm, v_final_norm, v_hgrn_w_in, v_hgrn_lb_logits, v_hgrn_gnorm, v_hgrn_w_out, v_m_w_in, v_m_conv_w, v_m_conv_b, v_m_dt_bias, v_m_A_log, v_m_D, v_m_norm, v_m_w_out, v_f_w_up, v_f_conv_w, v_f_conv_b, v_f_w_down)))
    me = 4 * lax.axis_index("x") + 2 * lax.axis_index("y") + lax.axis_index("c")

    full = {k: w[k] for k in SMALL if k not in SMALL_SHARDED}
    for k in COL_SHARDED:
        full[k] = _join_last(_exchange(w[k].astype(BF16), "gather", name="gather_" + k))
    for k in ROW_SHARDED:
        g = _exchange(w[k].astype(BF16), "gather", name="gather_" + k)
        full[k] = g.transpose(1, 0, 2, 3).reshape(g.shape[1], N_DEV * g.shape[2], g.shape[3])
    shard_shapes = [w[k].shape for k in SMALL_SHARDED]
    gathered = _exchange(_pack([w[k] for k in SMALL_SHARDED]), "gather", name="gather_small_params")
    for k, g in zip(SMALL_SHARDED, _unpack(gathered, shard_shapes, lead=(N_DEV,))):
        full[k] = _join_last(g)

    loss, grad_x, grads = _local_step(x, loss_target, full)

    out = {}
    for k in COL_SHARDED + ROW_SHARDED:
        if k in COL_SHARDED:
            parts = _split_last(grads[k])
        else:
            gk = grads[k]
            parts = gk.reshape(gk.shape[0], N_DEV, gk.shape[1] // N_DEV, gk.shape[2]).transpose(1, 0, 2, 3)
        land = _exchange(parts, "scatter", name="scatter_" + k)
        rows, n = math.prod(w[k].shape[:-1]), w[k].shape[-1]
        res = _adamw_sharded(land.reshape(N_DEV, rows, n), w[k].reshape(rows, n), m[k].reshape(rows, n), v[k].reshape(rows, n),
                             name="adamw_" + k)
        out[k] = tuple(r.reshape(w[k].shape) for r in res)

    small_full_shapes = [(1, 1)] + [grads[k].shape for k in SMALL]
    packed = _pack([loss] + [grads[k] for k in SMALL])
    summed = _sum_devices(_exchange(packed, "gather", name="gather_small_grads"), name="sum_small_grads")
    parts = _unpack(summed, small_full_shapes)
    loss_all, small_g = parts[0], dict(zip(SMALL, parts[1:]))
    for k in SMALL_SHARDED:
        n = w[k].shape[-1]
        small_g[k] = lax.dynamic_slice_in_dim(small_g[k], me * n, n, axis=small_g[k].ndim - 1)
    shapes = [w[k].shape for k in SMALL]
    res = _adamw_packed(_pack([small_g[k] for k in SMALL]), _pack([w[k] for k in SMALL]), _pack([m[k] for k in SMALL]),
                        _pack([v[k] for k in SMALL]), name="adamw_small")
    res = [_unpack(r, shapes) for r in res]
    for i, k in enumerate(SMALL):
        out[k] = (small_g[k], res[0][i], res[1][i], res[2][i])

    return (loss_all.reshape(()), grad_x, *[out[k][0] for k in WEIGHTS], *[out[k][1] for k in WEIGHTS],
            *[out[k][2] for k in WEIGHTS], *[out[k][3] for k in WEIGHTS])
```

```python
import functools
import math

import jax
import jax.numpy as jnp
from jax import lax
from jax.experimental import pallas as pl
from jax.experimental.pallas import tpu as pltpu

F32 = jnp.float32
BF16 = jnp.bfloat16
HIGHEST = lax.Precision.HIGHEST

NORM_EPS = 1e-5
ADAM_LR, ADAM_B1, ADAM_B2, ADAM_EPS, ADAM_WD, ADAM_STEP = 0.001, 0.9, 0.999, 1e-08, 0.01, 10

N_DEV = 8
LANES = 128
V7X_VMEM_BYTES = 64 * 1024 * 1024
VMEM_LIMIT = V7X_VMEM_BYTES * 3 // 4

HGRN_HEAD = 128
HGRN_CHUNK = 64
HGRN_SUB = 16
M_HEADDIM = 64
M_GROUPS = 8
M_D_STATE = 128
M_CONV = 4
M_CHUNK = 128
FFN_CONV = 3
CONV_COLS = 256


def _params(dims=None, **kw):
    return pltpu.CompilerParams(dimension_semantics=dims, vmem_limit_bytes=VMEM_LIMIT, **kw)


def _tile(dim, target, align=LANES):
    t = (min(target, dim) // align) * align
    while t >= align:
        if dim % t == 0:
            return t
        t -= align
    return dim


class _Buf:
    def __init__(self, rows, n, dtype):
        self.arr = lax.empty((N_DEV, rows, n), dtype)


class _Job:
    def __init__(self, src, mode, buf, r0=0, r1=None, q0=0):
        rows = src.shape[-2]
        self.src, self.mode, self.buf, self.r0, self.r1, self.q0 = src, mode, buf, r0, rows if r1 is None else r1, q0


def _job_copies(job, src_ref, out_ref, send_sems, recv_sems, local_sem):
    x, y, c = lax.axis_index("x"), lax.axis_index("y"), lax.axis_index("c")
    me = 4 * x + 2 * y + c
    n = job.r1 - job.r0
    rows, land = pl.ds(job.r0, n), pl.ds(job.q0, n)
    part = (lambda d: src_ref.at[d, rows]) if job.mode == "scatter" else (lambda d: src_ref.at[rows])
    local = pltpu.make_async_copy(part(me), out_ref.at[me, land], local_sem)
    pairs = []
    for k in range(1, N_DEV):
        kx, ky, kc = (k >> 2) & 1, (k >> 1) & 1, k & 1
        px, py, pc = (1 - x if kx else x), (1 - y if ky else y), (1 - c if kc else c)
        peer = 4 * px + 2 * py + pc
        sems = dict(send_sem=send_sems.at[k - 1], recv_sem=recv_sems.at[k - 1], device_id=(px, py, pc),
                    device_id_type=pl.DeviceIdType.MESH)
        send = pltpu.make_async_remote_copy(src_ref=part(peer), dst_ref=out_ref.at[me, land], **sems)
        landing = pltpu.make_async_remote_copy(src_ref=part(me), dst_ref=out_ref.at[peer, land], **sems)
        pairs.append((send, landing))
    return local, pairs


def _jobs_start(jobs, src_refs, out_refs, send_sems, recv_sems, local_sems):
    for i, job in enumerate(jobs):
        local, pairs = _job_copies(job, src_refs[i], out_refs[i], send_sems.at[i], recv_sems.at[i], local_sems.at[i])
        local.start()
        for send, _ in pairs:
            send.start()


def _jobs_wait(jobs, src_refs, out_refs, send_sems, recv_sems, local_sems):
    for i, job in enumerate(jobs):
        local, pairs = _job_copies(job, src_refs[i], out_refs[i], send_sems.at[i], recv_sems.at[i], local_sems.at[i])
        for _, landing in pairs:
            landing.wait_recv()
        for send, _ in pairs:
            send.wait_send()
        local.wait()


def _job_scratch(n):
    return [pltpu.SemaphoreType.DMA((n, N_DEV - 1)), pltpu.SemaphoreType.DMA((n, N_DEV - 1)), pltpu.SemaphoreType.DMA((n,))]


def _buf_shapes(jobs):
    assert len({id(j.buf) for j in jobs}) == len(jobs), "one job per buffer and call"
    return tuple(jax.ShapeDtypeStruct(j.buf.arr.shape, j.buf.arr.dtype) for j in jobs)


def _exchange(jobs, *, name):
    n = len(jobs)

    def body(*refs):
        srcs, dsts, sems = refs[:n], refs[2 * n:3 * n], refs[3 * n:]
        _jobs_start(jobs, srcs, dsts, *sems)
        _jobs_wait(jobs, srcs, dsts, *sems)

    hbm = pl.BlockSpec(memory_space=pl.ANY)
    outs = pl.pallas_call(
        body, name=name, out_shape=_buf_shapes(jobs), in_specs=[hbm] * (2 * n), out_specs=(hbm,) * n,
        input_output_aliases={n + i: i for i in range(n)},
        scratch_shapes=_job_scratch(n), compiler_params=pltpu.CompilerParams(has_side_effects=True),
    )(*[j.src for j in jobs], *[j.buf.arr for j in jobs])
    for j, o in zip(jobs, outs):
        j.buf.arr = o


def _call(body, *, name, out_shape, grid, in_specs, out_specs, scratch_shapes=(), dims, jobs=()):
    single = not isinstance(out_shape, (tuple, list))
    out_shape = (out_shape,) if single else tuple(out_shape)
    out_specs = (out_specs,) if single else tuple(out_specs)
    n_in, n_out, n_scr, nj = len(in_specs), len(out_shape), len(scratch_shapes), len(jobs)
    if not jobs:
        plain = pl.pallas_call(body, name=name, out_shape=out_shape, grid=grid, in_specs=list(in_specs), out_specs=out_specs,
                               scratch_shapes=list(scratch_shapes), compiler_params=_params(dims))

        def run_plain(*args):
            r = plain(*args)
            return r[0] if single else tuple(r)

        return run_plain

    def carrying(*refs):
        ins, srcs = refs[:n_in], refs[n_in:n_in + nj]
        o0 = n_in + 2 * nj
        outs, dsts = refs[o0:o0 + n_out], refs[o0 + n_out:o0 + n_out + nj]
        scr, sems = refs[o0 + n_out + nj:o0 + n_out + nj + n_scr], refs[o0 + n_out + nj + n_scr:]
        ids = [pl.program_id(d) for d in range(len(grid))]
        first = functools.reduce(jnp.logical_and, [i == 0 for i in ids])
        last = functools.reduce(jnp.logical_and, [i == g - 1 for i, g in zip(ids, grid)])

        @pl.when(first)
        def _():
            _jobs_start(jobs, srcs, dsts, *sems)

        body(*ins, *outs, *scr)

        @pl.when(last)
        def _():
            _jobs_wait(jobs, srcs, dsts, *sems)

    hbm = pl.BlockSpec(memory_space=pl.ANY)
    call = pl.pallas_call(
        carrying, name=name, out_shape=out_shape + _buf_shapes(jobs), grid=grid,
        in_specs=list(in_specs) + [hbm] * (2 * nj), out_specs=out_specs + (hbm,) * nj,
        input_output_aliases={n_in + nj + i: n_out + i for i in range(nj)},
        scratch_shapes=list(scratch_shapes) + _job_scratch(nj),
        compiler_params=_params(("arbitrary",) * len(grid), has_side_effects=True))

    def run(*args):
        r = call(*args, *[j.src for j in jobs], *[j.buf.arr for j in jobs])
        for j, o in zip(jobs, r[n_out:]):
            j.buf.arr = o
        return r[0] if single else tuple(r[:n_out])

    return run


def _dg(a, b, ca, cb):
    return lax.dot_general(a.astype(BF16), b.astype(BF16), (((ca,), (cb,)), ((), ())), preferred_element_type=F32)


@jax.custom_vjp
def _dot_nn(a, b):
    return _dg(a, b, 1, 0)


def _dot_nn_f(a, b):
    return _dg(a, b, 1, 0), (a, b)


def _dot_nn_b(res, g):
    a, b = res
    return _dg(g, b, 1, 1), _dg(a, g, 0, 0)


_dot_nn.defvjp(_dot_nn_f, _dot_nn_b)


@jax.custom_vjp
def _dot_nt(a, b):
    return _dg(a, b, 1, 1)


def _dot_nt_f(a, b):
    return _dg(a, b, 1, 1), (a, b)


def _dot_nt_b(res, g):
    a, b = res
    return _dg(g, b, 1, 0), _dg(g, a, 0, 0)


_dot_nt.defvjp(_dot_nt_f, _dot_nt_b)


@jax.custom_vjp
def _dot_tn(a, b):
    return _dg(a, b, 0, 0)


def _dot_tn_f(a, b):
    return _dg(a, b, 0, 0), (a, b)


def _dot_tn_b(res, g):
    a, b = res
    return _dg(b, g, 1, 1), _dg(a, g, 1, 0)


_dot_tn.defvjp(_dot_tn_f, _dot_tn_b)


def _tri(n):
    return (lax.broadcasted_iota(jnp.int32, (n, n), 0) >= lax.broadcasted_iota(jnp.int32, (n, n), 1)).astype(F32)


def _silu(x):
    return x * jax.nn.sigmoid(x)


def _matmul(a, b, mode, out_dtype, *, name, add=None, tm=1024, tn=1024, tk=512, jobs=()):
    if mode == "nn":
        (M, K), (K2, N) = a.shape, b.shape
    elif mode == "nt":
        (M, K), (N, K2) = a.shape, b.shape
    else:
        (K, M), (K2, N) = a.shape, b.shape
    assert K == K2, (a.shape, b.shape, mode)
    tm, tn, tk = _tile(M, tm, 8 if M % LANES else LANES), _tile(N, tn), _tile(K, tk)
    nk = K // tk
    ca, cb = {"nn": (1, 0), "nt": (1, 1), "tn": (0, 0)}[mode]

    def body(*refs):
        if add is None:
            a_ref, b_ref, o_ref, acc_ref = refs
        else:
            a_ref, b_ref, add_ref, o_ref, acc_ref = refs
        k = pl.program_id(2)

        @pl.when(k == 0)
        def _():
            acc_ref[...] = jnp.zeros_like(acc_ref)

        acc_ref[...] += _dg(a_ref[...], b_ref[...], ca, cb)

        @pl.when(k == nk - 1)
        def _():
            r = acc_ref[...]
            if add is not None:
                r = r + add_ref[...]
            o_ref[...] = r.astype(o_ref.dtype)

    a_spec = pl.BlockSpec((tk, tm), lambda i, j, k: (k, i)) if mode == "tn" else pl.BlockSpec((tm, tk), lambda i, j, k: (i, k))
    b_spec = pl.BlockSpec((tn, tk), lambda i, j, k: (j, k)) if mode == "nt" else pl.BlockSpec((tk, tn), lambda i, j, k: (k, j))
    o_spec = pl.BlockSpec((tm, tn), lambda i, j, k: (i, j))
    in_specs, args = [a_spec, b_spec], [a, b]
    if add is not None:
        in_specs.append(o_spec)
        args.append(add)
    return _call(
        body, name=name, out_shape=jax.ShapeDtypeStruct((M, N), out_dtype), grid=(M // tm, N // tn, nk),
        in_specs=in_specs, out_specs=o_spec, scratch_shapes=[pltpu.VMEM((tm, tn), F32)],
        dims=("parallel", "parallel", "arbitrary"), jobs=jobs,
    )(*args)


def _rms_fwd(h, w, *, name):
    T, D = h.shape
    tr = _tile(T, 256, 8)

    def body(h_ref, w_ref, u_ref):
        x = h_ref[...]
        u_ref[...] = (x * lax.rsqrt(jnp.mean(x * x, axis=-1, keepdims=True) + NORM_EPS) * w_ref[...]).astype(u_ref.dtype)

    return pl.pallas_call(
        body, name=name, out_shape=jax.ShapeDtypeStruct((T, D), BF16), grid=(T // tr,),
        in_specs=[pl.BlockSpec((tr, D), lambda i: (i, 0)), pl.BlockSpec((1, D), lambda i: (0, 0))],
        out_specs=pl.BlockSpec((tr, D), lambda i: (i, 0)), compiler_params=_params(("parallel",)),
    )(h, w.reshape(1, D))


def _rms_bwd(h, w, du, dh_in, *, name):
    T, D = h.shape
    tr = _tile(T, 256, 8)

    def body(h_ref, w_ref, du_ref, dhin_ref, dh_ref, dw_ref):
        x = h_ref[...]
        g = du_ref[...].astype(F32)
        rstd = lax.rsqrt(jnp.mean(x * x, axis=-1, keepdims=True) + NORM_EPS)
        xhat = x * rstd
        gx = g * w_ref[...]
        dh_ref[...] = dhin_ref[...] + rstd * (gx - xhat * jnp.mean(gx * xhat, axis=-1, keepdims=True))

        @pl.when(pl.program_id(0) == 0)
        def _():
            dw_ref[...] = jnp.zeros_like(dw_ref)

        dw_ref[...] += jnp.sum(g * xhat, axis=0, keepdims=True)

    row = pl.BlockSpec((tr, D), lambda i: (i, 0))
    vec = pl.BlockSpec((1, D), lambda i: (0, 0))
    return pl.pallas_call(
        body, name=name, out_shape=(jax.ShapeDtypeStruct((T, D), F32), jax.ShapeDtypeStruct((1, D), F32)), grid=(T // tr,),
        in_specs=[row, vec, row, row], out_specs=(row, vec), compiler_params=_params(("arbitrary",)),
    )(h, w.reshape(1, D), du, dh_in)


def _loss_head(h, w, target, *, name):
    T, D = h.shape
    tr = _tile(T, 256, 8)

    def body(h_ref, w_ref, t_ref, loss_ref, dh_ref, dw_ref):
        x = h_ref[...]
        rstd = lax.rsqrt(jnp.mean(x * x, axis=-1, keepdims=True) + NORM_EPS)
        xhat = x * rstd
        err = xhat * w_ref[...] - t_ref[...]
        g = err * (1.0 / D)
        gx = g * w_ref[...]
        dh_ref[...] = rstd * (gx - xhat * jnp.mean(gx * xhat, axis=-1, keepdims=True))

        @pl.when(pl.program_id(0) == 0)
        def _():
            dw_ref[...] = jnp.zeros_like(dw_ref)
            loss_ref[...] = jnp.zeros_like(loss_ref)

        dw_ref[...] += jnp.sum(g * xhat, axis=0, keepdims=True)
        loss_ref[...] += (0.5 / D) * jnp.sum(jnp.sum(err * err, axis=-1, keepdims=True), axis=0, keepdims=True)

    row = pl.BlockSpec((tr, D), lambda i: (i, 0))
    vec = pl.BlockSpec((1, D), lambda i: (0, 0))
    one = pl.BlockSpec((1, 1), lambda i: (0, 0))
    return pl.pallas_call(
        body, name=name,
        out_shape=(jax.ShapeDtypeStruct((1, 1), F32), jax.ShapeDtypeStruct((T, D), F32), jax.ShapeDtypeStruct((1, D), F32)),
        grid=(T // tr,), in_specs=[row, vec, row], out_specs=(one, row, vec), compiler_params=_params(("arbitrary",)),
    )(h, w.reshape(1, D), target)


def _shift_down(x, s):
    if s == 0:
        return x
    rows = lax.broadcasted_iota(jnp.int32, x.shape, 0)
    return jnp.where(rows >= s, pltpu.roll(x, s, 0), 0.0)


def _shift_up(x, s):
    if s == 0:
        return x
    n = x.shape[0]
    rows = lax.broadcasted_iota(jnp.int32, x.shape, 0)
    return jnp.where(rows < n - s, pltpu.roll(x, n - s, 0), 0.0)


def _conv_pre(a, w_ref, b_ref, taps):
    pre = b_ref[0:1, :] + w_ref[taps - 1:taps, :] * a
    for k in range(taps - 1):
        pre = pre + w_ref[k:k + 1, :] * _shift_down(a, taps - 1 - k)
    return pre


def _conv_bwd(a, dpre, w_ref, taps):
    da = w_ref[taps - 1:taps, :] * dpre
    rows = []
    for k in range(taps - 1):
        da = da + w_ref[k:k + 1, :] * _shift_up(dpre, taps - 1 - k)
        rows.append(jnp.sum(dpre * _shift_down(a, taps - 1 - k), axis=0, keepdims=True))
    rows.append(jnp.sum(dpre * a, axis=0, keepdims=True))
    rows.append(jnp.sum(dpre, axis=0, keepdims=True))
    return da, rows


def _ffn_gate_fwd(a, cw, cb, Bl, L, *, name, jobs=()):
    T, F2 = a.shape
    F = F2 // 2
    tc = _tile(F, CONV_COLS)
    nj = F // tc

    def body(ag_ref, au_ref, wg_ref, wu_ref, bg_ref, bu_ref, y_ref):
        hg = _conv_pre(ag_ref[...].astype(F32), wg_ref, bg_ref, FFN_CONV)
        hu = _conv_pre(au_ref[...].astype(F32), wu_ref, bu_ref, FFN_CONV)
        y_ref[...] = (_silu(hg) * hu).astype(y_ref.dtype)

    blk = lambda off: pl.BlockSpec((L, tc), lambda b, j: (b, j + off))
    wblk = lambda off: pl.BlockSpec((FFN_CONV, tc), lambda b, j: (0, j + off))
    bblk = lambda off: pl.BlockSpec((1, tc), lambda b, j: (0, j + off))
    return _call(
        body, name=name, out_shape=jax.ShapeDtypeStruct((T, F), BF16), grid=(Bl, nj),
        in_specs=[blk(0), blk(nj), wblk(0), wblk(nj), bblk(0), bblk(nj)], out_specs=blk(0),
        dims=("parallel", "parallel"), jobs=jobs,
    )(a, a, cw, cw, cb.reshape(1, F2), cb.reshape(1, F2))


def _ffn_gate_bwd(a, cw, cb, dy, Bl, L, *, name, jobs=()):
    T, F2 = a.shape
    F = F2 // 2
    tc = _tile(F, CONV_COLS)
    nj = F // tc

    def body(ag_ref, au_ref, wg_ref, wu_ref, bg_ref, bu_ref, dy_ref, dag_ref, dau_ref, pg_ref, pu_ref):
        ag, au = ag_ref[...].astype(F32), au_ref[...].astype(F32)
        hg = _conv_pre(ag, wg_ref, bg_ref, FFN_CONV)
        hu = _conv_pre(au, wu_ref, bu_ref, FFN_CONV)
        g = dy_ref[...].astype(F32)
        s = jax.nn.sigmoid(hg)
        dhg = g * hu * (s * (1.0 + hg * (1.0 - s)))
        dhu = g * (hg * s)
        dag, rg = _conv_bwd(ag, dhg, wg_ref, FFN_CONV)
        dau, ru = _conv_bwd(au, dhu, wu_ref, FFN_CONV)
        dag_ref[...] = dag.astype(dag_ref.dtype)
        dau_ref[...] = dau.astype(dau_ref.dtype)

        @pl.when(pl.program_id(1) == 0)
        def _():
            pg_ref[...] = jnp.zeros_like(pg_ref)
            pu_ref[...] = jnp.zeros_like(pu_ref)

        for k in range(FFN_CONV + 1):
            pg_ref[k:k + 1, :] += rg[k]
            pu_ref[k:k + 1, :] += ru[k]

    blk = lambda off: pl.BlockSpec((L, tc), lambda j, b: (b, j + off))
    wblk = lambda off: pl.BlockSpec((FFN_CONV, tc), lambda j, b: (0, j + off))
    bblk = lambda off: pl.BlockSpec((1, tc), lambda j, b: (0, j + off))
    pblk = lambda off: pl.BlockSpec((8, tc), lambda j, b: (0, j + off))
    dag, dau, pg, pu = _call(
        body, name=name,
        out_shape=(jax.ShapeDtypeStruct((T, F), BF16), jax.ShapeDtypeStruct((T, F), BF16),
                   jax.ShapeDtypeStruct((8, F), F32), jax.ShapeDtypeStruct((8, F), F32)),
        grid=(nj, Bl),
        in_specs=[blk(0), blk(nj), wblk(0), wblk(nj), bblk(0), bblk(nj), blk(0)],
        out_specs=(blk(0), blk(0), pblk(0), pblk(0)),
        dims=("parallel", "arbitrary"), jobs=jobs,
    )(a, a, cw, cw, cb.reshape(1, F2), cb.reshape(1, F2), dy)
    return dag, dau, jnp.concatenate([pg, pu], axis=1)


def _mconv_fwd(xbc, cw, cb, Bl, L, *, name):
    T, W = xbc.shape
    tc = _tile(W, CONV_COLS)

    def body(a_ref, w_ref, b_ref, y_ref):
        y_ref[...] = _silu(_conv_pre(a_ref[...], w_ref, b_ref, M_CONV))

    blk = pl.BlockSpec((L, tc), lambda b, j: (b, j))
    return pl.pallas_call(
        body, name=name, out_shape=jax.ShapeDtypeStruct((T, W), F32), grid=(Bl, W // tc),
        in_specs=[blk, pl.BlockSpec((M_CONV, tc), lambda b, j: (0, j)), pl.BlockSpec((1, tc), lambda b, j: (0, j))],
        out_specs=blk, compiler_params=_params(("parallel", "parallel")),
    )(xbc, cw, cb.reshape(1, W))


def _mconv_bwd(xbc, cw, cb, dy, Bl, L, *, name):
    T, W = xbc.shape
    tc = _tile(W, CONV_COLS)

    def body(a_ref, w_ref, b_ref, dy_ref, da_ref, p_ref):
        a = a_ref[...]
        pre = _conv_pre(a, w_ref, b_ref, M_CONV)
        s = jax.nn.sigmoid(pre)
        dpre = dy_ref[...] * (s * (1.0 + pre * (1.0 - s)))
        da, rows = _conv_bwd(a, dpre, w_ref, M_CONV)
        da_ref[...] = da.astype(da_ref.dtype)

        @pl.when(pl.program_id(1) == 0)
        def _():
            p_ref[...] = jnp.zeros_like(p_ref)

        for k in range(M_CONV + 1):
            p_ref[k:k + 1, :] += rows[k]

    blk = pl.BlockSpec((L, tc), lambda j, b: (b, j))
    return pl.pallas_call(
        body, name=name, out_shape=(jax.ShapeDtypeStruct((T, W), BF16), jax.ShapeDtypeStruct((8, W), F32)),
        grid=(W // tc, Bl),
        in_specs=[blk, pl.BlockSpec((M_CONV, tc), lambda j, b: (0, j)), pl.BlockSpec((1, tc), lambda j, b: (0, j)), blk],
        out_specs=(blk, pl.BlockSpec((8, tc), lambda j, b: (0, j))),
        compiler_params=_params(("parallel", "arbitrary")),
    )(xbc, cw, cb.reshape(1, W), dy)


def _hgrn_chunk(st, qr, fr, v, gr, lb, gn):
    C = qr.shape[0]
    q = _silu(qr)
    logf = jnp.log(lb + (1.0 - lb) * jax.nn.sigmoid(fr))
    k = (1.0 - lb) * jax.nn.sigmoid(-fr)
    b = jnp.dot(_tri(C), logf, precision=HIGHEST, preferred_element_type=F32)
    o = _dot_nt(q * jnp.exp(b), st)
    keep = lax.broadcasted_iota(jnp.int32, (HGRN_SUB, HGRN_SUB), 0) >= lax.broadcasted_iota(jnp.int32, (HGRN_SUB, HGRN_SUB), 1)
    cols = lax.broadcasted_iota(jnp.int32, (HGRN_SUB, C), 1)
    parts = []
    for blk in range(C // HGRN_SUB):
        r0 = blk * HGRN_SUB
        bi, qi, ki, vi = b[r0:r0 + HGRN_SUB], q[r0:r0 + HGRN_SUB], k[r0:r0 + HGRN_SUB], v[r0:r0 + HGRN_SUB]
        decay = jnp.exp(jnp.minimum(bi[:, None, :] - bi[None, :, :], 0.0))
        a_diag = jnp.where(keep, jnp.sum(qi[:, None, :] * ki[None, :, :] * decay, axis=-1), 0.0)
        oi = _dot_nn(a_diag, vi)
        if blk > 0:
            bs = b[r0 - 1:r0]
            a_off = _dot_nt(qi * jnp.exp(bi - bs), k * jnp.exp(jnp.minimum(bs - b, 0.0)))
            oi = oi + _dot_nn(jnp.where(cols < r0, a_off, 0.0), v)
        parts.append(oi)
    o = o + jnp.concatenate(parts, axis=0)
    bl = b[C - 1:C]
    st1 = st * jnp.exp(bl) + _dot_tn(v, k * jnp.exp(bl - b))
    og = o * lax.rsqrt(jnp.mean(o * o, axis=-1, keepdims=True) + NORM_EPS) * gn * _silu(gr)
    return st1, og


def _hgrn_specs(Bl, L, H, order):
    def at(off):
        if order == "bh":
            return pl.BlockSpec((L, HGRN_HEAD), lambda b, h: (b, h + off))
        return pl.BlockSpec((L, HGRN_HEAD), lambda h, b: (b, h + off))
    return [at(0), at(H), at(2 * H), at(3 * H)]


def _hgrn_fwd(P, lb, gn, Bl, L, *, name, jobs=()):
    T, D4 = P.shape
    D = D4 // 4
    H = D // HGRN_HEAD
    C = HGRN_CHUNK
    NC = L // C

    def body(q_ref, f_ref, v_ref, g_ref, lb_ref, gn_ref, og_ref, st_ref):
        lbv, gnv = lb_ref[...], gn_ref[...]

        def step(c, st):
            r = pl.ds(pl.multiple_of(c * C, C), C)
            st_ref[c] = st
            st1, og = _hgrn_chunk(st, q_ref[r, :], f_ref[r, :], v_ref[r, :], g_ref[r, :], lbv, gnv)
            og_ref[r, :] = og.astype(og_ref.dtype)
            return st1

        lax.fori_loop(0, NC, step, jnp.zeros((HGRN_HEAD, HGRN_HEAD), F32))

    return _call(
        body, name=name,
        out_shape=(jax.ShapeDtypeStruct((T, D), BF16), jax.ShapeDtypeStruct((Bl, H, NC, HGRN_HEAD, HGRN_HEAD), F32)),
        grid=(Bl, H),
        in_specs=_hgrn_specs(Bl, L, H, "bh") + [pl.BlockSpec((1, HGRN_HEAD), lambda b, h: (0, h)), pl.BlockSpec((1, HGRN_HEAD), lambda b, h: (0, 0))],
        out_specs=(pl.BlockSpec((L, HGRN_HEAD), lambda b, h: (b, h)),
                   pl.BlockSpec((None, None, NC, HGRN_HEAD, HGRN_HEAD), lambda b, h: (b, h, 0, 0, 0))),
        dims=("parallel", "parallel"), jobs=jobs,
    )(P, P, P, P, lb.reshape(1, D), gn.reshape(1, HGRN_HEAD))


def _hgrn_bwd(P, lb, gn, states, dog, Bl, L, *, name, jobs=()):
    T, D4 = P.shape
    D = D4 // 4
    H = D // HGRN_HEAD
    C = HGRN_CHUNK
    NC = L // C

    def body(q_ref, f_ref, v_ref, g_ref, lb_ref, gn_ref, st_ref, dog_ref, dp_ref, dlb_ref, dgn_ref):
        lbv, gnv = lb_ref[...], gn_ref[...]

        @pl.when(pl.program_id(1) == 0)
        def _():
            dlb_ref[...] = jnp.zeros_like(dlb_ref)

        @pl.when((pl.program_id(0) == 0) & (pl.program_id(1) == 0))
        def _():
            dgn_ref[...] = jnp.zeros_like(dgn_ref)

        def step(i, dst):
            c = NC - 1 - i
            r = pl.ds(pl.multiple_of(c * C, C), C)
            _, vjp = jax.vjp(_hgrn_chunk, st_ref[c], q_ref[r, :], f_ref[r, :], v_ref[r, :], g_ref[r, :], lbv, gnv)
            dst0, dq, df, dv, dg, dlb, dgn = vjp((dst, dog_ref[r, :].astype(F32)))
            dp_ref[0, r, :] = dq.astype(dp_ref.dtype)
            dp_ref[1, r, :] = df.astype(dp_ref.dtype)
            dp_ref[2, r, :] = dv.astype(dp_ref.dtype)
            dp_ref[3, r, :] = dg.astype(dp_ref.dtype)
            dlb_ref[...] += dlb
            dgn_ref[...] += dgn
            return dst0

        lax.fori_loop(0, NC, step, jnp.zeros((HGRN_HEAD, HGRN_HEAD), F32))

    return _call(
        body, name=name,
        out_shape=(jax.ShapeDtypeStruct((4, T, D), BF16), jax.ShapeDtypeStruct((1, D), F32), jax.ShapeDtypeStruct((1, HGRN_HEAD), F32)),
        grid=(H, Bl),
        in_specs=_hgrn_specs(Bl, L, H, "hb") + [
            pl.BlockSpec((1, HGRN_HEAD), lambda h, b: (0, h)), pl.BlockSpec((1, HGRN_HEAD), lambda h, b: (0, 0)),
            pl.BlockSpec((None, None, NC, HGRN_HEAD, HGRN_HEAD), lambda h, b: (b, h, 0, 0, 0)),
            pl.BlockSpec((L, HGRN_HEAD), lambda h, b: (b, h))],
        out_specs=(pl.BlockSpec((4, L, HGRN_HEAD), lambda h, b: (0, b, h)),
                   pl.BlockSpec((1, HGRN_HEAD), lambda h, b: (0, h)), pl.BlockSpec((1, HGRN_HEAD), lambda h, b: (0, 0))),
        dims=("arbitrary", "arbitrary"), jobs=jobs,
    )(P, P, P, P, lb.reshape(1, D), gn.reshape(1, HGRN_HEAD), states, dog)


def _lower_bounds(logits):
    p = jax.nn.softmax(logits, axis=0)
    rows, run = [], jnp.zeros_like(p[0:1])
    for i in range(logits.shape[0]):
        run = run + p[i:i + 1]
        rows.append(run - p[0:1])
    return jnp.concatenate(rows, axis=0) if len(rows) > 1 else rows[0]


def _lb_fwd(logits, *, name):
    def body(x_ref, o_ref):
        o_ref[...] = _lower_bounds(x_ref[...])
    return pl.pallas_call(body, name=name, out_shape=jax.ShapeDtypeStruct(logits.shape, F32))(logits)


def _lb_bwd(logits, dlb, *, name):
    def body(x_ref, g_ref, o_ref):
        _, vjp = jax.vjp(_lower_bounds, x_ref[...])
        o_ref[...] = vjp(g_ref[...])[0]
    return pl.pallas_call(body, name=name, out_shape=jax.ShapeDtypeStruct(logits.shape, F32))(logits, dlb)


def _ssd_pre(dtc_raw, dtr_raw, bias_c, bias_r, alog_c, alog_r):
    C = dtc_raw.shape[0]
    dt_c = jax.nn.softplus(dtc_raw + bias_c)
    dt_r = jax.nn.softplus(dtr_raw + bias_r)
    acs_c = jnp.dot(_tri(C), dt_c * -jnp.exp(alog_c), precision=HIGHEST, preferred_element_type=F32)
    acs_r = lax.dot_general(dt_r * -jnp.exp(alog_r), _tri(C), (((1,), (1,)), ((), ())), precision=HIGHEST, preferred_element_type=F32)
    return dt_c, acs_c, acs_r


def _ssd_head(h0, xs, G, Bm, Cm, dt, acs, acs_row, dskip):
    C = xs.shape[0]
    causal = lax.broadcasted_iota(jnp.int32, (C, C), 0) >= lax.broadcasted_iota(jnp.int32, (C, C), 1)
    Lm = jnp.where(causal, jnp.exp(jnp.minimum(acs - acs_row, 0.0)), 0.0)
    X = xs * dt
    y = _dot_nn(G * Lm, X) + _dot_nt(Cm * jnp.exp(acs), h0) + dskip * xs
    alast = acs[C - 1:C]
    h1 = jnp.exp(alast) * h0 + _dot_tn(X, Bm * jnp.exp(alast - acs))
    return y, h1


def _ssd_layout(Bl, L, d_inner, order):
    G = M_GROUPS
    hpg = d_inner // M_HEADDIM // G
    gw = hpg * M_HEADDIM
    ix = (lambda b, g: (b, g)) if order == "bg" else (lambda g, b: (b, g))
    def cols(width, off):
        if order == "bg":
            return pl.BlockSpec((L, width), lambda b, g: (b, g + off))
        return pl.BlockSpec((L, width), lambda g, b: (b, g + off))
    def per_group(shape):
        n = len(shape)
        if order == "bg":
            return pl.BlockSpec((None,) + shape, lambda b, g: (g,) + (0,) * n)
        return pl.BlockSpec((None,) + shape, lambda g, b: (g,) + (0,) * n)
    if order == "bg":
        dtc = pl.BlockSpec((None, L, LANES), lambda b, g: (g, b, 0))
        dtr = pl.BlockSpec((None, 8, L), lambda b, g: (g, 0, b))
    else:
        dtc = pl.BlockSpec((None, L, LANES), lambda g, b: (g, b, 0))
        dtr = pl.BlockSpec((None, 8, L), lambda g, b: (g, 0, b))
    nb = d_inner // M_D_STATE
    return hpg, gw, cols, per_group, dtc, dtr, nb


def _ssd_fwd(xa, dtc, dtr, prm, Bl, L, d_inner, *, name, jobs=()):
    T = xa.shape[0]
    C = M_CHUNK
    NC = L // C
    G = M_GROUPS
    hpg, gw, cols, per_group, dtc_spec, dtr_spec, nb = _ssd_layout(Bl, L, d_inner, "bg")

    def body(xs_ref, b_ref, c_ref, dtc_ref, dtr_ref, bc_ref, br_ref, ac_ref, ar_ref, dk_ref, y_ref, hs_ref,
             dt_s, acs_s, acr_s, h_s):
        h_s[...] = jnp.zeros_like(h_s)

        def step(c, carry):
            r = pl.ds(pl.multiple_of(c * C, C), C)
            dt_c, acs_c, acs_r = _ssd_pre(dtc_ref[r, :], dtr_ref[:, r], bc_ref[...], br_ref[...], ac_ref[...], ar_ref[...])
            dt_s[...] = dt_c
            acs_s[...] = acs_c
            acr_s[...] = acs_r
            Bm, Cm = b_ref[r, :], c_ref[r, :]
            Gm = _dot_nt(Cm, Bm)
            for j in range(hpg):
                pc = slice(j * M_HEADDIM, (j + 1) * M_HEADDIM)
                h0 = h_s[j]
                hs_ref[c, j] = h0
                y, h1 = _ssd_head(h0, xs_ref[r, pc], Gm, Bm, Cm, dt_s[:, j:j + 1], acs_s[:, j:j + 1], acr_s[j:j + 1, :],
                                  dk_ref[0:1, j:j + 1])
                y_ref[r, pc] = y
                h_s[j] = h1
            return carry

        lax.fori_loop(0, NC, step, 0)

    return _call(
        body, name=name,
        out_shape=(jax.ShapeDtypeStruct((T, d_inner), F32), jax.ShapeDtypeStruct((Bl, G, NC, hpg, M_HEADDIM, M_D_STATE), F32)),
        grid=(Bl, G),
        in_specs=[cols(gw, 0), cols(M_D_STATE, nb), cols(M_D_STATE, nb + G), dtc_spec, dtr_spec,
                  per_group((1, LANES)), per_group((8, 1)), per_group((1, LANES)), per_group((8, 1)), per_group((1, LANES))],
        out_specs=(cols(gw, 0), pl.BlockSpec((None, None, NC, hpg, M_HEADDIM, M_D_STATE), lambda b, g: (b, g, 0, 0, 0, 0))),
        scratch_shapes=[pltpu.VMEM((C, LANES), F32), pltpu.VMEM((C, LANES), F32), pltpu.VMEM((8, C), F32),
                        pltpu.VMEM((hpg, M_HEADDIM, M_D_STATE), F32)],
        dims=("parallel", "parallel"), jobs=jobs,
    )(xa, xa, xa, dtc, dtr, *prm)


def _ssd_bwd(xa, dtc, dtr, prm, states, dy, Bl, L, d_inner, *, name, jobs=()):
    T = xa.shape[0]
    C = M_CHUNK
    NC = L // C
    G = M_GROUPS
    hpg, gw, cols, per_group, dtc_spec, dtr_spec, nb = _ssd_layout(Bl, L, d_inner, "gb")

    def body(xs_ref, b_ref, c_ref, dtc_ref, dtr_ref, bc_ref, br_ref, ac_ref, ar_ref, dk_ref, hs_ref, dy_ref,
             dxs_ref, db_ref, dc_ref, ddtc_ref, ddtr_ref, gbc_ref, gbr_ref, gac_ref, gar_ref, gdk_ref,
             dt_s, acs_s, acr_s, ddt_s, dacs_s, dacr_s, dh_s):
        dh_s[...] = jnp.zeros_like(dh_s)

        @pl.when(pl.program_id(1) == 0)
        def _():
            for ref in (gbc_ref, gbr_ref, gac_ref, gar_ref, gdk_ref):
                ref[...] = jnp.zeros_like(ref)

        def step(i, carry):
            c = NC - 1 - i
            r = pl.ds(pl.multiple_of(c * C, C), C)
            pre_in = (dtc_ref[r, :], dtr_ref[:, r], bc_ref[...], br_ref[...], ac_ref[...], ar_ref[...])
            (dt_c, acs_c, acs_r), pre_vjp = jax.vjp(_ssd_pre, *pre_in)
            dt_s[...] = dt_c
            acs_s[...] = acs_c
            acr_s[...] = acs_r
            ddt_s[...] = jnp.zeros_like(ddt_s)
            dacs_s[...] = jnp.zeros_like(dacs_s)
            dacr_s[...] = jnp.zeros_like(dacr_s)
            Bm, Cm = b_ref[r, :], c_ref[r, :]
            Gm = _dot_nt(Cm, Bm)
            dG = jnp.zeros((C, C), F32)
            dB = jnp.zeros((C, M_D_STATE), F32)
            dC = jnp.zeros((C, M_D_STATE), F32)
            for j in range(hpg):
                pc = slice(j * M_HEADDIM, (j + 1) * M_HEADDIM)
                one = slice(j, j + 1)
                _, vjp = jax.vjp(_ssd_head, hs_ref[c, j], xs_ref[r, pc], Gm, Bm, Cm, dt_s[:, one], acs_s[:, one], acr_s[one, :],
                                 dk_ref[0:1, one])
                dh0, dxs, dGj, dBj, dCj, ddt, dacs, dacr, ddk = vjp((dy_ref[r, pc], dh_s[j]))
                dh_s[j] = dh0
                dxs_ref[r, pc] = dxs
                dG, dB, dC = dG + dGj, dB + dBj, dC + dCj
                ddt_s[:, one] = ddt
                dacs_s[:, one] = dacs
                dacr_s[one, :] = dacr
                gdk_ref[0:1, one] += ddk
            db_ref[r, :] = dB + _dot_tn(dG, Cm)
            dc_ref[r, :] = dC + _dot_nn(dG, Bm)
            ddtc, ddtr, gbc, gbr, gac, gar = pre_vjp((ddt_s[...], dacs_s[...], dacr_s[...]))
            ddtc_ref[r, :] = ddtc
            ddtr_ref[:, r] = ddtr
            gbc_ref[...] += gbc
            gbr_ref[...] += gbr
            gac_ref[...] += gac
            gar_ref[...] += gar
            return carry

        lax.fori_loop(0, NC, step, 0)

    GN = G * M_D_STATE
    bc_cols = lambda off: pl.BlockSpec((L, M_D_STATE), lambda g, b: (b, g + off))
    outs = _call(
        body, name=name,
        out_shape=(jax.ShapeDtypeStruct((T, d_inner), F32), jax.ShapeDtypeStruct((T, GN), F32), jax.ShapeDtypeStruct((T, GN), F32),
                   jax.ShapeDtypeStruct(dtc.shape, F32), jax.ShapeDtypeStruct(dtr.shape, F32),
                   jax.ShapeDtypeStruct((G, 1, LANES), F32), jax.ShapeDtypeStruct((G, 8, 1), F32),
                   jax.ShapeDtypeStruct((G, 1, LANES), F32), jax.ShapeDtypeStruct((G, 8, 1), F32),
                   jax.ShapeDtypeStruct((G, 1, LANES), F32)),
        grid=(G, Bl),
        in_specs=[cols(gw, 0), cols(M_D_STATE, nb), cols(M_D_STATE, nb + G), dtc_spec, dtr_spec,
                  per_group((1, LANES)), per_group((8, 1)), per_group((1, LANES)), per_group((8, 1)), per_group((1, LANES)),
                  pl.BlockSpec((None, None, NC, hpg, M_HEADDIM, M_D_STATE), lambda g, b: (b, g, 0, 0, 0, 0)), cols(gw, 0)],
        out_specs=(cols(gw, 0), bc_cols(0), bc_cols(0), dtc_spec, dtr_spec,
                   per_group((1, LANES)), per_group((8, 1)), per_group((1, LANES)), per_group((8, 1)), per_group((1, LANES))),
        scratch_shapes=[pltpu.VMEM((C, LANES), F32), pltpu.VMEM((C, LANES), F32), pltpu.VMEM((8, C), F32),
                        pltpu.VMEM((C, LANES), F32), pltpu.VMEM((C, LANES), F32), pltpu.VMEM((8, C), F32),
                        pltpu.VMEM((hpg, M_HEADDIM, M_D_STATE), F32)],
        dims=("arbitrary", "arbitrary"), jobs=jobs,
    )(xa, xa, xa, dtc, dtr, *prm, states, dy)
    return outs


def _mgate(y, z, w):
    t = y * _silu(z)
    return t * lax.rsqrt(jnp.mean(t * t, axis=-1, keepdims=True) + NORM_EPS) * w


def _mgate_fwd(y, z, w, *, name):
    T, d_inner = y.shape
    gw = d_inner // M_GROUPS
    tr = _tile(T, 512, 8)

    def body(y_ref, z_ref, w_ref, o_ref):
        o_ref[...] = _mgate(y_ref[...], z_ref[...], w_ref[...]).astype(o_ref.dtype)

    blk = pl.BlockSpec((tr, gw), lambda i, g: (i, g))
    return pl.pallas_call(
        body, name=name, out_shape=jax.ShapeDtypeStruct((T, d_inner), BF16), grid=(T // tr, M_GROUPS),
        in_specs=[blk, blk, pl.BlockSpec((1, gw), lambda i, g: (0, g))], out_specs=blk,
        compiler_params=_params(("parallel", "parallel")),
    )(y, z, w.reshape(1, d_inner))


def _mgate_bwd(y, z, w, dyg, *, name):
    T, d_inner = y.shape
    gw = d_inner // M_GROUPS
    tr = _tile(T, 512, 8)

    def body(y_ref, z_ref, w_ref, g_ref, dy_ref, dz_ref, dw_ref):
        _, vjp = jax.vjp(_mgate, y_ref[...], z_ref[...], w_ref[...])
        dy, dz, dw = vjp(g_ref[...].astype(F32))
        dy_ref[...] = dy
        dz_ref[...] = dz.astype(dz_ref.dtype)

        @pl.when(pl.program_id(1) == 0)
        def _():
            dw_ref[...] = jnp.zeros_like(dw_ref)

        dw_ref[...] += dw

    blk = pl.BlockSpec((tr, gw), lambda g, i: (i, g))
    vec = pl.BlockSpec((1, gw), lambda g, i: (0, g))
    return pl.pallas_call(
        body, name=name,
        out_shape=(jax.ShapeDtypeStruct((T, d_inner), F32), jax.ShapeDtypeStruct((T, d_inner), BF16), jax.ShapeDtypeStruct((1, d_inner), F32)),
        grid=(M_GROUPS, T // tr), in_specs=[blk, blk, vec, blk], out_specs=(blk, blk, vec),
        compiler_params=_params(("parallel", "arbitrary")),
    )(y, z, w.reshape(1, d_inner), dyg)


def _adamw_math(w, g, m, v):
    m1 = ADAM_B1 * m + (1.0 - ADAM_B1) * g
    v1 = ADAM_B2 * v + (1.0 - ADAM_B2) * (g * g)
    m_hat = m1 / (1.0 - ADAM_B1 ** ADAM_STEP)
    v_hat = v1 / (1.0 - ADAM_B2 ** ADAM_STEP)
    delta = -ADAM_LR * (m_hat / (jnp.sqrt(v_hat) + ADAM_EPS) + ADAM_WD * w)
    return delta, m1, v1


def _adamw_sharded(land, w, m, v, *, name, jobs=()):
    R, n = w.shape
    tr = _tile(R, 128, 8)

    def body(l_ref, w_ref, m_ref, v_ref, g_ref, d_ref, m1_ref, v1_ref):
        g = l_ref[0].astype(F32)
        for k in range(1, N_DEV):
            g = g + l_ref[k].astype(F32)
        d, m1, v1 = _adamw_math(w_ref[...], g, m_ref[...], v_ref[...])
        g_ref[...] = g
        d_ref[...] = d
        m1_ref[...] = m1
        v1_ref[...] = v1

    blk = pl.BlockSpec((tr, n), lambda i: (i, 0))
    out = jax.ShapeDtypeStruct((R, n), F32)
    return _call(
        body, name=name, out_shape=(out, out, out, out), grid=(R // tr,),
        in_specs=[pl.BlockSpec((N_DEV, tr, n), lambda i: (0, i, 0)), blk, blk, blk], out_specs=(blk, blk, blk, blk),
        dims=("parallel",), jobs=jobs,
    )(land, w, m, v)


def _sum_devices(parts, *, name):
    _, R, n = parts.shape
    tr = _tile(R, 512, 8)

    def body(p_ref, o_ref):
        g = p_ref[0]
        for k in range(1, N_DEV):
            g = g + p_ref[k]
        o_ref[...] = g

    return pl.pallas_call(
        body, name=name, out_shape=jax.ShapeDtypeStruct((R, n), F32), grid=(R // tr,),
        in_specs=[pl.BlockSpec((N_DEV, tr, n), lambda i: (0, i, 0))], out_specs=pl.BlockSpec((tr, n), lambda i: (i, 0)),
        compiler_params=_params(("parallel",)),
    )(parts)


def _adamw_packed(g, w, m, v, *, name):
    R, n = w.shape
    tr = _tile(R, 512, 8)

    def body(g_ref, w_ref, m_ref, v_ref, d_ref, m1_ref, v1_ref):
        d, m1, v1 = _adamw_math(w_ref[...], g_ref[...], m_ref[...], v_ref[...])
        d_ref[...] = d
        m1_ref[...] = m1
        v1_ref[...] = v1

    blk = pl.BlockSpec((tr, n), lambda i: (i, 0))
    out = jax.ShapeDtypeStruct((R, n), F32)
    return pl.pallas_call(
        body, name=name, out_shape=(out, out, out), grid=(R // tr,), in_specs=[blk] * 4, out_specs=(blk, blk, blk),
        compiler_params=_params(("parallel",)),
    )(g, w, m, v)


PACK_ALIGN = 8 * LANES


def _pack(arrs):
    parts = []
    for a in arrs:
        flat = a.reshape(-1).astype(F32)
        pad = (-flat.shape[0]) % PACK_ALIGN
        parts.append(jnp.pad(flat, (0, pad)).reshape(-1, LANES))
    return jnp.concatenate(parts, axis=0)


def _unpack(packed, shapes, lead=()):
    out, r = [], 0
    nl = len(lead)
    for s in shapes:
        n = math.prod(s)
        rows = (n + PACK_ALIGN - 1) // PACK_ALIGN * (PACK_ALIGN // LANES)
        part = lax.slice_in_dim(packed, r, r + rows, axis=nl)
        out.append(part.reshape(lead + (rows * LANES,))[..., :n].reshape(lead + tuple(s)))
        r += rows
    return out


EXCHANGE_BYTES_PER_US = 80e3
EXCHANGE_ROW_ALIGN = 128
CARRY_FILL = 0.9

CALL_US = {
    "hgrn_in_fwd": 247, "hgrn_scan_fwd": 884, "hgrn_out_fwd": 67, "ffn_up_fwd": 314, "ffn_gate_fwd": 94, "ffn_down_fwd": 152,
    "mamba_z_fwd": 126, "mamba_xbc_fwd": 186, "mamba_scan_fwd": 487, "mamba_out_fwd": 115,
    "ffn_down_bwd_x": 259, "ffn_down_bwd_w": 268, "ffn_gate_bwd": 240, "ffn_up_bwd_x": 292, "ffn_up_bwd_w": 300,
    "hgrn_out_bwd_x": 64, "hgrn_out_bwd_w": 61, "hgrn_scan_bwd": 2093, "hgrn_in_bwd_x": 222, "hgrn_in_bwd_w": 216,
    "mamba_out_bwd_x": 126, "mamba_out_bwd_w": 118, "mamba_scan_bwd": 1518, "mamba_z_bwd_x": 126, "mamba_xbc_bwd_x": 169,
    "mamba_z_bwd_w": 110, "mamba_xbc_bwd_w": 164,
    "adamw_hgrn_w_in": 65, "adamw_hgrn_w_out": 16, "adamw_m_w_in": 87, "adamw_m_w_out": 32, "adamw_f_w_up": 168,
    "adamw_f_w_down": 109,
}


class _Plan:
    def __init__(self):
        self.queue = []

    def push(self, src, mode, buf, q0=0):
        rows, n = src.shape[-2], src.shape[-1]
        us_per_row = (N_DEV - 1) * n * src.dtype.itemsize / EXCHANGE_BYTES_PER_US
        self.queue.append(dict(src=src, mode=mode, buf=buf, r0=0, r1=rows, q0=q0, us_per_row=us_per_row))

    def take(self, name):
        budget, jobs, used = CARRY_FILL * CALL_US[name], [], set()
        while self.queue:
            it = self.queue[0]
            left = it["r1"] - it["r0"]
            fit = int(budget / it["us_per_row"])
            rows = left if fit >= left else fit // EXCHANGE_ROW_ALIGN * EXCHANGE_ROW_ALIGN
            if rows <= 0 or id(it["buf"]) in used:
                break
            jobs.append(_Job(it["src"], it["mode"], it["buf"], it["r0"], it["r0"] + rows, it["q0"] + it["r0"]))
            used.add(id(it["buf"]))
            budget -= rows * it["us_per_row"]
            it["r0"] += rows
            if it["r0"] == it["r1"]:
                self.queue.pop(0)
            else:
                break
        return jobs

    def finish(self, buf, *, name):
        mine = [it for it in self.queue if it["buf"] is buf]
        self.queue = [it for it in self.queue if it["buf"] is not buf]
        for i, it in enumerate(mine):
            _exchange([_Job(it["src"], it["mode"], buf, it["r0"], it["r1"], it["q0"] + it["r0"])], name=f"{name}_{i}" if i else name)
        return buf.arr


def _pad_to(a, axis, size):
    pad = [(0, 0)] * a.ndim
    pad[axis] = (0, size - a.shape[axis])
    return jnp.pad(a, pad)


def _ssd_operands(dt_raw, dt_bias, a_log, dskip):
    T, heads = dt_raw.shape
    G = M_GROUPS
    hpg = heads // G
    t3 = dt_raw.reshape(T, G, hpg)
    dtc = _pad_to(t3.transpose(1, 0, 2), 2, LANES)
    dtr = _pad_to(t3.transpose(1, 2, 0), 1, 8)
    col = lambda p: _pad_to(p.reshape(G, 1, hpg), 2, LANES)
    row = lambda p: _pad_to(p.reshape(G, hpg, 1), 1, 8)
    return dtc, dtr, (col(dt_bias), row(dt_bias), col(a_log), row(a_log), col(dskip))


def _ssd_unpack(ddtc, ddtr, gbc, gbr, gac, gar, gdk, heads):
    G = M_GROUPS
    hpg = heads // G
    T = ddtc.shape[1]
    ddt = ddtc[:, :, :hpg].transpose(1, 0, 2).reshape(T, heads) + ddtr[:, :hpg, :].transpose(2, 0, 1).reshape(T, heads)
    both = lambda c, r: c[:, 0, :hpg].reshape(heads) + r[:, :hpg, 0].reshape(heads)
    return ddt, both(gbc, gbr), both(gac, gar), gdk[:, 0, :hpg].reshape(heads)


class _LazyWeights:
    def __init__(self, name, shape, fetch):
        self.name, self.shape, self.fetch, self.got = name, shape, fetch, {}

    def __getitem__(self, layer):
        if layer not in self.got:
            self.got[layer] = self.fetch(self.name, layer)
        return self.got[layer]


class _GradSink:
    def __init__(self, name, layers, push):
        self.name, self.shape, self.push = name, (layers,), push

    def __setitem__(self, layer, g):
        self.push(self.name, layer, g)


def _local_step(x, target, W, plan, push):
    Bl, L, D = x.shape
    T = Bl * L
    depth = W["mix_norm"].shape[0]
    d_inner = W["m_w_out"].shape[1]
    heads = d_inner // M_HEADDIM
    conv_dim = W["m_conv_w"].shape[2]

    def mm(a, b, mode, out_dtype, *, name, add=None):
        return _matmul(a, b, mode, out_dtype, name=name, add=add, jobs=plan.take(name) if name in CALL_US else ())

    lb = _lb_fwd(W["hgrn_lb_logits"], name="hgrn_lower_bounds")
    h = x.reshape(T, D)
    saved = []
    for i in range(depth):
        j = i // 2
        s = {"h": h}
        u = _rms_fwd(h, W["mix_norm"][i], name="mix_norm_fwd")
        s["u"] = u
        if i % 2 == 0:
            P = mm(u, W["hgrn_w_in"][j], "nn", F32, name="hgrn_in_fwd")
            og, st = _hgrn_fwd(P, lb[j], W["hgrn_gnorm"][j], Bl, L, name="hgrn_scan_fwd", jobs=plan.take("hgrn_scan_fwd"))
            h = mm(og, W["hgrn_w_out"][j], "nn", F32, add=h, name="hgrn_out_fwd")
            s.update(P=P, og=og, st=st)
        else:
            w_in = W["m_w_in"][j]
            z = mm(u, w_in[:, :d_inner], "nn", F32, name="mamba_z_fwd")
            xbc = mm(u, w_in[:, d_inner:d_inner + conv_dim], "nn", F32, name="mamba_xbc_fwd")
            dt_raw = mm(u, w_in[:, d_inner + conv_dim:], "nn", F32, name="mamba_dt_fwd")
            xa = _mconv_fwd(xbc, W["m_conv_w"][j], W["m_conv_b"][j], Bl, L, name="mamba_conv_fwd")
            dtc, dtr, prm = _ssd_operands(dt_raw, W["m_dt_bias"][j], W["m_A_log"][j], W["m_D"][j])
            y, hs = _ssd_fwd(xa, dtc, dtr, prm, Bl, L, d_inner, name="mamba_scan_fwd", jobs=plan.take("mamba_scan_fwd"))
            yg = _mgate_fwd(y, z, W["m_norm"][j], name="mamba_gate_fwd")
            h = mm(yg, W["m_w_out"][j], "nn", F32, add=h, name="mamba_out_fwd")
            s.update(z=z, xbc=xbc, xa=xa, dtc=dtc, dtr=dtr, prm=prm, y=y, hs=hs, yg=yg)
        s["h2"] = h
        u2 = _rms_fwd(h, W["ffn_norm"][i], name="ffn_norm_fwd")
        a = mm(u2, W["f_w_up"][i], "nn", BF16, name="ffn_up_fwd")
        yf = _ffn_gate_fwd(a, W["f_conv_w"][i], W["f_conv_b"][i], Bl, L, name="ffn_gate_fwd", jobs=plan.take("ffn_gate_fwd"))
        h = mm(yf, W["f_w_down"][i], "nn", F32, add=h, name="ffn_down_fwd")
        s.update(u2=u2, a=a, yf=yf)
        saved.append(s)

    loss, dh, d_final = _loss_head(h, W["final_norm"], target.reshape(T, D), name="loss_head")

    g = {k: (_GradSink(k, W[k].shape[0], push) if isinstance(W[k], _LazyWeights) else [None] * W[k].shape[0])
         for k in W if k != "final_norm"}
    g["final_norm"] = d_final.reshape(D)
    dlb = [None] * W["hgrn_lb_logits"].shape[0]
    for i in reversed(range(depth)):
        j = i // 2
        s = saved[i]
        g["f_w_down"][i] = mm(s["yf"], dh, "tn", BF16, name="ffn_down_bwd_w")
        dyf = mm(dh, W["f_w_down"][i], "nt", BF16, name="ffn_down_bwd_x")
        dag, dau, pf = _ffn_gate_bwd(s["a"], W["f_conv_w"][i], W["f_conv_b"][i], dyf, Bl, L, name="ffn_gate_bwd",
                                     jobs=plan.take("ffn_gate_bwd"))
        da = jnp.concatenate([dag, dau], axis=1)
        g["f_w_up"][i] = mm(s["u2"], da, "tn", BF16, name="ffn_up_bwd_w")
        du2 = mm(da, W["f_w_up"][i], "nt", F32, name="ffn_up_bwd_x")
        g["f_conv_w"][i], g["f_conv_b"][i] = pf[:FFN_CONV], pf[FFN_CONV]
        dh, dn = _rms_bwd(s["h2"], W["ffn_norm"][i], du2, dh, name="ffn_norm_bwd")
        g["ffn_norm"][i] = dn.reshape(D)
        if i % 2 == 0:
            g["hgrn_w_out"][j] = mm(s["og"], dh, "tn", BF16, name="hgrn_out_bwd_w")
            dog = mm(dh, W["hgrn_w_out"][j], "nt", BF16, name="hgrn_out_bwd_x")
            dP4, dlb_j, dgn = _hgrn_bwd(s["P"], lb[j], W["hgrn_gnorm"][j], s["st"], dog, Bl, L, name="hgrn_scan_bwd",
                                        jobs=plan.take("hgrn_scan_bwd"))
            dP = dP4.transpose(1, 0, 2).reshape(T, 4 * D)
            g["hgrn_w_in"][j] = mm(s["u"], dP, "tn", BF16, name="hgrn_in_bwd_w")
            du = mm(dP, W["hgrn_w_in"][j], "nt", F32, name="hgrn_in_bwd_x")
            dlb[j] = dlb_j
            g["hgrn_gnorm"][j] = dgn.reshape(HGRN_HEAD)
        else:
            w_in = W["m_w_in"][j]
            g["m_w_out"][j] = mm(s["yg"], dh, "tn", BF16, name="mamba_out_bwd_w")
            dyg = mm(dh, W["m_w_out"][j], "nt", BF16, name="mamba_out_bwd_x")
            dy, dz, dnw = _mgate_bwd(s["y"], s["z"], W["m_norm"][j], dyg, name="mamba_gate_bwd")
            dxs, dB, dC, ddtc, ddtr, gbc, gbr, gac, gar, gdk = _ssd_bwd(
                s["xa"], s["dtc"], s["dtr"], s["prm"], s["hs"], dy, Bl, L, d_inner, name="mamba_scan_bwd",
                jobs=plan.take("mamba_scan_bwd"))
            ddt, g["m_dt_bias"][j], g["m_A_log"][j], g["m_D"][j] = _ssd_unpack(ddtc, ddtr, gbc, gbr, gac, gar, gdk, heads)
            dxa = jnp.concatenate([dxs, dB, dC], axis=1)
            dxbc, pm = _mconv_bwd(s["xbc"], W["m_conv_w"][j], W["m_conv_b"][j], dxa, Bl, L, name="mamba_conv_bwd")
            g["m_conv_w"][j], g["m_conv_b"][j], g["m_norm"][j] = pm[:M_CONV], pm[M_CONV], dnw.reshape(d_inner)
            ddt = ddt.astype(BF16)
            g["m_w_in"][j] = jnp.concatenate([
                mm(s["u"], dz, "tn", BF16, name="mamba_z_bwd_w"), mm(s["u"], dxbc, "tn", BF16, name="mamba_xbc_bwd_w"),
                mm(s["u"], ddt, "tn", BF16, name="mamba_dt_bwd_w")], axis=1)
            du = mm(dz, w_in[:, :d_inner], "nt", F32, name="mamba_z_bwd_x")
            du = mm(dxbc, w_in[:, d_inner:d_inner + conv_dim], "nt", F32, add=du, name="mamba_xbc_bwd_x")
            du = mm(ddt, w_in[:, d_inner + conv_dim:], "nt", F32, add=du, name="mamba_dt_bwd_x")
        dh, dn = _rms_bwd(s["h"], W["mix_norm"][i], du, dh, name="mix_norm_bwd")
        g["mix_norm"][i] = dn.reshape(D)
    g["hgrn_lb_logits"] = _lb_bwd(W["hgrn_lb_logits"], jnp.concatenate(dlb, axis=0), name="hgrn_lower_bounds_bwd")
    grads = {k: (jnp.stack(v) if isinstance(v, list) else v) for k, v in g.items() if not isinstance(v, _GradSink)}
    return loss, dh.reshape(Bl, L, D), grads


WEIGHTS = ("mix_norm", "ffn_norm", "final_norm", "hgrn_w_in", "hgrn_lb_logits", "hgrn_gnorm", "hgrn_w_out", "m_w_in",
           "m_conv_w", "m_conv_b", "m_dt_bias", "m_A_log", "m_D", "m_norm", "m_w_out", "f_w_up", "f_conv_w", "f_conv_b",
           "f_w_down")
COL_SHARDED = ("hgrn_w_in", "m_w_in", "f_w_up")
ROW_SHARDED = ("hgrn_w_out", "m_w_out", "f_w_down")
SMALL_SHARDED = ("m_conv_w", "m_conv_b", "m_norm", "f_conv_w")
SMALL = tuple(k for k in WEIGHTS if k not in COL_SHARDED + ROW_SHARDED)


def _join_last(g):
    nd = g.ndim
    t = g.transpose(tuple(range(1, nd - 1)) + (0, nd - 1))
    return t.reshape(t.shape[:-2] + (N_DEV * g.shape[-1],))


def kernel(x, mix_norm, ffn_norm, final_norm, hgrn_w_in, hgrn_lb_logits, hgrn_gnorm, hgrn_w_out, m_w_in, m_conv_w, m_conv_b, m_dt_bias, m_A_log, m_D, m_norm, m_w_out, f_w_up, f_conv_w, f_conv_b, f_w_down, loss_target, m_mix_norm, m_ffn_norm, m_final_norm, m_hgrn_w_in, m_hgrn_lb_logits, m_hgrn_gnorm, m_hgrn_w_out, m_m_w_in, m_m_conv_w, m_m_conv_b, m_m_dt_bias, m_m_A_log, m_m_D, m_m_norm, m_m_w_out, m_f_w_up, m_f_conv_w, m_f_conv_b, m_f_w_down, v_mix_norm, v_ffn_norm, v_final_norm, v_hgrn_w_in, v_hgrn_lb_logits, v_hgrn_gnorm, v_hgrn_w_out, v_m_w_in, v_m_conv_w, v_m_conv_b, v_m_dt_bias, v_m_A_log, v_m_D, v_m_norm, v_m_w_out, v_f_w_up, v_f_conv_w, v_f_conv_b, v_f_w_down):
    w = dict(zip(WEIGHTS, (mix_norm, ffn_norm, final_norm, hgrn_w_in, hgrn_lb_logits, hgrn_gnorm, hgrn_w_out, m_w_in, m_conv_w, m_conv_b, m_dt_bias, m_A_log, m_D, m_norm, m_w_out, f_w_up, f_conv_w, f_conv_b, f_w_down)))
    m = dict(zip(WEIGHTS, (m_mix_norm, m_ffn_norm, m_final_norm, m_hgrn_w_in, m_hgrn_lb_logits, m_hgrn_gnorm, m_hgrn_w_out, m_m_w_in, m_m_conv_w, m_m_conv_b, m_m_dt_bias, m_m_A_log, m_m_D, m_m_norm, m_m_w_out, m_f_w_up, m_f_conv_w, m_f_conv_b, m_f_w_down)))
    v = dict(zip(WEIGHTS, (v_mix_norm, v_ffn_norm, v_final_norm, v_hgrn_w_in, v_hgrn_lb_logits, v_hgrn_gnorm, v_hgrn_w_out, v_m_w_in, v_m_conv_w, v_m_conv_b, v_m_dt_bias, v_m_A_log, v_m_D, v_m_norm, v_m_w_out, v_f_w_up, v_f_conv_w, v_f_conv_b, v_f_w_down)))
    me = 4 * lax.axis_index("x") + 2 * lax.axis_index("y") + lax.axis_index("c")

    def gather_now(a, *, name):
        buf = _Buf(a.shape[0], a.shape[1], a.dtype)
        _exchange([_Job(a, "gather", buf)], name=name)
        return buf.arr

    plan = _Plan()
    depth = mix_norm.shape[0]
    use_order = []
    for i in range(depth):
        use_order += [("hgrn_w_in", i // 2), ("hgrn_w_out", i // 2)] if i % 2 == 0 else [("m_w_in", i // 2), ("m_w_out", i // 2)]
        use_order += [("f_w_up", i), ("f_w_down", i)]
    gathers = {}
    for k, layer in use_order:
        shard = w[k][layer].astype(BF16)
        gathers[k, layer] = _Buf(shard.shape[0], shard.shape[1], BF16)
        plan.push(shard, "gather", gathers[k, layer])

    def fetch(k, layer):
        g = plan.finish(gathers[k, layer], name=f"gather_{k}")
        if k in COL_SHARDED:
            return g.transpose(1, 0, 2).reshape(g.shape[1], N_DEV * g.shape[2])
        return g.reshape(N_DEV * g.shape[1], g.shape[2])

    full = {k: w[k] for k in SMALL if k not in SMALL_SHARDED}
    for k in COL_SHARDED:
        full[k] = _LazyWeights(k, (w[k].shape[0], w[k].shape[1], N_DEV * w[k].shape[2]), fetch)
    for k in ROW_SHARDED:
        full[k] = _LazyWeights(k, (w[k].shape[0], N_DEV * w[k].shape[1], w[k].shape[2]), fetch)
    shard_shapes = [w[k].shape for k in SMALL_SHARDED]
    gathered = gather_now(_pack([w[k] for k in SMALL_SHARDED]), name="gather_small_params")
    for k, g in zip(SMALL_SHARDED, _unpack(gathered, shard_shapes, lead=(N_DEV,))):
        full[k] = _join_last(g)

    lands = {k: _Buf(math.prod(w[k].shape[:-1]), w[k].shape[-1], BF16) for k in COL_SHARDED + ROW_SHARDED}

    def push(k, layer, g):
        if k in COL_SHARDED:
            parts = g.reshape(g.shape[0], N_DEV, g.shape[1] // N_DEV).transpose(1, 0, 2)
        else:
            parts = g.reshape(N_DEV, g.shape[0] // N_DEV, g.shape[1])
        plan.push(parts, "scatter", lands[k], q0=layer * parts.shape[1])

    loss, grad_x, grads = _local_step(x, loss_target, full, plan, push)

    out = {}
    for k in ("m_w_out", "m_w_in", "f_w_down", "f_w_up", "hgrn_w_out", "hgrn_w_in"):
        land = plan.finish(lands[k], name="scatter_" + k)
        rows, n = math.prod(w[k].shape[:-1]), w[k].shape[-1]
        res = _adamw_sharded(land, w[k].reshape(rows, n), m[k].reshape(rows, n), v[k].reshape(rows, n),
                             name="adamw_" + k, jobs=plan.take("adamw_" + k))
        out[k] = tuple(r.reshape(w[k].shape) for r in res)

    small_full_shapes = [(1, 1)] + [grads[k].shape for k in SMALL]
    packed = _pack([loss] + [grads[k] for k in SMALL])
    summed = _sum_devices(gather_now(packed, name="gather_small_grads"), name="sum_small_grads")
    parts = _unpack(summed, small_full_shapes)
    loss_all, small_g = parts[0], dict(zip(SMALL, parts[1:]))
    for k in SMALL_SHARDED:
        n = w[k].shape[-1]
        small_g[k] = lax.dynamic_slice_in_dim(small_g[k], me * n, n, axis=small_g[k].ndim - 1)
    shapes = [w[k].shape for k in SMALL]
    res = _adamw_packed(_pack([small_g[k] for k in SMALL]), _pack([w[k] for k in SMALL]), _pack([m[k] for k in SMALL]),
                        _pack([v[k] for k in SMALL]), name="adamw_small")
    res = [_unpack(r, shapes) for r in res]
    for i, k in enumerate(SMALL):
        out[k] = (small_g[k], res[0][i], res[1][i], res[2][i])

    return (loss_all.reshape(()), grad_x, *[out[k][0] for k in WEIGHTS], *[out[k][1] for k in WEIGHTS],
            *[out[k][2] for k in WEIGHTS], *[out[k][3] for k in WEIGHTS])
```

```python
import functools
import math

import jax
import jax.numpy as jnp
from jax import lax
from jax.experimental import pallas as pl
from jax.experimental.pallas import tpu as pltpu

F32 = jnp.float32
BF16 = jnp.bfloat16
HIGHEST = lax.Precision.HIGHEST

NORM_EPS = 1e-5
ADAM_LR, ADAM_B1, ADAM_B2, ADAM_EPS, ADAM_WD, ADAM_STEP = 0.001, 0.9, 0.999, 1e-08, 0.01, 10

N_DEV = 8
LANES = 128
V7X_VMEM_BYTES = 64 * 1024 * 1024
VMEM_LIMIT = V7X_VMEM_BYTES * 3 // 4

HGRN_HEAD = 128
HGRN_CHUNK = 64
HGRN_SUB = 16
M_HEADDIM = 64
M_GROUPS = 8
M_D_STATE = 128
M_CONV = 4
M_CHUNK = 128
FFN_CONV = 3
CONV_COLS = 256
HGRN_PAR = 2


def _params(dims=None, **kw):
    return pltpu.CompilerParams(dimension_semantics=dims, vmem_limit_bytes=VMEM_LIMIT, **kw)


def _tile(dim, target, align=LANES):
    t = (min(target, dim) // align) * align
    while t >= align:
        if dim % t == 0:
            return t
        if 2 * t < target and dim <= 2 * target:
            return dim
        t -= align
    return dim


class _Buf:
    def __init__(self, rows, n, dtype):
        self.arr = lax.empty((N_DEV, rows, n), dtype)


class _Job:
    def __init__(self, src, mode, buf, r0=0, r1=None, q0=0):
        rows = src.shape[-2]
        self.src, self.mode, self.buf, self.r0, self.r1, self.q0 = src, mode, buf, r0, rows if r1 is None else r1, q0


def _job_copies(job, src_ref, out_ref, send_sems, recv_sems, local_sem):
    x, y, c = lax.axis_index("x"), lax.axis_index("y"), lax.axis_index("c")
    me = 4 * x + 2 * y + c
    n = job.r1 - job.r0
    rows, land = pl.ds(job.r0, n), pl.ds(job.q0, n)
    part = (lambda d: src_ref.at[d, rows]) if job.mode == "scatter" else (lambda d: src_ref.at[rows])
    local = pltpu.make_async_copy(part(me), out_ref.at[me, land], local_sem)
    pairs = []
    for k in range(1, N_DEV):
        kx, ky, kc = (k >> 2) & 1, (k >> 1) & 1, k & 1
        px, py, pc = (1 - x if kx else x), (1 - y if ky else y), (1 - c if kc else c)
        peer = 4 * px + 2 * py + pc
        sems = dict(send_sem=send_sems.at[k - 1], recv_sem=recv_sems.at[k - 1], device_id=(px, py, pc),
                    device_id_type=pl.DeviceIdType.MESH)
        send = pltpu.make_async_remote_copy(src_ref=part(peer), dst_ref=out_ref.at[me, land], **sems)
        landing = pltpu.make_async_remote_copy(src_ref=part(me), dst_ref=out_ref.at[peer, land], **sems)
        pairs.append((send, landing))
    return local, pairs


def _jobs_start(jobs, src_refs, out_refs, send_sems, recv_sems, local_sems):
    for i, job in enumerate(jobs):
        local, pairs = _job_copies(job, src_refs[i], out_refs[i], send_sems.at[i], recv_sems.at[i], local_sems.at[i])
        local.start()
        for send, _ in pairs:
            send.start()


def _jobs_wait(jobs, src_refs, out_refs, send_sems, recv_sems, local_sems):
    for i, job in enumerate(jobs):
        local, pairs = _job_copies(job, src_refs[i], out_refs[i], send_sems.at[i], recv_sems.at[i], local_sems.at[i])
        for _, landing in pairs:
            landing.wait_recv()
        for send, _ in pairs:
            send.wait_send()
        local.wait()


def _job_scratch(n):
    return [pltpu.SemaphoreType.DMA((n, N_DEV - 1)), pltpu.SemaphoreType.DMA((n, N_DEV - 1)), pltpu.SemaphoreType.DMA((n,))]


def _buf_shapes(jobs):
    assert len({id(j.buf) for j in jobs}) == len(jobs), "one job per buffer and call"
    return tuple(jax.ShapeDtypeStruct(j.buf.arr.shape, j.buf.arr.dtype) for j in jobs)


def _exchange(jobs, *, name):
    n = len(jobs)

    def body(*refs):
        srcs, dsts, sems = refs[:n], refs[2 * n:3 * n], refs[3 * n:]
        _jobs_start(jobs, srcs, dsts, *sems)
        _jobs_wait(jobs, srcs, dsts, *sems)

    hbm = pl.BlockSpec(memory_space=pl.ANY)
    outs = pl.pallas_call(
        body, name=name, out_shape=_buf_shapes(jobs), in_specs=[hbm] * (2 * n), out_specs=(hbm,) * n,
        input_output_aliases={n + i: i for i in range(n)},
        scratch_shapes=_job_scratch(n), compiler_params=pltpu.CompilerParams(has_side_effects=True),
    )(*[j.src for j in jobs], *[j.buf.arr for j in jobs])
    for j, o in zip(jobs, outs):
        j.buf.arr = o


def _call(body, *, name, out_shape, grid, in_specs, out_specs, scratch_shapes=(), dims, jobs=()):
    single = not isinstance(out_shape, (tuple, list))
    out_shape = (out_shape,) if single else tuple(out_shape)
    out_specs = (out_specs,) if single else tuple(out_specs)
    n_in, n_out, n_scr, nj = len(in_specs), len(out_shape), len(scratch_shapes), len(jobs)
    if not jobs:
        plain = pl.pallas_call(body, name=name, out_shape=out_shape, grid=grid, in_specs=list(in_specs), out_specs=out_specs,
                               scratch_shapes=list(scratch_shapes), compiler_params=_params(dims))

        def run_plain(*args):
            r = plain(*args)
            return r[0] if single else tuple(r)

        return run_plain

    def carrying(*refs):
        ins, srcs = refs[:n_in], refs[n_in:n_in + nj]
        o0 = n_in + 2 * nj
        outs, dsts = refs[o0:o0 + n_out], refs[o0 + n_out:o0 + n_out + nj]
        scr, sems = refs[o0 + n_out + nj:o0 + n_out + nj + n_scr], refs[o0 + n_out + nj + n_scr:]
        ids = [pl.program_id(d) for d in range(len(grid))]
        first = functools.reduce(jnp.logical_and, [i == 0 for i in ids])
        last = functools.reduce(jnp.logical_and, [i == g - 1 for i, g in zip(ids, grid)])

        @pl.when(first)
        def _():
            _jobs_start(jobs, srcs, dsts, *sems)

        body(*ins, *outs, *scr)

        @pl.when(last)
        def _():
            _jobs_wait(jobs, srcs, dsts, *sems)

    hbm = pl.BlockSpec(memory_space=pl.ANY)
    call = pl.pallas_call(
        carrying, name=name, out_shape=out_shape + _buf_shapes(jobs), grid=grid,
        in_specs=list(in_specs) + [hbm] * (2 * nj), out_specs=out_specs + (hbm,) * nj,
        input_output_aliases={n_in + nj + i: n_out + i for i in range(nj)},
        scratch_shapes=list(scratch_shapes) + _job_scratch(nj),
        compiler_params=_params(("arbitrary",) * len(grid), has_side_effects=True))

    def run(*args):
        r = call(*args, *[j.src for j in jobs], *[j.buf.arr for j in jobs])
        for j, o in zip(jobs, r[n_out:]):
            j.buf.arr = o
        return r[0] if single else tuple(r[:n_out])

    return run


def _dg(a, b, ca, cb):
    return lax.dot_general(a.astype(BF16), b.astype(BF16), (((ca,), (cb,)), ((), ())), preferred_element_type=F32)


@jax.custom_vjp
def _dot_nn(a, b):
    return _dg(a, b, 1, 0)


def _dot_nn_f(a, b):
    return _dg(a, b, 1, 0), (a, b)


def _dot_nn_b(res, g):
    a, b = res
    return _dg(g, b, 1, 1), _dg(a, g, 0, 0)


_dot_nn.defvjp(_dot_nn_f, _dot_nn_b)


@jax.custom_vjp
def _dot_nt(a, b):
    return _dg(a, b, 1, 1)


def _dot_nt_f(a, b):
    return _dg(a, b, 1, 1), (a, b)


def _dot_nt_b(res, g):
    a, b = res
    return _dg(g, b, 1, 0), _dg(g, a, 0, 0)


_dot_nt.defvjp(_dot_nt_f, _dot_nt_b)


@jax.custom_vjp
def _dot_tn(a, b):
    return _dg(a, b, 0, 0)


def _dot_tn_f(a, b):
    return _dg(a, b, 0, 0), (a, b)


def _dot_tn_b(res, g):
    a, b = res
    return _dg(b, g, 1, 1), _dg(a, g, 1, 0)


_dot_tn.defvjp(_dot_tn_f, _dot_tn_b)


def _tri(n):
    return (lax.broadcasted_iota(jnp.int32, (n, n), 0) >= lax.broadcasted_iota(jnp.int32, (n, n), 1)).astype(F32)


def _silu(x):
    return x * jax.nn.sigmoid(x)


def _matmul(a, b, mode, out_dtype, *, name, add=None, out_stack=None, tm=1024, tn=1024, tk=2048, jobs=()):
    a_n = a.shape[2] if a.ndim == 3 else None
    b_n = b.shape[2] if b.ndim == 3 else None
    a_shape = (a.shape[1], a.shape[0] * a.shape[2]) if a_n else a.shape
    b_shape = (b.shape[1], b.shape[0] * b.shape[2]) if b_n else b.shape
    if mode == "nn":
        (M, K), (K2, N) = a_shape, b_shape
    elif mode == "nt":
        (M, K), (N, K2) = a_shape, b_shape
    else:
        (K, M), (K2, N) = a_shape, b_shape
    assert K == K2, (a.shape, b.shape, mode)
    n_lim = {"m": M, "n": N, "k": K}
    for dim, n in (("m" if mode == "tn" else "k", a_n), ("k" if mode == "nt" else "n", b_n), ("n", N // out_stack if out_stack else None)):
        if n:
            n_lim[dim] = math.gcd(n_lim[dim], n)
    tm, tn, tk = _tile(n_lim["m"], tm, 8 if M % LANES else LANES), _tile(n_lim["n"], tn), _tile(n_lim["k"], tk)
    nk = K // tk
    ca, cb = {"nn": (1, 0), "nt": (1, 1), "tn": (0, 0)}[mode]

    def body(*refs):
        a_ref, b_ref = refs[:2]
        add_ref = refs[2] if add is not None else None
        o_ref = refs[2 + (add is not None)]

        def finish(r):
            if add is not None:
                r = r + add_ref[...]
            o_ref[...] = r.astype(o_ref.dtype)

        if nk == 1:
            finish(_dg(a_ref[...], b_ref[...], ca, cb))
            return
        acc_ref = refs[-1]
        k = pl.program_id(2)

        @pl.when(k == 0)
        def _():
            acc_ref[...] = _dg(a_ref[...], b_ref[...], ca, cb)

        @pl.when((k > 0) & (k < nk - 1))
        def _():
            acc_ref[...] += _dg(a_ref[...], b_ref[...], ca, cb)

        @pl.when(k == nk - 1)
        def _():
            finish(acc_ref[...] + _dg(a_ref[...], b_ref[...], ca, cb))

    def spec(rows, cols, t_rows, t_cols, n):
        if not n:
            return pl.BlockSpec((t_rows, t_cols), lambda i, j, k: (rows(i, j, k), cols(i, j, k)))
        per = n // t_cols
        return pl.BlockSpec((None, t_rows, t_cols), lambda i, j, k: (cols(i, j, k) // per, rows(i, j, k), cols(i, j, k) % per))

    gi, gj, gk = (lambda i, j, k: i), (lambda i, j, k: j), (lambda i, j, k: k)
    a_spec = spec(gk, gi, tk, tm, a_n) if mode == "tn" else spec(gi, gk, tm, tk, a_n)
    b_spec = spec(gj, gk, tn, tk, b_n) if mode == "nt" else spec(gk, gj, tk, tn, b_n)
    o_spec = spec(gi, gj, tm, tn, N // out_stack if out_stack else None)
    out_shape = (out_stack, M, N // out_stack) if out_stack else (M, N)
    in_specs, args = [a_spec, b_spec], [a, b]
    if add is not None:
        assert not out_stack
        in_specs.append(o_spec)
        args.append(add)
    return _call(
        body, name=name, out_shape=jax.ShapeDtypeStruct(out_shape, out_dtype), grid=(M // tm, N // tn, nk),
        in_specs=in_specs, out_specs=o_spec, scratch_shapes=[pltpu.VMEM((tm, tn), F32)] if nk > 1 else [],
        dims=("parallel", "parallel", "arbitrary"), jobs=jobs,
    )(*args)


def _rms_fwd(h, w, *, name):
    T, D = h.shape
    tr = _tile(T, 256, 8)

    def body(h_ref, w_ref, u_ref):
        x = h_ref[...]
        u_ref[...] = (x * lax.rsqrt(jnp.mean(x * x, axis=-1, keepdims=True) + NORM_EPS) * w_ref[...]).astype(u_ref.dtype)

    return pl.pallas_call(
        body, name=name, out_shape=jax.ShapeDtypeStruct((T, D), BF16), grid=(T // tr,),
        in_specs=[pl.BlockSpec((tr, D), lambda i: (i, 0)), pl.BlockSpec((1, D), lambda i: (0, 0))],
        out_specs=pl.BlockSpec((tr, D), lambda i: (i, 0)), compiler_params=_params(("parallel",)),
    )(h, w.reshape(1, D))


def _rms_bwd(h, w, du, dh_in, *, name):
    T, D = h.shape
    tr = _tile(T, 256, 8)

    def body(h_ref, w_ref, du_ref, dhin_ref, dh_ref, dw_ref):
        x = h_ref[...]
        g = du_ref[...].astype(F32)
        rstd = lax.rsqrt(jnp.mean(x * x, axis=-1, keepdims=True) + NORM_EPS)
        xhat = x * rstd
        gx = g * w_ref[...]
        dh_ref[...] = dhin_ref[...] + rstd * (gx - xhat * jnp.mean(gx * xhat, axis=-1, keepdims=True))

        @pl.when(pl.program_id(0) == 0)
        def _():
            dw_ref[...] = jnp.zeros_like(dw_ref)

        dw_ref[...] += jnp.sum(g * xhat, axis=0, keepdims=True)

    row = pl.BlockSpec((tr, D), lambda i: (i, 0))
    vec = pl.BlockSpec((1, D), lambda i: (0, 0))
    return pl.pallas_call(
        body, name=name, out_shape=(jax.ShapeDtypeStruct((T, D), F32), jax.ShapeDtypeStruct((1, D), F32)), grid=(T // tr,),
        in_specs=[row, vec, row, row], out_specs=(row, vec), compiler_params=_params(("arbitrary",)),
    )(h, w.reshape(1, D), du, dh_in)


def _loss_head(h, w, target, *, name):
    T, D = h.shape
    tr = _tile(T, 256, 8)

    def body(h_ref, w_ref, t_ref, loss_ref, dh_ref, dw_ref):
        x = h_ref[...]
        rstd = lax.rsqrt(jnp.mean(x * x, axis=-1, keepdims=True) + NORM_EPS)
        xhat = x * rstd
        err = xhat * w_ref[...] - t_ref[...]
        g = err * (1.0 / D)
        gx = g * w_ref[...]
        dh_ref[...] = rstd * (gx - xhat * jnp.mean(gx * xhat, axis=-1, keepdims=True))

        @pl.when(pl.program_id(0) == 0)
        def _():
            dw_ref[...] = jnp.zeros_like(dw_ref)
            loss_ref[...] = jnp.zeros_like(loss_ref)

        dw_ref[...] += jnp.sum(g * xhat, axis=0, keepdims=True)
        loss_ref[...] += (0.5 / D) * jnp.sum(jnp.sum(err * err, axis=-1, keepdims=True), axis=0, keepdims=True)

    row = pl.BlockSpec((tr, D), lambda i: (i, 0))
    vec = pl.BlockSpec((1, D), lambda i: (0, 0))
    one = pl.BlockSpec((1, 1), lambda i: (0, 0))
    return pl.pallas_call(
        body, name=name,
        out_shape=(jax.ShapeDtypeStruct((1, 1), F32), jax.ShapeDtypeStruct((T, D), F32), jax.ShapeDtypeStruct((1, D), F32)),
        grid=(T // tr,), in_specs=[row, vec, row], out_specs=(one, row, vec), compiler_params=_params(("arbitrary",)),
    )(h, w.reshape(1, D), target)


def _shift_down(x, s):
    if s == 0:
        return x
    rows = lax.broadcasted_iota(jnp.int32, x.shape, 0)
    return jnp.where(rows >= s, pltpu.roll(x, s, 0), 0.0)


def _shift_up(x, s):
    if s == 0:
        return x
    n = x.shape[0]
    rows = lax.broadcasted_iota(jnp.int32, x.shape, 0)
    return jnp.where(rows < n - s, pltpu.roll(x, n - s, 0), 0.0)


def _conv_pre(a, w_ref, b_ref, taps):
    pre = b_ref[0:1, :] + w_ref[taps - 1:taps, :] * a
    for k in range(taps - 1):
        pre = pre + w_ref[k:k + 1, :] * _shift_down(a, taps - 1 - k)
    return pre


def _conv_bwd(a, dpre, w_ref, taps):
    da = w_ref[taps - 1:taps, :] * dpre
    rows = []
    for k in range(taps - 1):
        da = da + w_ref[k:k + 1, :] * _shift_up(dpre, taps - 1 - k)
        rows.append(jnp.sum(dpre * _shift_down(a, taps - 1 - k), axis=0, keepdims=True))
    rows.append(jnp.sum(dpre * a, axis=0, keepdims=True))
    rows.append(jnp.sum(dpre, axis=0, keepdims=True))
    return da, rows


def _ffn_gate_fwd(a, cw, cb, Bl, L, *, name, jobs=()):
    T, F2 = a.shape
    F = F2 // 2
    tc = _tile(F, CONV_COLS)
    nj = F // tc

    def body(ag_ref, au_ref, wg_ref, wu_ref, bg_ref, bu_ref, y_ref):
        hg = _conv_pre(ag_ref[...].astype(F32), wg_ref, bg_ref, FFN_CONV)
        hu = _conv_pre(au_ref[...].astype(F32), wu_ref, bu_ref, FFN_CONV)
        y_ref[...] = (_silu(hg) * hu).astype(y_ref.dtype)

    blk = lambda off: pl.BlockSpec((L, tc), lambda b, j: (b, j + off))
    wblk = lambda off: pl.BlockSpec((FFN_CONV, tc), lambda b, j: (0, j + off))
    bblk = lambda off: pl.BlockSpec((1, tc), lambda b, j: (0, j + off))
    return _call(
        body, name=name, out_shape=jax.ShapeDtypeStruct((T, F), BF16), grid=(Bl, nj),
        in_specs=[blk(0), blk(nj), wblk(0), wblk(nj), bblk(0), bblk(nj)], out_specs=blk(0),
        dims=("parallel", "parallel"), jobs=jobs,
    )(a, a, cw, cw, cb.reshape(1, F2), cb.reshape(1, F2))


def _ffn_gate_bwd(a, cw, cb, dy, Bl, L, *, name, jobs=()):
    T, F2 = a.shape
    F = F2 // 2
    tc = _tile(F, CONV_COLS)
    nj = F // tc

    def body(ag_ref, au_ref, wg_ref, wu_ref, bg_ref, bu_ref, dy_ref, da_ref, pg_ref, pu_ref):
        ag, au = ag_ref[...].astype(F32), au_ref[...].astype(F32)
        hg = _conv_pre(ag, wg_ref, bg_ref, FFN_CONV)
        hu = _conv_pre(au, wu_ref, bu_ref, FFN_CONV)
        g = dy_ref[...].astype(F32)
        s = jax.nn.sigmoid(hg)
        dhg = g * hu * (s * (1.0 + hg * (1.0 - s)))
        dhu = g * (hg * s)
        dag, rg = _conv_bwd(ag, dhg, wg_ref, FFN_CONV)
        dau, ru = _conv_bwd(au, dhu, wu_ref, FFN_CONV)
        da_ref[0] = dag.astype(da_ref.dtype)
        da_ref[1] = dau.astype(da_ref.dtype)

        @pl.when(pl.program_id(1) == 0)
        def _():
            pg_ref[...] = jnp.zeros_like(pg_ref)
            pu_ref[...] = jnp.zeros_like(pu_ref)

        for k in range(FFN_CONV + 1):
            pg_ref[k:k + 1, :] += rg[k]
            pu_ref[k:k + 1, :] += ru[k]

    blk = lambda off: pl.BlockSpec((L, tc), lambda j, b: (b, j + off))
    wblk = lambda off: pl.BlockSpec((FFN_CONV, tc), lambda j, b: (0, j + off))
    bblk = lambda off: pl.BlockSpec((1, tc), lambda j, b: (0, j + off))
    pblk = lambda off: pl.BlockSpec((8, tc), lambda j, b: (0, j + off))
    da, pg, pu = _call(
        body, name=name,
        out_shape=(jax.ShapeDtypeStruct((2, T, F), BF16), jax.ShapeDtypeStruct((8, F), F32), jax.ShapeDtypeStruct((8, F), F32)),
        grid=(nj, Bl),
        in_specs=[blk(0), blk(nj), wblk(0), wblk(nj), bblk(0), bblk(nj), blk(0)],
        out_specs=(pl.BlockSpec((2, L, tc), lambda j, b: (0, b, j)), pblk(0), pblk(0)),
        dims=("parallel", "arbitrary"), jobs=jobs,
    )(a, a, cw, cw, cb.reshape(1, F2), cb.reshape(1, F2), dy)
    return da, jnp.concatenate([pg, pu], axis=1)


def _mconv_fwd(xbc, cw, cb, Bl, L, *, name):
    T, W = xbc.shape
    tc = _tile(W, CONV_COLS)

    def body(a_ref, w_ref, b_ref, y_ref):
        y_ref[...] = _silu(_conv_pre(a_ref[...], w_ref, b_ref, M_CONV))

    blk = pl.BlockSpec((L, tc), lambda b, j: (b, j))
    return pl.pallas_call(
        body, name=name, out_shape=jax.ShapeDtypeStruct((T, W), F32), grid=(Bl, W // tc),
        in_specs=[blk, pl.BlockSpec((M_CONV, tc), lambda b, j: (0, j)), pl.BlockSpec((1, tc), lambda b, j: (0, j))],
        out_specs=blk, compiler_params=_params(("parallel", "parallel")),
    )(xbc, cw, cb.reshape(1, W))


def _mconv_bwd(xbc, cw, cb, dy, Bl, L, *, name):
    T, W = xbc.shape
    tc = _tile(W, CONV_COLS)

    def body(a_ref, w_ref, b_ref, dy_ref, da_ref, p_ref):
        a = a_ref[...]
        pre = _conv_pre(a, w_ref, b_ref, M_CONV)
        s = jax.nn.sigmoid(pre)
        dpre = dy_ref[...] * (s * (1.0 + pre * (1.0 - s)))
        da, rows = _conv_bwd(a, dpre, w_ref, M_CONV)
        da_ref[...] = da.astype(da_ref.dtype)

        @pl.when(pl.program_id(1) == 0)
        def _():
            p_ref[...] = jnp.zeros_like(p_ref)

        for k in range(M_CONV + 1):
            p_ref[k:k + 1, :] += rows[k]

    blk = pl.BlockSpec((L, tc), lambda j, b: (b, j))
    return pl.pallas_call(
        body, name=name, out_shape=(jax.ShapeDtypeStruct((T, W), BF16), jax.ShapeDtypeStruct((8, W), F32)),
        grid=(W // tc, Bl),
        in_specs=[blk, pl.BlockSpec((M_CONV, tc), lambda j, b: (0, j)), pl.BlockSpec((1, tc), lambda j, b: (0, j)), blk],
        out_specs=(blk, pl.BlockSpec((8, tc), lambda j, b: (0, j))),
        compiler_params=_params(("parallel", "arbitrary")),
    )(xbc, cw, cb.reshape(1, W), dy)


def _hgrn_chunk(st, qr, fr, v, gr, lb, gn):
    C = qr.shape[0]
    q = _silu(qr)
    logf = jnp.log(lb + (1.0 - lb) * jax.nn.sigmoid(fr))
    k = (1.0 - lb) * jax.nn.sigmoid(-fr)
    b = jnp.dot(_tri(C), logf, precision=HIGHEST, preferred_element_type=F32)
    o = _dot_nt(q * jnp.exp(b), st)
    keep = lax.broadcasted_iota(jnp.int32, (HGRN_SUB, HGRN_SUB), 0) >= lax.broadcasted_iota(jnp.int32, (HGRN_SUB, HGRN_SUB), 1)
    cols = lax.broadcasted_iota(jnp.int32, (HGRN_SUB, C), 1)
    parts = []
    for blk in range(C // HGRN_SUB):
        r0 = blk * HGRN_SUB
        bi, qi, ki, vi = b[r0:r0 + HGRN_SUB], q[r0:r0 + HGRN_SUB], k[r0:r0 + HGRN_SUB], v[r0:r0 + HGRN_SUB]
        decay = jnp.exp(jnp.minimum(bi[:, None, :] - bi[None, :, :], 0.0))
        a_diag = jnp.where(keep, jnp.sum(qi[:, None, :] * ki[None, :, :] * decay, axis=-1), 0.0)
        oi = _dot_nn(a_diag, vi)
        if blk > 0:
            bs = b[r0 - 1:r0]
            a_off = _dot_nt(qi * jnp.exp(bi - bs), k * jnp.exp(jnp.minimum(bs - b, 0.0)))
            oi = oi + _dot_nn(jnp.where(cols < r0, a_off, 0.0), v)
        parts.append(oi)
    o = o + jnp.concatenate(parts, axis=0)
    bl = b[C - 1:C]
    st1 = st * jnp.exp(bl) + _dot_tn(v, k * jnp.exp(bl - b))
    og = o * lax.rsqrt(jnp.mean(o * o, axis=-1, keepdims=True) + NORM_EPS) * gn * _silu(gr)
    return st1, og


def _hgrn_specs(Bl, L, H, order):
    W, HB = HGRN_PAR * HGRN_HEAD, H // HGRN_PAR

    def at(off):
        if order == "bh":
            return pl.BlockSpec((L, W), lambda b, h: (b, h + off))
        return pl.BlockSpec((L, W), lambda h, b: (b, h + off))
    return [at(0), at(HB), at(2 * HB), at(3 * HB)]


def _hgrn_fwd(P, lb, gn, Bl, L, *, name, jobs=()):
    T, D4 = P.shape
    D = D4 // 4
    H = D // HGRN_HEAD
    C = HGRN_CHUNK
    NC = L // C
    W, HB = HGRN_PAR * HGRN_HEAD, H // HGRN_PAR
    heads = [slice(p * HGRN_HEAD, (p + 1) * HGRN_HEAD) for p in range(HGRN_PAR)]

    def body(q_ref, f_ref, v_ref, g_ref, lb_ref, gn_ref, og_ref, st_ref):
        gnv = gn_ref[...]

        def step(c, sts):
            r = pl.ds(pl.multiple_of(c * C, C), C)
            new = []
            for p, hd in enumerate(heads):
                st_ref[p, c] = sts[p]
                st1, og = _hgrn_chunk(sts[p], q_ref[r, hd], f_ref[r, hd], v_ref[r, hd], g_ref[r, hd], lb_ref[:, hd], gnv)
                og_ref[r, hd] = og.astype(og_ref.dtype)
                new.append(st1)
            return tuple(new)

        lax.fori_loop(0, NC, step, tuple(jnp.zeros((HGRN_HEAD, HGRN_HEAD), F32) for _ in heads))

    return _call(
        body, name=name,
        out_shape=(jax.ShapeDtypeStruct((T, D), BF16), jax.ShapeDtypeStruct((Bl, H, NC, HGRN_HEAD, HGRN_HEAD), F32)),
        grid=(Bl, HB),
        in_specs=_hgrn_specs(Bl, L, H, "bh") + [pl.BlockSpec((1, W), lambda b, h: (0, h)), pl.BlockSpec((1, HGRN_HEAD), lambda b, h: (0, 0))],
        out_specs=(pl.BlockSpec((L, W), lambda b, h: (b, h)),
                   pl.BlockSpec((None, HGRN_PAR, NC, HGRN_HEAD, HGRN_HEAD), lambda b, h: (b, h, 0, 0, 0))),
        dims=("parallel", "parallel"), jobs=jobs,
    )(P, P, P, P, lb.reshape(1, D), gn.reshape(1, HGRN_HEAD))


def _hgrn_bwd(P, lb, gn, states, dog, Bl, L, *, name, jobs=()):
    T, D4 = P.shape
    D = D4 // 4
    H = D // HGRN_HEAD
    C = HGRN_CHUNK
    NC = L // C
    W, HB = HGRN_PAR * HGRN_HEAD, H // HGRN_PAR
    heads = [slice(p * HGRN_HEAD, (p + 1) * HGRN_HEAD) for p in range(HGRN_PAR)]

    def body(q_ref, f_ref, v_ref, g_ref, lb_ref, gn_ref, st_ref, dog_ref, dp_ref, dlb_ref, dgn_ref):
        gnv = gn_ref[...]

        @pl.when(pl.program_id(1) == 0)
        def _():
            dlb_ref[...] = jnp.zeros_like(dlb_ref)

        @pl.when((pl.program_id(0) == 0) & (pl.program_id(1) == 0))
        def _():
            dgn_ref[...] = jnp.zeros_like(dgn_ref)

        def step(i, dsts):
            c = NC - 1 - i
            r = pl.ds(pl.multiple_of(c * C, C), C)
            new, dgn_sum = [], jnp.zeros((1, HGRN_HEAD), F32)
            for p, hd in enumerate(heads):
                _, vjp = jax.vjp(_hgrn_chunk, st_ref[p, c], q_ref[r, hd], f_ref[r, hd], v_ref[r, hd], g_ref[r, hd],
                                 lb_ref[:, hd], gnv)
                dst0, dq, df, dv, dg, dlb, dgn = vjp((dsts[p], dog_ref[r, hd].astype(F32)))
                dp_ref[0, r, hd] = dq.astype(dp_ref.dtype)
                dp_ref[1, r, hd] = df.astype(dp_ref.dtype)
                dp_ref[2, r, hd] = dv.astype(dp_ref.dtype)
                dp_ref[3, r, hd] = dg.astype(dp_ref.dtype)
                dlb_ref[:, hd] += dlb
                dgn_sum = dgn_sum + dgn
                new.append(dst0)
            dgn_ref[...] += dgn_sum
            return tuple(new)

        lax.fori_loop(0, NC, step, tuple(jnp.zeros((HGRN_HEAD, HGRN_HEAD), F32) for _ in heads))

    return _call(
        body, name=name,
        out_shape=(jax.ShapeDtypeStruct((4, T, D), BF16), jax.ShapeDtypeStruct((1, D), F32), jax.ShapeDtypeStruct((1, HGRN_HEAD), F32)),
        grid=(HB, Bl),
        in_specs=_hgrn_specs(Bl, L, H, "hb") + [
            pl.BlockSpec((1, W), lambda h, b: (0, h)), pl.BlockSpec((1, HGRN_HEAD), lambda h, b: (0, 0)),
            pl.BlockSpec((None, HGRN_PAR, NC, HGRN_HEAD, HGRN_HEAD), lambda h, b: (b, h, 0, 0, 0)),
            pl.BlockSpec((L, W), lambda h, b: (b, h))],
        out_specs=(pl.BlockSpec((4, L, W), lambda h, b: (0, b, h)),
                   pl.BlockSpec((1, W), lambda h, b: (0, h)), pl.BlockSpec((1, HGRN_HEAD), lambda h, b: (0, 0))),
        dims=("arbitrary", "arbitrary"), jobs=jobs,
    )(P, P, P, P, lb.reshape(1, D), gn.reshape(1, HGRN_HEAD), states, dog)


def _lower_bounds(logits):
    p = jax.nn.softmax(logits, axis=0)
    rows, run = [], jnp.zeros_like(p[0:1])
    for i in range(logits.shape[0]):
        run = run + p[i:i + 1]
        rows.append(run - p[0:1])
    return jnp.concatenate(rows, axis=0) if len(rows) > 1 else rows[0]


def _lb_fwd(logits, *, name):
    def body(x_ref, o_ref):
        o_ref[...] = _lower_bounds(x_ref[...])
    return pl.pallas_call(body, name=name, out_shape=jax.ShapeDtypeStruct(logits.shape, F32))(logits)


def _lb_bwd(logits, dlb, *, name):
    def body(x_ref, g_ref, o_ref):
        _, vjp = jax.vjp(_lower_bounds, x_ref[...])
        o_ref[...] = vjp(g_ref[...])[0]
    return pl.pallas_call(body, name=name, out_shape=jax.ShapeDtypeStruct(logits.shape, F32))(logits, dlb)


def _ssd_pre(dtc_raw, dtr_raw, bias_c, bias_r, alog_c, alog_r):
    C = dtc_raw.shape[0]
    dt_c = jax.nn.softplus(dtc_raw + bias_c)
    dt_r = jax.nn.softplus(dtr_raw + bias_r)
    acs_c = jnp.dot(_tri(C), dt_c * -jnp.exp(alog_c), precision=HIGHEST, preferred_element_type=F32)
    acs_r = lax.dot_general(dt_r * -jnp.exp(alog_r), _tri(C), (((1,), (1,)), ((), ())), precision=HIGHEST, preferred_element_type=F32)
    return dt_c, acs_c, acs_r


def _ssd_head(h0, xs, G, Bm, Cm, dt, acs, acs_row, dskip):
    C = xs.shape[0]
    causal = lax.broadcasted_iota(jnp.int32, (C, C), 0) >= lax.broadcasted_iota(jnp.int32, (C, C), 1)
    Lm = jnp.where(causal, jnp.exp(jnp.minimum(acs - acs_row, 0.0)), 0.0)
    X = xs * dt
    y = _dot_nn(G * Lm, X) + _dot_nt(Cm * jnp.exp(acs), h0) + dskip * xs
    alast = acs[C - 1:C]
    h1 = jnp.exp(alast) * h0 + _dot_tn(X, Bm * jnp.exp(alast - acs))
    return y, h1


def _ssd_layout(Bl, L, d_inner, order):
    G = M_GROUPS
    hpg = d_inner // M_HEADDIM // G
    gw = hpg * M_HEADDIM
    ix = (lambda b, g: (b, g)) if order == "bg" else (lambda g, b: (b, g))
    def cols(width, off):
        if order == "bg":
            return pl.BlockSpec((L, width), lambda b, g: (b, g + off))
        return pl.BlockSpec((L, width), lambda g, b: (b, g + off))
    def per_group(shape):
        n = len(shape)
        if order == "bg":
            return pl.BlockSpec((None,) + shape, lambda b, g: (g,) + (0,) * n)
        return pl.BlockSpec((None,) + shape, lambda g, b: (g,) + (0,) * n)
    if order == "bg":
        dtc = pl.BlockSpec((None, L, LANES), lambda b, g: (g, b, 0))
        dtr = pl.BlockSpec((None, 8, L), lambda b, g: (g, 0, b))
    else:
        dtc = pl.BlockSpec((None, L, LANES), lambda g, b: (g, b, 0))
        dtr = pl.BlockSpec((None, 8, L), lambda g, b: (g, 0, b))
    nb = d_inner // M_D_STATE
    return hpg, gw, cols, per_group, dtc, dtr, nb


def _ssd_fwd(xa, dtc, dtr, prm, Bl, L, d_inner, *, name, jobs=()):
    T = xa.shape[0]
    C = M_CHUNK
    NC = L // C
    G = M_GROUPS
    hpg, gw, cols, per_group, dtc_spec, dtr_spec, nb = _ssd_layout(Bl, L, d_inner, "bg")

    def body(xs_ref, b_ref, c_ref, dtc_ref, dtr_ref, bc_ref, br_ref, ac_ref, ar_ref, dk_ref, y_ref, hs_ref,
             dt_s, acs_s, acr_s, h_s):
        h_s[...] = jnp.zeros_like(h_s)

        def step(c, carry):
            r = pl.ds(pl.multiple_of(c * C, C), C)
            dt_c, acs_c, acs_r = _ssd_pre(dtc_ref[r, :], dtr_ref[:, r], bc_ref[...], br_ref[...], ac_ref[...], ar_ref[...])
            dt_s[...] = dt_c
            acs_s[...] = acs_c
            acr_s[...] = acs_r
            Bm, Cm = b_ref[r, :], c_ref[r, :]
            Gm = _dot_nt(Cm, Bm)
            for j in range(hpg):
                pc = slice(j * M_HEADDIM, (j + 1) * M_HEADDIM)
                h0 = h_s[j]
                hs_ref[c, j] = h0
                y, h1 = _ssd_head(h0, xs_ref[r, pc], Gm, Bm, Cm, dt_s[:, j:j + 1], acs_s[:, j:j + 1], acr_s[j:j + 1, :],
                                  dk_ref[0:1, j:j + 1])
                y_ref[r, pc] = y
                h_s[j] = h1
            return carry

        lax.fori_loop(0, NC, step, 0)

    return _call(
        body, name=name,
        out_shape=(jax.ShapeDtypeStruct((T, d_inner), F32), jax.ShapeDtypeStruct((Bl, G, NC, hpg, M_HEADDIM, M_D_STATE), F32)),
        grid=(Bl, G),
        in_specs=[cols(gw, 0), cols(M_D_STATE, nb), cols(M_D_STATE, nb + G), dtc_spec, dtr_spec,
                  per_group((1, LANES)), per_group((8, 1)), per_group((1, LANES)), per_group((8, 1)), per_group((1, LANES))],
        out_specs=(cols(gw, 0), pl.BlockSpec((None, None, NC, hpg, M_HEADDIM, M_D_STATE), lambda b, g: (b, g, 0, 0, 0, 0))),
        scratch_shapes=[pltpu.VMEM((C, LANES), F32), pltpu.VMEM((C, LANES), F32), pltpu.VMEM((8, C), F32),
                        pltpu.VMEM((hpg, M_HEADDIM, M_D_STATE), F32)],
        dims=("parallel", "parallel"), jobs=jobs,
    )(xa, xa, xa, dtc, dtr, *prm)


def _ssd_bwd(xa, dtc, dtr, prm, states, dy, Bl, L, d_inner, *, name, jobs=()):
    T = xa.shape[0]
    C = M_CHUNK
    NC = L // C
    G = M_GROUPS
    hpg, gw, cols, per_group, dtc_spec, dtr_spec, nb = _ssd_layout(Bl, L, d_inner, "gb")

    def body(xs_ref, b_ref, c_ref, dtc_ref, dtr_ref, bc_ref, br_ref, ac_ref, ar_ref, dk_ref, hs_ref, dy_ref,
             dxs_ref, db_ref, dc_ref, ddtc_ref, ddtr_ref, gbc_ref, gbr_ref, gac_ref, gar_ref, gdk_ref,
             dt_s, acs_s, acr_s, ddt_s, dacs_s, dacr_s, dh_s):
        dh_s[...] = jnp.zeros_like(dh_s)

        @pl.when(pl.program_id(1) == 0)
        def _():
            for ref in (gbc_ref, gbr_ref, gac_ref, gar_ref, gdk_ref):
                ref[...] = jnp.zeros_like(ref)

        def step(i, carry):
            c = NC - 1 - i
            r = pl.ds(pl.multiple_of(c * C, C), C)
            pre_in = (dtc_ref[r, :], dtr_ref[:, r], bc_ref[...], br_ref[...], ac_ref[...], ar_ref[...])
            (dt_c, acs_c, acs_r), pre_vjp = jax.vjp(_ssd_pre, *pre_in)
            dt_s[...] = dt_c
            acs_s[...] = acs_c
            acr_s[...] = acs_r
            ddt_s[...] = jnp.zeros_like(ddt_s)
            dacs_s[...] = jnp.zeros_like(dacs_s)
            dacr_s[...] = jnp.zeros_like(dacr_s)
            Bm, Cm = b_ref[r, :], c_ref[r, :]
            Gm = _dot_nt(Cm, Bm)
            dG = jnp.zeros((C, C), F32)
            dB = jnp.zeros((C, M_D_STATE), F32)
            dC = jnp.zeros((C, M_D_STATE), F32)
            for j in range(hpg):
                pc = slice(j * M_HEADDIM, (j + 1) * M_HEADDIM)
                one = slice(j, j + 1)
                _, vjp = jax.vjp(_ssd_head, hs_ref[c, j], xs_ref[r, pc], Gm, Bm, Cm, dt_s[:, one], acs_s[:, one], acr_s[one, :],
                                 dk_ref[0:1, one])
                dh0, dxs, dGj, dBj, dCj, ddt, dacs, dacr, ddk = vjp((dy_ref[r, pc], dh_s[j]))
                dh_s[j] = dh0
                dxs_ref[r, pc] = dxs
                dG, dB, dC = dG + dGj, dB + dBj, dC + dCj
                ddt_s[:, one] = ddt
                dacs_s[:, one] = dacs
                dacr_s[one, :] = dacr
                gdk_ref[0:1, one] += ddk
            db_ref[r, :] = dB + _dot_tn(dG, Cm)
            dc_ref[r, :] = dC + _dot_nn(dG, Bm)
            ddtc, ddtr, gbc, gbr, gac, gar = pre_vjp((ddt_s[...], dacs_s[...], dacr_s[...]))
            ddtc_ref[r, :] = ddtc
            ddtr_ref[:, r] = ddtr
            gbc_ref[...] += gbc
            gbr_ref[...] += gbr
            gac_ref[...] += gac
            gar_ref[...] += gar
            return carry

        lax.fori_loop(0, NC, step, 0)

    GN = G * M_D_STATE
    bc_cols = lambda off: pl.BlockSpec((L, M_D_STATE), lambda g, b: (b, g + off))
    outs = _call(
        body, name=name,
        out_shape=(jax.ShapeDtypeStruct((T, d_inner), F32), jax.ShapeDtypeStruct((T, GN), F32), jax.ShapeDtypeStruct((T, GN), F32),
                   jax.ShapeDtypeStruct(dtc.shape, F32), jax.ShapeDtypeStruct(dtr.shape, F32),
                   jax.ShapeDtypeStruct((G, 1, LANES), F32), jax.ShapeDtypeStruct((G, 8, 1), F32),
                   jax.ShapeDtypeStruct((G, 1, LANES), F32), jax.ShapeDtypeStruct((G, 8, 1), F32),
                   jax.ShapeDtypeStruct((G, 1, LANES), F32)),
        grid=(G, Bl),
        in_specs=[cols(gw, 0), cols(M_D_STATE, nb), cols(M_D_STATE, nb + G), dtc_spec, dtr_spec,
                  per_group((1, LANES)), per_group((8, 1)), per_group((1, LANES)), per_group((8, 1)), per_group((1, LANES)),
                  pl.BlockSpec((None, None, NC, hpg, M_HEADDIM, M_D_STATE), lambda g, b: (b, g, 0, 0, 0, 0)), cols(gw, 0)],
        out_specs=(cols(gw, 0), bc_cols(0), bc_cols(0), dtc_spec, dtr_spec,
                   per_group((1, LANES)), per_group((8, 1)), per_group((1, LANES)), per_group((8, 1)), per_group((1, LANES))),
        scratch_shapes=[pltpu.VMEM((C, LANES), F32), pltpu.VMEM((C, LANES), F32), pltpu.VMEM((8, C), F32),
                        pltpu.VMEM((C, LANES), F32), pltpu.VMEM((C, LANES), F32), pltpu.VMEM((8, C), F32),
                        pltpu.VMEM((hpg, M_HEADDIM, M_D_STATE), F32)],
        dims=("arbitrary", "arbitrary"), jobs=jobs,
    )(xa, xa, xa, dtc, dtr, *prm, states, dy)
    return outs


def _mgate(y, z, w):
    t = y * _silu(z)
    return t * lax.rsqrt(jnp.mean(t * t, axis=-1, keepdims=True) + NORM_EPS) * w


def _mgate_fwd(y, z, w, *, name):
    T, d_inner = y.shape
    gw = d_inner // M_GROUPS
    tr = _tile(T, 512, 8)

    def body(y_ref, z_ref, w_ref, o_ref):
        o_ref[...] = _mgate(y_ref[...], z_ref[...], w_ref[...]).astype(o_ref.dtype)

    blk = pl.BlockSpec((tr, gw), lambda i, g: (i, g))
    return pl.pallas_call(
        body, name=name, out_shape=jax.ShapeDtypeStruct((T, d_inner), BF16), grid=(T // tr, M_GROUPS),
        in_specs=[blk, blk, pl.BlockSpec((1, gw), lambda i, g: (0, g))], out_specs=blk,
        compiler_params=_params(("parallel", "parallel")),
    )(y, z, w.reshape(1, d_inner))


def _mgate_bwd(y, z, w, dyg, *, name):
    T, d_inner = y.shape
    gw = d_inner // M_GROUPS
    tr = _tile(T, 512, 8)

    def body(y_ref, z_ref, w_ref, g_ref, dy_ref, dz_ref, dw_ref):
        _, vjp = jax.vjp(_mgate, y_ref[...], z_ref[...], w_ref[...])
        dy, dz, dw = vjp(g_ref[...].astype(F32))
        dy_ref[...] = dy
        dz_ref[...] = dz.astype(dz_ref.dtype)

        @pl.when(pl.program_id(1) == 0)
        def _():
            dw_ref[...] = jnp.zeros_like(dw_ref)

        dw_ref[...] += dw

    blk = pl.BlockSpec((tr, gw), lambda g, i: (i, g))
    vec = pl.BlockSpec((1, gw), lambda g, i: (0, g))
    return pl.pallas_call(
        body, name=name,
        out_shape=(jax.ShapeDtypeStruct((T, d_inner), F32), jax.ShapeDtypeStruct((T, d_inner), BF16), jax.ShapeDtypeStruct((1, d_inner), F32)),
        grid=(M_GROUPS, T // tr), in_specs=[blk, blk, vec, blk], out_specs=(blk, blk, vec),
        compiler_params=_params(("parallel", "arbitrary")),
    )(y, z, w.reshape(1, d_inner), dyg)


def _adamw_math(w, g, m, v):
    m1 = ADAM_B1 * m + (1.0 - ADAM_B1) * g
    v1 = ADAM_B2 * v + (1.0 - ADAM_B2) * (g * g)
    m_hat = m1 / (1.0 - ADAM_B1 ** ADAM_STEP)
    v_hat = v1 / (1.0 - ADAM_B2 ** ADAM_STEP)
    delta = -ADAM_LR * (m_hat / (jnp.sqrt(v_hat) + ADAM_EPS) + ADAM_WD * w)
    return delta, m1, v1


def _adamw_sharded(land, w, m, v, *, name, jobs=()):
    R, n = w.shape
    tr = _tile(R, 128, 8)

    def body(l_ref, w_ref, m_ref, v_ref, g_ref, d_ref, m1_ref, v1_ref):
        g = l_ref[0].astype(F32)
        for k in range(1, N_DEV):
            g = g + l_ref[k].astype(F32)
        d, m1, v1 = _adamw_math(w_ref[...], g, m_ref[...], v_ref[...])
        g_ref[...] = g
        d_ref[...] = d
        m1_ref[...] = m1
        v1_ref[...] = v1

    blk = pl.BlockSpec((tr, n), lambda i: (i, 0))
    out = jax.ShapeDtypeStruct((R, n), F32)
    return _call(
        body, name=name, out_shape=(out, out, out, out), grid=(R // tr,),
        in_specs=[pl.BlockSpec((N_DEV, tr, n), lambda i: (0, i, 0)), blk, blk, blk], out_specs=(blk, blk, blk, blk),
        dims=("parallel",), jobs=jobs,
    )(land, w, m, v)


def _sum_devices(parts, *, name):
    _, R, n = parts.shape
    tr = _tile(R, 512, 8)

    def body(p_ref, o_ref):
        g = p_ref[0]
        for k in range(1, N_DEV):
            g = g + p_ref[k]
        o_ref[...] = g

    return pl.pallas_call(
        body, name=name, out_shape=jax.ShapeDtypeStruct((R, n), F32), grid=(R // tr,),
        in_specs=[pl.BlockSpec((N_DEV, tr, n), lambda i: (0, i, 0))], out_specs=pl.BlockSpec((tr, n), lambda i: (i, 0)),
        compiler_params=_params(("parallel",)),
    )(parts)


def _adamw_packed(g, w, m, v, *, name):
    R, n = w.shape
    tr = _tile(R, 512, 8)

    def body(g_ref, w_ref, m_ref, v_ref, d_ref, m1_ref, v1_ref):
        d, m1, v1 = _adamw_math(w_ref[...], g_ref[...], m_ref[...], v_ref[...])
        d_ref[...] = d
        m1_ref[...] = m1
        v1_ref[...] = v1

    blk = pl.BlockSpec((tr, n), lambda i: (i, 0))
    out = jax.ShapeDtypeStruct((R, n), F32)
    return pl.pallas_call(
        body, name=name, out_shape=(out, out, out), grid=(R // tr,), in_specs=[blk] * 4, out_specs=(blk, blk, blk),
        compiler_params=_params(("parallel",)),
    )(g, w, m, v)


PACK_ALIGN = 8 * LANES


def _pack(arrs):
    parts = []
    for a in arrs:
        flat = a.reshape(-1).astype(F32)
        pad = (-flat.shape[0]) % PACK_ALIGN
        parts.append(jnp.pad(flat, (0, pad)).reshape(-1, LANES))
    return jnp.concatenate(parts, axis=0)


def _unpack(packed, shapes, lead=()):
    out, r = [], 0
    nl = len(lead)
    for s in shapes:
        n = math.prod(s)
        rows = (n + PACK_ALIGN - 1) // PACK_ALIGN * (PACK_ALIGN // LANES)
        part = lax.slice_in_dim(packed, r, r + rows, axis=nl)
        out.append(part.reshape(lead + (rows * LANES,))[..., :n].reshape(lead + tuple(s)))
        r += rows
    return out


EXCHANGE_BYTES_PER_US = 80e3
EXCHANGE_ROW_ALIGN = 128
CARRY_FILL = 0.9

CALL_US = {
    "hgrn_in_fwd": 200, "hgrn_scan_fwd": 650, "hgrn_out_fwd": 55, "ffn_up_fwd": 250, "ffn_gate_fwd": 94, "ffn_down_fwd": 125,
    "mamba_z_fwd": 100, "mamba_xbc_fwd": 150, "mamba_scan_fwd": 487, "mamba_out_fwd": 95,
    "ffn_down_bwd_x": 210, "ffn_down_bwd_w": 215, "ffn_gate_bwd": 240, "ffn_up_bwd_x": 235, "ffn_up_bwd_w": 240,
    "hgrn_out_bwd_x": 52, "hgrn_out_bwd_w": 50, "hgrn_scan_bwd": 1850, "hgrn_in_bwd_x": 180, "hgrn_in_bwd_w": 175,
    "mamba_out_bwd_x": 100, "mamba_out_bwd_w": 95, "mamba_scan_bwd": 1518, "mamba_z_bwd_x": 100, "mamba_xbc_bwd_x": 135,
    "mamba_z_bwd_w": 90, "mamba_xbc_bwd_w": 130,
    "adamw_hgrn_w_in": 65, "adamw_hgrn_w_out": 16, "adamw_m_w_in": 87, "adamw_m_w_out": 32, "adamw_f_w_up": 168,
    "adamw_f_w_down": 109,
}


class _Plan:
    def __init__(self):
        self.queue = []

    def push(self, src, mode, buf, q0=0):
        rows, n = src.shape[-2], src.shape[-1]
        us_per_row = (N_DEV - 1) * n * src.dtype.itemsize / EXCHANGE_BYTES_PER_US
        self.queue.append(dict(src=src, mode=mode, buf=buf, r0=0, r1=rows, q0=q0, us_per_row=us_per_row))

    def take(self, name):
        budget, jobs, used = CARRY_FILL * CALL_US[name], [], set()
        while self.queue:
            it = self.queue[0]
            left = it["r1"] - it["r0"]
            fit = int(budget / it["us_per_row"])
            rows = left if fit >= left else fit // EXCHANGE_ROW_ALIGN * EXCHANGE_ROW_ALIGN
            if rows <= 0 or id(it["buf"]) in used:
                break
            jobs.append(_Job(it["src"], it["mode"], it["buf"], it["r0"], it["r0"] + rows, it["q0"] + it["r0"]))
            used.add(id(it["buf"]))
            budget -= rows * it["us_per_row"]
            it["r0"] += rows
            if it["r0"] == it["r1"]:
                self.queue.pop(0)
            else:
                break
        return jobs

    def finish(self, buf, *, name):
        mine = [it for it in self.queue if it["buf"] is buf]
        self.queue = [it for it in self.queue if it["buf"] is not buf]
        for i, it in enumerate(mine):
            _exchange([_Job(it["src"], it["mode"], buf, it["r0"], it["r1"], it["q0"] + it["r0"])], name=f"{name}_{i}" if i else name)
        return buf.arr


def _pad_to(a, axis, size):
    pad = [(0, 0)] * a.ndim
    pad[axis] = (0, size - a.shape[axis])
    return jnp.pad(a, pad)


def _ssd_operands(dt_raw, dt_bias, a_log, dskip):
    T, heads = dt_raw.shape
    G = M_GROUPS
    hpg = heads // G
    t3 = dt_raw.reshape(T, G, hpg)
    dtc = _pad_to(t3.transpose(1, 0, 2), 2, LANES)
    dtr = _pad_to(t3.transpose(1, 2, 0), 1, 8)
    col = lambda p: _pad_to(p.reshape(G, 1, hpg), 2, LANES)
    row = lambda p: _pad_to(p.reshape(G, hpg, 1), 1, 8)
    return dtc, dtr, (col(dt_bias), row(dt_bias), col(a_log), row(a_log), col(dskip))


def _ssd_unpack(ddtc, ddtr, gbc, gbr, gac, gar, gdk, heads):
    G = M_GROUPS
    hpg = heads // G
    T = ddtc.shape[1]
    ddt = ddtc[:, :, :hpg].transpose(1, 0, 2).reshape(T, heads) + ddtr[:, :hpg, :].transpose(2, 0, 1).reshape(T, heads)
    both = lambda c, r: c[:, 0, :hpg].reshape(heads) + r[:, :hpg, 0].reshape(heads)
    return ddt, both(gbc, gbr), both(gac, gar), gdk[:, 0, :hpg].reshape(heads)


class _LazyWeights:
    def __init__(self, name, shape, fetch):
        self.name, self.shape, self.fetch, self.got = name, shape, fetch, {}

    def __getitem__(self, layer):
        if layer not in self.got:
            self.got[layer] = self.fetch(self.name, layer)
        return self.got[layer]


class _GradSink:
    def __init__(self, name, layers, push):
        self.name, self.shape, self.push = name, (layers,), push

    def __setitem__(self, layer, g):
        self.push(self.name, layer, g)


def _local_step(x, target, W, plan, push):
    Bl, L, D = x.shape
    T = Bl * L
    depth = W["mix_norm"].shape[0]
    d_inner = W["m_w_out"].shape[1]
    heads = d_inner // M_HEADDIM
    conv_dim = W["m_conv_w"].shape[2]

    def mm(a, b, mode, out_dtype, *, name, add=None, out_stack=None):
        return _matmul(a, b, mode, out_dtype, name=name, add=add, out_stack=out_stack,
                       jobs=plan.take(name) if name in CALL_US else ())

    lb = _lb_fwd(W["hgrn_lb_logits"], name="hgrn_lower_bounds")
    h = x.reshape(T, D)
    saved = []
    for i in range(depth):
        j = i // 2
        s = {"h": h}
        u = _rms_fwd(h, W["mix_norm"][i], name="mix_norm_fwd")
        s["u"] = u
        if i % 2 == 0:
            P = mm(u, W["hgrn_w_in"][j], "nn", F32, name="hgrn_in_fwd")
            og, st = _hgrn_fwd(P, lb[j], W["hgrn_gnorm"][j], Bl, L, name="hgrn_scan_fwd", jobs=plan.take("hgrn_scan_fwd"))
            h = mm(og, W["hgrn_w_out"][j], "nn", F32, add=h, name="hgrn_out_fwd")
            s.update(P=P, og=og, st=st)
        else:
            w_in = W["m_w_in"][j]
            z = mm(u, w_in[:, :d_inner], "nn", F32, name="mamba_z_fwd")
            xbc = mm(u, w_in[:, d_inner:d_inner + conv_dim], "nn", F32, name="mamba_xbc_fwd")
            dt_raw = mm(u, w_in[:, d_inner + conv_dim:], "nn", F32, name="mamba_dt_fwd")
            xa = _mconv_fwd(xbc, W["m_conv_w"][j], W["m_conv_b"][j], Bl, L, name="mamba_conv_fwd")
            dtc, dtr, prm = _ssd_operands(dt_raw, W["m_dt_bias"][j], W["m_A_log"][j], W["m_D"][j])
            y, hs = _ssd_fwd(xa, dtc, dtr, prm, Bl, L, d_inner, name="mamba_scan_fwd", jobs=plan.take("mamba_scan_fwd"))
            yg = _mgate_fwd(y, z, W["m_norm"][j], name="mamba_gate_fwd")
            h = mm(yg, W["m_w_out"][j], "nn", F32, add=h, name="mamba_out_fwd")
            s.update(z=z, xbc=xbc, xa=xa, dtc=dtc, dtr=dtr, prm=prm, y=y, hs=hs, yg=yg)
        s["h2"] = h
        u2 = _rms_fwd(h, W["ffn_norm"][i], name="ffn_norm_fwd")
        a = mm(u2, W["f_w_up"][i], "nn", BF16, name="ffn_up_fwd")
        yf = _ffn_gate_fwd(a, W["f_conv_w"][i], W["f_conv_b"][i], Bl, L, name="ffn_gate_fwd", jobs=plan.take("ffn_gate_fwd"))
        h = mm(yf, W["f_w_down"][i], "nn", F32, add=h, name="ffn_down_fwd")
        s.update(u2=u2, a=a, yf=yf)
        saved.append(s)

    loss, dh, d_final = _loss_head(h, W["final_norm"], target.reshape(T, D), name="loss_head")

    g = {k: (_GradSink(k, W[k].shape[0], push) if isinstance(W[k], _LazyWeights) else [None] * W[k].shape[0])
         for k in W if k != "final_norm"}
    g["final_norm"] = d_final.reshape(D)
    dlb = [None] * W["hgrn_lb_logits"].shape[0]
    for i in reversed(range(depth)):
        j = i // 2
        s = saved[i]
        g["f_w_down"][i] = mm(s["yf"], dh, "tn", BF16, name="ffn_down_bwd_w")
        dyf = mm(dh, W["f_w_down"][i], "nt", BF16, name="ffn_down_bwd_x")
        da, pf = _ffn_gate_bwd(s["a"], W["f_conv_w"][i], W["f_conv_b"][i], dyf, Bl, L, name="ffn_gate_bwd",
                               jobs=plan.take("ffn_gate_bwd"))
        g["f_w_up"][i] = mm(s["u2"], da, "tn", BF16, out_stack=N_DEV, name="ffn_up_bwd_w")
        du2 = mm(da, W["f_w_up"][i], "nt", F32, name="ffn_up_bwd_x")
        g["f_conv_w"][i], g["f_conv_b"][i] = pf[:FFN_CONV], pf[FFN_CONV]
        dh, dn = _rms_bwd(s["h2"], W["ffn_norm"][i], du2, dh, name="ffn_norm_bwd")
        g["ffn_norm"][i] = dn.reshape(D)
        if i % 2 == 0:
            g["hgrn_w_out"][j] = mm(s["og"], dh, "tn", BF16, name="hgrn_out_bwd_w")
            dog = mm(dh, W["hgrn_w_out"][j], "nt", BF16, name="hgrn_out_bwd_x")
            dP4, dlb_j, dgn = _hgrn_bwd(s["P"], lb[j], W["hgrn_gnorm"][j], s["st"], dog, Bl, L, name="hgrn_scan_bwd",
                                        jobs=plan.take("hgrn_scan_bwd"))
            g["hgrn_w_in"][j] = mm(s["u"], dP4, "tn", BF16, out_stack=N_DEV, name="hgrn_in_bwd_w")
            du = mm(dP4, W["hgrn_w_in"][j], "nt", F32, name="hgrn_in_bwd_x")
            dlb[j] = dlb_j
            g["hgrn_gnorm"][j] = dgn.reshape(HGRN_HEAD)
        else:
            w_in = W["m_w_in"][j]
            g["m_w_out"][j] = mm(s["yg"], dh, "tn", BF16, name="mamba_out_bwd_w")
            dyg = mm(dh, W["m_w_out"][j], "nt", BF16, name="mamba_out_bwd_x")
            dy, dz, dnw = _mgate_bwd(s["y"], s["z"], W["m_norm"][j], dyg, name="mamba_gate_bwd")
            dxs, dB, dC, ddtc, ddtr, gbc, gbr, gac, gar, gdk = _ssd_bwd(
                s["xa"], s["dtc"], s["dtr"], s["prm"], s["hs"], dy, Bl, L, d_inner, name="mamba_scan_bwd",
                jobs=plan.take("mamba_scan_bwd"))
            ddt, g["m_dt_bias"][j], g["m_A_log"][j], g["m_D"][j] = _ssd_unpack(ddtc, ddtr, gbc, gbr, gac, gar, gdk, heads)
            dxa = jnp.concatenate([dxs, dB, dC], axis=1)
            dxbc, pm = _mconv_bwd(s["xbc"], W["m_conv_w"][j], W["m_conv_b"][j], dxa, Bl, L, name="mamba_conv_bwd")
            g["m_conv_w"][j], g["m_conv_b"][j], g["m_norm"][j] = pm[:M_CONV], pm[M_CONV], dnw.reshape(d_inner)
            ddt = ddt.astype(BF16)
            g["m_w_in"][j] = jnp.concatenate([
                mm(s["u"], dz, "tn", BF16, name="mamba_z_bwd_w"), mm(s["u"], dxbc, "tn", BF16, name="mamba_xbc_bwd_w"),
                mm(s["u"], ddt, "tn", BF16, name="mamba_dt_bwd_w")], axis=1)
            du = mm(dz, w_in[:, :d_inner], "nt", F32, name="mamba_z_bwd_x")
            du = mm(dxbc, w_in[:, d_inner:d_inner + conv_dim], "nt", F32, add=du, name="mamba_xbc_bwd_x")
            du = mm(ddt, w_in[:, d_inner + conv_dim:], "nt", F32, add=du, name="mamba_dt_bwd_x")
        dh, dn = _rms_bwd(s["h"], W["mix_norm"][i], du, dh, name="mix_norm_bwd")
        g["mix_norm"][i] = dn.reshape(D)
    g["hgrn_lb_logits"] = _lb_bwd(W["hgrn_lb_logits"], jnp.concatenate(dlb, axis=0), name="hgrn_lower_bounds_bwd")
    grads = {k: (jnp.stack(v) if isinstance(v, list) else v) for k, v in g.items() if not isinstance(v, _GradSink)}
    return loss, dh.reshape(Bl, L, D), grads


WEIGHTS = ("mix_norm", "ffn_norm", "final_norm", "hgrn_w_in", "hgrn_lb_logits", "hgrn_gnorm", "hgrn_w_out", "m_w_in",
           "m_conv_w", "m_conv_b", "m_dt_bias", "m_A_log", "m_D", "m_norm", "m_w_out", "f_w_up", "f_conv_w", "f_conv_b",
           "f_w_down")
COL_SHARDED = ("hgrn_w_in", "m_w_in", "f_w_up")
STACKED = ("hgrn_w_in", "f_w_up")
ROW_SHARDED = ("hgrn_w_out", "m_w_out", "f_w_down")
SMALL_SHARDED = ("m_conv_w", "m_conv_b", "m_norm", "f_conv_w")
SMALL = tuple(k for k in WEIGHTS if k not in COL_SHARDED + ROW_SHARDED)


def _join_last(g):
    nd = g.ndim
    t = g.transpose(tuple(range(1, nd - 1)) + (0, nd - 1))
    return t.reshape(t.shape[:-2] + (N_DEV * g.shape[-1],))


def kernel(x, mix_norm, ffn_norm, final_norm, hgrn_w_in, hgrn_lb_logits, hgrn_gnorm, hgrn_w_out, m_w_in, m_conv_w, m_conv_b, m_dt_bias, m_A_log, m_D, m_norm, m_w_out, f_w_up, f_conv_w, f_conv_b, f_w_down, loss_target, m_mix_norm, m_ffn_norm, m_final_norm, m_hgrn_w_in, m_hgrn_lb_logits, m_hgrn_gnorm, m_hgrn_w_out, m_m_w_in, m_m_conv_w, m_m_conv_b, m_m_dt_bias, m_m_A_log, m_m_D, m_m_norm, m_m_w_out, m_f_w_up, m_f_conv_w, m_f_conv_b, m_f_w_down, v_mix_norm, v_ffn_norm, v_final_norm, v_hgrn_w_in, v_hgrn_lb_logits, v_hgrn_gnorm, v_hgrn_w_out, v_m_w_in, v_m_conv_w, v_m_conv_b, v_m_dt_bias, v_m_A_log, v_m_D, v_m_norm, v_m_w_out, v_f_w_up, v_f_conv_w, v_f_conv_b, v_f_w_down):
    w = dict(zip(WEIGHTS, (mix_norm, ffn_norm, final_norm, hgrn_w_in, hgrn_lb_logits, hgrn_gnorm, hgrn_w_out, m_w_in, m_conv_w, m_conv_b, m_dt_bias, m_A_log, m_D, m_norm, m_w_out, f_w_up, f_conv_w, f_conv_b, f_w_down)))
    m = dict(zip(WEIGHTS, (m_mix_norm, m_ffn_norm, m_final_norm, m_hgrn_w_in, m_hgrn_lb_logits, m_hgrn_gnorm, m_hgrn_w_out, m_m_w_in, m_m_conv_w, m_m_conv_b, m_m_dt_bias, m_m_A_log, m_m_D, m_m_norm, m_m_w_out, m_f_w_up, m_f_conv_w, m_f_conv_b, m_f_w_down)))
    v = dict(zip(WEIGHTS, (v_mix_norm, v_ffn_norm, v_final_norm, v_hgrn_w_in, v_hgrn_lb_logits, v_hgrn_gnorm, v_hgrn_w_out, v_m_w_in, v_m_conv_w, v_m_conv_b, v_m_dt_bias, v_m_A_log, v_m_D, v_m_norm, v_m_w_out, v_f_w_up, v_f_conv_w, v_f_conv_b, v_f_w_down)))
    me = 4 * lax.axis_index("x") + 2 * lax.axis_index("y") + lax.axis_index("c")

    def gather_now(a, *, name):
        buf = _Buf(a.shape[0], a.shape[1], a.dtype)
        _exchange([_Job(a, "gather", buf)], name=name)
        return buf.arr

    plan = _Plan()
    depth = mix_norm.shape[0]
    use_order = []
    for i in range(depth):
        use_order += [("hgrn_w_in", i // 2), ("hgrn_w_out", i // 2)] if i % 2 == 0 else [("m_w_in", i // 2), ("m_w_out", i // 2)]
        use_order += [("f_w_up", i), ("f_w_down", i)]
    gathers = {}
    for k, layer in use_order:
        shard = w[k][layer].astype(BF16)
        gathers[k, layer] = _Buf(shard.shape[0], shard.shape[1], BF16)
        plan.push(shard, "gather", gathers[k, layer])

    def fetch(k, layer):
        g = plan.finish(gathers[k, layer], name=f"gather_{k}")
        if k in STACKED:
            return g
        if k in COL_SHARDED:
            return g.transpose(1, 0, 2).reshape(g.shape[1], N_DEV * g.shape[2])
        return g.reshape(N_DEV * g.shape[1], g.shape[2])

    full = {k: w[k] for k in SMALL if k not in SMALL_SHARDED}
    for k in COL_SHARDED:
        full[k] = _LazyWeights(k, (w[k].shape[0], w[k].shape[1], N_DEV * w[k].shape[2]), fetch)
    for k in ROW_SHARDED:
        full[k] = _LazyWeights(k, (w[k].shape[0], N_DEV * w[k].shape[1], w[k].shape[2]), fetch)
    shard_shapes = [w[k].shape for k in SMALL_SHARDED]
    gathered = gather_now(_pack([w[k] for k in SMALL_SHARDED]), name="gather_small_params")
    for k, g in zip(SMALL_SHARDED, _unpack(gathered, shard_shapes, lead=(N_DEV,))):
        full[k] = _join_last(g)

    lands = {k: _Buf(math.prod(w[k].shape[:-1]), w[k].shape[-1], BF16) for k in COL_SHARDED + ROW_SHARDED}

    def push(k, layer, g):
        if k in STACKED:
            parts = g
        elif k in COL_SHARDED:
            parts = g.reshape(g.shape[0], N_DEV, g.shape[1] // N_DEV).transpose(1, 0, 2)
        else:
            parts = g.reshape(N_DEV, g.shape[0] // N_DEV, g.shape[1])
        plan.push(parts, "scatter", lands[k], q0=layer * parts.shape[1])

    loss, grad_x, grads = _local_step(x, loss_target, full, plan, push)

    out = {}
    for k in ("m_w_out", "m_w_in", "f_w_down", "f_w_up", "hgrn_w_out", "hgrn_w_in"):
        land = plan.finish(lands[k], name="scatter_" + k)
        rows, n = math.prod(w[k].shape[:-1]), w[k].shape[-1]
        res = _adamw_sharded(land, w[k].reshape(rows, n), m[k].reshape(rows, n), v[k].reshape(rows, n),
                             name="adamw_" + k, jobs=plan.take("adamw_" + k))
        out[k] = tuple(r.reshape(w[k].shape) for r in res)

    small_full_shapes = [(1, 1)] + [grads[k].shape for k in SMALL]
    packed = _pack([loss] + [grads[k] for k in SMALL])
    summed = _sum_devices(gather_now(packed, name="gather_small_grads"), name="sum_small_grads")
    parts = _unpack(summed, small_full_shapes)
    loss_all, small_g = parts[0], dict(zip(SMALL, parts[1:]))
    for k in SMALL_SHARDED:
        n = w[k].shape[-1]
        small_g[k] = lax.dynamic_slice_in_dim(small_g[k], me * n, n, axis=small_g[k].ndim - 1)
    shapes = [w[k].shape for k in SMALL]
    res = _adamw_packed(_pack([small_g[k] for k in SMALL]), _pack([w[k] for k in SMALL]), _pack([m[k] for k in SMALL]),
                        _pack([v[k] for k in SMALL]), name="adamw_small")
    res = [_unpack(r, shapes) for r in res]
    for i, k in enumerate(SMALL):
        out[k] = (small_g[k], res[0][i], res[1][i], res[2][i])

    return (loss_all.reshape(()), grad_x, *[out[k][0] for k in WEIGHTS], *[out[k][1] for k in WEIGHTS],
            *[out[k][2] for k in WEIGHTS], *[out[k][3] for k in WEIGHTS])
```

```python
import functools
import math

import jax
import jax.numpy as jnp
from jax import lax
from jax.experimental import pallas as pl
from jax.experimental.pallas import tpu as pltpu

F32 = jnp.float32
BF16 = jnp.bfloat16
HIGHEST = lax.Precision.HIGHEST

NORM_EPS = 1e-5
ADAM_LR, ADAM_B1, ADAM_B2, ADAM_EPS, ADAM_WD, ADAM_STEP = 0.001, 0.9, 0.999, 1e-08, 0.01, 10

N_DEV = 8
LANES = 128
V7X_VMEM_BYTES = 64 * 1024 * 1024
VMEM_LIMIT = V7X_VMEM_BYTES * 3 // 4

HGRN_HEAD = 128
HGRN_CHUNK = 128
HGRN_SUB = 16
M_HEADDIM = 64
M_GROUPS = 8
M_D_STATE = 128
M_CONV = 4
M_CHUNK = 256
FFN_CONV = 3
CONV_COLS = 256
HGRN_PAR = 2


def _params(dims=None, **kw):
    return pltpu.CompilerParams(dimension_semantics=dims, vmem_limit_bytes=VMEM_LIMIT, **kw)


def _tile(dim, target, align=LANES):
    t = (min(target, dim) // align) * align
    while t >= align:
        if dim % t == 0:
            return t
        if 2 * t < target and dim <= 2 * target:
            return dim
        t -= align
    return dim


class _Buf:
    def __init__(self, rows, n, dtype):
        self.arr = lax.empty((N_DEV, rows, n), dtype)


SIBLING = 1
OTHER_CHIPS = (2, 4, 6)


class _Job:
    def __init__(self, src, mode, buf, r0=0, r1=None, q0=0):
        rows = src.shape[-2]
        self.src, self.mode, self.buf, self.r0, self.r1, self.q0 = src, mode, buf, r0, rows if r1 is None else r1, q0


def _job_copies(job, src_ref, out_ref, send_sems, recv_sems, local_sem):
    x, y, c = lax.axis_index("x"), lax.axis_index("y"), lax.axis_index("c")
    me = 4 * x + 2 * y + c
    n = job.r1 - job.r0
    rows, land = pl.ds(job.r0, n), pl.ds(job.q0, n)

    def peer_of(k):
        kx, ky, kc = (k >> 2) & 1, (k >> 1) & 1, k & 1
        px, py, pc = (1 - x if kx else x), (1 - y if ky else y), (1 - c if kc else c)
        return 4 * px + 2 * py + pc, (px, py, pc)

    pairs = []
    if job.mode == "forward":
        sib, sib_id = peer_of(SIBLING)
        for i, k in enumerate(OTHER_CHIPS):
            mine, _ = peer_of(k)
            theirs, _ = peer_of(k ^ SIBLING)
            sems = dict(send_sem=send_sems.at[i], recv_sem=recv_sems.at[i], device_id=sib_id, device_id_type=pl.DeviceIdType.MESH)
            send = pltpu.make_async_remote_copy(src_ref=out_ref.at[mine, land], dst_ref=out_ref.at[mine, land], **sems)
            landing = pltpu.make_async_remote_copy(src_ref=out_ref.at[mine, land], dst_ref=out_ref.at[theirs, land], **sems)
            pairs.append((send, landing))
        return None, pairs
    part = (lambda d: src_ref.at[d, rows]) if job.mode == "scatter" else (lambda d: src_ref.at[rows])
    local = pltpu.make_async_copy(part(me), out_ref.at[me, land], local_sem)
    for k in ((SIBLING,) + OTHER_CHIPS if job.mode == "gather_chip" else range(1, N_DEV)):
        peer, peer_id = peer_of(k)
        sems = dict(send_sem=send_sems.at[k - 1], recv_sem=recv_sems.at[k - 1], device_id=peer_id,
                    device_id_type=pl.DeviceIdType.MESH)
        send = pltpu.make_async_remote_copy(src_ref=part(peer), dst_ref=out_ref.at[me, land], **sems)
        landing = pltpu.make_async_remote_copy(src_ref=part(me), dst_ref=out_ref.at[peer, land], **sems)
        pairs.append((send, landing))
    return local, pairs


def _jobs_start(jobs, src_refs, out_refs, send_sems, recv_sems, local_sems):
    for i, job in enumerate(jobs):
        local, pairs = _job_copies(job, src_refs[i], out_refs[i], send_sems.at[i], recv_sems.at[i], local_sems.at[i])
        if local is not None:
            local.start()
        for send, _ in pairs:
            send.start()


def _jobs_wait(jobs, src_refs, out_refs, send_sems, recv_sems, local_sems):
    for i, job in enumerate(jobs):
        local, pairs = _job_copies(job, src_refs[i], out_refs[i], send_sems.at[i], recv_sems.at[i], local_sems.at[i])
        for _, landing in pairs:
            landing.wait_recv()
        for send, _ in pairs:
            send.wait_send()
        if local is not None:
            local.wait()


def _job_scratch(n):
    return [pltpu.SemaphoreType.DMA((n, N_DEV - 1)), pltpu.SemaphoreType.DMA((n, N_DEV - 1)), pltpu.SemaphoreType.DMA((n,))]


def _buf_shapes(jobs):
    assert len({id(j.buf) for j in jobs}) == len(jobs), "one job per buffer and call"
    return tuple(jax.ShapeDtypeStruct(j.buf.arr.shape, j.buf.arr.dtype) for j in jobs)


def _exchange(jobs, *, name):
    n = len(jobs)

    def body(*refs):
        srcs, dsts, sems = refs[:n], refs[2 * n:3 * n], refs[3 * n:]
        _jobs_start(jobs, srcs, dsts, *sems)
        _jobs_wait(jobs, srcs, dsts, *sems)

    hbm = pl.BlockSpec(memory_space=pl.ANY)
    outs = pl.pallas_call(
        body, name=name, out_shape=_buf_shapes(jobs), in_specs=[hbm] * (2 * n), out_specs=(hbm,) * n,
        input_output_aliases={n + i: i for i in range(n)},
        scratch_shapes=_job_scratch(n), compiler_params=pltpu.CompilerParams(has_side_effects=True),
    )(*[j.src for j in jobs], *[j.buf.arr for j in jobs])
    for j, o in zip(jobs, outs):
        j.buf.arr = o


def _call(body, *, name, out_shape, grid, in_specs, out_specs, scratch_shapes=(), dims, jobs=()):
    single = not isinstance(out_shape, (tuple, list))
    out_shape = (out_shape,) if single else tuple(out_shape)
    out_specs = (out_specs,) if single else tuple(out_specs)
    n_in, n_out, n_scr, nj = len(in_specs), len(out_shape), len(scratch_shapes), len(jobs)
    if not jobs:
        plain = pl.pallas_call(body, name=name, out_shape=out_shape, grid=grid, in_specs=list(in_specs), out_specs=out_specs,
                               scratch_shapes=list(scratch_shapes), compiler_params=_params(dims))

        def run_plain(*args):
            r = plain(*args)
            return r[0] if single else tuple(r)

        return run_plain

    def carrying(*refs):
        ins, srcs = refs[:n_in], refs[n_in:n_in + nj]
        o0 = n_in + 2 * nj
        outs, dsts = refs[o0:o0 + n_out], refs[o0 + n_out:o0 + n_out + nj]
        scr, sems = refs[o0 + n_out + nj:o0 + n_out + nj + n_scr], refs[o0 + n_out + nj + n_scr:]
        ids = [pl.program_id(d) for d in range(len(grid))]
        first = functools.reduce(jnp.logical_and, [i == 0 for i in ids])
        last = functools.reduce(jnp.logical_and, [i == g - 1 for i, g in zip(ids, grid)])

        @pl.when(first)
        def _():
            _jobs_start(jobs, srcs, dsts, *sems)

        body(*ins, *outs, *scr)

        @pl.when(last)
        def _():
            _jobs_wait(jobs, srcs, dsts, *sems)

    hbm = pl.BlockSpec(memory_space=pl.ANY)
    call = pl.pallas_call(
        carrying, name=name, out_shape=out_shape + _buf_shapes(jobs), grid=grid,
        in_specs=list(in_specs) + [hbm] * (2 * nj), out_specs=out_specs + (hbm,) * nj,
        input_output_aliases={n_in + nj + i: n_out + i for i in range(nj)},
        scratch_shapes=list(scratch_shapes) + _job_scratch(nj),
        compiler_params=_params(("arbitrary",) * len(grid), has_side_effects=True))

    def run(*args):
        r = call(*args, *[j.src for j in jobs], *[j.buf.arr for j in jobs])
        for j, o in zip(jobs, r[n_out:]):
            j.buf.arr = o
        return r[0] if single else tuple(r[:n_out])

    return run


def _dg(a, b, ca, cb):
    return lax.dot_general(a.astype(BF16), b.astype(BF16), (((ca,), (cb,)), ((), ())), preferred_element_type=F32)


@jax.custom_vjp
def _dot_nn(a, b):
    return _dg(a, b, 1, 0)


def _dot_nn_f(a, b):
    return _dg(a, b, 1, 0), (a, b)


def _dot_nn_b(res, g):
    a, b = res
    return _dg(g, b, 1, 1), _dg(a, g, 0, 0)


_dot_nn.defvjp(_dot_nn_f, _dot_nn_b)


@jax.custom_vjp
def _dot_nt(a, b):
    return _dg(a, b, 1, 1)


def _dot_nt_f(a, b):
    return _dg(a, b, 1, 1), (a, b)


def _dot_nt_b(res, g):
    a, b = res
    return _dg(g, b, 1, 0), _dg(g, a, 0, 0)


_dot_nt.defvjp(_dot_nt_f, _dot_nt_b)


@jax.custom_vjp
def _dot_tn(a, b):
    return _dg(a, b, 0, 0)


def _dot_tn_f(a, b):
    return _dg(a, b, 0, 0), (a, b)


def _dot_tn_b(res, g):
    a, b = res
    return _dg(b, g, 1, 1), _dg(a, g, 1, 0)


_dot_tn.defvjp(_dot_tn_f, _dot_tn_b)


def _tri(n):
    return (lax.broadcasted_iota(jnp.int32, (n, n), 0) >= lax.broadcasted_iota(jnp.int32, (n, n), 1)).astype(F32)


def _silu(x):
    return x * jax.nn.sigmoid(x)


def _matmul(a, b, mode, out_dtype, *, name, add=None, out_stack=None, tm=1024, tn=1024, tk=2048, jobs=()):
    a_n = a.shape[2] if a.ndim == 3 else None
    b_n = b.shape[2] if b.ndim == 3 else None
    a_shape = (a.shape[1], a.shape[0] * a.shape[2]) if a_n else a.shape
    b_shape = (b.shape[1], b.shape[0] * b.shape[2]) if b_n else b.shape
    if mode == "nn":
        (M, K), (K2, N) = a_shape, b_shape
    elif mode == "nt":
        (M, K), (N, K2) = a_shape, b_shape
    else:
        (K, M), (K2, N) = a_shape, b_shape
    assert K == K2, (a.shape, b.shape, mode)
    n_lim = {"m": M, "n": N, "k": K}
    for dim, n in (("m" if mode == "tn" else "k", a_n), ("k" if mode == "nt" else "n", b_n), ("n", N // out_stack if out_stack else None)):
        if n:
            n_lim[dim] = math.gcd(n_lim[dim], n)
    tm, tn, tk = _tile(n_lim["m"], tm, 8 if M % LANES else LANES), _tile(n_lim["n"], tn), _tile(n_lim["k"], tk)
    nk = K // tk
    ca, cb = {"nn": (1, 0), "nt": (1, 1), "tn": (0, 0)}[mode]

    def body(*refs):
        a_ref, b_ref = refs[:2]
        add_ref = refs[2] if add is not None else None
        o_ref = refs[2 + (add is not None)]

        def finish(r):
            if add is not None:
                r = r + add_ref[...]
            o_ref[...] = r.astype(o_ref.dtype)

        if nk == 1:
            finish(_dg(a_ref[...], b_ref[...], ca, cb))
            return
        acc_ref = refs[-1]
        k = pl.program_id(2)

        @pl.when(k == 0)
        def _():
            acc_ref[...] = _dg(a_ref[...], b_ref[...], ca, cb)

        @pl.when((k > 0) & (k < nk - 1))
        def _():
            acc_ref[...] += _dg(a_ref[...], b_ref[...], ca, cb)

        @pl.when(k == nk - 1)
        def _():
            finish(acc_ref[...] + _dg(a_ref[...], b_ref[...], ca, cb))

    def spec(rows, cols, t_rows, t_cols, n):
        if not n:
            return pl.BlockSpec((t_rows, t_cols), lambda i, j, k: (rows(i, j, k), cols(i, j, k)))
        per = n // t_cols
        return pl.BlockSpec((None, t_rows, t_cols), lambda i, j, k: (cols(i, j, k) // per, rows(i, j, k), cols(i, j, k) % per))

    gi, gj, gk = (lambda i, j, k: i), (lambda i, j, k: j), (lambda i, j, k: k)
    a_spec = spec(gk, gi, tk, tm, a_n) if mode == "tn" else spec(gi, gk, tm, tk, a_n)
    b_spec = spec(gj, gk, tn, tk, b_n) if mode == "nt" else spec(gk, gj, tk, tn, b_n)
    o_spec = spec(gi, gj, tm, tn, N // out_stack if out_stack else None)
    out_shape = (out_stack, M, N // out_stack) if out_stack else (M, N)
    in_specs, args = [a_spec, b_spec], [a, b]
    if add is not None:
        assert not out_stack
        in_specs.append(o_spec)
        args.append(add)
    return _call(
        body, name=name, out_shape=jax.ShapeDtypeStruct(out_shape, out_dtype), grid=(M // tm, N // tn, nk),
        in_specs=in_specs, out_specs=o_spec, scratch_shapes=[pltpu.VMEM((tm, tn), F32)] if nk > 1 else [],
        dims=("parallel", "parallel", "arbitrary"), jobs=jobs,
    )(*args)


def _rms_fwd(h, w, *, name):
    T, D = h.shape
    tr = _tile(T, 256, 8)

    def body(h_ref, w_ref, u_ref):
        x = h_ref[...]
        u_ref[...] = (x * lax.rsqrt(jnp.mean(x * x, axis=-1, keepdims=True) + NORM_EPS) * w_ref[...]).astype(u_ref.dtype)

    return pl.pallas_call(
        body, name=name, out_shape=jax.ShapeDtypeStruct((T, D), BF16), grid=(T // tr,),
        in_specs=[pl.BlockSpec((tr, D), lambda i: (i, 0)), pl.BlockSpec((1, D), lambda i: (0, 0))],
        out_specs=pl.BlockSpec((tr, D), lambda i: (i, 0)), compiler_params=_params(("parallel",)),
    )(h, w.reshape(1, D))


def _rms_bwd(h, w, du, dh_in, *, name):
    T, D = h.shape
    tr = _tile(T, 256, 8)

    def body(h_ref, w_ref, du_ref, dhin_ref, dh_ref, dw_ref):
        x = h_ref[...]
        g = du_ref[...].astype(F32)
        rstd = lax.rsqrt(jnp.mean(x * x, axis=-1, keepdims=True) + NORM_EPS)
        xhat = x * rstd
        gx = g * w_ref[...]
        dh_ref[...] = dhin_ref[...] + rstd * (gx - xhat * jnp.mean(gx * xhat, axis=-1, keepdims=True))

        @pl.when(pl.program_id(0) == 0)
        def _():
            dw_ref[...] = jnp.zeros_like(dw_ref)

        dw_ref[...] += jnp.sum(g * xhat, axis=0, keepdims=True)

    row = pl.BlockSpec((tr, D), lambda i: (i, 0))
    vec = pl.BlockSpec((1, D), lambda i: (0, 0))
    return pl.pallas_call(
        body, name=name, out_shape=(jax.ShapeDtypeStruct((T, D), F32), jax.ShapeDtypeStruct((1, D), F32)), grid=(T // tr,),
        in_specs=[row, vec, row, row], out_specs=(row, vec), compiler_params=_params(("arbitrary",)),
    )(h, w.reshape(1, D), du, dh_in)


def _loss_head(h, w, target, *, name):
    T, D = h.shape
    tr = _tile(T, 256, 8)

    def body(h_ref, w_ref, t_ref, loss_ref, dh_ref, dw_ref):
        x = h_ref[...]
        rstd = lax.rsqrt(jnp.mean(x * x, axis=-1, keepdims=True) + NORM_EPS)
        xhat = x * rstd
        err = xhat * w_ref[...] - t_ref[...]
        g = err * (1.0 / D)
        gx = g * w_ref[...]
        dh_ref[...] = rstd * (gx - xhat * jnp.mean(gx * xhat, axis=-1, keepdims=True))

        @pl.when(pl.program_id(0) == 0)
        def _():
            dw_ref[...] = jnp.zeros_like(dw_ref)
            loss_ref[...] = jnp.zeros_like(loss_ref)

        dw_ref[...] += jnp.sum(g * xhat, axis=0, keepdims=True)
        loss_ref[...] += (0.5 / D) * jnp.sum(jnp.sum(err * err, axis=-1, keepdims=True), axis=0, keepdims=True)

    row = pl.BlockSpec((tr, D), lambda i: (i, 0))
    vec = pl.BlockSpec((1, D), lambda i: (0, 0))
    one = pl.BlockSpec((1, 1), lambda i: (0, 0))
    return pl.pallas_call(
        body, name=name,
        out_shape=(jax.ShapeDtypeStruct((1, 1), F32), jax.ShapeDtypeStruct((T, D), F32), jax.ShapeDtypeStruct((1, D), F32)),
        grid=(T // tr,), in_specs=[row, vec, row], out_specs=(one, row, vec), compiler_params=_params(("arbitrary",)),
    )(h, w.reshape(1, D), target)


def _shift_down(x, s):
    if s == 0:
        return x
    rows = lax.broadcasted_iota(jnp.int32, x.shape, 0)
    return jnp.where(rows >= s, pltpu.roll(x, s, 0), 0.0)


def _shift_up(x, s):
    if s == 0:
        return x
    n = x.shape[0]
    rows = lax.broadcasted_iota(jnp.int32, x.shape, 0)
    return jnp.where(rows < n - s, pltpu.roll(x, n - s, 0), 0.0)


def _conv_pre(a, w_ref, b_ref, taps):
    pre = b_ref[0:1, :] + w_ref[taps - 1:taps, :] * a
    for k in range(taps - 1):
        pre = pre + w_ref[k:k + 1, :] * _shift_down(a, taps - 1 - k)
    return pre


def _conv_bwd(a, dpre, w_ref, taps):
    da = w_ref[taps - 1:taps, :] * dpre
    rows = []
    for k in range(taps - 1):
        da = da + w_ref[k:k + 1, :] * _shift_up(dpre, taps - 1 - k)
        rows.append(jnp.sum(dpre * _shift_down(a, taps - 1 - k), axis=0, keepdims=True))
    rows.append(jnp.sum(dpre * a, axis=0, keepdims=True))
    rows.append(jnp.sum(dpre, axis=0, keepdims=True))
    return da, rows


def _ffn_gate_fwd(a, cw, cb, Bl, L, *, name, jobs=()):
    T, F2 = a.shape
    F = F2 // 2
    tc = _tile(F, CONV_COLS)
    nj = F // tc

    def body(ag_ref, au_ref, wg_ref, wu_ref, bg_ref, bu_ref, y_ref):
        hg = _conv_pre(ag_ref[...].astype(F32), wg_ref, bg_ref, FFN_CONV)
        hu = _conv_pre(au_ref[...].astype(F32), wu_ref, bu_ref, FFN_CONV)
        y_ref[...] = (_silu(hg) * hu).astype(y_ref.dtype)

    blk = lambda off: pl.BlockSpec((L, tc), lambda b, j: (b, j + off))
    wblk = lambda off: pl.BlockSpec((FFN_CONV, tc), lambda b, j: (0, j + off))
    bblk = lambda off: pl.BlockSpec((1, tc), lambda b, j: (0, j + off))
    return _call(
        body, name=name, out_shape=jax.ShapeDtypeStruct((T, F), BF16), grid=(Bl, nj),
        in_specs=[blk(0), blk(nj), wblk(0), wblk(nj), bblk(0), bblk(nj)], out_specs=blk(0),
        dims=("parallel", "parallel"), jobs=jobs,
    )(a, a, cw, cw, cb.reshape(1, F2), cb.reshape(1, F2))


def _ffn_gate_bwd(a, cw, cb, dy, Bl, L, *, name, jobs=()):
    T, F2 = a.shape
    F = F2 // 2
    tc = _tile(F, CONV_COLS)
    nj = F // tc

    def body(ag_ref, au_ref, wg_ref, wu_ref, bg_ref, bu_ref, dy_ref, da_ref, pg_ref, pu_ref):
        ag, au = ag_ref[...].astype(F32), au_ref[...].astype(F32)
        hg = _conv_pre(ag, wg_ref, bg_ref, FFN_CONV)
        hu = _conv_pre(au, wu_ref, bu_ref, FFN_CONV)
        g = dy_ref[...].astype(F32)
        s = jax.nn.sigmoid(hg)
        dhg = g * hu * (s * (1.0 + hg * (1.0 - s)))
        dhu = g * (hg * s)
        dag, rg = _conv_bwd(ag, dhg, wg_ref, FFN_CONV)
        dau, ru = _conv_bwd(au, dhu, wu_ref, FFN_CONV)
        da_ref[0] = dag.astype(da_ref.dtype)
        da_ref[1] = dau.astype(da_ref.dtype)

        @pl.when(pl.program_id(1) == 0)
        def _():
            pg_ref[...] = jnp.zeros_like(pg_ref)
            pu_ref[...] = jnp.zeros_like(pu_ref)

        for k in range(FFN_CONV + 1):
            pg_ref[k:k + 1, :] += rg[k]
            pu_ref[k:k + 1, :] += ru[k]

    blk = lambda off: pl.BlockSpec((L, tc), lambda j, b: (b, j + off))
    wblk = lambda off: pl.BlockSpec((FFN_CONV, tc), lambda j, b: (0, j + off))
    bblk = lambda off: pl.BlockSpec((1, tc), lambda j, b: (0, j + off))
    pblk = lambda off: pl.BlockSpec((8, tc), lambda j, b: (0, j + off))
    da, pg, pu = _call(
        body, name=name,
        out_shape=(jax.ShapeDtypeStruct((2, T, F), BF16), jax.ShapeDtypeStruct((8, F), F32), jax.ShapeDtypeStruct((8, F), F32)),
        grid=(nj, Bl),
        in_specs=[blk(0), blk(nj), wblk(0), wblk(nj), bblk(0), bblk(nj), blk(0)],
        out_specs=(pl.BlockSpec((2, L, tc), lambda j, b: (0, b, j)), pblk(0), pblk(0)),
        dims=("parallel", "arbitrary"), jobs=jobs,
    )(a, a, cw, cw, cb.reshape(1, F2), cb.reshape(1, F2), dy)
    return da, jnp.concatenate([pg, pu], axis=1)


def _mconv_fwd(xbc, cw, cb, Bl, L, *, name):
    T, W = xbc.shape
    tc = _tile(W, CONV_COLS)

    def body(a_ref, w_ref, b_ref, y_ref):
        y_ref[...] = _silu(_conv_pre(a_ref[...], w_ref, b_ref, M_CONV))

    blk = pl.BlockSpec((L, tc), lambda b, j: (b, j))
    return pl.pallas_call(
        body, name=name, out_shape=jax.ShapeDtypeStruct((T, W), F32), grid=(Bl, W // tc),
        in_specs=[blk, pl.BlockSpec((M_CONV, tc), lambda b, j: (0, j)), pl.BlockSpec((1, tc), lambda b, j: (0, j))],
        out_specs=blk, compiler_params=_params(("parallel", "parallel")),
    )(xbc, cw, cb.reshape(1, W))


def _mconv_bwd(xbc, cw, cb, dy, Bl, L, *, name):
    T, W = xbc.shape
    tc = _tile(W, CONV_COLS)

    def body(a_ref, w_ref, b_ref, dy_ref, da_ref, p_ref):
        a = a_ref[...]
        pre = _conv_pre(a, w_ref, b_ref, M_CONV)
        s = jax.nn.sigmoid(pre)
        dpre = dy_ref[...] * (s * (1.0 + pre * (1.0 - s)))
        da, rows = _conv_bwd(a, dpre, w_ref, M_CONV)
        da_ref[...] = da.astype(da_ref.dtype)

        @pl.when(pl.program_id(1) == 0)
        def _():
            p_ref[...] = jnp.zeros_like(p_ref)

        for k in range(M_CONV + 1):
            p_ref[k:k + 1, :] += rows[k]

    blk = pl.BlockSpec((L, tc), lambda j, b: (b, j))
    return pl.pallas_call(
        body, name=name, out_shape=(jax.ShapeDtypeStruct((T, W), BF16), jax.ShapeDtypeStruct((8, W), F32)),
        grid=(W // tc, Bl),
        in_specs=[blk, pl.BlockSpec((M_CONV, tc), lambda j, b: (0, j)), pl.BlockSpec((1, tc), lambda j, b: (0, j)), blk],
        out_specs=(blk, pl.BlockSpec((8, tc), lambda j, b: (0, j))),
        compiler_params=_params(("parallel", "arbitrary")),
    )(xbc, cw, cb.reshape(1, W), dy)


def _hgrn_chunk(st, qr, fr, v, gr, lb, gn):
    C = qr.shape[0]
    q = _silu(qr)
    logf = jnp.log(lb + (1.0 - lb) * jax.nn.sigmoid(fr))
    k = (1.0 - lb) * jax.nn.sigmoid(-fr)
    b = jnp.dot(_tri(C), logf, precision=HIGHEST, preferred_element_type=F32)
    o = _dot_nt(q * jnp.exp(b), st)
    keep = lax.broadcasted_iota(jnp.int32, (HGRN_SUB, HGRN_SUB), 0) >= lax.broadcasted_iota(jnp.int32, (HGRN_SUB, HGRN_SUB), 1)
    cols = lax.broadcasted_iota(jnp.int32, (HGRN_SUB, C), 1)
    parts = []
    for blk in range(C // HGRN_SUB):
        r0 = blk * HGRN_SUB
        bi, qi, ki, vi = b[r0:r0 + HGRN_SUB], q[r0:r0 + HGRN_SUB], k[r0:r0 + HGRN_SUB], v[r0:r0 + HGRN_SUB]
        decay = jnp.exp(jnp.minimum(bi[:, None, :] - bi[None, :, :], 0.0))
        a_diag = jnp.where(keep, jnp.sum(qi[:, None, :] * ki[None, :, :] * decay, axis=-1), 0.0)
        oi = _dot_nn(a_diag, vi)
        if blk > 0:
            bs = b[r0 - 1:r0]
            a_off = _dot_nt(qi * jnp.exp(bi - bs), k * jnp.exp(jnp.minimum(bs - b, 0.0)))
            oi = oi + _dot_nn(jnp.where(cols < r0, a_off, 0.0), v)
        parts.append(oi)
    o = o + jnp.concatenate(parts, axis=0)
    bl = b[C - 1:C]
    st1 = st * jnp.exp(bl) + _dot_tn(v, k * jnp.exp(bl - b))
    og = o * lax.rsqrt(jnp.mean(o * o, axis=-1, keepdims=True) + NORM_EPS) * gn * _silu(gr)
    return st1, og


def _hgrn_specs(Bl, L, H, order):
    W, HB = HGRN_PAR * HGRN_HEAD, H // HGRN_PAR

    def at(off):
        if order == "bh":
            return pl.BlockSpec((L, W), lambda b, h: (b, h + off))
        return pl.BlockSpec((L, W), lambda h, b: (b, h + off))
    return [at(0), at(HB), at(2 * HB), at(3 * HB)]


def _hgrn_fwd(P, lb, gn, Bl, L, *, name, jobs=()):
    T, D4 = P.shape
    D = D4 // 4
    H = D // HGRN_HEAD
    C = HGRN_CHUNK
    NC = L // C
    W, HB = HGRN_PAR * HGRN_HEAD, H // HGRN_PAR
    heads = [slice(p * HGRN_HEAD, (p + 1) * HGRN_HEAD) for p in range(HGRN_PAR)]

    def body(q_ref, f_ref, v_ref, g_ref, lb_ref, gn_ref, og_ref, st_ref):
        gnv = gn_ref[...]

        def step(c, sts):
            r = pl.ds(pl.multiple_of(c * C, C), C)
            new = []
            for p, hd in enumerate(heads):
                st_ref[p, c] = sts[p]
                st1, og = _hgrn_chunk(sts[p], q_ref[r, hd], f_ref[r, hd], v_ref[r, hd], g_ref[r, hd], lb_ref[:, hd], gnv)
                og_ref[r, hd] = og.astype(og_ref.dtype)
                new.append(st1)
            return tuple(new)

        lax.fori_loop(0, NC, step, tuple(jnp.zeros((HGRN_HEAD, HGRN_HEAD), F32) for _ in heads))

    return _call(
        body, name=name,
        out_shape=(jax.ShapeDtypeStruct((T, D), BF16), jax.ShapeDtypeStruct((Bl, H, NC, HGRN_HEAD, HGRN_HEAD), F32)),
        grid=(Bl, HB),
        in_specs=_hgrn_specs(Bl, L, H, "bh") + [pl.BlockSpec((1, W), lambda b, h: (0, h)), pl.BlockSpec((1, HGRN_HEAD), lambda b, h: (0, 0))],
        out_specs=(pl.BlockSpec((L, W), lambda b, h: (b, h)),
                   pl.BlockSpec((None, HGRN_PAR, NC, HGRN_HEAD, HGRN_HEAD), lambda b, h: (b, h, 0, 0, 0))),
        dims=("parallel", "parallel"), jobs=jobs,
    )(P, P, P, P, lb.reshape(1, D), gn.reshape(1, HGRN_HEAD))


def _hgrn_bwd(P, lb, gn, states, dog, Bl, L, *, name, jobs=()):
    T, D4 = P.shape
    D = D4 // 4
    H = D // HGRN_HEAD
    C = HGRN_CHUNK
    NC = L // C
    W, HB = HGRN_PAR * HGRN_HEAD, H // HGRN_PAR
    heads = [slice(p * HGRN_HEAD, (p + 1) * HGRN_HEAD) for p in range(HGRN_PAR)]

    def body(q_ref, f_ref, v_ref, g_ref, lb_ref, gn_ref, st_ref, dog_ref, dp_ref, dlb_ref, dgn_ref):
        gnv = gn_ref[...]

        @pl.when(pl.program_id(1) == 0)
        def _():
            dlb_ref[...] = jnp.zeros_like(dlb_ref)

        @pl.when((pl.program_id(0) == 0) & (pl.program_id(1) == 0))
        def _():
            dgn_ref[...] = jnp.zeros_like(dgn_ref)

        def step(i, dsts):
            c = NC - 1 - i
            r = pl.ds(pl.multiple_of(c * C, C), C)
            new, dgn_sum = [], jnp.zeros((1, HGRN_HEAD), F32)
            for p, hd in enumerate(heads):
                _, vjp = jax.vjp(_hgrn_chunk, st_ref[p, c], q_ref[r, hd], f_ref[r, hd], v_ref[r, hd], g_ref[r, hd],
                                 lb_ref[:, hd], gnv)
                dst0, dq, df, dv, dg, dlb, dgn = vjp((dsts[p], dog_ref[r, hd].astype(F32)))
                dp_ref[0, r, hd] = dq.astype(dp_ref.dtype)
                dp_ref[1, r, hd] = df.astype(dp_ref.dtype)
                dp_ref[2, r, hd] = dv.astype(dp_ref.dtype)
                dp_ref[3, r, hd] = dg.astype(dp_ref.dtype)
                dlb_ref[:, hd] += dlb
                dgn_sum = dgn_sum + dgn
                new.append(dst0)
            dgn_ref[...] += dgn_sum
            return tuple(new)

        lax.fori_loop(0, NC, step, tuple(jnp.zeros((HGRN_HEAD, HGRN_HEAD), F32) for _ in heads))

    return _call(
        body, name=name,
        out_shape=(jax.ShapeDtypeStruct((4, T, D), BF16), jax.ShapeDtypeStruct((1, D), F32), jax.ShapeDtypeStruct((1, HGRN_HEAD), F32)),
        grid=(HB, Bl),
        in_specs=_hgrn_specs(Bl, L, H, "hb") + [
            pl.BlockSpec((1, W), lambda h, b: (0, h)), pl.BlockSpec((1, HGRN_HEAD), lambda h, b: (0, 0)),
            pl.BlockSpec((None, HGRN_PAR, NC, HGRN_HEAD, HGRN_HEAD), lambda h, b: (b, h, 0, 0, 0)),
            pl.BlockSpec((L, W), lambda h, b: (b, h))],
        out_specs=(pl.BlockSpec((4, L, W), lambda h, b: (0, b, h)),
                   pl.BlockSpec((1, W), lambda h, b: (0, h)), pl.BlockSpec((1, HGRN_HEAD), lambda h, b: (0, 0))),
        dims=("arbitrary", "arbitrary"), jobs=jobs,
    )(P, P, P, P, lb.reshape(1, D), gn.reshape(1, HGRN_HEAD), states, dog)


def _lower_bounds(logits):
    p = jax.nn.softmax(logits, axis=0)
    rows, run = [], jnp.zeros_like(p[0:1])
    for i in range(logits.shape[0]):
        run = run + p[i:i + 1]
        rows.append(run - p[0:1])
    return jnp.concatenate(rows, axis=0) if len(rows) > 1 else rows[0]


def _lb_fwd(logits, *, name):
    def body(x_ref, o_ref):
        o_ref[...] = _lower_bounds(x_ref[...])
    return pl.pallas_call(body, name=name, out_shape=jax.ShapeDtypeStruct(logits.shape, F32))(logits)


def _lb_bwd(logits, dlb, *, name):
    def body(x_ref, g_ref, o_ref):
        _, vjp = jax.vjp(_lower_bounds, x_ref[...])
        o_ref[...] = vjp(g_ref[...])[0]
    return pl.pallas_call(body, name=name, out_shape=jax.ShapeDtypeStruct(logits.shape, F32))(logits, dlb)


def _ssd_pre(dtc_raw, dtr_raw, bias_c, bias_r, alog_c, alog_r):
    C = dtc_raw.shape[0]
    dt_c = jax.nn.softplus(dtc_raw + bias_c)
    dt_r = jax.nn.softplus(dtr_raw + bias_r)
    acs_c = jnp.dot(_tri(C), dt_c * -jnp.exp(alog_c), precision=HIGHEST, preferred_element_type=F32)
    acs_r = lax.dot_general(dt_r * -jnp.exp(alog_r), _tri(C), (((1,), (1,)), ((), ())), precision=HIGHEST, preferred_element_type=F32)
    return dt_c, acs_c, acs_r


def _ssd_head(h0, xs, G, Bm, Cm, dt, acs, acs_row, dskip):
    C = xs.shape[0]
    causal = lax.broadcasted_iota(jnp.int32, (C, C), 0) >= lax.broadcasted_iota(jnp.int32, (C, C), 1)
    Lm = jnp.where(causal, jnp.exp(jnp.minimum(acs - acs_row, 0.0)), 0.0)
    X = xs * dt
    y = _dot_nn(G * Lm, X) + _dot_nt(Cm * jnp.exp(acs), h0) + dskip * xs
    alast = acs[C - 1:C]
    h1 = jnp.exp(alast) * h0 + _dot_tn(X, Bm * jnp.exp(alast - acs))
    return y, h1


def _ssd_layout(Bl, L, d_inner, order):
    G = M_GROUPS
    hpg = d_inner // M_HEADDIM // G
    gw = hpg * M_HEADDIM
    ix = (lambda b, g: (b, g)) if order == "bg" else (lambda g, b: (b, g))
    def cols(width, off):
        if order == "bg":
            return pl.BlockSpec((L, width), lambda b, g: (b, g + off))
        return pl.BlockSpec((L, width), lambda g, b: (b, g + off))
    def per_group(shape):
        n = len(shape)
        if order == "bg":
            return pl.BlockSpec((None,) + shape, lambda b, g: (g,) + (0,) * n)
        return pl.BlockSpec((None,) + shape, lambda g, b: (g,) + (0,) * n)
    if order == "bg":
        dtc = pl.BlockSpec((None, L, LANES), lambda b, g: (g, b, 0))
        dtr = pl.BlockSpec((None, 8, L), lambda b, g: (g, 0, b))
    else:
        dtc = pl.BlockSpec((None, L, LANES), lambda g, b: (g, b, 0))
        dtr = pl.BlockSpec((None, 8, L), lambda g, b: (g, 0, b))
    nb = d_inner // M_D_STATE
    return hpg, gw, cols, per_group, dtc, dtr, nb


def _ssd_fwd(xa, dtc, dtr, prm, Bl, L, d_inner, *, name, jobs=()):
    T = xa.shape[0]
    C = M_CHUNK
    NC = L // C
    G = M_GROUPS
    hpg, gw, cols, per_group, dtc_spec, dtr_spec, nb = _ssd_layout(Bl, L, d_inner, "bg")

    def body(xs_ref, b_ref, c_ref, dtc_ref, dtr_ref, bc_ref, br_ref, ac_ref, ar_ref, dk_ref, y_ref, hs_ref,
             dt_s, acs_s, acr_s, h_s):
        h_s[...] = jnp.zeros_like(h_s)

        def step(c, carry):
            r = pl.ds(pl.multiple_of(c * C, C), C)
            dt_c, acs_c, acs_r = _ssd_pre(dtc_ref[r, :], dtr_ref[:, r], bc_ref[...], br_ref[...], ac_ref[...], ar_ref[...])
            dt_s[...] = dt_c
            acs_s[...] = acs_c
            acr_s[...] = acs_r
            Bm, Cm = b_ref[r, :], c_ref[r, :]
            Gm = _dot_nt(Cm, Bm)
            for j in range(hpg):
                pc = slice(j * M_HEADDIM, (j + 1) * M_HEADDIM)
                h0 = h_s[j]
                hs_ref[c, j] = h0
                y, h1 = _ssd_head(h0, xs_ref[r, pc], Gm, Bm, Cm, dt_s[:, j:j + 1], acs_s[:, j:j + 1], acr_s[j:j + 1, :],
                                  dk_ref[0:1, j:j + 1])
                y_ref[r, pc] = y
                h_s[j] = h1
            return carry

        lax.fori_loop(0, NC, step, 0)

    return _call(
        body, name=name,
        out_shape=(jax.ShapeDtypeStruct((T, d_inner), F32), jax.ShapeDtypeStruct((Bl, G, NC, hpg, M_HEADDIM, M_D_STATE), F32)),
        grid=(Bl, G),
        in_specs=[cols(gw, 0), cols(M_D_STATE, nb), cols(M_D_STATE, nb + G), dtc_spec, dtr_spec,
                  per_group((1, LANES)), per_group((8, 1)), per_group((1, LANES)), per_group((8, 1)), per_group((1, LANES))],
        out_specs=(cols(gw, 0), pl.BlockSpec((None, None, NC, hpg, M_HEADDIM, M_D_STATE), lambda b, g: (b, g, 0, 0, 0, 0))),
        scratch_shapes=[pltpu.VMEM((C, LANES), F32), pltpu.VMEM((C, LANES), F32), pltpu.VMEM((8, C), F32),
                        pltpu.VMEM((hpg, M_HEADDIM, M_D_STATE), F32)],
        dims=("parallel", "parallel"), jobs=jobs,
    )(xa, xa, xa, dtc, dtr, *prm)


def _ssd_bwd(xa, dtc, dtr, prm, states, dy, Bl, L, d_inner, *, name, jobs=()):
    T = xa.shape[0]
    C = M_CHUNK
    NC = L // C
    G = M_GROUPS
    hpg, gw, cols, per_group, dtc_spec, dtr_spec, nb = _ssd_layout(Bl, L, d_inner, "gb")

    def body(xs_ref, b_ref, c_ref, dtc_ref, dtr_ref, bc_ref, br_ref, ac_ref, ar_ref, dk_ref, hs_ref, dy_ref,
             dxs_ref, db_ref, dc_ref, ddtc_ref, ddtr_ref, gbc_ref, gbr_ref, gac_ref, gar_ref, gdk_ref,
             dt_s, acs_s, acr_s, ddt_s, dacs_s, dacr_s, dh_s):
        dh_s[...] = jnp.zeros_like(dh_s)

        @pl.when(pl.program_id(1) == 0)
        def _():
            for ref in (gbc_ref, gbr_ref, gac_ref, gar_ref, gdk_ref):
                ref[...] = jnp.zeros_like(ref)

        def step(i, carry):
            c = NC - 1 - i
            r = pl.ds(pl.multiple_of(c * C, C), C)
            pre_in = (dtc_ref[r, :], dtr_ref[:, r], bc_ref[...], br_ref[...], ac_ref[...], ar_ref[...])
            (dt_c, acs_c, acs_r), pre_vjp = jax.vjp(_ssd_pre, *pre_in)
            dt_s[...] = dt_c
            acs_s[...] = acs_c
            acr_s[...] = acs_r
            ddt_s[...] = jnp.zeros_like(ddt_s)
            dacs_s[...] = jnp.zeros_like(dacs_s)
            dacr_s[...] = jnp.zeros_like(dacr_s)
            Bm, Cm = b_ref[r, :], c_ref[r, :]
            Gm = _dot_nt(Cm, Bm)
            dG = jnp.zeros((C, C), F32)
            dB = jnp.zeros((C, M_D_STATE), F32)
            dC = jnp.zeros((C, M_D_STATE), F32)
            for j in range(hpg):
                pc = slice(j * M_HEADDIM, (j + 1) * M_HEADDIM)
                one = slice(j, j + 1)
                _, vjp = jax.vjp(_ssd_head, hs_ref[c, j], xs_ref[r, pc], Gm, Bm, Cm, dt_s[:, one], acs_s[:, one], acr_s[one, :],
                                 dk_ref[0:1, one])
                dh0, dxs, dGj, dBj, dCj, ddt, dacs, dacr, ddk = vjp((dy_ref[r, pc], dh_s[j]))
                dh_s[j] = dh0
                dxs_ref[r, pc] = dxs
                dG, dB, dC = dG + dGj, dB + dBj, dC + dCj
                ddt_s[:, one] = ddt
                dacs_s[:, one] = dacs
                dacr_s[one, :] = dacr
                gdk_ref[0:1, one] += ddk
            db_ref[r, :] = dB + _dot_tn(dG, Cm)
            dc_ref[r, :] = dC + _dot_nn(dG, Bm)
            ddtc, ddtr, gbc, gbr, gac, gar = pre_vjp((ddt_s[...], dacs_s[...], dacr_s[...]))
            ddtc_ref[r, :] = ddtc
            ddtr_ref[:, r] = ddtr
            gbc_ref[...] += gbc
            gbr_ref[...] += gbr
            gac_ref[...] += gac
            gar_ref[...] += gar
            return carry

        lax.fori_loop(0, NC, step, 0)

    GN = G * M_D_STATE
    bc_cols = lambda off: pl.BlockSpec((L, M_D_STATE), lambda g, b: (b, g + off))
    outs = _call(
        body, name=name,
        out_shape=(jax.ShapeDtypeStruct((T, d_inner), F32), jax.ShapeDtypeStruct((T, GN), F32), jax.ShapeDtypeStruct((T, GN), F32),
                   jax.ShapeDtypeStruct(dtc.shape, F32), jax.ShapeDtypeStruct(dtr.shape, F32),
                   jax.ShapeDtypeStruct((G, 1, LANES), F32), jax.ShapeDtypeStruct((G, 8, 1), F32),
                   jax.ShapeDtypeStruct((G, 1, LANES), F32), jax.ShapeDtypeStruct((G, 8, 1), F32),
                   jax.ShapeDtypeStruct((G, 1, LANES), F32)),
        grid=(G, Bl),
        in_specs=[cols(gw, 0), cols(M_D_STATE, nb), cols(M_D_STATE, nb + G), dtc_spec, dtr_spec,
                  per_group((1, LANES)), per_group((8, 1)), per_group((1, LANES)), per_group((8, 1)), per_group((1, LANES)),
                  pl.BlockSpec((None, None, NC, hpg, M_HEADDIM, M_D_STATE), lambda g, b: (b, g, 0, 0, 0, 0)), cols(gw, 0)],
        out_specs=(cols(gw, 0), bc_cols(0), bc_cols(0), dtc_spec, dtr_spec,
                   per_group((1, LANES)), per_group((8, 1)), per_group((1, LANES)), per_group((8, 1)), per_group((1, LANES))),
        scratch_shapes=[pltpu.VMEM((C, LANES), F32), pltpu.VMEM((C, LANES), F32), pltpu.VMEM((8, C), F32),
                        pltpu.VMEM((C, LANES), F32), pltpu.VMEM((C, LANES), F32), pltpu.VMEM((8, C), F32),
                        pltpu.VMEM((hpg, M_HEADDIM, M_D_STATE), F32)],
        dims=("arbitrary", "arbitrary"), jobs=jobs,
    )(xa, xa, xa, dtc, dtr, *prm, states, dy)
    return outs


def _mgate(y, z, w):
    t = y * _silu(z)
    return t * lax.rsqrt(jnp.mean(t * t, axis=-1, keepdims=True) + NORM_EPS) * w


def _mgate_fwd(y, z, w, *, name):
    T, d_inner = y.shape
    gw = d_inner // M_GROUPS
    tr = _tile(T, 512, 8)

    def body(y_ref, z_ref, w_ref, o_ref):
        o_ref[...] = _mgate(y_ref[...], z_ref[...], w_ref[...]).astype(o_ref.dtype)

    blk = pl.BlockSpec((tr, gw), lambda i, g: (i, g))
    return pl.pallas_call(
        body, name=name, out_shape=jax.ShapeDtypeStruct((T, d_inner), BF16), grid=(T // tr, M_GROUPS),
        in_specs=[blk, blk, pl.BlockSpec((1, gw), lambda i, g: (0, g))], out_specs=blk,
        compiler_params=_params(("parallel", "parallel")),
    )(y, z, w.reshape(1, d_inner))


def _mgate_bwd(y, z, w, dyg, *, name):
    T, d_inner = y.shape
    gw = d_inner // M_GROUPS
    tr = _tile(T, 512, 8)

    def body(y_ref, z_ref, w_ref, g_ref, dy_ref, dz_ref, dw_ref):
        _, vjp = jax.vjp(_mgate, y_ref[...], z_ref[...], w_ref[...])
        dy, dz, dw = vjp(g_ref[...].astype(F32))
        dy_ref[...] = dy
        dz_ref[...] = dz.astype(dz_ref.dtype)

        @pl.when(pl.program_id(1) == 0)
        def _():
            dw_ref[...] = jnp.zeros_like(dw_ref)

        dw_ref[...] += dw

    blk = pl.BlockSpec((tr, gw), lambda g, i: (i, g))
    vec = pl.BlockSpec((1, gw), lambda g, i: (0, g))
    return pl.pallas_call(
        body, name=name,
        out_shape=(jax.ShapeDtypeStruct((T, d_inner), F32), jax.ShapeDtypeStruct((T, d_inner), BF16), jax.ShapeDtypeStruct((1, d_inner), F32)),
        grid=(M_GROUPS, T // tr), in_specs=[blk, blk, vec, blk], out_specs=(blk, blk, vec),
        compiler_params=_params(("parallel", "arbitrary")),
    )(y, z, w.reshape(1, d_inner), dyg)


def _adamw_math(w, g, m, v):
    m1 = ADAM_B1 * m + (1.0 - ADAM_B1) * g
    v1 = ADAM_B2 * v + (1.0 - ADAM_B2) * (g * g)
    m_hat = m1 / (1.0 - ADAM_B1 ** ADAM_STEP)
    v_hat = v1 / (1.0 - ADAM_B2 ** ADAM_STEP)
    delta = -ADAM_LR * (m_hat / (jnp.sqrt(v_hat) + ADAM_EPS) + ADAM_WD * w)
    return delta, m1, v1


def _adamw_sharded(land, w, m, v, *, name, jobs=()):
    R, n = w.shape
    tr = _tile(R, 128, 8)

    def body(l_ref, w_ref, m_ref, v_ref, g_ref, d_ref, m1_ref, v1_ref):
        g = l_ref[0].astype(F32)
        for k in range(1, N_DEV):
            g = g + l_ref[k].astype(F32)
        d, m1, v1 = _adamw_math(w_ref[...], g, m_ref[...], v_ref[...])
        g_ref[...] = g
        d_ref[...] = d
        m1_ref[...] = m1
        v1_ref[...] = v1

    blk = pl.BlockSpec((tr, n), lambda i: (i, 0))
    out = jax.ShapeDtypeStruct((R, n), F32)
    return _call(
        body, name=name, out_shape=(out, out, out, out), grid=(R // tr,),
        in_specs=[pl.BlockSpec((N_DEV, tr, n), lambda i: (0, i, 0)), blk, blk, blk], out_specs=(blk, blk, blk, blk),
        dims=("parallel",), jobs=jobs,
    )(land, w, m, v)


def _sum_devices(parts, *, name):
    _, R, n = parts.shape
    tr = _tile(R, 512, 8)

    def body(p_ref, o_ref):
        g = p_ref[0]
        for k in range(1, N_DEV):
            g = g + p_ref[k]
        o_ref[...] = g

    return pl.pallas_call(
        body, name=name, out_shape=jax.ShapeDtypeStruct((R, n), F32), grid=(R // tr,),
        in_specs=[pl.BlockSpec((N_DEV, tr, n), lambda i: (0, i, 0))], out_specs=pl.BlockSpec((tr, n), lambda i: (i, 0)),
        compiler_params=_params(("parallel",)),
    )(parts)


def _adamw_packed(g, w, m, v, *, name):
    R, n = w.shape
    tr = _tile(R, 512, 8)

    def body(g_ref, w_ref, m_ref, v_ref, d_ref, m1_ref, v1_ref):
        d, m1, v1 = _adamw_math(w_ref[...], g_ref[...], m_ref[...], v_ref[...])
        d_ref[...] = d
        m1_ref[...] = m1
        v1_ref[...] = v1

    blk = pl.BlockSpec((tr, n), lambda i: (i, 0))
    out = jax.ShapeDtypeStruct((R, n), F32)
    return pl.pallas_call(
        body, name=name, out_shape=(out, out, out), grid=(R // tr,), in_specs=[blk] * 4, out_specs=(blk, blk, blk),
        compiler_params=_params(("parallel",)),
    )(g, w, m, v)


PACK_ALIGN = 8 * LANES


def _pack(arrs):
    parts = []
    for a in arrs:
        flat = a.reshape(-1).astype(F32)
        pad = (-flat.shape[0]) % PACK_ALIGN
        parts.append(jnp.pad(flat, (0, pad)).reshape(-1, LANES))
    return jnp.concatenate(parts, axis=0)


def _unpack(packed, shapes, lead=()):
    out, r = [], 0
    nl = len(lead)
    for s in shapes:
        n = math.prod(s)
        rows = (n + PACK_ALIGN - 1) // PACK_ALIGN * (PACK_ALIGN // LANES)
        part = lax.slice_in_dim(packed, r, r + rows, axis=nl)
        out.append(part.reshape(lead + (rows * LANES,))[..., :n].reshape(lead + tuple(s)))
        r += rows
    return out


EXCHANGE_BYTES_PER_US = 80e3
CHIP_GATHER_BYTES_PER_US = 65e3
SIBLING_BYTES_PER_US = 250e3
EXCHANGE_ROW_ALIGN = 128
CARRY_FILL = 0.9

CALL_US = {
    "hgrn_in_fwd": 160, "hgrn_scan_fwd": 780, "hgrn_out_fwd": 60, "ffn_up_fwd": 230, "ffn_gate_fwd": 96, "ffn_down_fwd": 130,
    "mamba_z_fwd": 85, "mamba_xbc_fwd": 123, "mamba_scan_fwd": 350, "mamba_out_fwd": 91,
    "ffn_down_bwd_x": 187, "ffn_down_bwd_w": 186, "ffn_gate_bwd": 240, "ffn_up_bwd_x": 251, "ffn_up_bwd_w": 229,
    "hgrn_out_bwd_x": 50, "hgrn_out_bwd_w": 50, "hgrn_scan_bwd": 1700, "hgrn_in_bwd_x": 188, "hgrn_in_bwd_w": 158,
    "mamba_out_bwd_x": 89, "mamba_out_bwd_w": 92, "mamba_scan_bwd": 1170, "mamba_z_bwd_x": 110, "mamba_xbc_bwd_x": 136,
    "mamba_z_bwd_w": 82, "mamba_xbc_bwd_w": 120,
    "adamw_hgrn_w_in": 65, "adamw_hgrn_w_out": 16, "adamw_m_w_in": 87, "adamw_m_w_out": 32, "adamw_f_w_up": 168,
    "adamw_f_w_down": 109,
}


class _Plan:
    def __init__(self):
        self.queue = []
        self.handoff = []

    def push(self, src, mode, buf, q0=0):
        rows, n = src.shape[-2], src.shape[-1]
        blocks = len(OTHER_CHIPS) if mode == "gather_chip" else N_DEV - 1
        rate = CHIP_GATHER_BYTES_PER_US if mode == "gather_chip" else EXCHANGE_BYTES_PER_US
        self.queue.append(dict(src=src, mode=mode, buf=buf, r0=0, r1=rows, q0=q0, us_per_row=blocks * n * src.dtype.itemsize / rate))

    def _hand_on(self, it, r0, r1):
        q0, q1 = it["q0"] + r0, it["q0"] + r1
        if self.handoff and self.handoff[-1]["buf"] is it["buf"] and self.handoff[-1]["r1"] == q0:
            self.handoff[-1]["r1"] = q1
            return
        us = len(OTHER_CHIPS) * it["src"].shape[-1] * it["src"].dtype.itemsize / SIBLING_BYTES_PER_US
        self.handoff.append(dict(src=jnp.zeros((8, LANES), it["src"].dtype), mode="forward", buf=it["buf"], r0=q0, r1=q1, q0=0,
                                 us_per_row=us))

    def take(self, name):
        budget, jobs, used, spawned = CARRY_FILL * CALL_US[name], [], set(), []
        for queue in (self.handoff, self.queue):
            while queue:
                it = queue[0]
                left = it["r1"] - it["r0"]
                fit = int(budget / it["us_per_row"])
                rows = left if fit >= left else fit // EXCHANGE_ROW_ALIGN * EXCHANGE_ROW_ALIGN
                if rows <= 0 or id(it["buf"]) in used:
                    break
                jobs.append(_Job(it["src"], it["mode"], it["buf"], it["r0"], it["r0"] + rows, it["q0"] + it["r0"]))
                used.add(id(it["buf"]))
                budget -= rows * it["us_per_row"]
                if it["mode"] == "gather_chip":
                    spawned.append((it, it["r0"], it["r0"] + rows))
                it["r0"] += rows
                if it["r0"] == it["r1"]:
                    queue.pop(0)
                else:
                    break
        for it, r0, r1 in spawned:
            self._hand_on(it, r0, r1)
        return jobs

    def finish(self, buf, *, name):
        mine = [it for it in self.queue if it["buf"] is buf]
        self.queue = [it for it in self.queue if it["buf"] is not buf]
        for i, it in enumerate(mine):
            _exchange([_Job(it["src"], it["mode"], buf, it["r0"], it["r1"], it["q0"] + it["r0"])], name=f"{name}_{i}" if i else name)
            if it["mode"] == "gather_chip":
                self._hand_on(it, it["r0"], it["r1"])
        mine = [it for it in self.handoff if it["buf"] is buf]
        self.handoff = [it for it in self.handoff if it["buf"] is not buf]
        for i, it in enumerate(mine):
            _exchange([_Job(it["src"], "forward", buf, it["r0"], it["r1"], it["r0"])], name=f"{name}_handoff_{i}" if i else f"{name}_handoff")
        return buf.arr


def _pad_to(a, axis, size):
    pad = [(0, 0)] * a.ndim
    pad[axis] = (0, size - a.shape[axis])
    return jnp.pad(a, pad)


def _ssd_operands(dt_raw, dt_bias, a_log, dskip):
    T, heads = dt_raw.shape
    G = M_GROUPS
    hpg = heads // G
    t3 = dt_raw.reshape(T, G, hpg)
    dtc = _pad_to(t3.transpose(1, 0, 2), 2, LANES)
    dtr = _pad_to(t3.transpose(1, 2, 0), 1, 8)
    col = lambda p: _pad_to(p.reshape(G, 1, hpg), 2, LANES)
    row = lambda p: _pad_to(p.reshape(G, hpg, 1), 1, 8)
    return dtc, dtr, (col(dt_bias), row(dt_bias), col(a_log), row(a_log), col(dskip))


def _ssd_unpack(ddtc, ddtr, gbc, gbr, gac, gar, gdk, heads):
    G = M_GROUPS
    hpg = heads // G
    T = ddtc.shape[1]
    ddt = ddtc[:, :, :hpg].transpose(1, 0, 2).reshape(T, heads) + ddtr[:, :hpg, :].transpose(2, 0, 1).reshape(T, heads)
    both = lambda c, r: c[:, 0, :hpg].reshape(heads) + r[:, :hpg, 0].reshape(heads)
    return ddt, both(gbc, gbr), both(gac, gar), gdk[:, 0, :hpg].reshape(heads)


class _LazyWeights:
    def __init__(self, name, shape, fetch):
        self.name, self.shape, self.fetch, self.got = name, shape, fetch, {}

    def __getitem__(self, layer):
        if layer not in self.got:
            self.got[layer] = self.fetch(self.name, layer)
        return self.got[layer]


class _GradSink:
    def __init__(self, name, layers, push):
        self.name, self.shape, self.push = name, (layers,), push

    def __setitem__(self, layer, g):
        self.push(self.name, layer, g)


def _local_step(x, target, W, plan, push):
    Bl, L, D = x.shape
    T = Bl * L
    depth = W["mix_norm"].shape[0]
    d_inner = W["m_w_out"].shape[1]
    heads = d_inner // M_HEADDIM
    conv_dim = W["m_conv_w"].shape[2]

    def mm(a, b, mode, out_dtype, *, name, add=None, out_stack=None):
        return _matmul(a, b, mode, out_dtype, name=name, add=add, out_stack=out_stack,
                       jobs=plan.take(name) if name in CALL_US else ())

    lb = _lb_fwd(W["hgrn_lb_logits"], name="hgrn_lower_bounds")
    h = x.reshape(T, D)
    saved = []
    for i in range(depth):
        j = i // 2
        s = {"h": h}
        u = _rms_fwd(h, W["mix_norm"][i], name="mix_norm_fwd")
        s["u"] = u
        if i % 2 == 0:
            P = mm(u, W["hgrn_w_in"][j], "nn", F32, name="hgrn_in_fwd")
            og, st = _hgrn_fwd(P, lb[j], W["hgrn_gnorm"][j], Bl, L, name="hgrn_scan_fwd", jobs=plan.take("hgrn_scan_fwd"))
            h = mm(og, W["hgrn_w_out"][j], "nn", F32, add=h, name="hgrn_out_fwd")
            s.update(P=P, og=og, st=st)
        else:
            w_in = W["m_w_in"][j]
            z = mm(u, w_in[:, :d_inner], "nn", F32, name="mamba_z_fwd")
            xbc = mm(u, w_in[:, d_inner:d_inner + conv_dim], "nn", F32, name="mamba_xbc_fwd")
            dt_raw = mm(u, w_in[:, d_inner + conv_dim:], "nn", F32, name="mamba_dt_fwd")
            xa = _mconv_fwd(xbc, W["m_conv_w"][j], W["m_conv_b"][j], Bl, L, name="mamba_conv_fwd")
            dtc, dtr, prm = _ssd_operands(dt_raw, W["m_dt_bias"][j], W["m_A_log"][j], W["m_D"][j])
            y, hs = _ssd_fwd(xa, dtc, dtr, prm, Bl, L, d_inner, name="mamba_scan_fwd", jobs=plan.take("mamba_scan_fwd"))
            yg = _mgate_fwd(y, z, W["m_norm"][j], name="mamba_gate_fwd")
            h = mm(yg, W["m_w_out"][j], "nn", F32, add=h, name="mamba_out_fwd")
            s.update(z=z, xbc=xbc, xa=xa, dtc=dtc, dtr=dtr, prm=prm, y=y, hs=hs, yg=yg)
        s["h2"] = h
        u2 = _rms_fwd(h, W["ffn_norm"][i], name="ffn_norm_fwd")
        a = mm(u2, W["f_w_up"][i], "nn", BF16, name="ffn_up_fwd")
        yf = _ffn_gate_fwd(a, W["f_conv_w"][i], W["f_conv_b"][i], Bl, L, name="ffn_gate_fwd", jobs=plan.take("ffn_gate_fwd"))
        h = mm(yf, W["f_w_down"][i], "nn", F32, add=h, name="ffn_down_fwd")
        s.update(u2=u2, a=a, yf=yf)
        saved.append(s)

    loss, dh, d_final = _loss_head(h, W["final_norm"], target.reshape(T, D), name="loss_head")

    g = {k: (_GradSink(k, W[k].shape[0], push) if isinstance(W[k], _LazyWeights) else [None] * W[k].shape[0])
         for k in W if k != "final_norm"}
    g["final_norm"] = d_final.reshape(D)
    dlb = [None] * W["hgrn_lb_logits"].shape[0]
    for i in reversed(range(depth)):
        j = i // 2
        s = saved[i]
        g["f_w_down"][i] = mm(s["yf"], dh, "tn", BF16, name="ffn_down_bwd_w")
        dyf = mm(dh, W["f_w_down"][i], "nt", BF16, name="ffn_down_bwd_x")
        da, pf = _ffn_gate_bwd(s["a"], W["f_conv_w"][i], W["f_conv_b"][i], dyf, Bl, L, name="ffn_gate_bwd",
                               jobs=plan.take("ffn_gate_bwd"))
        g["f_w_up"][i] = mm(s["u2"], da, "tn", BF16, out_stack=N_DEV, name="ffn_up_bwd_w")
        du2 = mm(da, W["f_w_up"][i], "nt", F32, name="ffn_up_bwd_x")
        g["f_conv_w"][i], g["f_conv_b"][i] = pf[:FFN_CONV], pf[FFN_CONV]
        dh, dn = _rms_bwd(s["h2"], W["ffn_norm"][i], du2, dh, name="ffn_norm_bwd")
        g["ffn_norm"][i] = dn.reshape(D)
        if i % 2 == 0:
            g["hgrn_w_out"][j] = mm(s["og"], dh, "tn", BF16, name="hgrn_out_bwd_w")
            dog = mm(dh, W["hgrn_w_out"][j], "nt", BF16, name="hgrn_out_bwd_x")
            dP4, dlb_j, dgn = _hgrn_bwd(s["P"], lb[j], W["hgrn_gnorm"][j], s["st"], dog, Bl, L, name="hgrn_scan_bwd",
                                        jobs=plan.take("hgrn_scan_bwd"))
            g["hgrn_w_in"][j] = mm(s["u"], dP4, "tn", BF16, out_stack=N_DEV, name="hgrn_in_bwd_w")
            du = mm(dP4, W["hgrn_w_in"][j], "nt", F32, name="hgrn_in_bwd_x")
            dlb[j] = dlb_j
            g["hgrn_gnorm"][j] = dgn.reshape(HGRN_HEAD)
        else:
            w_in = W["m_w_in"][j]
            g["m_w_out"][j] = mm(s["yg"], dh, "tn", BF16, name="mamba_out_bwd_w")
            dyg = mm(dh, W["m_w_out"][j], "nt", BF16, name="mamba_out_bwd_x")
            dy, dz, dnw = _mgate_bwd(s["y"], s["z"], W["m_norm"][j], dyg, name="mamba_gate_bwd")
            dxs, dB, dC, ddtc, ddtr, gbc, gbr, gac, gar, gdk = _ssd_bwd(
                s["xa"], s["dtc"], s["dtr"], s["prm"], s["hs"], dy, Bl, L, d_inner, name="mamba_scan_bwd",
                jobs=plan.take("mamba_scan_bwd"))
            ddt, g["m_dt_bias"][j], g["m_A_log"][j], g["m_D"][j] = _ssd_unpack(ddtc, ddtr, gbc, gbr, gac, gar, gdk, heads)
            dxa = jnp.concatenate([dxs, dB, dC], axis=1)
            dxbc, pm = _mconv_bwd(s["xbc"], W["m_conv_w"][j], W["m_conv_b"][j], dxa, Bl, L, name="mamba_conv_bwd")
            g["m_conv_w"][j], g["m_conv_b"][j], g["m_norm"][j] = pm[:M_CONV], pm[M_CONV], dnw.reshape(d_inner)
            ddt = ddt.astype(BF16)
            g["m_w_in"][j] = jnp.concatenate([
                mm(s["u"], dz, "tn", BF16, name="mamba_z_bwd_w"), mm(s["u"], dxbc, "tn", BF16, name="mamba_xbc_bwd_w"),
                mm(s["u"], ddt, "tn", BF16, name="mamba_dt_bwd_w")], axis=1)
            du = mm(dz, w_in[:, :d_inner], "nt", F32, name="mamba_z_bwd_x")
            du = mm(dxbc, w_in[:, d_inner:d_inner + conv_dim], "nt", F32, add=du, name="mamba_xbc_bwd_x")
            du = mm(ddt, w_in[:, d_inner + conv_dim:], "nt", F32, add=du, name="mamba_dt_bwd_x")
        dh, dn = _rms_bwd(s["h"], W["mix_norm"][i], du, dh, name="mix_norm_bwd")
        g["mix_norm"][i] = dn.reshape(D)
    g["hgrn_lb_logits"] = _lb_bwd(W["hgrn_lb_logits"], jnp.concatenate(dlb, axis=0), name="hgrn_lower_bounds_bwd")
    grads = {k: (jnp.stack(v) if isinstance(v, list) else v) for k, v in g.items() if not isinstance(v, _GradSink)}
    return loss, dh.reshape(Bl, L, D), grads


WEIGHTS = ("mix_norm", "ffn_norm", "final_norm", "hgrn_w_in", "hgrn_lb_logits", "hgrn_gnorm", "hgrn_w_out", "m_w_in",
           "m_conv_w", "m_conv_b", "m_dt_bias", "m_A_log", "m_D", "m_norm", "m_w_out", "f_w_up", "f_conv_w", "f_conv_b",
           "f_w_down")
COL_SHARDED = ("hgrn_w_in", "m_w_in", "f_w_up")
STACKED = ("hgrn_w_in", "f_w_up")
ROW_SHARDED = ("hgrn_w_out", "m_w_out", "f_w_down")
SMALL_SHARDED = ("m_conv_w", "m_conv_b", "m_norm", "f_conv_w")
SMALL = tuple(k for k in WEIGHTS if k not in COL_SHARDED + ROW_SHARDED)


def _join_last(g):
    nd = g.ndim
    t = g.transpose(tuple(range(1, nd - 1)) + (0, nd - 1))
    return t.reshape(t.shape[:-2] + (N_DEV * g.shape[-1],))


def kernel(x, mix_norm, ffn_norm, final_norm, hgrn_w_in, hgrn_lb_logits, hgrn_gnorm, hgrn_w_out, m_w_in, m_conv_w, m_conv_b, m_dt_bias, m_A_log, m_D, m_norm, m_w_out, f_w_up, f_conv_w, f_conv_b, f_w_down, loss_target, m_mix_norm, m_ffn_norm, m_final_norm, m_hgrn_w_in, m_hgrn_lb_logits, m_hgrn_gnorm, m_hgrn_w_out, m_m_w_in, m_m_conv_w, m_m_conv_b, m_m_dt_bias, m_m_A_log, m_m_D, m_m_norm, m_m_w_out, m_f_w_up, m_f_conv_w, m_f_conv_b, m_f_w_down, v_mix_norm, v_ffn_norm, v_final_norm, v_hgrn_w_in, v_hgrn_lb_logits, v_hgrn_gnorm, v_hgrn_w_out, v_m_w_in, v_m_conv_w, v_m_conv_b, v_m_dt_bias, v_m_A_log, v_m_D, v_m_norm, v_m_w_out, v_f_w_up, v_f_conv_w, v_f_conv_b, v_f_w_down):
    w = dict(zip(WEIGHTS, (mix_norm, ffn_norm, final_norm, hgrn_w_in, hgrn_lb_logits, hgrn_gnorm, hgrn_w_out, m_w_in, m_conv_w, m_conv_b, m_dt_bias, m_A_log, m_D, m_norm, m_w_out, f_w_up, f_conv_w, f_conv_b, f_w_down)))
    m = dict(zip(WEIGHTS, (m_mix_norm, m_ffn_norm, m_final_norm, m_hgrn_w_in, m_hgrn_lb_logits, m_hgrn_gnorm, m_hgrn_w_out, m_m_w_in, m_m_conv_w, m_m_conv_b, m_m_dt_bias, m_m_A_log, m_m_D, m_m_norm, m_m_w_out, m_f_w_up, m_f_conv_w, m_f_conv_b, m_f_w_down)))
    v = dict(zip(WEIGHTS, (v_mix_norm, v_ffn_norm, v_final_norm, v_hgrn_w_in, v_hgrn_lb_logits, v_hgrn_gnorm, v_hgrn_w_out, v_m_w_in, v_m_conv_w, v_m_conv_b, v_m_dt_bias, v_m_A_log, v_m_D, v_m_norm, v_m_w_out, v_f_w_up, v_f_conv_w, v_f_conv_b, v_f_w_down)))
    me = 4 * lax.axis_index("x") + 2 * lax.axis_index("y") + lax.axis_index("c")

    def gather_now(a, *, name):
        buf = _Buf(a.shape[0], a.shape[1], a.dtype)
        _exchange([_Job(a, "gather", buf)], name=name)
        return buf.arr

    plan = _Plan()
    depth = mix_norm.shape[0]
    use_order = []
    for i in range(depth):
        use_order += [("hgrn_w_in", i // 2), ("hgrn_w_out", i // 2)] if i % 2 == 0 else [("m_w_in", i // 2), ("m_w_out", i // 2)]
        use_order += [("f_w_up", i), ("f_w_down", i)]
    gathers = {}
    for k, layer in use_order:
        shard = w[k][layer].astype(BF16)
        gathers[k, layer] = _Buf(shard.shape[0], shard.shape[1], BF16)
        plan.push(shard, "gather_chip", gathers[k, layer])

    def fetch(k, layer):
        g = plan.finish(gathers[k, layer], name=f"gather_{k}")
        if k in STACKED:
            return g
        if k in COL_SHARDED:
            return g.transpose(1, 0, 2).reshape(g.shape[1], N_DEV * g.shape[2])
        return g.reshape(N_DEV * g.shape[1], g.shape[2])

    full = {k: w[k] for k in SMALL if k not in SMALL_SHARDED}
    for k in COL_SHARDED:
        full[k] = _LazyWeights(k, (w[k].shape[0], w[k].shape[1], N_DEV * w[k].shape[2]), fetch)
    for k in ROW_SHARDED:
        full[k] = _LazyWeights(k, (w[k].shape[0], N_DEV * w[k].shape[1], w[k].shape[2]), fetch)
    shard_shapes = [w[k].shape for k in SMALL_SHARDED]
    gathered = gather_now(_pack([w[k] for k in SMALL_SHARDED]), name="gather_small_params")
    for k, g in zip(SMALL_SHARDED, _unpack(gathered, shard_shapes, lead=(N_DEV,))):
        full[k] = _join_last(g)

    lands = {k: _Buf(math.prod(w[k].shape[:-1]), w[k].shape[-1], BF16) for k in COL_SHARDED + ROW_SHARDED}

    def push(k, layer, g):
        if k in STACKED:
            parts = g
        elif k in COL_SHARDED:
            parts = g.reshape(g.shape[0], N_DEV, g.shape[1] // N_DEV).transpose(1, 0, 2)
        else:
            parts = g.reshape(N_DEV, g.shape[0] // N_DEV, g.shape[1])
        plan.push(parts, "scatter", lands[k], q0=layer * parts.shape[1])

    loss, grad_x, grads = _local_step(x, loss_target, full, plan, push)

    out = {}
    for k in ("m_w_out", "m_w_in", "f_w_down", "f_w_up", "hgrn_w_out", "hgrn_w_in"):
        land = plan.finish(lands[k], name="scatter_" + k)
        rows, n = math.prod(w[k].shape[:-1]), w[k].shape[-1]
        res = _adamw_sharded(land, w[k].reshape(rows, n), m[k].reshape(rows, n), v[k].reshape(rows, n),
                             name="adamw_" + k, jobs=plan.take("adamw_" + k))
        out[k] = tuple(r.reshape(w[k].shape) for r in res)

    small_full_shapes = [(1, 1)] + [grads[k].shape for k in SMALL]
    packed = _pack([loss] + [grads[k] for k in SMALL])
    summed = _sum_devices(gather_now(packed, name="gather_small_grads"), name="sum_small_grads")
    parts = _unpack(summed, small_full_shapes)
    loss_all, small_g = parts[0], dict(zip(SMALL, parts[1:]))
    for k in SMALL_SHARDED:
        n = w[k].shape[-1]
        small_g[k] = lax.dynamic_slice_in_dim(small_g[k], me * n, n, axis=small_g[k].ndim - 1)
    shapes = [w[k].shape for k in SMALL]
    res = _adamw_packed(_pack([small_g[k] for k in SMALL]), _pack([w[k] for k in SMALL]), _pack([m[k] for k in SMALL]),
                        _pack([v[k] for k in SMALL]), name="adamw_small")
    res = [_unpack(r, shapes) for r in res]
    for i, k in enumerate(SMALL):
        out[k] = (small_g[k], res[0][i], res[1][i], res[2][i])

    return (loss_all.reshape(()), grad_x, *[out[k][0] for k in WEIGHTS], *[out[k][1] for k in WEIGHTS],
            *[out[k][2] for k in WEIGHTS], *[out[k][3] for k in WEIGHTS])
```

```python
import functools
import math

import jax
import jax.numpy as jnp
from jax import lax
from jax.experimental import pallas as pl
from jax.experimental.pallas import tpu as pltpu

F32 = jnp.float32
BF16 = jnp.bfloat16
HIGHEST = lax.Precision.HIGHEST

NORM_EPS = 1e-5
ADAM_LR, ADAM_B1, ADAM_B2, ADAM_EPS, ADAM_WD, ADAM_STEP = 0.001, 0.9, 0.999, 1e-08, 0.01, 10

N_DEV = 8
LANES = 128
V7X_VMEM_BYTES = 64 * 1024 * 1024
VMEM_LIMIT = V7X_VMEM_BYTES * 3 // 4

HGRN_HEAD = 128
HGRN_CHUNK = 128
HGRN_SUB = 16
M_HEADDIM = 64
M_GROUPS = 8
M_D_STATE = 128
M_CONV = 4
M_CHUNK = 256
FFN_CONV = 3
CONV_COLS = 256
HGRN_PAR = 2


def _params(dims=None, **kw):
    return pltpu.CompilerParams(dimension_semantics=dims, vmem_limit_bytes=VMEM_LIMIT, **kw)


def _tile(dim, target, align=LANES):
    t = (min(target, dim) // align) * align
    while t >= align:
        if dim % t == 0:
            return t
        if 2 * t < target and dim <= 2 * target:
            return dim
        t -= align
    return dim


class _Buf:
    def __init__(self, rows, n, dtype):
        self.arr = lax.empty((N_DEV, rows, n), dtype)


SIBLING = 1
OTHER_CHIPS = (2, 4, 6)


class _Job:
    def __init__(self, src, mode, buf, r0=0, r1=None, q0=0):
        rows = src.shape[-2]
        self.src, self.mode, self.buf, self.r0, self.r1, self.q0 = src, mode, buf, r0, rows if r1 is None else r1, q0


def _job_copies(job, src_ref, out_ref, send_sems, recv_sems, local_sem):
    x, y, c = lax.axis_index("x"), lax.axis_index("y"), lax.axis_index("c")
    me = 4 * x + 2 * y + c
    n = job.r1 - job.r0
    rows, land = pl.ds(job.r0, n), pl.ds(job.q0, n)

    def peer_of(k):
        kx, ky, kc = (k >> 2) & 1, (k >> 1) & 1, k & 1
        px, py, pc = (1 - x if kx else x), (1 - y if ky else y), (1 - c if kc else c)
        return 4 * px + 2 * py + pc, (px, py, pc)

    pairs = []
    if job.mode == "forward":
        sib, sib_id = peer_of(SIBLING)
        for i, k in enumerate(OTHER_CHIPS):
            mine, _ = peer_of(k)
            theirs, _ = peer_of(k ^ SIBLING)
            sems = dict(send_sem=send_sems.at[i], recv_sem=recv_sems.at[i], device_id=sib_id, device_id_type=pl.DeviceIdType.MESH)
            send = pltpu.make_async_remote_copy(src_ref=out_ref.at[mine, land], dst_ref=out_ref.at[mine, land], **sems)
            landing = pltpu.make_async_remote_copy(src_ref=out_ref.at[mine, land], dst_ref=out_ref.at[theirs, land], **sems)
            pairs.append((send, landing))
        return None, pairs
    part = (lambda d: src_ref.at[d, rows]) if job.mode == "scatter" else (lambda d: src_ref.at[rows])
    local = pltpu.make_async_copy(part(me), out_ref.at[me, land], local_sem)
    for k in ((SIBLING,) + OTHER_CHIPS if job.mode == "gather_chip" else range(1, N_DEV)):
        peer, peer_id = peer_of(k)
        sems = dict(send_sem=send_sems.at[k - 1], recv_sem=recv_sems.at[k - 1], device_id=peer_id,
                    device_id_type=pl.DeviceIdType.MESH)
        send = pltpu.make_async_remote_copy(src_ref=part(peer), dst_ref=out_ref.at[me, land], **sems)
        landing = pltpu.make_async_remote_copy(src_ref=part(me), dst_ref=out_ref.at[peer, land], **sems)
        pairs.append((send, landing))
    return local, pairs


def _jobs_start(jobs, src_refs, out_refs, send_sems, recv_sems, local_sems):
    for i, job in enumerate(jobs):
        local, pairs = _job_copies(job, src_refs[i], out_refs[i], send_sems.at[i], recv_sems.at[i], local_sems.at[i])
        if local is not None:
            local.start()
        for send, _ in pairs:
            send.start()


def _jobs_wait(jobs, src_refs, out_refs, send_sems, recv_sems, local_sems):
    for i, job in enumerate(jobs):
        local, pairs = _job_copies(job, src_refs[i], out_refs[i], send_sems.at[i], recv_sems.at[i], local_sems.at[i])
        for _, landing in pairs:
            landing.wait_recv()
        for send, _ in pairs:
            send.wait_send()
        if local is not None:
            local.wait()


def _job_scratch(n):
    return [pltpu.SemaphoreType.DMA((n, N_DEV - 1)), pltpu.SemaphoreType.DMA((n, N_DEV - 1)), pltpu.SemaphoreType.DMA((n,))]


def _buf_shapes(jobs):
    assert len({id(j.buf) for j in jobs}) == len(jobs), "one job per buffer and call"
    return tuple(jax.ShapeDtypeStruct(j.buf.arr.shape, j.buf.arr.dtype) for j in jobs)


def _exchange(jobs, *, name):
    n = len(jobs)

    def body(*refs):
        srcs, dsts, sems = refs[:n], refs[2 * n:3 * n], refs[3 * n:]
        _jobs_start(jobs, srcs, dsts, *sems)
        _jobs_wait(jobs, srcs, dsts, *sems)

    hbm = pl.BlockSpec(memory_space=pl.ANY)
    outs = pl.pallas_call(
        body, name=name, out_shape=_buf_shapes(jobs), in_specs=[hbm] * (2 * n), out_specs=(hbm,) * n,
        input_output_aliases={n + i: i for i in range(n)},
        scratch_shapes=_job_scratch(n), compiler_params=pltpu.CompilerParams(has_side_effects=True),
    )(*[j.src for j in jobs], *[j.buf.arr for j in jobs])
    for j, o in zip(jobs, outs):
        j.buf.arr = o


def _call(body, *, name, out_shape, grid, in_specs, out_specs, scratch_shapes=(), dims, jobs=()):
    single = not isinstance(out_shape, (tuple, list))
    out_shape = (out_shape,) if single else tuple(out_shape)
    out_specs = (out_specs,) if single else tuple(out_specs)
    n_in, n_out, n_scr, nj = len(in_specs), len(out_shape), len(scratch_shapes), len(jobs)
    if not jobs:
        plain = pl.pallas_call(body, name=name, out_shape=out_shape, grid=grid, in_specs=list(in_specs), out_specs=out_specs,
                               scratch_shapes=list(scratch_shapes), compiler_params=_params(dims))

        def run_plain(*args):
            r = plain(*args)
            return r[0] if single else tuple(r)

        return run_plain

    def carrying(*refs):
        ins, srcs = refs[:n_in], refs[n_in:n_in + nj]
        o0 = n_in + 2 * nj
        outs, dsts = refs[o0:o0 + n_out], refs[o0 + n_out:o0 + n_out + nj]
        scr, sems = refs[o0 + n_out + nj:o0 + n_out + nj + n_scr], refs[o0 + n_out + nj + n_scr:]
        ids = [pl.program_id(d) for d in range(len(grid))]
        first = functools.reduce(jnp.logical_and, [i == 0 for i in ids])
        last = functools.reduce(jnp.logical_and, [i == g - 1 for i, g in zip(ids, grid)])

        @pl.when(first)
        def _():
            _jobs_start(jobs, srcs, dsts, *sems)

        body(*ins, *outs, *scr)

        @pl.when(last)
        def _():
            _jobs_wait(jobs, srcs, dsts, *sems)

    hbm = pl.BlockSpec(memory_space=pl.ANY)
    call = pl.pallas_call(
        carrying, name=name, out_shape=out_shape + _buf_shapes(jobs), grid=grid,
        in_specs=list(in_specs) + [hbm] * (2 * nj), out_specs=out_specs + (hbm,) * nj,
        input_output_aliases={n_in + nj + i: n_out + i for i in range(nj)},
        scratch_shapes=list(scratch_shapes) + _job_scratch(nj),
        compiler_params=_params(("arbitrary",) * len(grid), has_side_effects=True))

    def run(*args):
        r = call(*args, *[j.src for j in jobs], *[j.buf.arr for j in jobs])
        for j, o in zip(jobs, r[n_out:]):
            j.buf.arr = o
        return r[0] if single else tuple(r[:n_out])

    return run


def _dg(a, b, ca, cb):
    return lax.dot_general(a.astype(BF16), b.astype(BF16), (((ca,), (cb,)), ((), ())), preferred_element_type=F32)


@jax.custom_vjp
def _dot_nn(a, b):
    return _dg(a, b, 1, 0)


def _dot_nn_f(a, b):
    return _dg(a, b, 1, 0), (a, b)


def _dot_nn_b(res, g):
    a, b = res
    return _dg(g, b, 1, 1), _dg(a, g, 0, 0)


_dot_nn.defvjp(_dot_nn_f, _dot_nn_b)


@jax.custom_vjp
def _dot_nt(a, b):
    return _dg(a, b, 1, 1)


def _dot_nt_f(a, b):
    return _dg(a, b, 1, 1), (a, b)


def _dot_nt_b(res, g):
    a, b = res
    return _dg(g, b, 1, 0), _dg(g, a, 0, 0)


_dot_nt.defvjp(_dot_nt_f, _dot_nt_b)


@jax.custom_vjp
def _dot_tn(a, b):
    return _dg(a, b, 0, 0)


def _dot_tn_f(a, b):
    return _dg(a, b, 0, 0), (a, b)


def _dot_tn_b(res, g):
    a, b = res
    return _dg(b, g, 1, 1), _dg(a, g, 1, 0)


_dot_tn.defvjp(_dot_tn_f, _dot_tn_b)


def _tri(n):
    return (lax.broadcasted_iota(jnp.int32, (n, n), 0) >= lax.broadcasted_iota(jnp.int32, (n, n), 1)).astype(F32)


def _silu(x):
    return x * jax.nn.sigmoid(x)


def _matmul(a, b, mode, out_dtype, *, name, add=None, out_stack=None, tm=1024, tn=1024, tk=2048, jobs=()):
    a_n = a.shape[2] if a.ndim == 3 else None
    b_n = b.shape[2] if b.ndim == 3 else None
    a_shape = (a.shape[1], a.shape[0] * a.shape[2]) if a_n else a.shape
    b_shape = (b.shape[1], b.shape[0] * b.shape[2]) if b_n else b.shape
    if mode == "nn":
        (M, K), (K2, N) = a_shape, b_shape
    elif mode == "nt":
        (M, K), (N, K2) = a_shape, b_shape
    else:
        (K, M), (K2, N) = a_shape, b_shape
    assert K == K2, (a.shape, b.shape, mode)
    n_lim = {"m": M, "n": N, "k": K}
    for dim, n in (("m" if mode == "tn" else "k", a_n), ("k" if mode == "nt" else "n", b_n), ("n", N // out_stack if out_stack else None)):
        if n:
            n_lim[dim] = math.gcd(n_lim[dim], n)
    tm, tn, tk = _tile(n_lim["m"], tm, 8 if M % LANES else LANES), _tile(n_lim["n"], tn), _tile(n_lim["k"], tk)
    nk = K // tk
    ca, cb = {"nn": (1, 0), "nt": (1, 1), "tn": (0, 0)}[mode]

    def body(*refs):
        a_ref, b_ref = refs[:2]
        add_ref = refs[2] if add is not None else None
        o_ref = refs[2 + (add is not None)]

        def finish(r):
            if add is not None:
                r = r + add_ref[...]
            o_ref[...] = r.astype(o_ref.dtype)

        if nk == 1:
            finish(_dg(a_ref[...], b_ref[...], ca, cb))
            return
        acc_ref = refs[-1]
        k = pl.program_id(2)

        @pl.when(k == 0)
        def _():
            acc_ref[...] = _dg(a_ref[...], b_ref[...], ca, cb)

        @pl.when((k > 0) & (k < nk - 1))
        def _():
            acc_ref[...] += _dg(a_ref[...], b_ref[...], ca, cb)

        @pl.when(k == nk - 1)
        def _():
            finish(acc_ref[...] + _dg(a_ref[...], b_ref[...], ca, cb))

    def spec(rows, cols, t_rows, t_cols, n):
        if not n:
            return pl.BlockSpec((t_rows, t_cols), lambda i, j, k: (rows(i, j, k), cols(i, j, k)))
        per = n // t_cols
        return pl.BlockSpec((None, t_rows, t_cols), lambda i, j, k: (cols(i, j, k) // per, rows(i, j, k), cols(i, j, k) % per))

    gi, gj, gk = (lambda i, j, k: i), (lambda i, j, k: j), (lambda i, j, k: k)
    a_spec = spec(gk, gi, tk, tm, a_n) if mode == "tn" else spec(gi, gk, tm, tk, a_n)
    b_spec = spec(gj, gk, tn, tk, b_n) if mode == "nt" else spec(gk, gj, tk, tn, b_n)
    o_spec = spec(gi, gj, tm, tn, N // out_stack if out_stack else None)
    out_shape = (out_stack, M, N // out_stack) if out_stack else (M, N)
    in_specs, args = [a_spec, b_spec], [a, b]
    if add is not None:
        assert not out_stack
        in_specs.append(o_spec)
        args.append(add)
    return _call(
        body, name=name, out_shape=jax.ShapeDtypeStruct(out_shape, out_dtype), grid=(M // tm, N // tn, nk),
        in_specs=in_specs, out_specs=o_spec, scratch_shapes=[pltpu.VMEM((tm, tn), F32)] if nk > 1 else [],
        dims=("parallel", "parallel", "arbitrary"), jobs=jobs,
    )(*args)


def _rms_fwd(h, w, *, name):
    T, D = h.shape
    tr = _tile(T, 256, 8)

    def body(h_ref, w_ref, u_ref):
        x = h_ref[...]
        u_ref[...] = (x * lax.rsqrt(jnp.mean(x * x, axis=-1, keepdims=True) + NORM_EPS) * w_ref[...]).astype(u_ref.dtype)

    return pl.pallas_call(
        body, name=name, out_shape=jax.ShapeDtypeStruct((T, D), BF16), grid=(T // tr,),
        in_specs=[pl.BlockSpec((tr, D), lambda i: (i, 0)), pl.BlockSpec((1, D), lambda i: (0, 0))],
        out_specs=pl.BlockSpec((tr, D), lambda i: (i, 0)), compiler_params=_params(("parallel",)),
    )(h, w.reshape(1, D))


def _rms_bwd(h, w, du, dh_in, *, name):
    T, D = h.shape
    tr = _tile(T, 256, 8)

    def body(h_ref, w_ref, du_ref, dhin_ref, dh_ref, dw_ref):
        x = h_ref[...]
        g = du_ref[...].astype(F32)
        rstd = lax.rsqrt(jnp.mean(x * x, axis=-1, keepdims=True) + NORM_EPS)
        xhat = x * rstd
        gx = g * w_ref[...]
        dh_ref[...] = dhin_ref[...] + rstd * (gx - xhat * jnp.mean(gx * xhat, axis=-1, keepdims=True))

        @pl.when(pl.program_id(0) == 0)
        def _():
            dw_ref[...] = jnp.zeros_like(dw_ref)

        dw_ref[...] += jnp.sum(g * xhat, axis=0, keepdims=True)

    row = pl.BlockSpec((tr, D), lambda i: (i, 0))
    vec = pl.BlockSpec((1, D), lambda i: (0, 0))
    return pl.pallas_call(
        body, name=name, out_shape=(jax.ShapeDtypeStruct((T, D), F32), jax.ShapeDtypeStruct((1, D), F32)), grid=(T // tr,),
        in_specs=[row, vec, row, row], out_specs=(row, vec), compiler_params=_params(("arbitrary",)),
    )(h, w.reshape(1, D), du, dh_in)


def _loss_head(h, w, target, *, name):
    T, D = h.shape
    tr = _tile(T, 256, 8)

    def body(h_ref, w_ref, t_ref, loss_ref, dh_ref, dw_ref):
        x = h_ref[...]
        rstd = lax.rsqrt(jnp.mean(x * x, axis=-1, keepdims=True) + NORM_EPS)
        xhat = x * rstd
        err = xhat * w_ref[...] - t_ref[...]
        g = err * (1.0 / D)
        gx = g * w_ref[...]
        dh_ref[...] = rstd * (gx - xhat * jnp.mean(gx * xhat, axis=-1, keepdims=True))

        @pl.when(pl.program_id(0) == 0)
        def _():
            dw_ref[...] = jnp.zeros_like(dw_ref)
            loss_ref[...] = jnp.zeros_like(loss_ref)

        dw_ref[...] += jnp.sum(g * xhat, axis=0, keepdims=True)
        loss_ref[...] += (0.5 / D) * jnp.sum(jnp.sum(err * err, axis=-1, keepdims=True), axis=0, keepdims=True)

    row = pl.BlockSpec((tr, D), lambda i: (i, 0))
    vec = pl.BlockSpec((1, D), lambda i: (0, 0))
    one = pl.BlockSpec((1, 1), lambda i: (0, 0))
    return pl.pallas_call(
        body, name=name,
        out_shape=(jax.ShapeDtypeStruct((1, 1), F32), jax.ShapeDtypeStruct((T, D), F32), jax.ShapeDtypeStruct((1, D), F32)),
        grid=(T // tr,), in_specs=[row, vec, row], out_specs=(one, row, vec), compiler_params=_params(("arbitrary",)),
    )(h, w.reshape(1, D), target)


def _shift_down(x, s):
    if s == 0:
        return x
    rows = lax.broadcasted_iota(jnp.int32, x.shape, 0)
    return jnp.where(rows >= s, pltpu.roll(x, s, 0), 0.0)


def _shift_up(x, s):
    if s == 0:
        return x
    n = x.shape[0]
    rows = lax.broadcasted_iota(jnp.int32, x.shape, 0)
    return jnp.where(rows < n - s, pltpu.roll(x, n - s, 0), 0.0)


def _conv_pre(a, w_ref, b_ref, taps):
    pre = b_ref[0:1, :] + w_ref[taps - 1:taps, :] * a
    for k in range(taps - 1):
        pre = pre + w_ref[k:k + 1, :] * _shift_down(a, taps - 1 - k)
    return pre


def _conv_bwd(a, dpre, w_ref, taps):
    da = w_ref[taps - 1:taps, :] * dpre
    rows = []
    for k in range(taps - 1):
        da = da + w_ref[k:k + 1, :] * _shift_up(dpre, taps - 1 - k)
        rows.append(jnp.sum(dpre * _shift_down(a, taps - 1 - k), axis=0, keepdims=True))
    rows.append(jnp.sum(dpre * a, axis=0, keepdims=True))
    rows.append(jnp.sum(dpre, axis=0, keepdims=True))
    return da, rows


def _ffn_gate_fwd(a, cw, cb, Bl, L, *, name, jobs=()):
    T, F2 = a.shape
    F = F2 // 2
    tc = _tile(F, CONV_COLS)
    nj = F // tc

    def body(ag_ref, au_ref, wg_ref, wu_ref, bg_ref, bu_ref, y_ref):
        hg = _conv_pre(ag_ref[...].astype(F32), wg_ref, bg_ref, FFN_CONV)
        hu = _conv_pre(au_ref[...].astype(F32), wu_ref, bu_ref, FFN_CONV)
        y_ref[...] = (_silu(hg) * hu).astype(y_ref.dtype)

    blk = lambda off: pl.BlockSpec((L, tc), lambda b, j: (b, j + off))
    wblk = lambda off: pl.BlockSpec((FFN_CONV, tc), lambda b, j: (0, j + off))
    bblk = lambda off: pl.BlockSpec((1, tc), lambda b, j: (0, j + off))
    return _call(
        body, name=name, out_shape=jax.ShapeDtypeStruct((T, F), BF16), grid=(Bl, nj),
        in_specs=[blk(0), blk(nj), wblk(0), wblk(nj), bblk(0), bblk(nj)], out_specs=blk(0),
        dims=("parallel", "parallel"), jobs=jobs,
    )(a, a, cw, cw, cb.reshape(1, F2), cb.reshape(1, F2))


def _ffn_gate_bwd(a, cw, cb, dy, Bl, L, *, name, jobs=()):
    T, F2 = a.shape
    F = F2 // 2
    tc = _tile(F, CONV_COLS)
    nj = F // tc

    def body(ag_ref, au_ref, wg_ref, wu_ref, bg_ref, bu_ref, dy_ref, da_ref, pg_ref, pu_ref):
        ag, au = ag_ref[...].astype(F32), au_ref[...].astype(F32)
        hg = _conv_pre(ag, wg_ref, bg_ref, FFN_CONV)
        hu = _conv_pre(au, wu_ref, bu_ref, FFN_CONV)
        g = dy_ref[...].astype(F32)
        s = jax.nn.sigmoid(hg)
        dhg = g * hu * (s * (1.0 + hg * (1.0 - s)))
        dhu = g * (hg * s)
        dag, rg = _conv_bwd(ag, dhg, wg_ref, FFN_CONV)
        dau, ru = _conv_bwd(au, dhu, wu_ref, FFN_CONV)
        da_ref[0] = dag.astype(da_ref.dtype)
        da_ref[1] = dau.astype(da_ref.dtype)

        @pl.when(pl.program_id(1) == 0)
        def _():
            pg_ref[...] = jnp.zeros_like(pg_ref)
            pu_ref[...] = jnp.zeros_like(pu_ref)

        for k in range(FFN_CONV + 1):
            pg_ref[k:k + 1, :] += rg[k]
            pu_ref[k:k + 1, :] += ru[k]

    blk = lambda off: pl.BlockSpec((L, tc), lambda j, b: (b, j + off))
    wblk = lambda off: pl.BlockSpec((FFN_CONV, tc), lambda j, b: (0, j + off))
    bblk = lambda off: pl.BlockSpec((1, tc), lambda j, b: (0, j + off))
    pblk = lambda off: pl.BlockSpec((8, tc), lambda j, b: (0, j + off))
    da, pg, pu = _call(
        body, name=name,
        out_shape=(jax.ShapeDtypeStruct((2, T, F), BF16), jax.ShapeDtypeStruct((8, F), F32), jax.ShapeDtypeStruct((8, F), F32)),
        grid=(nj, Bl),
        in_specs=[blk(0), blk(nj), wblk(0), wblk(nj), bblk(0), bblk(nj), blk(0)],
        out_specs=(pl.BlockSpec((2, L, tc), lambda j, b: (0, b, j)), pblk(0), pblk(0)),
        dims=("parallel", "arbitrary"), jobs=jobs,
    )(a, a, cw, cw, cb.reshape(1, F2), cb.reshape(1, F2), dy)
    return da, jnp.concatenate([pg, pu], axis=1)


def _mconv_fwd(xbc, cw, cb, Bl, L, *, name, jobs=()):
    T, W = xbc.shape
    tc = _tile(W, CONV_COLS)

    def body(a_ref, w_ref, b_ref, y_ref):
        y_ref[...] = _silu(_conv_pre(a_ref[...], w_ref, b_ref, M_CONV))

    blk = pl.BlockSpec((L, tc), lambda b, j: (b, j))
    return _call(
        body, name=name, out_shape=jax.ShapeDtypeStruct((T, W), F32), grid=(Bl, W // tc),
        in_specs=[blk, pl.BlockSpec((M_CONV, tc), lambda b, j: (0, j)), pl.BlockSpec((1, tc), lambda b, j: (0, j))],
        out_specs=blk, dims=("parallel", "parallel"), jobs=jobs,
    )(xbc, cw, cb.reshape(1, W))


def _mconv_bwd(xbc, cw, cb, dy, Bl, L, *, name):
    T, W = xbc.shape
    tc = _tile(W, CONV_COLS)

    def body(a_ref, w_ref, b_ref, dy_ref, da_ref, p_ref):
        a = a_ref[...]
        pre = _conv_pre(a, w_ref, b_ref, M_CONV)
        s = jax.nn.sigmoid(pre)
        dpre = dy_ref[...] * (s * (1.0 + pre * (1.0 - s)))
        da, rows = _conv_bwd(a, dpre, w_ref, M_CONV)
        da_ref[...] = da.astype(da_ref.dtype)

        @pl.when(pl.program_id(1) == 0)
        def _():
            p_ref[...] = jnp.zeros_like(p_ref)

        for k in range(M_CONV + 1):
            p_ref[k:k + 1, :] += rows[k]

    blk = pl.BlockSpec((L, tc), lambda j, b: (b, j))
    return pl.pallas_call(
        body, name=name, out_shape=(jax.ShapeDtypeStruct((T, W), BF16), jax.ShapeDtypeStruct((8, W), F32)),
        grid=(W // tc, Bl),
        in_specs=[blk, pl.BlockSpec((M_CONV, tc), lambda j, b: (0, j)), pl.BlockSpec((1, tc), lambda j, b: (0, j)), blk],
        out_specs=(blk, pl.BlockSpec((8, tc), lambda j, b: (0, j))),
        compiler_params=_params(("parallel", "arbitrary")),
    )(xbc, cw, cb.reshape(1, W), dy)


def _hgrn_chunk(st, qr, fr, v, gr, lb, gn):
    C = qr.shape[0]
    q = _silu(qr)
    logf = jnp.log(lb + (1.0 - lb) * jax.nn.sigmoid(fr))
    k = (1.0 - lb) * jax.nn.sigmoid(-fr)
    b = jnp.dot(_tri(C), logf, precision=HIGHEST, preferred_element_type=F32)
    o = _dot_nt(q * jnp.exp(b), st)
    keep = lax.broadcasted_iota(jnp.int32, (HGRN_SUB, HGRN_SUB), 0) >= lax.broadcasted_iota(jnp.int32, (HGRN_SUB, HGRN_SUB), 1)
    cols = lax.broadcasted_iota(jnp.int32, (HGRN_SUB, C), 1)
    parts = []
    for blk in range(C // HGRN_SUB):
        r0 = blk * HGRN_SUB
        bi, qi, ki, vi = b[r0:r0 + HGRN_SUB], q[r0:r0 + HGRN_SUB], k[r0:r0 + HGRN_SUB], v[r0:r0 + HGRN_SUB]
        decay = jnp.exp(jnp.minimum(bi[:, None, :] - bi[None, :, :], 0.0))
        a_diag = jnp.where(keep, jnp.sum(qi[:, None, :] * ki[None, :, :] * decay, axis=-1), 0.0)
        oi = _dot_nn(a_diag, vi)
        if blk > 0:
            bs = b[r0 - 1:r0]
            a_off = _dot_nt(qi * jnp.exp(bi - bs), k * jnp.exp(jnp.minimum(bs - b, 0.0)))
            oi = oi + _dot_nn(jnp.where(cols < r0, a_off, 0.0), v)
        parts.append(oi)
    o = o + jnp.concatenate(parts, axis=0)
    bl = b[C - 1:C]
    st1 = st * jnp.exp(bl) + _dot_tn(v, k * jnp.exp(bl - b))
    og = o * lax.rsqrt(jnp.mean(o * o, axis=-1, keepdims=True) + NORM_EPS) * gn * _silu(gr)
    return st1, og


def _hgrn_specs(Bl, L, H, order):
    W, HB = HGRN_PAR * HGRN_HEAD, H // HGRN_PAR

    def at(off):
        if order == "bh":
            return pl.BlockSpec((L, W), lambda b, h: (b, h + off))
        return pl.BlockSpec((L, W), lambda h, b: (b, h + off))
    return [at(0), at(HB), at(2 * HB), at(3 * HB)]


def _hgrn_fwd(P, lb, gn, Bl, L, *, name, jobs=()):
    T, D4 = P.shape
    D = D4 // 4
    H = D // HGRN_HEAD
    C = HGRN_CHUNK
    NC = L // C
    W, HB = HGRN_PAR * HGRN_HEAD, H // HGRN_PAR
    heads = [slice(p * HGRN_HEAD, (p + 1) * HGRN_HEAD) for p in range(HGRN_PAR)]

    def body(q_ref, f_ref, v_ref, g_ref, lb_ref, gn_ref, og_ref, st_ref):
        gnv = gn_ref[...]

        def step(c, sts):
            r = pl.ds(pl.multiple_of(c * C, C), C)
            new = []
            for p, hd in enumerate(heads):
                st_ref[p, c] = sts[p]
                st1, og = _hgrn_chunk(sts[p], q_ref[r, hd], f_ref[r, hd], v_ref[r, hd], g_ref[r, hd], lb_ref[:, hd], gnv)
                og_ref[r, hd] = og.astype(og_ref.dtype)
                new.append(st1)
            return tuple(new)

        lax.fori_loop(0, NC, step, tuple(jnp.zeros((HGRN_HEAD, HGRN_HEAD), F32) for _ in heads))

    return _call(
        body, name=name,
        out_shape=(jax.ShapeDtypeStruct((T, D), BF16), jax.ShapeDtypeStruct((Bl, H, NC, HGRN_HEAD, HGRN_HEAD), F32)),
        grid=(Bl, HB),
        in_specs=_hgrn_specs(Bl, L, H, "bh") + [pl.BlockSpec((1, W), lambda b, h: (0, h)), pl.BlockSpec((1, HGRN_HEAD), lambda b, h: (0, 0))],
        out_specs=(pl.BlockSpec((L, W), lambda b, h: (b, h)),
                   pl.BlockSpec((None, HGRN_PAR, NC, HGRN_HEAD, HGRN_HEAD), lambda b, h: (b, h, 0, 0, 0))),
        dims=("parallel", "parallel"), jobs=jobs,
    )(P, P, P, P, lb.reshape(1, D), gn.reshape(1, HGRN_HEAD))


def _hgrn_bwd(P, lb, gn, states, dog, Bl, L, *, name, jobs=()):
    T, D4 = P.shape
    D = D4 // 4
    H = D // HGRN_HEAD
    C = HGRN_CHUNK
    NC = L // C
    W, HB = HGRN_PAR * HGRN_HEAD, H // HGRN_PAR
    heads = [slice(p * HGRN_HEAD, (p + 1) * HGRN_HEAD) for p in range(HGRN_PAR)]

    def body(q_ref, f_ref, v_ref, g_ref, lb_ref, gn_ref, st_ref, dog_ref, dp_ref, dlb_ref, dgn_ref):
        gnv = gn_ref[...]

        @pl.when(pl.program_id(1) == 0)
        def _():
            dlb_ref[...] = jnp.zeros_like(dlb_ref)

        @pl.when((pl.program_id(0) == 0) & (pl.program_id(1) == 0))
        def _():
            dgn_ref[...] = jnp.zeros_like(dgn_ref)

        def step(i, dsts):
            c = NC - 1 - i
            r = pl.ds(pl.multiple_of(c * C, C), C)
            new, dgn_sum = [], jnp.zeros((1, HGRN_HEAD), F32)
            for p, hd in enumerate(heads):
                _, vjp = jax.vjp(_hgrn_chunk, st_ref[p, c], q_ref[r, hd], f_ref[r, hd], v_ref[r, hd], g_ref[r, hd],
                                 lb_ref[:, hd], gnv)
                dst0, dq, df, dv, dg, dlb, dgn = vjp((dsts[p], dog_ref[r, hd].astype(F32)))
                dp_ref[0, r, hd] = dq.astype(dp_ref.dtype)
                dp_ref[1, r, hd] = df.astype(dp_ref.dtype)
                dp_ref[2, r, hd] = dv.astype(dp_ref.dtype)
                dp_ref[3, r, hd] = dg.astype(dp_ref.dtype)
                dlb_ref[:, hd] += dlb
                dgn_sum = dgn_sum + dgn
                new.append(dst0)
            dgn_ref[...] += dgn_sum
            return tuple(new)

        lax.fori_loop(0, NC, step, tuple(jnp.zeros((HGRN_HEAD, HGRN_HEAD), F32) for _ in heads))

    return _call(
        body, name=name,
        out_shape=(jax.ShapeDtypeStruct((4, T, D), BF16), jax.ShapeDtypeStruct((1, D), F32), jax.ShapeDtypeStruct((1, HGRN_HEAD), F32)),
        grid=(HB, Bl),
        in_specs=_hgrn_specs(Bl, L, H, "hb") + [
            pl.BlockSpec((1, W), lambda h, b: (0, h)), pl.BlockSpec((1, HGRN_HEAD), lambda h, b: (0, 0)),
            pl.BlockSpec((None, HGRN_PAR, NC, HGRN_HEAD, HGRN_HEAD), lambda h, b: (b, h, 0, 0, 0)),
            pl.BlockSpec((L, W), lambda h, b: (b, h))],
        out_specs=(pl.BlockSpec((4, L, W), lambda h, b: (0, b, h)),
                   pl.BlockSpec((1, W), lambda h, b: (0, h)), pl.BlockSpec((1, HGRN_HEAD), lambda h, b: (0, 0))),
        dims=("arbitrary", "arbitrary"), jobs=jobs,
    )(P, P, P, P, lb.reshape(1, D), gn.reshape(1, HGRN_HEAD), states, dog)


def _lower_bounds(logits):
    p = jax.nn.softmax(logits, axis=0)
    rows, run = [], jnp.zeros_like(p[0:1])
    for i in range(logits.shape[0]):
        run = run + p[i:i + 1]
        rows.append(run - p[0:1])
    return jnp.concatenate(rows, axis=0) if len(rows) > 1 else rows[0]


def _lb_fwd(logits, *, name):
    def body(x_ref, o_ref):
        o_ref[...] = _lower_bounds(x_ref[...])
    return pl.pallas_call(body, name=name, out_shape=jax.ShapeDtypeStruct(logits.shape, F32))(logits)


def _lb_bwd(logits, dlb, *, name):
    def body(x_ref, g_ref, o_ref):
        _, vjp = jax.vjp(_lower_bounds, x_ref[...])
        o_ref[...] = vjp(g_ref[...])[0]
    return pl.pallas_call(body, name=name, out_shape=jax.ShapeDtypeStruct(logits.shape, F32))(logits, dlb)


def _ssd_pre(dtc_raw, dtr_raw, bias_c, bias_r, alog_c, alog_r):
    C = dtc_raw.shape[0]
    dt_c = jax.nn.softplus(dtc_raw + bias_c)
    dt_r = jax.nn.softplus(dtr_raw + bias_r)
    acs_c = jnp.dot(_tri(C), dt_c * -jnp.exp(alog_c), precision=HIGHEST, preferred_element_type=F32)
    acs_r = lax.dot_general(dt_r * -jnp.exp(alog_r), _tri(C), (((1,), (1,)), ((), ())), precision=HIGHEST, preferred_element_type=F32)
    return dt_c, acs_c, acs_r


def _ssd_head(h0, xs, G, Bm, Cm, dt, acs, acs_row, dskip):
    C = xs.shape[0]
    causal = lax.broadcasted_iota(jnp.int32, (C, C), 0) >= lax.broadcasted_iota(jnp.int32, (C, C), 1)
    Lm = jnp.where(causal, jnp.exp(jnp.minimum(acs - acs_row, 0.0)), 0.0)
    X = xs * dt
    y = _dot_nn(G * Lm, X) + _dot_nt(Cm * jnp.exp(acs), h0) + dskip * xs
    alast = acs[C - 1:C]
    h1 = jnp.exp(alast) * h0 + _dot_tn(X, Bm * jnp.exp(alast - acs))
    return y, h1


def _ssd_layout(Bl, L, d_inner, order):
    G = M_GROUPS
    hpg = d_inner // M_HEADDIM // G
    gw = hpg * M_HEADDIM
    ix = (lambda b, g: (b, g)) if order == "bg" else (lambda g, b: (b, g))
    def cols(width, off):
        if order == "bg":
            return pl.BlockSpec((L, width), lambda b, g: (b, g + off))
        return pl.BlockSpec((L, width), lambda g, b: (b, g + off))
    def per_group(shape):
        n = len(shape)
        if order == "bg":
            return pl.BlockSpec((None,) + shape, lambda b, g: (g,) + (0,) * n)
        return pl.BlockSpec((None,) + shape, lambda g, b: (g,) + (0,) * n)
    if order == "bg":
        dtc = pl.BlockSpec((None, L, LANES), lambda b, g: (g, b, 0))
        dtr = pl.BlockSpec((None, 8, L), lambda b, g: (g, 0, b))
    else:
        dtc = pl.BlockSpec((None, L, LANES), lambda g, b: (g, b, 0))
        dtr = pl.BlockSpec((None, 8, L), lambda g, b: (g, 0, b))
    nb = d_inner // M_D_STATE
    return hpg, gw, cols, per_group, dtc, dtr, nb


def _ssd_fwd(xa, dtc, dtr, prm, Bl, L, d_inner, *, name, jobs=()):
    T = xa.shape[0]
    C = M_CHUNK
    NC = L // C
    G = M_GROUPS
    hpg, gw, cols, per_group, dtc_spec, dtr_spec, nb = _ssd_layout(Bl, L, d_inner, "bg")

    def body(xs_ref, b_ref, c_ref, dtc_ref, dtr_ref, bc_ref, br_ref, ac_ref, ar_ref, dk_ref, y_ref, hs_ref,
             dt_s, acs_s, acr_s, h_s):
        h_s[...] = jnp.zeros_like(h_s)

        def step(c, carry):
            r = pl.ds(pl.multiple_of(c * C, C), C)
            dt_c, acs_c, acs_r = _ssd_pre(dtc_ref[r, :], dtr_ref[:, r], bc_ref[...], br_ref[...], ac_ref[...], ar_ref[...])
            dt_s[...] = dt_c
            acs_s[...] = acs_c
            acr_s[...] = acs_r
            Bm, Cm = b_ref[r, :], c_ref[r, :]
            Gm = _dot_nt(Cm, Bm)
            for j in range(hpg):
                pc = slice(j * M_HEADDIM, (j + 1) * M_HEADDIM)
                h0 = h_s[j]
                hs_ref[c, j] = h0
                y, h1 = _ssd_head(h0, xs_ref[r, pc], Gm, Bm, Cm, dt_s[:, j:j + 1], acs_s[:, j:j + 1], acr_s[j:j + 1, :],
                                  dk_ref[0:1, j:j + 1])
                y_ref[r, pc] = y
                h_s[j] = h1
            return carry

        lax.fori_loop(0, NC, step, 0)

    return _call(
        body, name=name,
        out_shape=(jax.ShapeDtypeStruct((T, d_inner), F32), jax.ShapeDtypeStruct((Bl, G, NC, hpg, M_HEADDIM, M_D_STATE), F32)),
        grid=(Bl, G),
        in_specs=[cols(gw, 0), cols(M_D_STATE, nb), cols(M_D_STATE, nb + G), dtc_spec, dtr_spec,
                  per_group((1, LANES)), per_group((8, 1)), per_group((1, LANES)), per_group((8, 1)), per_group((1, LANES))],
        out_specs=(cols(gw, 0), pl.BlockSpec((None, None, NC, hpg, M_HEADDIM, M_D_STATE), lambda b, g: (b, g, 0, 0, 0, 0))),
        scratch_shapes=[pltpu.VMEM((C, LANES), F32), pltpu.VMEM((C, LANES), F32), pltpu.VMEM((8, C), F32),
                        pltpu.VMEM((hpg, M_HEADDIM, M_D_STATE), F32)],
        dims=("parallel", "parallel"), jobs=jobs,
    )(xa, xa, xa, dtc, dtr, *prm)


def _ssd_bwd(xa, dtc, dtr, prm, states, dy, Bl, L, d_inner, *, name, jobs=()):
    T = xa.shape[0]
    C = M_CHUNK
    NC = L // C
    G = M_GROUPS
    hpg, gw, cols, per_group, dtc_spec, dtr_spec, nb = _ssd_layout(Bl, L, d_inner, "gb")

    def body(xs_ref, b_ref, c_ref, dtc_ref, dtr_ref, bc_ref, br_ref, ac_ref, ar_ref, dk_ref, hs_ref, dy_ref,
             dxs_ref, db_ref, dc_ref, ddtc_ref, ddtr_ref, gbc_ref, gbr_ref, gac_ref, gar_ref, gdk_ref,
             dt_s, acs_s, acr_s, ddt_s, dacs_s, dacr_s, dh_s):
        dh_s[...] = jnp.zeros_like(dh_s)

        @pl.when(pl.program_id(1) == 0)
        def _():
            for ref in (gbc_ref, gbr_ref, gac_ref, gar_ref, gdk_ref):
                ref[...] = jnp.zeros_like(ref)

        def step(i, carry):
            c = NC - 1 - i
            r = pl.ds(pl.multiple_of(c * C, C), C)
            pre_in = (dtc_ref[r, :], dtr_ref[:, r], bc_ref[...], br_ref[...], ac_ref[...], ar_ref[...])
            (dt_c, acs_c, acs_r), pre_vjp = jax.vjp(_ssd_pre, *pre_in)
            dt_s[...] = dt_c
            acs_s[...] = acs_c
            acr_s[...] = acs_r
            ddt_s[...] = jnp.zeros_like(ddt_s)
            dacs_s[...] = jnp.zeros_like(dacs_s)
            dacr_s[...] = jnp.zeros_like(dacr_s)
            Bm, Cm = b_ref[r, :], c_ref[r, :]
            Gm = _dot_nt(Cm, Bm)
            dG = jnp.zeros((C, C), F32)
            dB = jnp.zeros((C, M_D_STATE), F32)
            dC = jnp.zeros((C, M_D_STATE), F32)
            for j in range(hpg):
                pc = slice(j * M_HEADDIM, (j + 1) * M_HEADDIM)
                one = slice(j, j + 1)
                _, vjp = jax.vjp(_ssd_head, hs_ref[c, j], xs_ref[r, pc], Gm, Bm, Cm, dt_s[:, one], acs_s[:, one], acr_s[one, :],
                                 dk_ref[0:1, one])
                dh0, dxs, dGj, dBj, dCj, ddt, dacs, dacr, ddk = vjp((dy_ref[r, pc], dh_s[j]))
                dh_s[j] = dh0
                dxs_ref[r, pc] = dxs
                dG, dB, dC = dG + dGj, dB + dBj, dC + dCj
                ddt_s[:, one] = ddt
                dacs_s[:, one] = dacs
                dacr_s[one, :] = dacr
                gdk_ref[0:1, one] += ddk
            db_ref[r, :] = dB + _dot_tn(dG, Cm)
            dc_ref[r, :] = dC + _dot_nn(dG, Bm)
            ddtc, ddtr, gbc, gbr, gac, gar = pre_vjp((ddt_s[...], dacs_s[...], dacr_s[...]))
            ddtc_ref[r, :] = ddtc
            ddtr_ref[:, r] = ddtr
            gbc_ref[...] += gbc
            gbr_ref[...] += gbr
            gac_ref[...] += gac
            gar_ref[...] += gar
            return carry

        lax.fori_loop(0, NC, step, 0)

    GN = G * M_D_STATE
    bc_cols = lambda off: pl.BlockSpec((L, M_D_STATE), lambda g, b: (b, g + off))
    outs = _call(
        body, name=name,
        out_shape=(jax.ShapeDtypeStruct((T, d_inner), F32), jax.ShapeDtypeStruct((T, GN), F32), jax.ShapeDtypeStruct((T, GN), F32),
                   jax.ShapeDtypeStruct(dtc.shape, F32), jax.ShapeDtypeStruct(dtr.shape, F32),
                   jax.ShapeDtypeStruct((G, 1, LANES), F32), jax.ShapeDtypeStruct((G, 8, 1), F32),
                   jax.ShapeDtypeStruct((G, 1, LANES), F32), jax.ShapeDtypeStruct((G, 8, 1), F32),
                   jax.ShapeDtypeStruct((G, 1, LANES), F32)),
        grid=(G, Bl),
        in_specs=[cols(gw, 0), cols(M_D_STATE, nb), cols(M_D_STATE, nb + G), dtc_spec, dtr_spec,
                  per_group((1, LANES)), per_group((8, 1)), per_group((1, LANES)), per_group((8, 1)), per_group((1, LANES)),
                  pl.BlockSpec((None, None, NC, hpg, M_HEADDIM, M_D_STATE), lambda g, b: (b, g, 0, 0, 0, 0)), cols(gw, 0)],
        out_specs=(cols(gw, 0), bc_cols(0), bc_cols(0), dtc_spec, dtr_spec,
                   per_group((1, LANES)), per_group((8, 1)), per_group((1, LANES)), per_group((8, 1)), per_group((1, LANES))),
        scratch_shapes=[pltpu.VMEM((C, LANES), F32), pltpu.VMEM((C, LANES), F32), pltpu.VMEM((8, C), F32),
                        pltpu.VMEM((C, LANES), F32), pltpu.VMEM((C, LANES), F32), pltpu.VMEM((8, C), F32),
                        pltpu.VMEM((hpg, M_HEADDIM, M_D_STATE), F32)],
        dims=("arbitrary", "arbitrary"), jobs=jobs,
    )(xa, xa, xa, dtc, dtr, *prm, states, dy)
    return outs


def _mgate(y, z, w):
    t = y * _silu(z)
    return t * lax.rsqrt(jnp.mean(t * t, axis=-1, keepdims=True) + NORM_EPS) * w


def _mgate_fwd(y, z, w, *, name, jobs=()):
    T, d_inner = y.shape
    gw = d_inner // M_GROUPS
    tr = _tile(T, 512, 8)

    def body(y_ref, z_ref, w_ref, o_ref):
        o_ref[...] = _mgate(y_ref[...], z_ref[...], w_ref[...]).astype(o_ref.dtype)

    blk = pl.BlockSpec((tr, gw), lambda i, g: (i, g))
    return _call(
        body, name=name, out_shape=jax.ShapeDtypeStruct((T, d_inner), BF16), grid=(T // tr, M_GROUPS),
        in_specs=[blk, blk, pl.BlockSpec((1, gw), lambda i, g: (0, g))], out_specs=blk,
        dims=("parallel", "parallel"), jobs=jobs,
    )(y, z, w.reshape(1, d_inner))


def _mgate_bwd(y, z, w, dyg, *, name):
    T, d_inner = y.shape
    gw = d_inner // M_GROUPS
    tr = _tile(T, 512, 8)

    def body(y_ref, z_ref, w_ref, g_ref, dy_ref, dz_ref, dw_ref):
        _, vjp = jax.vjp(_mgate, y_ref[...], z_ref[...], w_ref[...])
        dy, dz, dw = vjp(g_ref[...].astype(F32))
        dy_ref[...] = dy
        dz_ref[...] = dz.astype(dz_ref.dtype)

        @pl.when(pl.program_id(1) == 0)
        def _():
            dw_ref[...] = jnp.zeros_like(dw_ref)

        dw_ref[...] += dw

    blk = pl.BlockSpec((tr, gw), lambda g, i: (i, g))
    vec = pl.BlockSpec((1, gw), lambda g, i: (0, g))
    return pl.pallas_call(
        body, name=name,
        out_shape=(jax.ShapeDtypeStruct((T, d_inner), F32), jax.ShapeDtypeStruct((T, d_inner), BF16), jax.ShapeDtypeStruct((1, d_inner), F32)),
        grid=(M_GROUPS, T // tr), in_specs=[blk, blk, vec, blk], out_specs=(blk, blk, vec),
        compiler_params=_params(("parallel", "arbitrary")),
    )(y, z, w.reshape(1, d_inner), dyg)


def _adamw_math(w, g, m, v):
    m1 = ADAM_B1 * m + (1.0 - ADAM_B1) * g
    v1 = ADAM_B2 * v + (1.0 - ADAM_B2) * (g * g)
    m_hat = m1 / (1.0 - ADAM_B1 ** ADAM_STEP)
    v_hat = v1 / (1.0 - ADAM_B2 ** ADAM_STEP)
    delta = -ADAM_LR * (m_hat / (jnp.sqrt(v_hat) + ADAM_EPS) + ADAM_WD * w)
    return delta, m1, v1


def _adamw_sharded(land, w, m, v, *, name, jobs=()):
    R, n = w.shape
    tr = _tile(R, 128, 8)

    def body(l_ref, w_ref, m_ref, v_ref, g_ref, d_ref, m1_ref, v1_ref):
        g = l_ref[0].astype(F32)
        for k in range(1, N_DEV):
            g = g + l_ref[k].astype(F32)
        d, m1, v1 = _adamw_math(w_ref[...], g, m_ref[...], v_ref[...])
        g_ref[...] = g
        d_ref[...] = d
        m1_ref[...] = m1
        v1_ref[...] = v1

    blk = pl.BlockSpec((tr, n), lambda i: (i, 0))
    out = jax.ShapeDtypeStruct((R, n), F32)
    return _call(
        body, name=name, out_shape=(out, out, out, out), grid=(R // tr,),
        in_specs=[pl.BlockSpec((N_DEV, tr, n), lambda i: (0, i, 0)), blk, blk, blk], out_specs=(blk, blk, blk, blk),
        dims=("parallel",), jobs=jobs,
    )(land, w, m, v)


def _sum_devices(parts, *, name):
    _, R, n = parts.shape
    tr = _tile(R, 512, 8)

    def body(p_ref, o_ref):
        g = p_ref[0]
        for k in range(1, N_DEV):
            g = g + p_ref[k]
        o_ref[...] = g

    return pl.pallas_call(
        body, name=name, out_shape=jax.ShapeDtypeStruct((R, n), F32), grid=(R // tr,),
        in_specs=[pl.BlockSpec((N_DEV, tr, n), lambda i: (0, i, 0))], out_specs=pl.BlockSpec((tr, n), lambda i: (i, 0)),
        compiler_params=_params(("parallel",)),
    )(parts)


def _adamw_packed(g, w, m, v, *, name):
    R, n = w.shape
    tr = _tile(R, 512, 8)

    def body(g_ref, w_ref, m_ref, v_ref, d_ref, m1_ref, v1_ref):
        d, m1, v1 = _adamw_math(w_ref[...], g_ref[...], m_ref[...], v_ref[...])
        d_ref[...] = d
        m1_ref[...] = m1
        v1_ref[...] = v1

    blk = pl.BlockSpec((tr, n), lambda i: (i, 0))
    out = jax.ShapeDtypeStruct((R, n), F32)
    return pl.pallas_call(
        body, name=name, out_shape=(out, out, out), grid=(R // tr,), in_specs=[blk] * 4, out_specs=(blk, blk, blk),
        compiler_params=_params(("parallel",)),
    )(g, w, m, v)


PACK_ALIGN = 8 * LANES


def _pack(arrs):
    parts = []
    for a in arrs:
        flat = a.reshape(-1).astype(F32)
        pad = (-flat.shape[0]) % PACK_ALIGN
        parts.append(jnp.pad(flat, (0, pad)).reshape(-1, LANES))
    return jnp.concatenate(parts, axis=0)


def _unpack(packed, shapes, lead=()):
    out, r = [], 0
    nl = len(lead)
    for s in shapes:
        n = math.prod(s)
        rows = (n + PACK_ALIGN - 1) // PACK_ALIGN * (PACK_ALIGN // LANES)
        part = lax.slice_in_dim(packed, r, r + rows, axis=nl)
        out.append(part.reshape(lead + (rows * LANES,))[..., :n].reshape(lead + tuple(s)))
        r += rows
    return out


EXCHANGE_BYTES_PER_US = 80e3
CHIP_GATHER_BYTES_PER_US = 68e3
SIBLING_BYTES_PER_US = 250e3
EXCHANGE_ROW_ALIGN = 64
CARRY_FILL = 1.0

CALL_US = {
    "hgrn_in_fwd": 160, "hgrn_scan_fwd": 780, "hgrn_out_fwd": 60, "ffn_up_fwd": 230, "ffn_gate_fwd": 96, "ffn_down_fwd": 130,
    "mamba_z_fwd": 85, "mamba_xbc_fwd": 123, "mamba_scan_fwd": 400, "mamba_out_fwd": 91, "mamba_conv_fwd": 95, "mamba_gate_fwd": 79,
    "ffn_down_bwd_x": 187, "ffn_down_bwd_w": 186, "ffn_gate_bwd": 240, "ffn_up_bwd_x": 251, "ffn_up_bwd_w": 229,
    "hgrn_out_bwd_x": 50, "hgrn_out_bwd_w": 50, "hgrn_scan_bwd": 1700, "hgrn_in_bwd_x": 188, "hgrn_in_bwd_w": 158,
    "mamba_out_bwd_x": 89, "mamba_out_bwd_w": 92, "mamba_scan_bwd": 1170, "mamba_z_bwd_x": 110, "mamba_xbc_bwd_x": 136,
    "mamba_z_bwd_w": 82, "mamba_xbc_bwd_w": 120,
    "adamw_hgrn_w_in": 65, "adamw_hgrn_w_out": 16, "adamw_m_w_in": 87, "adamw_m_w_out": 32, "adamw_f_w_up": 168,
    "adamw_f_w_down": 109,
}


class _Plan:
    def __init__(self):
        self.queue = []
        self.handoff = []

    def push(self, src, mode, buf, q0=0):
        rows, n = src.shape[-2], src.shape[-1]
        blocks = len(OTHER_CHIPS) if mode == "gather_chip" else N_DEV - 1
        rate = CHIP_GATHER_BYTES_PER_US if mode == "gather_chip" else EXCHANGE_BYTES_PER_US
        self.queue.append(dict(src=src, mode=mode, buf=buf, r0=0, r1=rows, q0=q0, us_per_row=blocks * n * src.dtype.itemsize / rate))

    def _hand_on(self, it, r0, r1):
        q0, q1 = it["q0"] + r0, it["q0"] + r1
        if self.handoff and self.handoff[-1]["buf"] is it["buf"] and self.handoff[-1]["r1"] == q0:
            self.handoff[-1]["r1"] = q1
            return
        us = len(OTHER_CHIPS) * it["src"].shape[-1] * it["src"].dtype.itemsize / SIBLING_BYTES_PER_US
        self.handoff.append(dict(src=jnp.zeros((8, LANES), it["src"].dtype), mode="forward", buf=it["buf"], r0=q0, r1=q1, q0=0,
                                 us_per_row=us))

    def take(self, name):
        budget, jobs, used, spawned = CARRY_FILL * CALL_US[name], [], set(), []
        for queue in (self.handoff, self.queue):
            i = 0
            while i < len(queue):
                it = queue[i]
                if id(it["buf"]) in used:
                    i += 1
                    continue
                left = it["r1"] - it["r0"]
                fit = int(budget / it["us_per_row"])
                rows = left if fit >= left else fit // EXCHANGE_ROW_ALIGN * EXCHANGE_ROW_ALIGN
                if rows <= 0:
                    break
                jobs.append(_Job(it["src"], it["mode"], it["buf"], it["r0"], it["r0"] + rows, it["q0"] + it["r0"]))
                used.add(id(it["buf"]))
                budget -= rows * it["us_per_row"]
                if it["mode"] == "gather_chip":
                    spawned.append((it, it["r0"], it["r0"] + rows))
                it["r0"] += rows
                if it["r0"] < it["r1"]:
                    break
                queue.pop(i)
        for it, r0, r1 in spawned:
            self._hand_on(it, r0, r1)
        return jobs

    def finish(self, buf, *, name):
        mine = [it for it in self.queue if it["buf"] is buf]
        self.queue = [it for it in self.queue if it["buf"] is not buf]
        for i, it in enumerate(mine):
            _exchange([_Job(it["src"], it["mode"], buf, it["r0"], it["r1"], it["q0"] + it["r0"])], name=f"{name}_{i}" if i else name)
            if it["mode"] == "gather_chip":
                self._hand_on(it, it["r0"], it["r1"])
        mine = [it for it in self.handoff if it["buf"] is buf]
        self.handoff = [it for it in self.handoff if it["buf"] is not buf]
        for i, it in enumerate(mine):
            _exchange([_Job(it["src"], "forward", buf, it["r0"], it["r1"], it["r0"])], name=f"{name}_handoff_{i}" if i else f"{name}_handoff")
        return buf.arr


def _pad_to(a, axis, size):
    pad = [(0, 0)] * a.ndim
    pad[axis] = (0, size - a.shape[axis])
    return jnp.pad(a, pad)


def _ssd_operands(dt_raw, dt_bias, a_log, dskip):
    T, heads = dt_raw.shape
    G = M_GROUPS
    hpg = heads // G
    t3 = dt_raw.reshape(T, G, hpg)
    dtc = _pad_to(t3.transpose(1, 0, 2), 2, LANES)
    dtr = _pad_to(t3.transpose(1, 2, 0), 1, 8)
    col = lambda p: _pad_to(p.reshape(G, 1, hpg), 2, LANES)
    row = lambda p: _pad_to(p.reshape(G, hpg, 1), 1, 8)
    return dtc, dtr, (col(dt_bias), row(dt_bias), col(a_log), row(a_log), col(dskip))


def _ssd_unpack(ddtc, ddtr, gbc, gbr, gac, gar, gdk, heads):
    G = M_GROUPS
    hpg = heads // G
    T = ddtc.shape[1]
    ddt = ddtc[:, :, :hpg].transpose(1, 0, 2).reshape(T, heads) + ddtr[:, :hpg, :].transpose(2, 0, 1).reshape(T, heads)
    both = lambda c, r: c[:, 0, :hpg].reshape(heads) + r[:, :hpg, 0].reshape(heads)
    return ddt, both(gbc, gbr), both(gac, gar), gdk[:, 0, :hpg].reshape(heads)


class _LazyWeights:
    def __init__(self, name, shape, fetch):
        self.name, self.shape, self.fetch, self.got = name, shape, fetch, {}

    def __getitem__(self, layer):
        if layer not in self.got:
            self.got[layer] = self.fetch(self.name, layer)
        return self.got[layer]


class _GradSink:
    def __init__(self, name, layers, push):
        self.name, self.shape, self.push = name, (layers,), push

    def __setitem__(self, layer, g):
        self.push(self.name, layer, g)


def _local_step(x, target, W, plan, push):
    Bl, L, D = x.shape
    T = Bl * L
    depth = W["mix_norm"].shape[0]
    d_inner = W["m_w_out"].shape[1]
    heads = d_inner // M_HEADDIM
    conv_dim = W["m_conv_w"].shape[2]

    def mm(a, b, mode, out_dtype, *, name, add=None, out_stack=None):
        return _matmul(a, b, mode, out_dtype, name=name, add=add, out_stack=out_stack,
                       jobs=plan.take(name) if name in CALL_US else ())

    lb = _lb_fwd(W["hgrn_lb_logits"], name="hgrn_lower_bounds")
    h = x.reshape(T, D)
    saved = []
    for i in range(depth):
        j = i // 2
        s = {"h": h}
        u = _rms_fwd(h, W["mix_norm"][i], name="mix_norm_fwd")
        s["u"] = u
        if i % 2 == 0:
            P = mm(u, W["hgrn_w_in"][j], "nn", F32, name="hgrn_in_fwd")
            og, st = _hgrn_fwd(P, lb[j], W["hgrn_gnorm"][j], Bl, L, name="hgrn_scan_fwd", jobs=plan.take("hgrn_scan_fwd"))
            h = mm(og, W["hgrn_w_out"][j], "nn", F32, add=h, name="hgrn_out_fwd")
            s.update(P=P, og=og, st=st)
        else:
            w_in = W["m_w_in"][j]
            z = mm(u, w_in[:, :d_inner], "nn", F32, name="mamba_z_fwd")
            xbc = mm(u, w_in[:, d_inner:d_inner + conv_dim], "nn", F32, name="mamba_xbc_fwd")
            dt_raw = mm(u, w_in[:, d_inner + conv_dim:], "nn", F32, name="mamba_dt_fwd")
            xa = _mconv_fwd(xbc, W["m_conv_w"][j], W["m_conv_b"][j], Bl, L, name="mamba_conv_fwd", jobs=plan.take("mamba_conv_fwd"))
            dtc, dtr, prm = _ssd_operands(dt_raw, W["m_dt_bias"][j], W["m_A_log"][j], W["m_D"][j])
            y, hs = _ssd_fwd(xa, dtc, dtr, prm, Bl, L, d_inner, name="mamba_scan_fwd", jobs=plan.take("mamba_scan_fwd"))
            yg = _mgate_fwd(y, z, W["m_norm"][j], name="mamba_gate_fwd", jobs=plan.take("mamba_gate_fwd"))
            h = mm(yg, W["m_w_out"][j], "nn", F32, add=h, name="mamba_out_fwd")
            s.update(z=z, xbc=xbc, xa=xa, dtc=dtc, dtr=dtr, prm=prm, y=y, hs=hs, yg=yg)
        s["h2"] = h
        u2 = _rms_fwd(h, W["ffn_norm"][i], name="ffn_norm_fwd")
        a = mm(u2, W["f_w_up"][i], "nn", BF16, name="ffn_up_fwd")
        yf = _ffn_gate_fwd(a, W["f_conv_w"][i], W["f_conv_b"][i], Bl, L, name="ffn_gate_fwd", jobs=plan.take("ffn_gate_fwd"))
        h = mm(yf, W["f_w_down"][i], "nn", F32, add=h, name="ffn_down_fwd")
        s.update(u2=u2, a=a, yf=yf)
        saved.append(s)

    loss, dh, d_final = _loss_head(h, W["final_norm"], target.reshape(T, D), name="loss_head")

    g = {k: (_GradSink(k, W[k].shape[0], push) if isinstance(W[k], _LazyWeights) else [None] * W[k].shape[0])
         for k in W if k != "final_norm"}
    g["final_norm"] = d_final.reshape(D)
    dlb = [None] * W["hgrn_lb_logits"].shape[0]
    for i in reversed(range(depth)):
        j = i // 2
        s = saved[i]
        g["f_w_down"][i] = mm(s["yf"], dh, "tn", BF16, name="ffn_down_bwd_w")
        dyf = mm(dh, W["f_w_down"][i], "nt", BF16, name="ffn_down_bwd_x")
        da, pf = _ffn_gate_bwd(s["a"], W["f_conv_w"][i], W["f_conv_b"][i], dyf, Bl, L, name="ffn_gate_bwd",
                               jobs=plan.take("ffn_gate_bwd"))
        g["f_w_up"][i] = mm(s["u2"], da, "tn", BF16, out_stack=N_DEV, name="ffn_up_bwd_w")
        du2 = mm(da, W["f_w_up"][i], "nt", F32, name="ffn_up_bwd_x")
        g["f_conv_w"][i], g["f_conv_b"][i] = pf[:FFN_CONV], pf[FFN_CONV]
        dh, dn = _rms_bwd(s["h2"], W["ffn_norm"][i], du2, dh, name="ffn_norm_bwd")
        g["ffn_norm"][i] = dn.reshape(D)
        if i % 2 == 0:
            g["hgrn_w_out"][j] = mm(s["og"], dh, "tn", BF16, name="hgrn_out_bwd_w")
            dog = mm(dh, W["hgrn_w_out"][j], "nt", BF16, name="hgrn_out_bwd_x")
            dP4, dlb_j, dgn = _hgrn_bwd(s["P"], lb[j], W["hgrn_gnorm"][j], s["st"], dog, Bl, L, name="hgrn_scan_bwd",
                                        jobs=plan.take("hgrn_scan_bwd"))
            g["hgrn_w_in"][j] = mm(s["u"], dP4, "tn", BF16, out_stack=N_DEV, name="hgrn_in_bwd_w")
            du = mm(dP4, W["hgrn_w_in"][j], "nt", F32, name="hgrn_in_bwd_x")
            dlb[j] = dlb_j
            g["hgrn_gnorm"][j] = dgn.reshape(HGRN_HEAD)
        else:
            w_in = W["m_w_in"][j]
            g["m_w_out"][j] = mm(s["yg"], dh, "tn", BF16, name="mamba_out_bwd_w")
            dyg = mm(dh, W["m_w_out"][j], "nt", BF16, name="mamba_out_bwd_x")
            dy, dz, dnw = _mgate_bwd(s["y"], s["z"], W["m_norm"][j], dyg, name="mamba_gate_bwd")
            dxs, dB, dC, ddtc, ddtr, gbc, gbr, gac, gar, gdk = _ssd_bwd(
                s["xa"], s["dtc"], s["dtr"], s["prm"], s["hs"], dy, Bl, L, d_inner, name="mamba_scan_bwd",
                jobs=plan.take("mamba_scan_bwd"))
            ddt, g["m_dt_bias"][j], g["m_A_log"][j], g["m_D"][j] = _ssd_unpack(ddtc, ddtr, gbc, gbr, gac, gar, gdk, heads)
            dxa = jnp.concatenate([dxs, dB, dC], axis=1)
            dxbc, pm = _mconv_bwd(s["xbc"], W["m_conv_w"][j], W["m_conv_b"][j], dxa, Bl, L, name="mamba_conv_bwd")
            g["m_conv_w"][j], g["m_conv_b"][j], g["m_norm"][j] = pm[:M_CONV], pm[M_CONV], dnw.reshape(d_inner)
            ddt = ddt.astype(BF16)
            g["m_w_in"][j] = jnp.concatenate([
                mm(s["u"], dz, "tn", BF16, name="mamba_z_bwd_w"), mm(s["u"], dxbc, "tn", BF16, name="mamba_xbc_bwd_w"),
                mm(s["u"], ddt, "tn", BF16, name="mamba_dt_bwd_w")], axis=1)
            du = mm(dz, w_in[:, :d_inner], "nt", F32, name="mamba_z_bwd_x")
            du = mm(dxbc, w_in[:, d_inner:d_inner + conv_dim], "nt", F32, add=du, name="mamba_xbc_bwd_x")
            du = mm(ddt, w_in[:, d_inner + conv_dim:], "nt", F32, add=du, name="mamba_dt_bwd_x")
        dh, dn = _rms_bwd(s["h"], W["mix_norm"][i], du, dh, name="mix_norm_bwd")
        g["mix_norm"][i] = dn.reshape(D)
    g["hgrn_lb_logits"] = _lb_bwd(W["hgrn_lb_logits"], jnp.concatenate(dlb, axis=0), name="hgrn_lower_bounds_bwd")
    grads = {k: (jnp.stack(v) if isinstance(v, list) else v) for k, v in g.items() if not isinstance(v, _GradSink)}
    return loss, dh.reshape(Bl, L, D), grads


WEIGHTS = ("mix_norm", "ffn_norm", "final_norm", "hgrn_w_in", "hgrn_lb_logits", "hgrn_gnorm", "hgrn_w_out", "m_w_in",
           "m_conv_w", "m_conv_b", "m_dt_bias", "m_A_log", "m_D", "m_norm", "m_w_out", "f_w_up", "f_conv_w", "f_conv_b",
           "f_w_down")
COL_SHARDED = ("hgrn_w_in", "m_w_in", "f_w_up")
STACKED = ("hgrn_w_in", "f_w_up")
ROW_SHARDED = ("hgrn_w_out", "m_w_out", "f_w_down")
SMALL_SHARDED = ("m_conv_w", "m_conv_b", "m_norm", "f_conv_w")
SMALL = tuple(k for k in WEIGHTS if k not in COL_SHARDED + ROW_SHARDED)


def _join_last(g):
    nd = g.ndim
    t = g.transpose(tuple(range(1, nd - 1)) + (0, nd - 1))
    return t.reshape(t.shape[:-2] + (N_DEV * g.shape[-1],))


def kernel(x, mix_norm, ffn_norm, final_norm, hgrn_w_in, hgrn_lb_logits, hgrn_gnorm, hgrn_w_out, m_w_in, m_conv_w, m_conv_b, m_dt_bias, m_A_log, m_D, m_norm, m_w_out, f_w_up, f_conv_w, f_conv_b, f_w_down, loss_target, m_mix_norm, m_ffn_norm, m_final_norm, m_hgrn_w_in, m_hgrn_lb_logits, m_hgrn_gnorm, m_hgrn_w_out, m_m_w_in, m_m_conv_w, m_m_conv_b, m_m_dt_bias, m_m_A_log, m_m_D, m_m_norm, m_m_w_out, m_f_w_up, m_f_conv_w, m_f_conv_b, m_f_w_down, v_mix_norm, v_ffn_norm, v_final_norm, v_hgrn_w_in, v_hgrn_lb_logits, v_hgrn_gnorm, v_hgrn_w_out, v_m_w_in, v_m_conv_w, v_m_conv_b, v_m_dt_bias, v_m_A_log, v_m_D, v_m_norm, v_m_w_out, v_f_w_up, v_f_conv_w, v_f_conv_b, v_f_w_down):
    w = dict(zip(WEIGHTS, (mix_norm, ffn_norm, final_norm, hgrn_w_in, hgrn_lb_logits, hgrn_gnorm, hgrn_w_out, m_w_in, m_conv_w, m_conv_b, m_dt_bias, m_A_log, m_D, m_norm, m_w_out, f_w_up, f_conv_w, f_conv_b, f_w_down)))
    m = dict(zip(WEIGHTS, (m_mix_norm, m_ffn_norm, m_final_norm, m_hgrn_w_in, m_hgrn_lb_logits, m_hgrn_gnorm, m_hgrn_w_out, m_m_w_in, m_m_conv_w, m_m_conv_b, m_m_dt_bias, m_m_A_log, m_m_D, m_m_norm, m_m_w_out, m_f_w_up, m_f_conv_w, m_f_conv_b, m_f_w_down)))
    v = dict(zip(WEIGHTS, (v_mix_norm, v_ffn_norm, v_final_norm, v_hgrn_w_in, v_hgrn_lb_logits, v_hgrn_gnorm, v_hgrn_w_out, v_m_w_in, v_m_conv_w, v_m_conv_b, v_m_dt_bias, v_m_A_log, v_m_D, v_m_norm, v_m_w_out, v_f_w_up, v_f_conv_w, v_f_conv_b, v_f_w_down)))
    me = 4 * lax.axis_index("x") + 2 * lax.axis_index("y") + lax.axis_index("c")

    def gather_now(a, *, name):
        buf = _Buf(a.shape[0], a.shape[1], a.dtype)
        _exchange([_Job(a, "gather", buf)], name=name)
        return buf.arr

    plan = _Plan()
    depth = mix_norm.shape[0]
    use_order = []
    for i in range(depth):
        use_order += [("hgrn_w_in", i // 2), ("hgrn_w_out", i // 2)] if i % 2 == 0 else [("m_w_in", i // 2), ("m_w_out", i // 2)]
        use_order += [("f_w_up", i), ("f_w_down", i)]
    gathers = {}
    for k, layer in use_order:
        shard = w[k][layer].astype(BF16)
        gathers[k, layer] = _Buf(shard.shape[0], shard.shape[1], BF16)
        plan.push(shard, "gather_chip", gathers[k, layer])

    def fetch(k, layer):
        g = plan.finish(gathers[k, layer], name=f"gather_{k}")
        if k in STACKED:
            return g
        if k in COL_SHARDED:
            return g.transpose(1, 0, 2).reshape(g.shape[1], N_DEV * g.shape[2])
        return g.reshape(N_DEV * g.shape[1], g.shape[2])

    full = {k: w[k] for k in SMALL if k not in SMALL_SHARDED}
    for k in COL_SHARDED:
        full[k] = _LazyWeights(k, (w[k].shape[0], w[k].shape[1], N_DEV * w[k].shape[2]), fetch)
    for k in ROW_SHARDED:
        full[k] = _LazyWeights(k, (w[k].shape[0], N_DEV * w[k].shape[1], w[k].shape[2]), fetch)
    shard_shapes = [w[k].shape for k in SMALL_SHARDED]
    gathered = gather_now(_pack([w[k] for k in SMALL_SHARDED]), name="gather_small_params")
    for k, g in zip(SMALL_SHARDED, _unpack(gathered, shard_shapes, lead=(N_DEV,))):
        full[k] = _join_last(g)

    lands = {k: _Buf(math.prod(w[k].shape[:-1]), w[k].shape[-1], BF16) for k in COL_SHARDED + ROW_SHARDED}

    def push(k, layer, g):
        if k in STACKED:
            parts = g
        elif k in COL_SHARDED:
            parts = g.reshape(g.shape[0], N_DEV, g.shape[1] // N_DEV).transpose(1, 0, 2)
        else:
            parts = g.reshape(N_DEV, g.shape[0] // N_DEV, g.shape[1])
        plan.push(parts, "scatter", lands[k], q0=layer * parts.shape[1])

    loss, grad_x, grads = _local_step(x, loss_target, full, plan, push)

    out = {}
    for k in ("m_w_out", "m_w_in", "f_w_down", "f_w_up", "hgrn_w_out", "hgrn_w_in"):
        land = plan.finish(lands[k], name="scatter_" + k)
        rows, n = math.prod(w[k].shape[:-1]), w[k].shape[-1]
        res = _adamw_sharded(land, w[k].reshape(rows, n), m[k].reshape(rows, n), v[k].reshape(rows, n),
                             name="adamw_" + k, jobs=plan.take("adamw_" + k))
        out[k] = tuple(r.reshape(w[k].shape) for r in res)

    small_full_shapes = [(1, 1)] + [grads[k].shape for k in SMALL]
    packed = _pack([loss] + [grads[k] for k in SMALL])
    summed = _sum_devices(gather_now(packed, name="gather_small_grads"), name="sum_small_grads")
    parts = _unpack(summed, small_full_shapes)
    loss_all, small_g = parts[0], dict(zip(SMALL, parts[1:]))
    for k in SMALL_SHARDED:
        n = w[k].shape[-1]
        small_g[k] = lax.dynamic_slice_in_dim(small_g[k], me * n, n, axis=small_g[k].ndim - 1)
    shapes = [w[k].shape for k in SMALL]
    res = _adamw_packed(_pack([small_g[k] for k in SMALL]), _pack([w[k] for k in SMALL]), _pack([m[k] for k in SMALL]),
                        _pack([v[k] for k in SMALL]), name="adamw_small")
    res = [_unpack(r, shapes) for r in res]
    for i, k in enumerate(SMALL):
        out[k] = (small_g[k], res[0][i], res[1][i], res[2][i])

    return (loss_all.reshape(()), grad_x, *[out[k][0] for k in WEIGHTS], *[out[k][1] for k in WEIGHTS],
            *[out[k][2] for k in WEIGHTS], *[out[k][3] for k in WEIGHTS])
```

```python
import functools
import math

import jax
import jax.numpy as jnp
from jax import lax
from jax.experimental import pallas as pl
from jax.experimental.pallas import tpu as pltpu

F32 = jnp.float32
BF16 = jnp.bfloat16
HIGHEST = lax.Precision.HIGHEST

NORM_EPS = 1e-5
ADAM_LR, ADAM_B1, ADAM_B2, ADAM_EPS, ADAM_WD, ADAM_STEP = 0.001, 0.9, 0.999, 1e-08, 0.01, 10

N_DEV = 8
LANES = 128
V7X_VMEM_BYTES = 64 * 1024 * 1024
VMEM_LIMIT = V7X_VMEM_BYTES * 3 // 4

HGRN_HEAD = 128
HGRN_CHUNK = 128
HGRN_SUB = 16
M_HEADDIM = 64
M_GROUPS = 8
M_D_STATE = 128
M_CONV = 4
M_CHUNK = 256
FFN_CONV = 3
CONV_COLS = 256
HGRN_PAR = 2


def _params(dims=None, **kw):
    return pltpu.CompilerParams(dimension_semantics=dims, vmem_limit_bytes=VMEM_LIMIT, **kw)


def _tile(dim, target, align=LANES, stretch=False):
    if stretch:
        t = _tile(dim, target, align)
        if 4 * t >= 3 * target:
            return t
        for up in range((target // align + 1) * align, 3 * target // 2 + 1, align):
            if dim % up == 0:
                return up
        return t
    t = (min(target, dim) // align) * align
    while t >= align:
        if dim % t == 0:
            return t
        if 2 * t < target and dim <= 2 * target:
            return dim
        t -= align
    return dim


class _Buf:
    def __init__(self, rows, n, dtype):
        self.arr = lax.empty((N_DEV, rows, n), dtype)


SIBLING = 1
OTHER_CHIPS = (2, 4, 6)


class _Job:
    def __init__(self, src, mode, buf, r0=0, r1=None, q0=0):
        rows = src.shape[-2]
        self.src, self.mode, self.buf, self.r0, self.r1, self.q0 = src, mode, buf, r0, rows if r1 is None else r1, q0


def _job_copies(job, src_ref, out_ref, send_sems, recv_sems, local_sem):
    x, y, c = lax.axis_index("x"), lax.axis_index("y"), lax.axis_index("c")
    me = 4 * x + 2 * y + c
    n = job.r1 - job.r0
    rows, land = pl.ds(job.r0, n), pl.ds(job.q0, n)

    def peer_of(k):
        kx, ky, kc = (k >> 2) & 1, (k >> 1) & 1, k & 1
        px, py, pc = (1 - x if kx else x), (1 - y if ky else y), (1 - c if kc else c)
        return 4 * px + 2 * py + pc, (px, py, pc)

    pairs = []
    if job.mode == "forward":
        sib, sib_id = peer_of(SIBLING)
        for i, k in enumerate(OTHER_CHIPS):
            mine, _ = peer_of(k)
            theirs, _ = peer_of(k ^ SIBLING)
            sems = dict(send_sem=send_sems.at[i], recv_sem=recv_sems.at[i], device_id=sib_id, device_id_type=pl.DeviceIdType.MESH)
            send = pltpu.make_async_remote_copy(src_ref=out_ref.at[mine, land], dst_ref=out_ref.at[mine, land], **sems)
            landing = pltpu.make_async_remote_copy(src_ref=out_ref.at[mine, land], dst_ref=out_ref.at[theirs, land], **sems)
            pairs.append((send, landing))
        return None, pairs
    part = (lambda d: src_ref.at[d, rows]) if job.mode == "scatter" else (lambda d: src_ref.at[rows])
    local = pltpu.make_async_copy(part(me), out_ref.at[me, land], local_sem)
    for k in ((SIBLING,) + OTHER_CHIPS if job.mode == "gather_chip" else range(1, N_DEV)):
        peer, peer_id = peer_of(k)
        sems = dict(send_sem=send_sems.at[k - 1], recv_sem=recv_sems.at[k - 1], device_id=peer_id,
                    device_id_type=pl.DeviceIdType.MESH)
        send = pltpu.make_async_remote_copy(src_ref=part(peer), dst_ref=out_ref.at[me, land], **sems)
        landing = pltpu.make_async_remote_copy(src_ref=part(me), dst_ref=out_ref.at[peer, land], **sems)
        pairs.append((send, landing))
    return local, pairs


def _jobs_start(jobs, src_refs, out_refs, send_sems, recv_sems, local_sems):
    for i, job in enumerate(jobs):
        local, pairs = _job_copies(job, src_refs[i], out_refs[i], send_sems.at[i], recv_sems.at[i], local_sems.at[i])
        if local is not None:
            local.start()
        for send, _ in pairs:
            send.start()


def _jobs_wait(jobs, src_refs, out_refs, send_sems, recv_sems, local_sems):
    for i, job in enumerate(jobs):
        local, pairs = _job_copies(job, src_refs[i], out_refs[i], send_sems.at[i], recv_sems.at[i], local_sems.at[i])
        for _, landing in pairs:
            landing.wait_recv()
        for send, _ in pairs:
            send.wait_send()
        if local is not None:
            local.wait()


def _job_scratch(n):
    return [pltpu.SemaphoreType.DMA((n, N_DEV - 1)), pltpu.SemaphoreType.DMA((n, N_DEV - 1)), pltpu.SemaphoreType.DMA((n,))]


def _buf_shapes(jobs):
    assert len({id(j.buf) for j in jobs}) == len(jobs), "one job per buffer and call"
    return tuple(jax.ShapeDtypeStruct(j.buf.arr.shape, j.buf.arr.dtype) for j in jobs)


def _exchange(jobs, *, name):
    n = len(jobs)

    def body(*refs):
        srcs, dsts, sems = refs[:n], refs[2 * n:3 * n], refs[3 * n:]
        _jobs_start(jobs, srcs, dsts, *sems)
        _jobs_wait(jobs, srcs, dsts, *sems)

    hbm = pl.BlockSpec(memory_space=pl.ANY)
    outs = pl.pallas_call(
        body, name=name, out_shape=_buf_shapes(jobs), in_specs=[hbm] * (2 * n), out_specs=(hbm,) * n,
        input_output_aliases={n + i: i for i in range(n)},
        scratch_shapes=_job_scratch(n), compiler_params=pltpu.CompilerParams(has_side_effects=True),
    )(*[j.src for j in jobs], *[j.buf.arr for j in jobs])
    for j, o in zip(jobs, outs):
        j.buf.arr = o


def _call(body, *, name, out_shape, grid, in_specs, out_specs, scratch_shapes=(), dims, jobs=()):
    single = not isinstance(out_shape, (tuple, list))
    out_shape = (out_shape,) if single else tuple(out_shape)
    out_specs = (out_specs,) if single else tuple(out_specs)
    n_in, n_out, n_scr, nj = len(in_specs), len(out_shape), len(scratch_shapes), len(jobs)
    if not jobs:
        plain = pl.pallas_call(body, name=name, out_shape=out_shape, grid=grid, in_specs=list(in_specs), out_specs=out_specs,
                               scratch_shapes=list(scratch_shapes), compiler_params=_params(dims))

        def run_plain(*args):
            r = plain(*args)
            return r[0] if single else tuple(r)

        return run_plain

    def carrying(*refs):
        ins, srcs = refs[:n_in], refs[n_in:n_in + nj]
        o0 = n_in + 2 * nj
        outs, dsts = refs[o0:o0 + n_out], refs[o0 + n_out:o0 + n_out + nj]
        scr, sems = refs[o0 + n_out + nj:o0 + n_out + nj + n_scr], refs[o0 + n_out + nj + n_scr:]
        ids = [pl.program_id(d) for d in range(len(grid))]
        first = functools.reduce(jnp.logical_and, [i == 0 for i in ids])
        last = functools.reduce(jnp.logical_and, [i == g - 1 for i, g in zip(ids, grid)])

        @pl.when(first)
        def _():
            _jobs_start(jobs, srcs, dsts, *sems)

        body(*ins, *outs, *scr)

        @pl.when(last)
        def _():
            _jobs_wait(jobs, srcs, dsts, *sems)

    hbm = pl.BlockSpec(memory_space=pl.ANY)
    call = pl.pallas_call(
        carrying, name=name, out_shape=out_shape + _buf_shapes(jobs), grid=grid,
        in_specs=list(in_specs) + [hbm] * (2 * nj), out_specs=out_specs + (hbm,) * nj,
        input_output_aliases={n_in + nj + i: n_out + i for i in range(nj)},
        scratch_shapes=list(scratch_shapes) + _job_scratch(nj),
        compiler_params=_params(("arbitrary",) * len(grid), has_side_effects=True))

    def run(*args):
        r = call(*args, *[j.src for j in jobs], *[j.buf.arr for j in jobs])
        for j, o in zip(jobs, r[n_out:]):
            j.buf.arr = o
        return r[0] if single else tuple(r[:n_out])

    return run


def _dg(a, b, ca, cb):
    return lax.dot_general(a.astype(BF16), b.astype(BF16), (((ca,), (cb,)), ((), ())), preferred_element_type=F32)


@jax.custom_vjp
def _dot_nn(a, b):
    return _dg(a, b, 1, 0)


def _dot_nn_f(a, b):
    return _dg(a, b, 1, 0), (a, b)


def _dot_nn_b(res, g):
    a, b = res
    return _dg(g, b, 1, 1), _dg(a, g, 0, 0)


_dot_nn.defvjp(_dot_nn_f, _dot_nn_b)


@jax.custom_vjp
def _dot_nt(a, b):
    return _dg(a, b, 1, 1)


def _dot_nt_f(a, b):
    return _dg(a, b, 1, 1), (a, b)


def _dot_nt_b(res, g):
    a, b = res
    return _dg(g, b, 1, 0), _dg(g, a, 0, 0)


_dot_nt.defvjp(_dot_nt_f, _dot_nt_b)


@jax.custom_vjp
def _dot_tn(a, b):
    return _dg(a, b, 0, 0)


def _dot_tn_f(a, b):
    return _dg(a, b, 0, 0), (a, b)


def _dot_tn_b(res, g):
    a, b = res
    return _dg(b, g, 1, 1), _dg(a, g, 1, 0)


_dot_tn.defvjp(_dot_tn_f, _dot_tn_b)


def _tri(n):
    return (lax.broadcasted_iota(jnp.int32, (n, n), 0) >= lax.broadcasted_iota(jnp.int32, (n, n), 1)).astype(F32)


def _silu(x):
    return x * jax.nn.sigmoid(x)


def _matmul(a, b, mode, out_dtype, *, name, add=None, out_stack=None, tm=1024, tn=1024, tk=2048, jobs=()):
    a_n = a.shape[2] if a.ndim == 3 else None
    b_n = b.shape[2] if b.ndim == 3 else None
    a_shape = (a.shape[1], a.shape[0] * a.shape[2]) if a_n else a.shape
    b_shape = (b.shape[1], b.shape[0] * b.shape[2]) if b_n else b.shape
    if mode == "nn":
        (M, K), (K2, N) = a_shape, b_shape
    elif mode == "nt":
        (M, K), (N, K2) = a_shape, b_shape
    else:
        (K, M), (K2, N) = a_shape, b_shape
    assert K == K2, (a.shape, b.shape, mode)
    n_lim = {"m": M, "n": N, "k": K}
    for dim, n in (("m" if mode == "tn" else "k", a_n), ("k" if mode == "nt" else "n", b_n), ("n", N // out_stack if out_stack else None)):
        if n:
            n_lim[dim] = math.gcd(n_lim[dim], n)
    tm, tn = _tile(n_lim["m"], tm, 8 if M % LANES else LANES, stretch=True), _tile(n_lim["n"], tn, stretch=True)
    tk = _tile(n_lim["k"], tk)
    nk = K // tk
    ca, cb = {"nn": (1, 0), "nt": (1, 1), "tn": (0, 0)}[mode]

    def body(*refs):
        a_ref, b_ref = refs[:2]
        add_ref = refs[2] if add is not None else None
        o_ref = refs[2 + (add is not None)]

        def finish(r):
            if add is not None:
                r = r + add_ref[...]
            o_ref[...] = r.astype(o_ref.dtype)

        if nk == 1:
            finish(_dg(a_ref[...], b_ref[...], ca, cb))
            return
        acc_ref = refs[-1]
        k = pl.program_id(2)

        @pl.when(k == 0)
        def _():
            acc_ref[...] = _dg(a_ref[...], b_ref[...], ca, cb)

        @pl.when((k > 0) & (k < nk - 1))
        def _():
            acc_ref[...] += _dg(a_ref[...], b_ref[...], ca, cb)

        @pl.when(k == nk - 1)
        def _():
            finish(acc_ref[...] + _dg(a_ref[...], b_ref[...], ca, cb))

    def spec(rows, cols, t_rows, t_cols, n):
        if not n:
            return pl.BlockSpec((t_rows, t_cols), lambda i, j, k: (rows(i, j, k), cols(i, j, k)))
        per = n // t_cols
        return pl.BlockSpec((None, t_rows, t_cols), lambda i, j, k: (cols(i, j, k) // per, rows(i, j, k), cols(i, j, k) % per))

    gi, gj, gk = (lambda i, j, k: i), (lambda i, j, k: j), (lambda i, j, k: k)
    a_spec = spec(gk, gi, tk, tm, a_n) if mode == "tn" else spec(gi, gk, tm, tk, a_n)
    b_spec = spec(gj, gk, tn, tk, b_n) if mode == "nt" else spec(gk, gj, tk, tn, b_n)
    o_spec = spec(gi, gj, tm, tn, N // out_stack if out_stack else None)
    out_shape = (out_stack, M, N // out_stack) if out_stack else (M, N)
    in_specs, args = [a_spec, b_spec], [a, b]
    if add is not None:
        assert not out_stack
        in_specs.append(o_spec)
        args.append(add)
    return _call(
        body, name=name, out_shape=jax.ShapeDtypeStruct(out_shape, out_dtype), grid=(M // tm, N // tn, nk),
        in_specs=in_specs, out_specs=o_spec, scratch_shapes=[pltpu.VMEM((tm, tn), F32)] if nk > 1 else [],
        dims=("parallel", "parallel", "arbitrary"), jobs=jobs,
    )(*args)


def _rms_fwd(h, w, *, name):
    T, D = h.shape
    tr = _tile(T, 256, 8)

    def body(h_ref, w_ref, u_ref):
        x = h_ref[...]
        u_ref[...] = (x * lax.rsqrt(jnp.mean(x * x, axis=-1, keepdims=True) + NORM_EPS) * w_ref[...]).astype(u_ref.dtype)

    return pl.pallas_call(
        body, name=name, out_shape=jax.ShapeDtypeStruct((T, D), BF16), grid=(T // tr,),
        in_specs=[pl.BlockSpec((tr, D), lambda i: (i, 0)), pl.BlockSpec((1, D), lambda i: (0, 0))],
        out_specs=pl.BlockSpec((tr, D), lambda i: (i, 0)), compiler_params=_params(("parallel",)),
    )(h, w.reshape(1, D))


def _rms_bwd(h, w, du, dh_in, *, name):
    T, D = h.shape
    tr = _tile(T, 256, 8)

    def body(h_ref, w_ref, du_ref, dhin_ref, dh_ref, dw_ref):
        x = h_ref[...]
        g = du_ref[...].astype(F32)
        rstd = lax.rsqrt(jnp.mean(x * x, axis=-1, keepdims=True) + NORM_EPS)
        xhat = x * rstd
        gx = g * w_ref[...]
        dh_ref[...] = dhin_ref[...] + rstd * (gx - xhat * jnp.mean(gx * xhat, axis=-1, keepdims=True))

        @pl.when(pl.program_id(0) == 0)
        def _():
            dw_ref[...] = jnp.zeros_like(dw_ref)

        dw_ref[...] += jnp.sum(g * xhat, axis=0, keepdims=True)

    row = pl.BlockSpec((tr, D), lambda i: (i, 0))
    vec = pl.BlockSpec((1, D), lambda i: (0, 0))
    return pl.pallas_call(
        body, name=name, out_shape=(jax.ShapeDtypeStruct((T, D), F32), jax.ShapeDtypeStruct((1, D), F32)), grid=(T // tr,),
        in_specs=[row, vec, row, row], out_specs=(row, vec), compiler_params=_params(("arbitrary",)),
    )(h, w.reshape(1, D), du, dh_in)


def _loss_head(h, w, target, *, name):
    T, D = h.shape
    tr = _tile(T, 256, 8)

    def body(h_ref, w_ref, t_ref, loss_ref, dh_ref, dw_ref):
        x = h_ref[...]
        rstd = lax.rsqrt(jnp.mean(x * x, axis=-1, keepdims=True) + NORM_EPS)
        xhat = x * rstd
        err = xhat * w_ref[...] - t_ref[...]
        g = err * (1.0 / D)
        gx = g * w_ref[...]
        dh_ref[...] = rstd * (gx - xhat * jnp.mean(gx * xhat, axis=-1, keepdims=True))

        @pl.when(pl.program_id(0) == 0)
        def _():
            dw_ref[...] = jnp.zeros_like(dw_ref)
            loss_ref[...] = jnp.zeros_like(loss_ref)

        dw_ref[...] += jnp.sum(g * xhat, axis=0, keepdims=True)
        loss_ref[...] += (0.5 / D) * jnp.sum(jnp.sum(err * err, axis=-1, keepdims=True), axis=0, keepdims=True)

    row = pl.BlockSpec((tr, D), lambda i: (i, 0))
    vec = pl.BlockSpec((1, D), lambda i: (0, 0))
    one = pl.BlockSpec((1, 1), lambda i: (0, 0))
    return pl.pallas_call(
        body, name=name,
        out_shape=(jax.ShapeDtypeStruct((1, 1), F32), jax.ShapeDtypeStruct((T, D), F32), jax.ShapeDtypeStruct((1, D), F32)),
        grid=(T // tr,), in_specs=[row, vec, row], out_specs=(one, row, vec), compiler_params=_params(("arbitrary",)),
    )(h, w.reshape(1, D), target)


def _shift_down(x, s):
    if s == 0:
        return x
    rows = lax.broadcasted_iota(jnp.int32, x.shape, 0)
    return jnp.where(rows >= s, pltpu.roll(x, s, 0), 0.0)


def _shift_up(x, s):
    if s == 0:
        return x
    n = x.shape[0]
    rows = lax.broadcasted_iota(jnp.int32, x.shape, 0)
    return jnp.where(rows < n - s, pltpu.roll(x, n - s, 0), 0.0)


def _conv_pre(a, w_ref, b_ref, taps):
    pre = b_ref[0:1, :] + w_ref[taps - 1:taps, :] * a
    for k in range(taps - 1):
        pre = pre + w_ref[k:k + 1, :] * _shift_down(a, taps - 1 - k)
    return pre


def _conv_bwd(a, dpre, w_ref, taps):
    da = w_ref[taps - 1:taps, :] * dpre
    rows = []
    for k in range(taps - 1):
        up = _shift_up(dpre, taps - 1 - k)
        da = da + w_ref[k:k + 1, :] * up
        rows.append(jnp.sum(up * a, axis=0, keepdims=True))
    rows.append(jnp.sum(dpre * a, axis=0, keepdims=True))
    rows.append(jnp.sum(dpre, axis=0, keepdims=True))
    return da, rows


def _ffn_gate_fwd(a, cw, cb, Bl, L, *, name, jobs=()):
    T, F2 = a.shape
    F = F2 // 2
    tc = _tile(F, CONV_COLS)
    nj = F // tc

    def body(ag_ref, au_ref, wg_ref, wu_ref, bg_ref, bu_ref, y_ref, h_ref):
        hg = _conv_pre(ag_ref[...].astype(F32), wg_ref, bg_ref, FFN_CONV)
        hu = _conv_pre(au_ref[...].astype(F32), wu_ref, bu_ref, FFN_CONV)
        y_ref[...] = (_silu(hg) * hu).astype(y_ref.dtype)
        h_ref[0] = hg.astype(h_ref.dtype)
        h_ref[1] = hu.astype(h_ref.dtype)

    blk = lambda off: pl.BlockSpec((L, tc), lambda b, j: (b, j + off))
    wblk = lambda off: pl.BlockSpec((FFN_CONV, tc), lambda b, j: (0, j + off))
    bblk = lambda off: pl.BlockSpec((1, tc), lambda b, j: (0, j + off))
    return _call(
        body, name=name, out_shape=(jax.ShapeDtypeStruct((T, F), BF16), jax.ShapeDtypeStruct((2, T, F), BF16)), grid=(Bl, nj),
        in_specs=[blk(0), blk(nj), wblk(0), wblk(nj), bblk(0), bblk(nj)],
        out_specs=(blk(0), pl.BlockSpec((2, L, tc), lambda b, j: (0, b, j))),
        dims=("parallel", "parallel"), jobs=jobs,
    )(a, a, cw, cw, cb.reshape(1, F2), cb.reshape(1, F2))


def _ffn_gate_bwd(a, h, cw, dy, Bl, L, *, name, jobs=()):
    T, F2 = a.shape
    F = F2 // 2
    tc = _tile(F, CONV_COLS)
    nj = F // tc

    def body(ag_ref, au_ref, h_ref, wg_ref, wu_ref, dy_ref, da_ref, pg_ref, pu_ref):
        ag, au = ag_ref[...].astype(F32), au_ref[...].astype(F32)
        hg, hu = h_ref[0].astype(F32), h_ref[1].astype(F32)
        g = dy_ref[...].astype(F32)
        s = jax.nn.sigmoid(hg)
        dhg = g * hu * (s * (1.0 + hg * (1.0 - s)))
        dhu = g * (hg * s)
        dag, rg = _conv_bwd(ag, dhg, wg_ref, FFN_CONV)
        dau, ru = _conv_bwd(au, dhu, wu_ref, FFN_CONV)
        da_ref[0] = dag.astype(da_ref.dtype)
        da_ref[1] = dau.astype(da_ref.dtype)

        @pl.when(pl.program_id(1) == 0)
        def _():
            pg_ref[...] = jnp.zeros_like(pg_ref)
            pu_ref[...] = jnp.zeros_like(pu_ref)

        for k in range(FFN_CONV + 1):
            pg_ref[k:k + 1, :] += rg[k]
            pu_ref[k:k + 1, :] += ru[k]

    blk = lambda off: pl.BlockSpec((L, tc), lambda j, b: (b, j + off))
    wblk = lambda off: pl.BlockSpec((FFN_CONV, tc), lambda j, b: (0, j + off))
    pblk = lambda off: pl.BlockSpec((8, tc), lambda j, b: (0, j + off))
    both = pl.BlockSpec((2, L, tc), lambda j, b: (0, b, j))
    da, pg, pu = _call(
        body, name=name,
        out_shape=(jax.ShapeDtypeStruct((2, T, F), BF16), jax.ShapeDtypeStruct((8, F), F32), jax.ShapeDtypeStruct((8, F), F32)),
        grid=(nj, Bl),
        in_specs=[blk(0), blk(nj), both, wblk(0), wblk(nj), blk(0)],
        out_specs=(both, pblk(0), pblk(0)),
        dims=("parallel", "arbitrary"), jobs=jobs,
    )(a, a, h, cw, cw, dy)
    return da, jnp.concatenate([pg, pu], axis=1)


def _mconv_fwd(xbc, cw, cb, Bl, L, *, name, jobs=()):
    T, W = xbc.shape
    tc = _tile(W, CONV_COLS)

    def body(a_ref, w_ref, b_ref, y_ref):
        y_ref[...] = _silu(_conv_pre(a_ref[...], w_ref, b_ref, M_CONV))

    blk = pl.BlockSpec((L, tc), lambda b, j: (b, j))
    return _call(
        body, name=name, out_shape=jax.ShapeDtypeStruct((T, W), F32), grid=(Bl, W // tc),
        in_specs=[blk, pl.BlockSpec((M_CONV, tc), lambda b, j: (0, j)), pl.BlockSpec((1, tc), lambda b, j: (0, j))],
        out_specs=blk, dims=("parallel", "parallel"), jobs=jobs,
    )(xbc, cw, cb.reshape(1, W))


def _mconv_bwd(xbc, cw, cb, dy, Bl, L, *, name):
    T, W = xbc.shape
    tc = _tile(W, CONV_COLS)

    def body(a_ref, w_ref, b_ref, dy_ref, da_ref, p_ref):
        a = a_ref[...]
        pre = _conv_pre(a, w_ref, b_ref, M_CONV)
        s = jax.nn.sigmoid(pre)
        dpre = dy_ref[...] * (s * (1.0 + pre * (1.0 - s)))
        da, rows = _conv_bwd(a, dpre, w_ref, M_CONV)
        da_ref[...] = da.astype(da_ref.dtype)

        @pl.when(pl.program_id(1) == 0)
        def _():
            p_ref[...] = jnp.zeros_like(p_ref)

        for k in range(M_CONV + 1):
            p_ref[k:k + 1, :] += rows[k]

    blk = pl.BlockSpec((L, tc), lambda j, b: (b, j))
    return pl.pallas_call(
        body, name=name, out_shape=(jax.ShapeDtypeStruct((T, W), BF16), jax.ShapeDtypeStruct((8, W), F32)),
        grid=(W // tc, Bl),
        in_specs=[blk, pl.BlockSpec((M_CONV, tc), lambda j, b: (0, j)), pl.BlockSpec((1, tc), lambda j, b: (0, j)), blk],
        out_specs=(blk, pl.BlockSpec((8, tc), lambda j, b: (0, j))),
        compiler_params=_params(("parallel", "arbitrary")),
    )(xbc, cw, cb.reshape(1, W), dy)


def _hgrn_chunk(st, qr, fr, v, gr, lb, gn):
    C = qr.shape[0]
    q = _silu(qr)
    logf = jnp.log(lb + (1.0 - lb) * jax.nn.sigmoid(fr))
    k = (1.0 - lb) * jax.nn.sigmoid(-fr)
    b = jnp.dot(_tri(C), logf, precision=HIGHEST, preferred_element_type=F32)
    o = _dot_nt(q * jnp.exp(b), st)
    keep = lax.broadcasted_iota(jnp.int32, (HGRN_SUB, HGRN_SUB), 0) >= lax.broadcasted_iota(jnp.int32, (HGRN_SUB, HGRN_SUB), 1)
    cols = lax.broadcasted_iota(jnp.int32, (HGRN_SUB, C), 1)
    parts = []
    for blk in range(C // HGRN_SUB):
        r0 = blk * HGRN_SUB
        bi, qi, ki, vi = b[r0:r0 + HGRN_SUB], q[r0:r0 + HGRN_SUB], k[r0:r0 + HGRN_SUB], v[r0:r0 + HGRN_SUB]
        decay = jnp.exp(jnp.minimum(bi[:, None, :] - bi[None, :, :], 0.0))
        a_diag = jnp.where(keep, jnp.sum(qi[:, None, :] * ki[None, :, :] * decay, axis=-1), 0.0)
        oi = _dot_nn(a_diag, vi)
        if blk > 0:
            bs = b[r0 - 1:r0]
            a_off = _dot_nt(qi * jnp.exp(bi - bs), k * jnp.exp(jnp.minimum(bs - b, 0.0)))
            oi = oi + _dot_nn(jnp.where(cols < r0, a_off, 0.0), v)
        parts.append(oi)
    o = o + jnp.concatenate(parts, axis=0)
    bl = b[C - 1:C]
    st1 = st * jnp.exp(bl) + _dot_tn(v, k * jnp.exp(bl - b))
    og = o * lax.rsqrt(jnp.mean(o * o, axis=-1, keepdims=True) + NORM_EPS) * gn * _silu(gr)
    return st1, og


def _hgrn_specs(Bl, L, H, order):
    W, HB = HGRN_PAR * HGRN_HEAD, H // HGRN_PAR

    def at(off):
        if order == "bh":
            return pl.BlockSpec((L, W), lambda b, h: (b, h + off))
        return pl.BlockSpec((L, W), lambda h, b: (b, h + off))
    return [at(0), at(HB), at(2 * HB), at(3 * HB)]


def _hgrn_fwd(P, lb, gn, Bl, L, *, name, jobs=()):
    T, D4 = P.shape
    D = D4 // 4
    H = D // HGRN_HEAD
    C = HGRN_CHUNK
    NC = L // C
    W, HB = HGRN_PAR * HGRN_HEAD, H // HGRN_PAR
    heads = [slice(p * HGRN_HEAD, (p + 1) * HGRN_HEAD) for p in range(HGRN_PAR)]

    def body(q_ref, f_ref, v_ref, g_ref, lb_ref, gn_ref, og_ref, st_ref):
        gnv = gn_ref[...]

        def step(c, sts):
            r = pl.ds(pl.multiple_of(c * C, C), C)
            new = []
            for p, hd in enumerate(heads):
                st_ref[p, c] = sts[p]
                st1, og = _hgrn_chunk(sts[p], q_ref[r, hd], f_ref[r, hd], v_ref[r, hd], g_ref[r, hd], lb_ref[:, hd], gnv)
                og_ref[r, hd] = og.astype(og_ref.dtype)
                new.append(st1)
            return tuple(new)

        lax.fori_loop(0, NC, step, tuple(jnp.zeros((HGRN_HEAD, HGRN_HEAD), F32) for _ in heads))

    return _call(
        body, name=name,
        out_shape=(jax.ShapeDtypeStruct((T, D), BF16), jax.ShapeDtypeStruct((Bl, H, NC, HGRN_HEAD, HGRN_HEAD), F32)),
        grid=(Bl, HB),
        in_specs=_hgrn_specs(Bl, L, H, "bh") + [pl.BlockSpec((1, W), lambda b, h: (0, h)), pl.BlockSpec((1, HGRN_HEAD), lambda b, h: (0, 0))],
        out_specs=(pl.BlockSpec((L, W), lambda b, h: (b, h)),
                   pl.BlockSpec((None, HGRN_PAR, NC, HGRN_HEAD, HGRN_HEAD), lambda b, h: (b, h, 0, 0, 0))),
        dims=("parallel", "parallel"), jobs=jobs,
    )(P, P, P, P, lb.reshape(1, D), gn.reshape(1, HGRN_HEAD))


def _hgrn_bwd(P, lb, gn, states, dog, Bl, L, *, name, jobs=()):
    T, D4 = P.shape
    D = D4 // 4
    H = D // HGRN_HEAD
    C = HGRN_CHUNK
    NC = L // C
    W, HB = HGRN_PAR * HGRN_HEAD, H // HGRN_PAR
    heads = [slice(p * HGRN_HEAD, (p + 1) * HGRN_HEAD) for p in range(HGRN_PAR)]

    def body(q_ref, f_ref, v_ref, g_ref, lb_ref, gn_ref, st_ref, dog_ref, dp_ref, dlb_ref, dgn_ref):
        gnv = gn_ref[...]

        @pl.when(pl.program_id(1) == 0)
        def _():
            dlb_ref[...] = jnp.zeros_like(dlb_ref)

        @pl.when((pl.program_id(0) == 0) & (pl.program_id(1) == 0))
        def _():
            dgn_ref[...] = jnp.zeros_like(dgn_ref)

        def step(i, dsts):
            c = NC - 1 - i
            r = pl.ds(pl.multiple_of(c * C, C), C)
            new, dgn_sum = [], jnp.zeros((1, HGRN_HEAD), F32)
            for p, hd in enumerate(heads):
                _, vjp = jax.vjp(_hgrn_chunk, st_ref[p, c], q_ref[r, hd], f_ref[r, hd], v_ref[r, hd], g_ref[r, hd],
                                 lb_ref[:, hd], gnv)
                dst0, dq, df, dv, dg, dlb, dgn = vjp((dsts[p], dog_ref[r, hd].astype(F32)))
                dp_ref[0, r, hd] = dq.astype(dp_ref.dtype)
                dp_ref[1, r, hd] = df.astype(dp_ref.dtype)
                dp_ref[2, r, hd] = dv.astype(dp_ref.dtype)
                dp_ref[3, r, hd] = dg.astype(dp_ref.dtype)
                dlb_ref[:, hd] += dlb
                dgn_sum = dgn_sum + dgn
                new.append(dst0)
            dgn_ref[...] += dgn_sum
            return tuple(new)

        lax.fori_loop(0, NC, step, tuple(jnp.zeros((HGRN_HEAD, HGRN_HEAD), F32) for _ in heads))

    return _call(
        body, name=name,
        out_shape=(jax.ShapeDtypeStruct((4, T, D), BF16), jax.ShapeDtypeStruct((1, D), F32), jax.ShapeDtypeStruct((1, HGRN_HEAD), F32)),
        grid=(HB, Bl),
        in_specs=_hgrn_specs(Bl, L, H, "hb") + [
            pl.BlockSpec((1, W), lambda h, b: (0, h)), pl.BlockSpec((1, HGRN_HEAD), lambda h, b: (0, 0)),
            pl.BlockSpec((None, HGRN_PAR, NC, HGRN_HEAD, HGRN_HEAD), lambda h, b: (b, h, 0, 0, 0)),
            pl.BlockSpec((L, W), lambda h, b: (b, h))],
        out_specs=(pl.BlockSpec((4, L, W), lambda h, b: (0, b, h)),
                   pl.BlockSpec((1, W), lambda h, b: (0, h)), pl.BlockSpec((1, HGRN_HEAD), lambda h, b: (0, 0))),
        dims=("arbitrary", "arbitrary"), jobs=jobs,
    )(P, P, P, P, lb.reshape(1, D), gn.reshape(1, HGRN_HEAD), states, dog)


def _lower_bounds(logits):
    p = jax.nn.softmax(logits, axis=0)
    rows, run = [], jnp.zeros_like(p[0:1])
    for i in range(logits.shape[0]):
        run = run + p[i:i + 1]
        rows.append(run - p[0:1])
    return jnp.concatenate(rows, axis=0) if len(rows) > 1 else rows[0]


def _lb_fwd(logits, *, name):
    def body(x_ref, o_ref):
        o_ref[...] = _lower_bounds(x_ref[...])
    return pl.pallas_call(body, name=name, out_shape=jax.ShapeDtypeStruct(logits.shape, F32))(logits)


def _lb_bwd(logits, dlb, *, name):
    def body(x_ref, g_ref, o_ref):
        _, vjp = jax.vjp(_lower_bounds, x_ref[...])
        o_ref[...] = vjp(g_ref[...])[0]
    return pl.pallas_call(body, name=name, out_shape=jax.ShapeDtypeStruct(logits.shape, F32))(logits, dlb)


def _ssd_pre(dtc_raw, dtr_raw, bias_c, bias_r, alog_c, alog_r):
    C = dtc_raw.shape[0]
    dt_c = jax.nn.softplus(dtc_raw + bias_c)
    dt_r = jax.nn.softplus(dtr_raw + bias_r)
    acs_c = jnp.dot(_tri(C), dt_c * -jnp.exp(alog_c), precision=HIGHEST, preferred_element_type=F32)
    acs_r = lax.dot_general(dt_r * -jnp.exp(alog_r), _tri(C), (((1,), (1,)), ((), ())), precision=HIGHEST, preferred_element_type=F32)
    return dt_c, acs_c, acs_r


def _ssd_head(h0, xs, G, Bm, Cm, dt, acs, acs_row, dskip):
    C = xs.shape[0]
    causal = lax.broadcasted_iota(jnp.int32, (C, C), 0) >= lax.broadcasted_iota(jnp.int32, (C, C), 1)
    Lm = jnp.where(causal, jnp.exp(jnp.minimum(acs - acs_row, 0.0)), 0.0)
    X = xs * dt
    y = _dot_nn(G * Lm, X) + _dot_nt(Cm * jnp.exp(acs), h0) + dskip * xs
    alast = acs[C - 1:C]
    h1 = jnp.exp(alast) * h0 + _dot_tn(X, Bm * jnp.exp(alast - acs))
    return y, h1


def _ssd_layout(Bl, L, d_inner, order):
    G = M_GROUPS
    hpg = d_inner // M_HEADDIM // G
    gw = hpg * M_HEADDIM
    ix = (lambda b, g: (b, g)) if order == "bg" else (lambda g, b: (b, g))
    def cols(width, off):
        if order == "bg":
            return pl.BlockSpec((L, width), lambda b, g: (b, g + off))
        return pl.BlockSpec((L, width), lambda g, b: (b, g + off))
    def per_group(shape):
        n = len(shape)
        if order == "bg":
            return pl.BlockSpec((None,) + shape, lambda b, g: (g,) + (0,) * n)
        return pl.BlockSpec((None,) + shape, lambda g, b: (g,) + (0,) * n)
    if order == "bg":
        dtc = pl.BlockSpec((None, L, LANES), lambda b, g: (g, b, 0))
        dtr = pl.BlockSpec((None, 8, L), lambda b, g: (g, 0, b))
    else:
        dtc = pl.BlockSpec((None, L, LANES), lambda g, b: (g, b, 0))
        dtr = pl.BlockSpec((None, 8, L), lambda g, b: (g, 0, b))
    nb = d_inner // M_D_STATE
    return hpg, gw, cols, per_group, dtc, dtr, nb


def _ssd_fwd(xa, dtc, dtr, prm, Bl, L, d_inner, *, name, jobs=()):
    T = xa.shape[0]
    C = M_CHUNK
    NC = L // C
    G = M_GROUPS
    hpg, gw, cols, per_group, dtc_spec, dtr_spec, nb = _ssd_layout(Bl, L, d_inner, "bg")

    def body(xs_ref, b_ref, c_ref, dtc_ref, dtr_ref, bc_ref, br_ref, ac_ref, ar_ref, dk_ref, y_ref, hs_ref,
             dt_s, acs_s, acr_s, h_s):
        h_s[...] = jnp.zeros_like(h_s)

        def step(c, carry):
            r = pl.ds(pl.multiple_of(c * C, C), C)
            dt_c, acs_c, acs_r = _ssd_pre(dtc_ref[r, :], dtr_ref[:, r], bc_ref[...], br_ref[...], ac_ref[...], ar_ref[...])
            dt_s[...] = dt_c
            acs_s[...] = acs_c
            acr_s[...] = acs_r
            Bm, Cm = b_ref[r, :], c_ref[r, :]
            Gm = _dot_nt(Cm, Bm)
            for j in range(hpg):
                pc = slice(j * M_HEADDIM, (j + 1) * M_HEADDIM)
                h0 = h_s[j]
                hs_ref[c, j] = h0
                y, h1 = _ssd_head(h0, xs_ref[r, pc], Gm, Bm, Cm, dt_s[:, j:j + 1], acs_s[:, j:j + 1], acr_s[j:j + 1, :],
                                  dk_ref[0:1, j:j + 1])
                y_ref[r, pc] = y
                h_s[j] = h1
            return carry

        lax.fori_loop(0, NC, step, 0)

    return _call(
        body, name=name,
        out_shape=(jax.ShapeDtypeStruct((T, d_inner), F32), jax.ShapeDtypeStruct((Bl, G, NC, hpg, M_HEADDIM, M_D_STATE), F32)),
        grid=(Bl, G),
        in_specs=[cols(gw, 0), cols(M_D_STATE, nb), cols(M_D_STATE, nb + G), dtc_spec, dtr_spec,
                  per_group((1, LANES)), per_group((8, 1)), per_group((1, LANES)), per_group((8, 1)), per_group((1, LANES))],
        out_specs=(cols(gw, 0), pl.BlockSpec((None, None, NC, hpg, M_HEADDIM, M_D_STATE), lambda b, g: (b, g, 0, 0, 0, 0))),
        scratch_shapes=[pltpu.VMEM((C, LANES), F32), pltpu.VMEM((C, LANES), F32), pltpu.VMEM((8, C), F32),
                        pltpu.VMEM((hpg, M_HEADDIM, M_D_STATE), F32)],
        dims=("parallel", "parallel"), jobs=jobs,
    )(xa, xa, xa, dtc, dtr, *prm)


def _ssd_bwd(xa, dtc, dtr, prm, states, dy, Bl, L, d_inner, *, name, jobs=()):
    T = xa.shape[0]
    C = M_CHUNK
    NC = L // C
    G = M_GROUPS
    hpg, gw, cols, per_group, dtc_spec, dtr_spec, nb = _ssd_layout(Bl, L, d_inner, "gb")

    def body(xs_ref, b_ref, c_ref, dtc_ref, dtr_ref, bc_ref, br_ref, ac_ref, ar_ref, dk_ref, hs_ref, dy_ref,
             dxs_ref, db_ref, dc_ref, ddtc_ref, ddtr_ref, gbc_ref, gbr_ref, gac_ref, gar_ref, gdk_ref,
             dt_s, acs_s, acr_s, ddt_s, dacs_s, dacr_s, dh_s):
        dh_s[...] = jnp.zeros_like(dh_s)

        @pl.when(pl.program_id(1) == 0)
        def _():
            for ref in (gbc_ref, gbr_ref, gac_ref, gar_ref, gdk_ref):
                ref[...] = jnp.zeros_like(ref)

        def step(i, carry):
            c = NC - 1 - i
            r = pl.ds(pl.multiple_of(c * C, C), C)
            pre_in = (dtc_ref[r, :], dtr_ref[:, r], bc_ref[...], br_ref[...], ac_ref[...], ar_ref[...])
            (dt_c, acs_c, acs_r), pre_vjp = jax.vjp(_ssd_pre, *pre_in)
            dt_s[...] = dt_c
            acs_s[...] = acs_c
            acr_s[...] = acs_r
            ddt_s[...] = jnp.zeros_like(ddt_s)
            dacs_s[...] = jnp.zeros_like(dacs_s)
            dacr_s[...] = jnp.zeros_like(dacr_s)
            Bm, Cm = b_ref[r, :], c_ref[r, :]
            Gm = _dot_nt(Cm, Bm)
            dG = jnp.zeros((C, C), F32)
            dB = jnp.zeros((C, M_D_STATE), F32)
            dC = jnp.zeros((C, M_D_STATE), F32)
            for j in range(hpg):
                pc = slice(j * M_HEADDIM, (j + 1) * M_HEADDIM)
                one = slice(j, j + 1)
                _, vjp = jax.vjp(_ssd_head, hs_ref[c, j], xs_ref[r, pc], Gm, Bm, Cm, dt_s[:, one], acs_s[:, one], acr_s[one, :],
                                 dk_ref[0:1, one])
                dh0, dxs, dGj, dBj, dCj, ddt, dacs, dacr, ddk = vjp((dy_ref[r, pc], dh_s[j]))
                dh_s[j] = dh0
                dxs_ref[r, pc] = dxs
                dG, dB, dC = dG + dGj, dB + dBj, dC + dCj
                ddt_s[:, one] = ddt
                dacs_s[:, one] = dacs
                dacr_s[one, :] = dacr
                gdk_ref[0:1, one] += ddk
            db_ref[r, :] = dB + _dot_tn(dG, Cm)
            dc_ref[r, :] = dC + _dot_nn(dG, Bm)
            ddtc, ddtr, gbc, gbr, gac, gar = pre_vjp((ddt_s[...], dacs_s[...], dacr_s[...]))
            ddtc_ref[r, :] = ddtc
            ddtr_ref[:, r] = ddtr
            gbc_ref[...] += gbc
            gbr_ref[...] += gbr
            gac_ref[...] += gac
            gar_ref[...] += gar
            return carry

        lax.fori_loop(0, NC, step, 0)

    GN = G * M_D_STATE
    bc_cols = lambda off: pl.BlockSpec((L, M_D_STATE), lambda g, b: (b, g + off))
    outs = _call(
        body, name=name,
        out_shape=(jax.ShapeDtypeStruct((T, d_inner), F32), jax.ShapeDtypeStruct((T, GN), F32), jax.ShapeDtypeStruct((T, GN), F32),
                   jax.ShapeDtypeStruct(dtc.shape, F32), jax.ShapeDtypeStruct(dtr.shape, F32),
                   jax.ShapeDtypeStruct((G, 1, LANES), F32), jax.ShapeDtypeStruct((G, 8, 1), F32),
                   jax.ShapeDtypeStruct((G, 1, LANES), F32), jax.ShapeDtypeStruct((G, 8, 1), F32),
                   jax.ShapeDtypeStruct((G, 1, LANES), F32)),
        grid=(G, Bl),
        in_specs=[cols(gw, 0), cols(M_D_STATE, nb), cols(M_D_STATE, nb + G), dtc_spec, dtr_spec,
                  per_group((1, LANES)), per_group((8, 1)), per_group((1, LANES)), per_group((8, 1)), per_group((1, LANES)),
                  pl.BlockSpec((None, None, NC, hpg, M_HEADDIM, M_D_STATE), lambda g, b: (b, g, 0, 0, 0, 0)), cols(gw, 0)],
        out_specs=(cols(gw, 0), bc_cols(0), bc_cols(0), dtc_spec, dtr_spec,
                   per_group((1, LANES)), per_group((8, 1)), per_group((1, LANES)), per_group((8, 1)), per_group((1, LANES))),
        scratch_shapes=[pltpu.VMEM((C, LANES), F32), pltpu.VMEM((C, LANES), F32), pltpu.VMEM((8, C), F32),
                        pltpu.VMEM((C, LANES), F32), pltpu.VMEM((C, LANES), F32), pltpu.VMEM((8, C), F32),
                        pltpu.VMEM((hpg, M_HEADDIM, M_D_STATE), F32)],
        dims=("arbitrary", "arbitrary"), jobs=jobs,
    )(xa, xa, xa, dtc, dtr, *prm, states, dy)
    return outs


def _mgate(y, z, w):
    t = y * _silu(z)
    return t * lax.rsqrt(jnp.mean(t * t, axis=-1, keepdims=True) + NORM_EPS) * w


def _mgate_fwd(y, z, w, *, name, jobs=()):
    T, d_inner = y.shape
    gw = d_inner // M_GROUPS
    tr = _tile(T, 512, 8)

    def body(y_ref, z_ref, w_ref, o_ref):
        o_ref[...] = _mgate(y_ref[...], z_ref[...], w_ref[...]).astype(o_ref.dtype)

    blk = pl.BlockSpec((tr, gw), lambda i, g: (i, g))
    return _call(
        body, name=name, out_shape=jax.ShapeDtypeStruct((T, d_inner), BF16), grid=(T // tr, M_GROUPS),
        in_specs=[blk, blk, pl.BlockSpec((1, gw), lambda i, g: (0, g))], out_specs=blk,
        dims=("parallel", "parallel"), jobs=jobs,
    )(y, z, w.reshape(1, d_inner))


def _mgate_bwd(y, z, w, dyg, *, name):
    T, d_inner = y.shape
    gw = d_inner // M_GROUPS
    tr = _tile(T, 512, 8)

    def body(y_ref, z_ref, w_ref, g_ref, dy_ref, dz_ref, dw_ref):
        _, vjp = jax.vjp(_mgate, y_ref[...], z_ref[...], w_ref[...])
        dy, dz, dw = vjp(g_ref[...].astype(F32))
        dy_ref[...] = dy
        dz_ref[...] = dz.astype(dz_ref.dtype)

        @pl.when(pl.program_id(1) == 0)
        def _():
            dw_ref[...] = jnp.zeros_like(dw_ref)

        dw_ref[...] += dw

    blk = pl.BlockSpec((tr, gw), lambda g, i: (i, g))
    vec = pl.BlockSpec((1, gw), lambda g, i: (0, g))
    return pl.pallas_call(
        body, name=name,
        out_shape=(jax.ShapeDtypeStruct((T, d_inner), F32), jax.ShapeDtypeStruct((T, d_inner), BF16), jax.ShapeDtypeStruct((1, d_inner), F32)),
        grid=(M_GROUPS, T // tr), in_specs=[blk, blk, vec, blk], out_specs=(blk, blk, vec),
        compiler_params=_params(("parallel", "arbitrary")),
    )(y, z, w.reshape(1, d_inner), dyg)


def _adamw_math(w, g, m, v):
    m1 = ADAM_B1 * m + (1.0 - ADAM_B1) * g
    v1 = ADAM_B2 * v + (1.0 - ADAM_B2) * (g * g)
    m_hat = m1 / (1.0 - ADAM_B1 ** ADAM_STEP)
    v_hat = v1 / (1.0 - ADAM_B2 ** ADAM_STEP)
    delta = -ADAM_LR * (m_hat / (jnp.sqrt(v_hat) + ADAM_EPS) + ADAM_WD * w)
    return delta, m1, v1


def _adamw_sharded(land, w, m, v, *, name, jobs=()):
    R, n = w.shape
    tr = _tile(R, 128, 8)

    def body(l_ref, w_ref, m_ref, v_ref, g_ref, d_ref, m1_ref, v1_ref):
        g = l_ref[0].astype(F32)
        for k in range(1, N_DEV):
            g = g + l_ref[k].astype(F32)
        d, m1, v1 = _adamw_math(w_ref[...], g, m_ref[...], v_ref[...])
        g_ref[...] = g
        d_ref[...] = d
        m1_ref[...] = m1
        v1_ref[...] = v1

    blk = pl.BlockSpec((tr, n), lambda i: (i, 0))
    out = jax.ShapeDtypeStruct((R, n), F32)
    return _call(
        body, name=name, out_shape=(out, out, out, out), grid=(R // tr,),
        in_specs=[pl.BlockSpec((N_DEV, tr, n), lambda i: (0, i, 0)), blk, blk, blk], out_specs=(blk, blk, blk, blk),
        dims=("parallel",), jobs=jobs,
    )(land, w, m, v)


def _sum_devices(parts, *, name):
    _, R, n = parts.shape
    tr = _tile(R, 512, 8)

    def body(p_ref, o_ref):
        g = p_ref[0]
        for k in range(1, N_DEV):
            g = g + p_ref[k]
        o_ref[...] = g

    return pl.pallas_call(
        body, name=name, out_shape=jax.ShapeDtypeStruct((R, n), F32), grid=(R // tr,),
        in_specs=[pl.BlockSpec((N_DEV, tr, n), lambda i: (0, i, 0))], out_specs=pl.BlockSpec((tr, n), lambda i: (i, 0)),
        compiler_params=_params(("parallel",)),
    )(parts)


def _adamw_packed(g, w, m, v, *, name):
    R, n = w.shape
    tr = _tile(R, 512, 8)

    def body(g_ref, w_ref, m_ref, v_ref, d_ref, m1_ref, v1_ref):
        d, m1, v1 = _adamw_math(w_ref[...], g_ref[...], m_ref[...], v_ref[...])
        d_ref[...] = d
        m1_ref[...] = m1
        v1_ref[...] = v1

    blk = pl.BlockSpec((tr, n), lambda i: (i, 0))
    out = jax.ShapeDtypeStruct((R, n), F32)
    return pl.pallas_call(
        body, name=name, out_shape=(out, out, out), grid=(R // tr,), in_specs=[blk] * 4, out_specs=(blk, blk, blk),
        compiler_params=_params(("parallel",)),
    )(g, w, m, v)


PACK_ALIGN = 8 * LANES


def _pack(arrs):
    parts = []
    for a in arrs:
        flat = a.reshape(-1).astype(F32)
        pad = (-flat.shape[0]) % PACK_ALIGN
        parts.append(jnp.pad(flat, (0, pad)).reshape(-1, LANES))
    return jnp.concatenate(parts, axis=0)


def _unpack(packed, shapes, lead=()):
    out, r = [], 0
    nl = len(lead)
    for s in shapes:
        n = math.prod(s)
        rows = (n + PACK_ALIGN - 1) // PACK_ALIGN * (PACK_ALIGN // LANES)
        part = lax.slice_in_dim(packed, r, r + rows, axis=nl)
        out.append(part.reshape(lead + (rows * LANES,))[..., :n].reshape(lead + tuple(s)))
        r += rows
    return out


EXCHANGE_BYTES_PER_US = 80e3
CHIP_GATHER_BYTES_PER_US = 68e3
SIBLING_BYTES_PER_US = 250e3
EXCHANGE_ROW_ALIGN = 64
CARRY_FILL = 1.0

CALL_US = {
    "hgrn_in_fwd": 160, "hgrn_scan_fwd": 780, "hgrn_out_fwd": 60, "ffn_up_fwd": 230, "ffn_gate_fwd": 96, "ffn_down_fwd": 130,
    "mamba_z_fwd": 85, "mamba_xbc_fwd": 123, "mamba_scan_fwd": 400, "mamba_out_fwd": 91, "mamba_conv_fwd": 95, "mamba_gate_fwd": 79,
    "ffn_down_bwd_x": 187, "ffn_down_bwd_w": 186, "ffn_gate_bwd": 240, "ffn_up_bwd_x": 251, "ffn_up_bwd_w": 229,
    "hgrn_out_bwd_x": 50, "hgrn_out_bwd_w": 50, "hgrn_scan_bwd": 1700, "hgrn_in_bwd_x": 188, "hgrn_in_bwd_w": 158,
    "mamba_out_bwd_x": 89, "mamba_out_bwd_w": 92, "mamba_scan_bwd": 1170, "mamba_z_bwd_x": 110, "mamba_xbc_bwd_x": 136,
    "mamba_z_bwd_w": 82, "mamba_xbc_bwd_w": 120,
    "adamw_hgrn_w_in": 65, "adamw_hgrn_w_out": 16, "adamw_m_w_in": 87, "adamw_m_w_out": 32, "adamw_f_w_up": 168,
    "adamw_f_w_down": 109,
}


class _Plan:
    def __init__(self):
        self.queue = []
        self.handoff = []

    def push(self, src, mode, buf, q0=0):
        rows, n = src.shape[-2], src.shape[-1]
        blocks = len(OTHER_CHIPS) if mode == "gather_chip" else N_DEV - 1
        rate = CHIP_GATHER_BYTES_PER_US if mode == "gather_chip" else EXCHANGE_BYTES_PER_US
        self.queue.append(dict(src=src, mode=mode, buf=buf, r0=0, r1=rows, q0=q0, us_per_row=blocks * n * src.dtype.itemsize / rate))

    def _hand_on(self, it, r0, r1):
        q0, q1 = it["q0"] + r0, it["q0"] + r1
        if self.handoff and self.handoff[-1]["buf"] is it["buf"] and self.handoff[-1]["r1"] == q0:
            self.handoff[-1]["r1"] = q1
            return
        us = len(OTHER_CHIPS) * it["src"].shape[-1] * it["src"].dtype.itemsize / SIBLING_BYTES_PER_US
        self.handoff.append(dict(src=jnp.zeros((8, LANES), it["src"].dtype), mode="forward", buf=it["buf"], r0=q0, r1=q1, q0=0,
                                 us_per_row=us))

    def take(self, name):
        budget, jobs, used, spawned = CARRY_FILL * CALL_US[name], [], set(), []
        for queue in (self.handoff, self.queue):
            i = 0
            while i < len(queue):
                it = queue[i]
                if id(it["buf"]) in used:
                    i += 1
                    continue
                left = it["r1"] - it["r0"]
                fit = int(budget / it["us_per_row"])
                rows = left if fit >= left else fit // EXCHANGE_ROW_ALIGN * EXCHANGE_ROW_ALIGN
                if rows <= 0:
                    break
                jobs.append(_Job(it["src"], it["mode"], it["buf"], it["r0"], it["r0"] + rows, it["q0"] + it["r0"]))
                used.add(id(it["buf"]))
                budget -= rows * it["us_per_row"]
                if it["mode"] == "gather_chip":
                    spawned.append((it, it["r0"], it["r0"] + rows))
                it["r0"] += rows
                if it["r0"] < it["r1"]:
                    break
                queue.pop(i)
        for it, r0, r1 in spawned:
            self._hand_on(it, r0, r1)
        return jobs

    def finish(self, buf, *, name):
        mine = [it for it in self.queue if it["buf"] is buf]
        self.queue = [it for it in self.queue if it["buf"] is not buf]
        for i, it in enumerate(mine):
            _exchange([_Job(it["src"], it["mode"], buf, it["r0"], it["r1"], it["q0"] + it["r0"])], name=f"{name}_{i}" if i else name)
            if it["mode"] == "gather_chip":
                self._hand_on(it, it["r0"], it["r1"])
        mine = [it for it in self.handoff if it["buf"] is buf]
        self.handoff = [it for it in self.handoff if it["buf"] is not buf]
        for i, it in enumerate(mine):
            _exchange([_Job(it["src"], "forward", buf, it["r0"], it["r1"], it["r0"])], name=f"{name}_handoff_{i}" if i else f"{name}_handoff")
        return buf.arr


def _pad_to(a, axis, size):
    pad = [(0, 0)] * a.ndim
    pad[axis] = (0, size - a.shape[axis])
    return jnp.pad(a, pad)


def _ssd_operands(dt_raw, dt_bias, a_log, dskip):
    T, heads = dt_raw.shape
    G = M_GROUPS
    hpg = heads // G
    t3 = dt_raw.reshape(T, G, hpg)
    dtc = _pad_to(t3.transpose(1, 0, 2), 2, LANES)
    dtr = _pad_to(t3.transpose(1, 2, 0), 1, 8)
    col = lambda p: _pad_to(p.reshape(G, 1, hpg), 2, LANES)
    row = lambda p: _pad_to(p.reshape(G, hpg, 1), 1, 8)
    return dtc, dtr, (col(dt_bias), row(dt_bias), col(a_log), row(a_log), col(dskip))


def _ssd_unpack(ddtc, ddtr, gbc, gbr, gac, gar, gdk, heads):
    G = M_GROUPS
    hpg = heads // G
    T = ddtc.shape[1]
    ddt = ddtc[:, :, :hpg].transpose(1, 0, 2).reshape(T, heads) + ddtr[:, :hpg, :].transpose(2, 0, 1).reshape(T, heads)
    both = lambda c, r: c[:, 0, :hpg].reshape(heads) + r[:, :hpg, 0].reshape(heads)
    return ddt, both(gbc, gbr), both(gac, gar), gdk[:, 0, :hpg].reshape(heads)


class _LazyWeights:
    def __init__(self, name, shape, fetch):
        self.name, self.shape, self.fetch, self.got = name, shape, fetch, {}

    def __getitem__(self, layer):
        if layer not in self.got:
            self.got[layer] = self.fetch(self.name, layer)
        return self.got[layer]


class _GradSink:
    def __init__(self, name, layers, push):
        self.name, self.shape, self.push = name, (layers,), push

    def __setitem__(self, layer, g):
        self.push(self.name, layer, g)


def _local_step(x, target, W, plan, push):
    Bl, L, D = x.shape
    T = Bl * L
    depth = W["mix_norm"].shape[0]
    d_inner = W["m_w_out"].shape[1]
    heads = d_inner // M_HEADDIM
    conv_dim = W["m_conv_w"].shape[2]

    def mm(a, b, mode, out_dtype, *, name, add=None, out_stack=None):
        return _matmul(a, b, mode, out_dtype, name=name, add=add, out_stack=out_stack,
                       jobs=plan.take(name) if name in CALL_US else ())

    lb = _lb_fwd(W["hgrn_lb_logits"], name="hgrn_lower_bounds")
    h = x.reshape(T, D)
    saved = []
    for i in range(depth):
        j = i // 2
        s = {"h": h}
        u = _rms_fwd(h, W["mix_norm"][i], name="mix_norm_fwd")
        s["u"] = u
        if i % 2 == 0:
            P = mm(u, W["hgrn_w_in"][j], "nn", F32, name="hgrn_in_fwd")
            og, st = _hgrn_fwd(P, lb[j], W["hgrn_gnorm"][j], Bl, L, name="hgrn_scan_fwd", jobs=plan.take("hgrn_scan_fwd"))
            h = mm(og, W["hgrn_w_out"][j], "nn", F32, add=h, name="hgrn_out_fwd")
            s.update(P=P, og=og, st=st)
        else:
            w_in = W["m_w_in"][j]
            z = mm(u, w_in[:, :d_inner], "nn", F32, name="mamba_z_fwd")
            xbc = mm(u, w_in[:, d_inner:d_inner + conv_dim], "nn", F32, name="mamba_xbc_fwd")
            dt_raw = mm(u, w_in[:, d_inner + conv_dim:], "nn", F32, name="mamba_dt_fwd")
            xa = _mconv_fwd(xbc, W["m_conv_w"][j], W["m_conv_b"][j], Bl, L, name="mamba_conv_fwd", jobs=plan.take("mamba_conv_fwd"))
            dtc, dtr, prm = _ssd_operands(dt_raw, W["m_dt_bias"][j], W["m_A_log"][j], W["m_D"][j])
            y, hs = _ssd_fwd(xa, dtc, dtr, prm, Bl, L, d_inner, name="mamba_scan_fwd", jobs=plan.take("mamba_scan_fwd"))
            yg = _mgate_fwd(y, z, W["m_norm"][j], name="mamba_gate_fwd", jobs=plan.take("mamba_gate_fwd"))
            h = mm(yg, W["m_w_out"][j], "nn", F32, add=h, name="mamba_out_fwd")
            s.update(z=z, xbc=xbc, xa=xa, dtc=dtc, dtr=dtr, prm=prm, y=y, hs=hs, yg=yg)
        s["h2"] = h
        u2 = _rms_fwd(h, W["ffn_norm"][i], name="ffn_norm_fwd")
        a = mm(u2, W["f_w_up"][i], "nn", BF16, name="ffn_up_fwd")
        yf, hf = _ffn_gate_fwd(a, W["f_conv_w"][i], W["f_conv_b"][i], Bl, L, name="ffn_gate_fwd", jobs=plan.take("ffn_gate_fwd"))
        h = mm(yf, W["f_w_down"][i], "nn", F32, add=h, name="ffn_down_fwd")
        s.update(u2=u2, a=a, yf=yf, hf=hf)
        saved.append(s)

    loss, dh, d_final = _loss_head(h, W["final_norm"], target.reshape(T, D), name="loss_head")

    g = {k: (_GradSink(k, W[k].shape[0], push) if isinstance(W[k], _LazyWeights) else [None] * W[k].shape[0])
         for k in W if k != "final_norm"}
    g["final_norm"] = d_final.reshape(D)
    dlb = [None] * W["hgrn_lb_logits"].shape[0]
    for i in reversed(range(depth)):
        j = i // 2
        s = saved[i]
        g["f_w_down"][i] = mm(s["yf"], dh, "tn", BF16, name="ffn_down_bwd_w")
        dyf = mm(dh, W["f_w_down"][i], "nt", BF16, name="ffn_down_bwd_x")
        da, pf = _ffn_gate_bwd(s["a"], s["hf"], W["f_conv_w"][i], dyf, Bl, L, name="ffn_gate_bwd",
                               jobs=plan.take("ffn_gate_bwd"))
        g["f_w_up"][i] = mm(s["u2"], da, "tn", BF16, out_stack=N_DEV, name="ffn_up_bwd_w")
        du2 = mm(da, W["f_w_up"][i], "nt", F32, name="ffn_up_bwd_x")
        g["f_conv_w"][i], g["f_conv_b"][i] = pf[:FFN_CONV], pf[FFN_CONV]
        dh, dn = _rms_bwd(s["h2"], W["ffn_norm"][i], du2, dh, name="ffn_norm_bwd")
        g["ffn_norm"][i] = dn.reshape(D)
        if i % 2 == 0:
            g["hgrn_w_out"][j] = mm(s["og"], dh, "tn", BF16, name="hgrn_out_bwd_w")
            dog = mm(dh, W["hgrn_w_out"][j], "nt", BF16, name="hgrn_out_bwd_x")
            dP4, dlb_j, dgn = _hgrn_bwd(s["P"], lb[j], W["hgrn_gnorm"][j], s["st"], dog, Bl, L, name="hgrn_scan_bwd",
                                        jobs=plan.take("hgrn_scan_bwd"))
            g["hgrn_w_in"][j] = mm(s["u"], dP4, "tn", BF16, out_stack=N_DEV, name="hgrn_in_bwd_w")
            du = mm(dP4, W["hgrn_w_in"][j], "nt", F32, name="hgrn_in_bwd_x")
            dlb[j] = dlb_j
            g["hgrn_gnorm"][j] = dgn.reshape(HGRN_HEAD)
        else:
            w_in = W["m_w_in"][j]
            g["m_w_out"][j] = mm(s["yg"], dh, "tn", BF16, name="mamba_out_bwd_w")
            dyg = mm(dh, W["m_w_out"][j], "nt", BF16, name="mamba_out_bwd_x")
            dy, dz, dnw = _mgate_bwd(s["y"], s["z"], W["m_norm"][j], dyg, name="mamba_gate_bwd")
            dxs, dB, dC, ddtc, ddtr, gbc, gbr, gac, gar, gdk = _ssd_bwd(
                s["xa"], s["dtc"], s["dtr"], s["prm"], s["hs"], dy, Bl, L, d_inner, name="mamba_scan_bwd",
                jobs=plan.take("mamba_scan_bwd"))
            ddt, g["m_dt_bias"][j], g["m_A_log"][j], g["m_D"][j] = _ssd_unpack(ddtc, ddtr, gbc, gbr, gac, gar, gdk, heads)
            dxa = jnp.concatenate([dxs, dB, dC], axis=1)
            dxbc, pm = _mconv_bwd(s["xbc"], W["m_conv_w"][j], W["m_conv_b"][j], dxa, Bl, L, name="mamba_conv_bwd")
            g["m_conv_w"][j], g["m_conv_b"][j], g["m_norm"][j] = pm[:M_CONV], pm[M_CONV], dnw.reshape(d_inner)
            ddt = ddt.astype(BF16)
            g["m_w_in"][j] = jnp.concatenate([
                mm(s["u"], dz, "tn", BF16, name="mamba_z_bwd_w"), mm(s["u"], dxbc, "tn", BF16, name="mamba_xbc_bwd_w"),
                mm(s["u"], ddt, "tn", BF16, name="mamba_dt_bwd_w")], axis=1)
            du = mm(dz, w_in[:, :d_inner], "nt", F32, name="mamba_z_bwd_x")
            du = mm(dxbc, w_in[:, d_inner:d_inner + conv_dim], "nt", F32, add=du, name="mamba_xbc_bwd_x")
            du = mm(ddt, w_in[:, d_inner + conv_dim:], "nt", F32, add=du, name="mamba_dt_bwd_x")
        dh, dn = _rms_bwd(s["h"], W["mix_norm"][i], du, dh, name="mix_norm_bwd")
        g["mix_norm"][i] = dn.reshape(D)
    g["hgrn_lb_logits"] = _lb_bwd(W["hgrn_lb_logits"], jnp.concatenate(dlb, axis=0), name="hgrn_lower_bounds_bwd")
    grads = {k: (jnp.stack(v) if isinstance(v, list) else v) for k, v in g.items() if not isinstance(v, _GradSink)}
    return loss, dh.reshape(Bl, L, D), grads


WEIGHTS = ("mix_norm", "ffn_norm", "final_norm", "hgrn_w_in", "hgrn_lb_logits", "hgrn_gnorm", "hgrn_w_out", "m_w_in",
           "m_conv_w", "m_conv_b", "m_dt_bias", "m_A_log", "m_D", "m_norm", "m_w_out", "f_w_up", "f_conv_w", "f_conv_b",
           "f_w_down")
COL_SHARDED = ("hgrn_w_in", "m_w_in", "f_w_up")
STACKED = ("hgrn_w_in", "f_w_up")
ROW_SHARDED = ("hgrn_w_out", "m_w_out", "f_w_down")
SMALL_SHARDED = ("m_conv_w", "m_conv_b", "m_norm", "f_conv_w")
SMALL = tuple(k for k in WEIGHTS if k not in COL_SHARDED + ROW_SHARDED)


def _join_last(g):
    nd = g.ndim
    t = g.transpose(tuple(range(1, nd - 1)) + (0, nd - 1))
    return t.reshape(t.shape[:-2] + (N_DEV * g.shape[-1],))


def kernel(x, mix_norm, ffn_norm, final_norm, hgrn_w_in, hgrn_lb_logits, hgrn_gnorm, hgrn_w_out, m_w_in, m_conv_w, m_conv_b, m_dt_bias, m_A_log, m_D, m_norm, m_w_out, f_w_up, f_conv_w, f_conv_b, f_w_down, loss_target, m_mix_norm, m_ffn_norm, m_final_norm, m_hgrn_w_in, m_hgrn_lb_logits, m_hgrn_gnorm, m_hgrn_w_out, m_m_w_in, m_m_conv_w, m_m_conv_b, m_m_dt_bias, m_m_A_log, m_m_D, m_m_norm, m_m_w_out, m_f_w_up, m_f_conv_w, m_f_conv_b, m_f_w_down, v_mix_norm, v_ffn_norm, v_final_norm, v_hgrn_w_in, v_hgrn_lb_logits, v_hgrn_gnorm, v_hgrn_w_out, v_m_w_in, v_m_conv_w, v_m_conv_b, v_m_dt_bias, v_m_A_log, v_m_D, v_m_norm, v_m_w_out, v_f_w_up, v_f_conv_w, v_f_conv_b, v_f_w_down):
    w = dict(zip(WEIGHTS, (mix_norm, ffn_norm, final_norm, hgrn_w_in, hgrn_lb_logits, hgrn_gnorm, hgrn_w_out, m_w_in, m_conv_w, m_conv_b, m_dt_bias, m_A_log, m_D, m_norm, m_w_out, f_w_up, f_conv_w, f_conv_b, f_w_down)))
    m = dict(zip(WEIGHTS, (m_mix_norm, m_ffn_norm, m_final_norm, m_hgrn_w_in, m_hgrn_lb_logits, m_hgrn_gnorm, m_hgrn_w_out, m_m_w_in, m_m_conv_w, m_m_conv_b, m_m_dt_bias, m_m_A_log, m_m_D, m_m_norm, m_m_w_out, m_f_w_up, m_f_conv_w, m_f_conv_b, m_f_w_down)))
    v = dict(zip(WEIGHTS, (v_mix_norm, v_ffn_norm, v_final_norm, v_hgrn_w_in, v_hgrn_lb_logits, v_hgrn_gnorm, v_hgrn_w_out, v_m_w_in, v_m_conv_w, v_m_conv_b, v_m_dt_bias, v_m_A_log, v_m_D, v_m_norm, v_m_w_out, v_f_w_up, v_f_conv_w, v_f_conv_b, v_f_w_down)))
    me = 4 * lax.axis_index("x") + 2 * lax.axis_index("y") + lax.axis_index("c")

    def gather_now(a, *, name):
        buf = _Buf(a.shape[0], a.shape[1], a.dtype)
        _exchange([_Job(a, "gather", buf)], name=name)
        return buf.arr

    plan = _Plan()
    depth = mix_norm.shape[0]
    use_order = []
    for i in range(depth):
        use_order += [("hgrn_w_in", i // 2), ("hgrn_w_out", i // 2)] if i % 2 == 0 else [("m_w_in", i // 2), ("m_w_out", i // 2)]
        use_order += [("f_w_up", i), ("f_w_down", i)]
    gathers = {}
    for k, layer in use_order:
        shard = w[k][layer].astype(BF16)
        gathers[k, layer] = _Buf(shard.shape[0], shard.shape[1], BF16)
        plan.push(shard, "gather_chip", gathers[k, layer])

    def fetch(k, layer):
        g = plan.finish(gathers[k, layer], name=f"gather_{k}")
        if k in STACKED:
            return g
        if k in COL_SHARDED:
            return g.transpose(1, 0, 2).reshape(g.shape[1], N_DEV * g.shape[2])
        return g.reshape(N_DEV * g.shape[1], g.shape[2])

    full = {k: w[k] for k in SMALL if k not in SMALL_SHARDED}
    for k in COL_SHARDED:
        full[k] = _LazyWeights(k, (w[k].shape[0], w[k].shape[1], N_DEV * w[k].shape[2]), fetch)
    for k in ROW_SHARDED:
        full[k] = _LazyWeights(k, (w[k].shape[0], N_DEV * w[k].shape[1], w[k].shape[2]), fetch)
    shard_shapes = [w[k].shape for k in SMALL_SHARDED]
    gathered = gather_now(_pack([w[k] for k in SMALL_SHARDED]), name="gather_small_params")
    for k, g in zip(SMALL_SHARDED, _unpack(gathered, shard_shapes, lead=(N_DEV,))):
        full[k] = _join_last(g)

    lands = {k: _Buf(math.prod(w[k].shape[:-1]), w[k].shape[-1], BF16) for k in COL_SHARDED + ROW_SHARDED}

    def push(k, layer, g):
        if k in STACKED:
            parts = g
        elif k in COL_SHARDED:
            parts = g.reshape(g.shape[0], N_DEV, g.shape[1] // N_DEV).transpose(1, 0, 2)
        else:
            parts = g.reshape(N_DEV, g.shape[0] // N_DEV, g.shape[1])
        plan.push(parts, "scatter", lands[k], q0=layer * parts.shape[1])

    loss, grad_x, grads = _local_step(x, loss_target, full, plan, push)

    small_full_shapes = [(1, 1)] + [grads[k].shape for k in SMALL]
    packed = _pack([loss] + [grads[k] for k in SMALL])
    small_parts = _Buf(packed.shape[0], packed.shape[1], F32)
    plan.push(packed, "gather", small_parts)
    out = {}
    for k in ("m_w_out", "m_w_in", "f_w_down", "f_w_up", "hgrn_w_out", "hgrn_w_in"):
        land = plan.finish(lands[k], name="scatter_" + k)
        rows, n = math.prod(w[k].shape[:-1]), w[k].shape[-1]
        res = _adamw_sharded(land, w[k].reshape(rows, n), m[k].reshape(rows, n), v[k].reshape(rows, n),
                             name="adamw_" + k, jobs=plan.take("adamw_" + k))
        out[k] = tuple(r.reshape(w[k].shape) for r in res)

    summed = _sum_devices(plan.finish(small_parts, name="gather_small_grads"), name="sum_small_grads")
    parts = _unpack(summed, small_full_shapes)
    loss_all, small_g = parts[0], dict(zip(SMALL, parts[1:]))
    for k in SMALL_SHARDED:
        n = w[k].shape[-1]
        small_g[k] = lax.dynamic_slice_in_dim(small_g[k], me * n, n, axis=small_g[k].ndim - 1)
    shapes = [w[k].shape for k in SMALL]
    res = _adamw_packed(_pack([small_g[k] for k in SMALL]), _pack([w[k] for k in SMALL]), _pack([m[k] for k in SMALL]),
                        _pack([v[k] for k in SMALL]), name="adamw_small")
    res = [_unpack(r, shapes) for r in res]
    for i, k in enumerate(SMALL):
        out[k] = (small_g[k], res[0][i], res[1][i], res[2][i])

    return (loss_all.reshape(()), grad_x, *[out[k][0] for k in WEIGHTS], *[out[k][1] for k in WEIGHTS],
            *[out[k][2] for k in WEIGHTS], *[out[k][3] for k in WEIGHTS])
```

```python
import functools
import math

import jax
import jax.numpy as jnp
from jax import lax
from jax.experimental import pallas as pl
from jax.experimental.pallas import tpu as pltpu

F32 = jnp.float32
BF16 = jnp.bfloat16
HIGHEST = lax.Precision.HIGHEST

NORM_EPS = 1e-5
ADAM_LR, ADAM_B1, ADAM_B2, ADAM_EPS, ADAM_WD, ADAM_STEP = 0.001, 0.9, 0.999, 1e-08, 0.01, 10

N_DEV = 8
LANES = 128
V7X_VMEM_BYTES = 64 * 1024 * 1024
VMEM_LIMIT = V7X_VMEM_BYTES * 3 // 4

HGRN_HEAD = 128
HGRN_CHUNK = 128
HGRN_SUB = 16
M_HEADDIM = 64
M_GROUPS = 8
M_D_STATE = 128
M_CONV = 4
M_CHUNK = 256
FFN_CONV = 3
CONV_COLS = 256
HGRN_PAR = 2


def _params(dims=None, **kw):
    return pltpu.CompilerParams(dimension_semantics=dims, vmem_limit_bytes=VMEM_LIMIT, **kw)


def _tile(dim, target, align=LANES, stretch=False):
    if stretch:
        t = _tile(dim, target, align)
        if 4 * t >= 3 * target:
            return t
        for up in range((target // align + 1) * align, 3 * target // 2 + 1, align):
            if dim % up == 0:
                return up
        return t
    t = (min(target, dim) // align) * align
    while t >= align:
        if dim % t == 0:
            return t
        if 2 * t < target and dim <= 2 * target:
            return dim
        t -= align
    return dim


class _Buf:
    def __init__(self, rows, n, dtype):
        self.arr = lax.empty((N_DEV, rows, n), dtype)


SIBLING = 1
OTHER_CHIPS = (2, 4, 6)


class _Job:
    def __init__(self, src, mode, buf, r0=0, r1=None, q0=0):
        rows = src.shape[-2]
        self.src, self.mode, self.buf, self.r0, self.r1, self.q0 = src, mode, buf, r0, rows if r1 is None else r1, q0


def _job_copies(job, src_ref, out_ref, send_sems, recv_sems, local_sem):
    x, y, c = lax.axis_index("x"), lax.axis_index("y"), lax.axis_index("c")
    me = 4 * x + 2 * y + c
    n = job.r1 - job.r0
    rows, land = pl.ds(job.r0, n), pl.ds(job.q0, n)

    def peer_of(k):
        kx, ky, kc = (k >> 2) & 1, (k >> 1) & 1, k & 1
        px, py, pc = (1 - x if kx else x), (1 - y if ky else y), (1 - c if kc else c)
        return 4 * px + 2 * py + pc, (px, py, pc)

    pairs = []
    if job.mode == "forward":
        sib, sib_id = peer_of(SIBLING)
        for i, k in enumerate(OTHER_CHIPS):
            mine, _ = peer_of(k)
            theirs, _ = peer_of(k ^ SIBLING)
            sems = dict(send_sem=send_sems.at[i], recv_sem=recv_sems.at[i], device_id=sib_id, device_id_type=pl.DeviceIdType.MESH)
            send = pltpu.make_async_remote_copy(src_ref=out_ref.at[mine, land], dst_ref=out_ref.at[mine, land], **sems)
            landing = pltpu.make_async_remote_copy(src_ref=out_ref.at[mine, land], dst_ref=out_ref.at[theirs, land], **sems)
            pairs.append((send, landing))
        return None, pairs
    part = (lambda d: src_ref.at[d, rows]) if job.mode == "scatter" else (lambda d: src_ref.at[rows])
    local = pltpu.make_async_copy(part(me), out_ref.at[me, land], local_sem)
    for k in ((SIBLING,) + OTHER_CHIPS if job.mode == "gather_chip" else range(1, N_DEV)):
        peer, peer_id = peer_of(k)
        sems = dict(send_sem=send_sems.at[k - 1], recv_sem=recv_sems.at[k - 1], device_id=peer_id,
                    device_id_type=pl.DeviceIdType.MESH)
        send = pltpu.make_async_remote_copy(src_ref=part(peer), dst_ref=out_ref.at[me, land], **sems)
        landing = pltpu.make_async_remote_copy(src_ref=part(me), dst_ref=out_ref.at[peer, land], **sems)
        pairs.append((send, landing))
    return local, pairs


def _jobs_start(jobs, src_refs, out_refs, send_sems, recv_sems, local_sems):
    for i, job in enumerate(jobs):
        local, pairs = _job_copies(job, src_refs[i], out_refs[i], send_sems.at[i], recv_sems.at[i], local_sems.at[i])
        if local is not None:
            local.start()
        for send, _ in pairs:
            send.start()


def _jobs_wait(jobs, src_refs, out_refs, send_sems, recv_sems, local_sems):
    for i, job in enumerate(jobs):
        local, pairs = _job_copies(job, src_refs[i], out_refs[i], send_sems.at[i], recv_sems.at[i], local_sems.at[i])
        for _, landing in pairs:
            landing.wait_recv()
        for send, _ in pairs:
            send.wait_send()
        if local is not None:
            local.wait()


def _job_scratch(n):
    return [pltpu.SemaphoreType.DMA((n, N_DEV - 1)), pltpu.SemaphoreType.DMA((n, N_DEV - 1)), pltpu.SemaphoreType.DMA((n,))]


def _buf_shapes(jobs):
    assert len({id(j.buf) for j in jobs}) == len(jobs), "one job per buffer and call"
    return tuple(jax.ShapeDtypeStruct(j.buf.arr.shape, j.buf.arr.dtype) for j in jobs)


def _exchange(jobs, *, name):
    n = len(jobs)

    def body(*refs):
        srcs, dsts, sems = refs[:n], refs[2 * n:3 * n], refs[3 * n:]
        _jobs_start(jobs, srcs, dsts, *sems)
        _jobs_wait(jobs, srcs, dsts, *sems)

    hbm = pl.BlockSpec(memory_space=pl.ANY)
    outs = pl.pallas_call(
        body, name=name, out_shape=_buf_shapes(jobs), in_specs=[hbm] * (2 * n), out_specs=(hbm,) * n,
        input_output_aliases={n + i: i for i in range(n)},
        scratch_shapes=_job_scratch(n), compiler_params=pltpu.CompilerParams(has_side_effects=True),
    )(*[j.src for j in jobs], *[j.buf.arr for j in jobs])
    for j, o in zip(jobs, outs):
        j.buf.arr = o


def _call(body, *, name, out_shape, grid, in_specs, out_specs, scratch_shapes=(), dims, jobs=()):
    single = not isinstance(out_shape, (tuple, list))
    out_shape = (out_shape,) if single else tuple(out_shape)
    out_specs = (out_specs,) if single else tuple(out_specs)
    n_in, n_out, n_scr, nj = len(in_specs), len(out_shape), len(scratch_shapes), len(jobs)
    if not jobs:
        plain = pl.pallas_call(body, name=name, out_shape=out_shape, grid=grid, in_specs=list(in_specs), out_specs=out_specs,
                               scratch_shapes=list(scratch_shapes), compiler_params=_params(dims))

        def run_plain(*args):
            r = plain(*args)
            return r[0] if single else tuple(r)

        return run_plain

    def carrying(*refs):
        ins, srcs = refs[:n_in], refs[n_in:n_in + nj]
        o0 = n_in + 2 * nj
        outs, dsts = refs[o0:o0 + n_out], refs[o0 + n_out:o0 + n_out + nj]
        scr, sems = refs[o0 + n_out + nj:o0 + n_out + nj + n_scr], refs[o0 + n_out + nj + n_scr:]
        ids = [pl.program_id(d) for d in range(len(grid))]
        first = functools.reduce(jnp.logical_and, [i == 0 for i in ids])
        last = functools.reduce(jnp.logical_and, [i == g - 1 for i, g in zip(ids, grid)])

        @pl.when(first)
        def _():
            _jobs_start(jobs, srcs, dsts, *sems)

        body(*ins, *outs, *scr)

        @pl.when(last)
        def _():
            _jobs_wait(jobs, srcs, dsts, *sems)

    hbm = pl.BlockSpec(memory_space=pl.ANY)
    call = pl.pallas_call(
        carrying, name=name, out_shape=out_shape + _buf_shapes(jobs), grid=grid,
        in_specs=list(in_specs) + [hbm] * (2 * nj), out_specs=out_specs + (hbm,) * nj,
        input_output_aliases={n_in + nj + i: n_out + i for i in range(nj)},
        scratch_shapes=list(scratch_shapes) + _job_scratch(nj),
        compiler_params=_params(("arbitrary",) * len(grid), has_side_effects=True))

    def run(*args):
        r = call(*args, *[j.src for j in jobs], *[j.buf.arr for j in jobs])
        for j, o in zip(jobs, r[n_out:]):
            j.buf.arr = o
        return r[0] if single else tuple(r[:n_out])

    return run


def _dg(a, b, ca, cb):
    return lax.dot_general(a.astype(BF16), b.astype(BF16), (((ca,), (cb,)), ((), ())), preferred_element_type=F32)


@jax.custom_vjp
def _dot_nn(a, b):
    return _dg(a, b, 1, 0)


def _dot_nn_f(a, b):
    return _dg(a, b, 1, 0), (a, b)


def _dot_nn_b(res, g):
    a, b = res
    return _dg(g, b, 1, 1), _dg(a, g, 0, 0)


_dot_nn.defvjp(_dot_nn_f, _dot_nn_b)


@jax.custom_vjp
def _dot_nt(a, b):
    return _dg(a, b, 1, 1)


def _dot_nt_f(a, b):
    return _dg(a, b, 1, 1), (a, b)


def _dot_nt_b(res, g):
    a, b = res
    return _dg(g, b, 1, 0), _dg(g, a, 0, 0)


_dot_nt.defvjp(_dot_nt_f, _dot_nt_b)


@jax.custom_vjp
def _dot_tn(a, b):
    return _dg(a, b, 0, 0)


def _dot_tn_f(a, b):
    return _dg(a, b, 0, 0), (a, b)


def _dot_tn_b(res, g):
    a, b = res
    return _dg(b, g, 1, 1), _dg(a, g, 1, 0)


_dot_tn.defvjp(_dot_tn_f, _dot_tn_b)


def _tri(n):
    return (lax.broadcasted_iota(jnp.int32, (n, n), 0) >= lax.broadcasted_iota(jnp.int32, (n, n), 1)).astype(F32)


def _silu(x):
    return x * jax.nn.sigmoid(x)


def _matmul(a, b, mode, out_dtype, *, name, add=None, out_stack=None, tm=1024, tn=1024, tk=2048, jobs=()):
    a_n = a.shape[2] if a.ndim == 3 else None
    b_n = b.shape[2] if b.ndim == 3 else None
    a_shape = (a.shape[1], a.shape[0] * a.shape[2]) if a_n else a.shape
    b_shape = (b.shape[1], b.shape[0] * b.shape[2]) if b_n else b.shape
    if mode == "nn":
        (M, K), (K2, N) = a_shape, b_shape
    elif mode == "nt":
        (M, K), (N, K2) = a_shape, b_shape
    else:
        (K, M), (K2, N) = a_shape, b_shape
    assert K == K2, (a.shape, b.shape, mode)
    n_lim = {"m": M, "n": N, "k": K}
    for dim, n in (("m" if mode == "tn" else "k", a_n), ("k" if mode == "nt" else "n", b_n), ("n", N // out_stack if out_stack else None)):
        if n:
            n_lim[dim] = math.gcd(n_lim[dim], n)
    tm, tn = _tile(n_lim["m"], tm, 8 if M % LANES else LANES, stretch=True), _tile(n_lim["n"], tn, stretch=True)
    tk = _tile(n_lim["k"], tk)
    nk = K // tk
    ca, cb = {"nn": (1, 0), "nt": (1, 1), "tn": (0, 0)}[mode]

    def body(*refs):
        a_ref, b_ref = refs[:2]
        add_ref = refs[2] if add is not None else None
        o_ref = refs[2 + (add is not None)]

        def finish(r):
            if add is not None:
                r = r + add_ref[...]
            o_ref[...] = r.astype(o_ref.dtype)

        if nk == 1:
            finish(_dg(a_ref[...], b_ref[...], ca, cb))
            return
        acc_ref = refs[-1]
        k = pl.program_id(2)

        @pl.when(k == 0)
        def _():
            acc_ref[...] = _dg(a_ref[...], b_ref[...], ca, cb)

        @pl.when((k > 0) & (k < nk - 1))
        def _():
            acc_ref[...] += _dg(a_ref[...], b_ref[...], ca, cb)

        @pl.when(k == nk - 1)
        def _():
            finish(acc_ref[...] + _dg(a_ref[...], b_ref[...], ca, cb))

    def spec(rows, cols, t_rows, t_cols, n):
        if not n:
            return pl.BlockSpec((t_rows, t_cols), lambda i, j, k: (rows(i, j, k), cols(i, j, k)))
        per = n // t_cols
        return pl.BlockSpec((None, t_rows, t_cols), lambda i, j, k: (cols(i, j, k) // per, rows(i, j, k), cols(i, j, k) % per))

    gi, gj, gk = (lambda i, j, k: i), (lambda i, j, k: j), (lambda i, j, k: k)
    a_spec = spec(gk, gi, tk, tm, a_n) if mode == "tn" else spec(gi, gk, tm, tk, a_n)
    b_spec = spec(gj, gk, tn, tk, b_n) if mode == "nt" else spec(gk, gj, tk, tn, b_n)
    o_spec = spec(gi, gj, tm, tn, N // out_stack if out_stack else None)
    out_shape = (out_stack, M, N // out_stack) if out_stack else (M, N)
    in_specs, args = [a_spec, b_spec], [a, b]
    if add is not None:
        assert not out_stack
        in_specs.append(o_spec)
        args.append(add)
    return _call(
        body, name=name, out_shape=jax.ShapeDtypeStruct(out_shape, out_dtype), grid=(M // tm, N // tn, nk),
        in_specs=in_specs, out_specs=o_spec, scratch_shapes=[pltpu.VMEM((tm, tn), F32)] if nk > 1 else [],
        dims=("parallel", "parallel", "arbitrary"), jobs=jobs,
    )(*args)


def _rms_fwd(h, w, *, name):
    T, D = h.shape
    tr = _tile(T, 256, 8)

    def body(h_ref, w_ref, u_ref):
        x = h_ref[...]
        u_ref[...] = (x * lax.rsqrt(jnp.mean(x * x, axis=-1, keepdims=True) + NORM_EPS) * w_ref[...]).astype(u_ref.dtype)

    return pl.pallas_call(
        body, name=name, out_shape=jax.ShapeDtypeStruct((T, D), BF16), grid=(T // tr,),
        in_specs=[pl.BlockSpec((tr, D), lambda i: (i, 0)), pl.BlockSpec((1, D), lambda i: (0, 0))],
        out_specs=pl.BlockSpec((tr, D), lambda i: (i, 0)), compiler_params=_params(("parallel",)),
    )(h, w.reshape(1, D))


def _rms_bwd(h, w, du, dh_in, *, name):
    T, D = h.shape
    tr = _tile(T, 256, 8)

    def body(h_ref, w_ref, du_ref, dhin_ref, dh_ref, dw_ref):
        x = h_ref[...]
        g = du_ref[...].astype(F32)
        rstd = lax.rsqrt(jnp.mean(x * x, axis=-1, keepdims=True) + NORM_EPS)
        xhat = x * rstd
        gx = g * w_ref[...]
        dh_ref[...] = dhin_ref[...] + rstd * (gx - xhat * jnp.mean(gx * xhat, axis=-1, keepdims=True))

        @pl.when(pl.program_id(0) == 0)
        def _():
            dw_ref[...] = jnp.zeros_like(dw_ref)

        dw_ref[...] += jnp.sum(g * xhat, axis=0, keepdims=True)

    row = pl.BlockSpec((tr, D), lambda i: (i, 0))
    vec = pl.BlockSpec((1, D), lambda i: (0, 0))
    return pl.pallas_call(
        body, name=name, out_shape=(jax.ShapeDtypeStruct((T, D), F32), jax.ShapeDtypeStruct((1, D), F32)), grid=(T // tr,),
        in_specs=[row, vec, row, row], out_specs=(row, vec), compiler_params=_params(("arbitrary",)),
    )(h, w.reshape(1, D), du, dh_in)


def _loss_head(h, w, target, *, name):
    T, D = h.shape
    tr = _tile(T, 256, 8)

    def body(h_ref, w_ref, t_ref, loss_ref, dh_ref, dw_ref):
        x = h_ref[...]
        rstd = lax.rsqrt(jnp.mean(x * x, axis=-1, keepdims=True) + NORM_EPS)
        xhat = x * rstd
        err = xhat * w_ref[...] - t_ref[...]
        g = err * (1.0 / D)
        gx = g * w_ref[...]
        dh_ref[...] = rstd * (gx - xhat * jnp.mean(gx * xhat, axis=-1, keepdims=True))

        @pl.when(pl.program_id(0) == 0)
        def _():
            dw_ref[...] = jnp.zeros_like(dw_ref)
            loss_ref[...] = jnp.zeros_like(loss_ref)

        dw_ref[...] += jnp.sum(g * xhat, axis=0, keepdims=True)
        loss_ref[...] += (0.5 / D) * jnp.sum(jnp.sum(err * err, axis=-1, keepdims=True), axis=0, keepdims=True)

    row = pl.BlockSpec((tr, D), lambda i: (i, 0))
    vec = pl.BlockSpec((1, D), lambda i: (0, 0))
    one = pl.BlockSpec((1, 1), lambda i: (0, 0))
    return pl.pallas_call(
        body, name=name,
        out_shape=(jax.ShapeDtypeStruct((1, 1), F32), jax.ShapeDtypeStruct((T, D), F32), jax.ShapeDtypeStruct((1, D), F32)),
        grid=(T // tr,), in_specs=[row, vec, row], out_specs=(one, row, vec), compiler_params=_params(("arbitrary",)),
    )(h, w.reshape(1, D), target)


def _shift_down(x, s):
    if s == 0:
        return x
    rows = lax.broadcasted_iota(jnp.int32, x.shape, 0)
    return jnp.where(rows >= s, pltpu.roll(x, s, 0), 0.0)


def _shift_up(x, s):
    if s == 0:
        return x
    n = x.shape[0]
    rows = lax.broadcasted_iota(jnp.int32, x.shape, 0)
    return jnp.where(rows < n - s, pltpu.roll(x, n - s, 0), 0.0)


def _conv_pre(a, w_ref, b_ref, taps):
    pre = b_ref[0:1, :] + w_ref[taps - 1:taps, :] * a
    for k in range(taps - 1):
        pre = pre + w_ref[k:k + 1, :] * _shift_down(a, taps - 1 - k)
    return pre


def _conv_bwd(a, dpre, w_ref, taps):
    da = w_ref[taps - 1:taps, :] * dpre
    rows = []
    for k in range(taps - 1):
        up = _shift_up(dpre, taps - 1 - k)
        da = da + w_ref[k:k + 1, :] * up
        rows.append(jnp.sum(up * a, axis=0, keepdims=True))
    rows.append(jnp.sum(dpre * a, axis=0, keepdims=True))
    rows.append(jnp.sum(dpre, axis=0, keepdims=True))
    return da, rows


def _ffn_gate_fwd(a, cw, cb, Bl, L, *, name, jobs=()):
    T, F2 = a.shape
    F = F2 // 2
    tc = _tile(F, CONV_COLS)
    nj = F // tc

    def body(ag_ref, au_ref, wg_ref, wu_ref, bg_ref, bu_ref, y_ref, h_ref):
        hg = _conv_pre(ag_ref[...].astype(F32), wg_ref, bg_ref, FFN_CONV)
        hu = _conv_pre(au_ref[...].astype(F32), wu_ref, bu_ref, FFN_CONV)
        y_ref[...] = (_silu(hg) * hu).astype(y_ref.dtype)
        h_ref[0] = hg.astype(h_ref.dtype)
        h_ref[1] = hu.astype(h_ref.dtype)

    blk = lambda off: pl.BlockSpec((L, tc), lambda b, j: (b, j + off))
    wblk = lambda off: pl.BlockSpec((FFN_CONV, tc), lambda b, j: (0, j + off))
    bblk = lambda off: pl.BlockSpec((1, tc), lambda b, j: (0, j + off))
    return _call(
        body, name=name, out_shape=(jax.ShapeDtypeStruct((T, F), BF16), jax.ShapeDtypeStruct((2, T, F), BF16)), grid=(Bl, nj),
        in_specs=[blk(0), blk(nj), wblk(0), wblk(nj), bblk(0), bblk(nj)],
        out_specs=(blk(0), pl.BlockSpec((2, L, tc), lambda b, j: (0, b, j))),
        dims=("parallel", "parallel"), jobs=jobs,
    )(a, a, cw, cw, cb.reshape(1, F2), cb.reshape(1, F2))


def _ffn_gate_bwd(a, h, cw, dy, Bl, L, *, name, jobs=()):
    T, F2 = a.shape
    F = F2 // 2
    tc = _tile(F, CONV_COLS)
    nj = F // tc

    def body(ag_ref, au_ref, h_ref, wg_ref, wu_ref, dy_ref, da_ref, pg_ref, pu_ref):
        ag, au = ag_ref[...].astype(F32), au_ref[...].astype(F32)
        hg, hu = h_ref[0].astype(F32), h_ref[1].astype(F32)
        g = dy_ref[...].astype(F32)
        s = jax.nn.sigmoid(hg)
        dhg = g * hu * (s * (1.0 + hg * (1.0 - s)))
        dhu = g * (hg * s)
        dag, rg = _conv_bwd(ag, dhg, wg_ref, FFN_CONV)
        dau, ru = _conv_bwd(au, dhu, wu_ref, FFN_CONV)
        da_ref[0] = dag.astype(da_ref.dtype)
        da_ref[1] = dau.astype(da_ref.dtype)

        @pl.when(pl.program_id(1) == 0)
        def _():
            pg_ref[...] = jnp.zeros_like(pg_ref)
            pu_ref[...] = jnp.zeros_like(pu_ref)

        for k in range(FFN_CONV + 1):
            pg_ref[k:k + 1, :] += rg[k]
            pu_ref[k:k + 1, :] += ru[k]

    blk = lambda off: pl.BlockSpec((L, tc), lambda j, b: (b, j + off))
    wblk = lambda off: pl.BlockSpec((FFN_CONV, tc), lambda j, b: (0, j + off))
    pblk = lambda off: pl.BlockSpec((8, tc), lambda j, b: (0, j + off))
    both = pl.BlockSpec((2, L, tc), lambda j, b: (0, b, j))
    da, pg, pu = _call(
        body, name=name,
        out_shape=(jax.ShapeDtypeStruct((2, T, F), BF16), jax.ShapeDtypeStruct((8, F), F32), jax.ShapeDtypeStruct((8, F), F32)),
        grid=(nj, Bl),
        in_specs=[blk(0), blk(nj), both, wblk(0), wblk(nj), blk(0)],
        out_specs=(both, pblk(0), pblk(0)),
        dims=("parallel", "arbitrary"), jobs=jobs,
    )(a, a, h, cw, cw, dy)
    return da, jnp.concatenate([pg, pu], axis=1)


def _mconv_fwd(xbc, cw, cb, Bl, L, *, name, jobs=()):
    T, W = xbc.shape
    tc = _tile(W, CONV_COLS)

    def body(a_ref, w_ref, b_ref, y_ref):
        y_ref[...] = _silu(_conv_pre(a_ref[...], w_ref, b_ref, M_CONV))

    blk = pl.BlockSpec((L, tc), lambda b, j: (b, j))
    return _call(
        body, name=name, out_shape=jax.ShapeDtypeStruct((T, W), F32), grid=(Bl, W // tc),
        in_specs=[blk, pl.BlockSpec((M_CONV, tc), lambda b, j: (0, j)), pl.BlockSpec((1, tc), lambda b, j: (0, j))],
        out_specs=blk, dims=("parallel", "parallel"), jobs=jobs,
    )(xbc, cw, cb.reshape(1, W))


def _mconv_bwd(xbc, cw, cb, dy, Bl, L, *, name):
    T, W = xbc.shape
    tc = _tile(W, CONV_COLS)

    def body(a_ref, w_ref, b_ref, dy_ref, da_ref, p_ref):
        a = a_ref[...]
        pre = _conv_pre(a, w_ref, b_ref, M_CONV)
        s = jax.nn.sigmoid(pre)
        dpre = dy_ref[...] * (s * (1.0 + pre * (1.0 - s)))
        da, rows = _conv_bwd(a, dpre, w_ref, M_CONV)
        da_ref[...] = da.astype(da_ref.dtype)

        @pl.when(pl.program_id(1) == 0)
        def _():
            p_ref[...] = jnp.zeros_like(p_ref)

        for k in range(M_CONV + 1):
            p_ref[k:k + 1, :] += rows[k]

    blk = pl.BlockSpec((L, tc), lambda j, b: (b, j))
    return pl.pallas_call(
        body, name=name, out_shape=(jax.ShapeDtypeStruct((T, W), BF16), jax.ShapeDtypeStruct((8, W), F32)),
        grid=(W // tc, Bl),
        in_specs=[blk, pl.BlockSpec((M_CONV, tc), lambda j, b: (0, j)), pl.BlockSpec((1, tc), lambda j, b: (0, j)), blk],
        out_specs=(blk, pl.BlockSpec((8, tc), lambda j, b: (0, j))),
        compiler_params=_params(("parallel", "arbitrary")),
    )(xbc, cw, cb.reshape(1, W), dy)


def _hgrn_chunk(st, qr, fr, v, gr, lb, gn):
    C = qr.shape[0]
    q = _silu(qr)
    logf = jnp.log(lb + (1.0 - lb) * jax.nn.sigmoid(fr))
    k = (1.0 - lb) * jax.nn.sigmoid(-fr)
    b = jnp.dot(_tri(C), logf, precision=HIGHEST, preferred_element_type=F32)
    o = _dot_nt(q * jnp.exp(b), st)
    keep = lax.broadcasted_iota(jnp.int32, (HGRN_SUB, HGRN_SUB), 0) >= lax.broadcasted_iota(jnp.int32, (HGRN_SUB, HGRN_SUB), 1)
    cols = lax.broadcasted_iota(jnp.int32, (HGRN_SUB, C), 1)
    parts = []
    for blk in range(C // HGRN_SUB):
        r0 = blk * HGRN_SUB
        bi, qi, ki, vi = b[r0:r0 + HGRN_SUB], q[r0:r0 + HGRN_SUB], k[r0:r0 + HGRN_SUB], v[r0:r0 + HGRN_SUB]
        decay = jnp.exp(jnp.minimum(bi[:, None, :] - bi[None, :, :], 0.0))
        a_diag = jnp.where(keep, jnp.sum(qi[:, None, :] * ki[None, :, :] * decay, axis=-1), 0.0)
        oi = _dot_nn(a_diag, vi)
        if blk > 0:
            bs = b[r0 - 1:r0]
            a_off = _dot_nt(qi * jnp.exp(bi - bs), k * jnp.exp(jnp.minimum(bs - b, 0.0)))
            oi = oi + _dot_nn(jnp.where(cols < r0, a_off, 0.0), v)
        parts.append(oi)
    o = o + jnp.concatenate(parts, axis=0)
    bl = b[C - 1:C]
    st1 = st * jnp.exp(bl) + _dot_tn(v, k * jnp.exp(bl - b))
    og = o * lax.rsqrt(jnp.mean(o * o, axis=-1, keepdims=True) + NORM_EPS) * gn * _silu(gr)
    return st1, og


def _hgrn_specs(Bl, L, H, order):
    W, HB = HGRN_PAR * HGRN_HEAD, H // HGRN_PAR

    def at(off):
        if order == "bh":
            return pl.BlockSpec((L, W), lambda b, h: (b, h + off))
        return pl.BlockSpec((L, W), lambda h, b: (b, h + off))
    return [at(0), at(HB), at(2 * HB), at(3 * HB)]


def _hgrn_fwd(P, lb, gn, Bl, L, *, name, jobs=()):
    T, D4 = P.shape
    D = D4 // 4
    H = D // HGRN_HEAD
    C = HGRN_CHUNK
    NC = L // C
    W, HB = HGRN_PAR * HGRN_HEAD, H // HGRN_PAR
    heads = [slice(p * HGRN_HEAD, (p + 1) * HGRN_HEAD) for p in range(HGRN_PAR)]

    def body(q_ref, f_ref, v_ref, g_ref, lb_ref, gn_ref, og_ref, st_ref):
        gnv = gn_ref[...]

        def step(c, sts):
            r = pl.ds(pl.multiple_of(c * C, C), C)
            new = []
            for p, hd in enumerate(heads):
                st_ref[p, c] = sts[p]
                st1, og = _hgrn_chunk(sts[p], q_ref[r, hd], f_ref[r, hd], v_ref[r, hd], g_ref[r, hd], lb_ref[:, hd], gnv)
                og_ref[r, hd] = og.astype(og_ref.dtype)
                new.append(st1)
            return tuple(new)

        lax.fori_loop(0, NC, step, tuple(jnp.zeros((HGRN_HEAD, HGRN_HEAD), F32) for _ in heads))

    return _call(
        body, name=name,
        out_shape=(jax.ShapeDtypeStruct((T, D), BF16), jax.ShapeDtypeStruct((Bl, H, NC, HGRN_HEAD, HGRN_HEAD), F32)),
        grid=(Bl, HB),
        in_specs=_hgrn_specs(Bl, L, H, "bh") + [pl.BlockSpec((1, W), lambda b, h: (0, h)), pl.BlockSpec((1, HGRN_HEAD), lambda b, h: (0, 0))],
        out_specs=(pl.BlockSpec((L, W), lambda b, h: (b, h)),
                   pl.BlockSpec((None, HGRN_PAR, NC, HGRN_HEAD, HGRN_HEAD), lambda b, h: (b, h, 0, 0, 0))),
        dims=("parallel", "parallel"), jobs=jobs,
    )(P, P, P, P, lb.reshape(1, D), gn.reshape(1, HGRN_HEAD))


def _hgrn_bwd(P, lb, gn, states, dog, Bl, L, *, name, jobs=()):
    T, D4 = P.shape
    D = D4 // 4
    H = D // HGRN_HEAD
    C = HGRN_CHUNK
    NC = L // C
    W, HB = HGRN_PAR * HGRN_HEAD, H // HGRN_PAR
    heads = [slice(p * HGRN_HEAD, (p + 1) * HGRN_HEAD) for p in range(HGRN_PAR)]

    def body(q_ref, f_ref, v_ref, g_ref, lb_ref, gn_ref, st_ref, dog_ref, dp_ref, dlb_ref, dgn_ref):
        gnv = gn_ref[...]

        @pl.when(pl.program_id(1) == 0)
        def _():
            dlb_ref[...] = jnp.zeros_like(dlb_ref)

        @pl.when((pl.program_id(0) == 0) & (pl.program_id(1) == 0))
        def _():
            dgn_ref[...] = jnp.zeros_like(dgn_ref)

        def step(i, dsts):
            c = NC - 1 - i
            r = pl.ds(pl.multiple_of(c * C, C), C)
            new, dgn_sum = [], jnp.zeros((1, HGRN_HEAD), F32)
            for p, hd in enumerate(heads):
                _, vjp = jax.vjp(_hgrn_chunk, st_ref[p, c], q_ref[r, hd], f_ref[r, hd], v_ref[r, hd], g_ref[r, hd],
                                 lb_ref[:, hd], gnv)
                dst0, dq, df, dv, dg, dlb, dgn = vjp((dsts[p], dog_ref[r, hd].astype(F32)))
                dp_ref[0, r, hd] = dq.astype(dp_ref.dtype)
                dp_ref[1, r, hd] = df.astype(dp_ref.dtype)
                dp_ref[2, r, hd] = dv.astype(dp_ref.dtype)
                dp_ref[3, r, hd] = dg.astype(dp_ref.dtype)
                dlb_ref[:, hd] += dlb
                dgn_sum = dgn_sum + dgn
                new.append(dst0)
            dgn_ref[...] += dgn_sum
            return tuple(new)

        lax.fori_loop(0, NC, step, tuple(jnp.zeros((HGRN_HEAD, HGRN_HEAD), F32) for _ in heads))

    return _call(
        body, name=name,
        out_shape=(jax.ShapeDtypeStruct((4, T, D), BF16), jax.ShapeDtypeStruct((1, D), F32), jax.ShapeDtypeStruct((1, HGRN_HEAD), F32)),
        grid=(HB, Bl),
        in_specs=_hgrn_specs(Bl, L, H, "hb") + [
            pl.BlockSpec((1, W), lambda h, b: (0, h)), pl.BlockSpec((1, HGRN_HEAD), lambda h, b: (0, 0)),
            pl.BlockSpec((None, HGRN_PAR, NC, HGRN_HEAD, HGRN_HEAD), lambda h, b: (b, h, 0, 0, 0)),
            pl.BlockSpec((L, W), lambda h, b: (b, h))],
        out_specs=(pl.BlockSpec((4, L, W), lambda h, b: (0, b, h)),
                   pl.BlockSpec((1, W), lambda h, b: (0, h)), pl.BlockSpec((1, HGRN_HEAD), lambda h, b: (0, 0))),
        dims=("arbitrary", "arbitrary"), jobs=jobs,
    )(P, P, P, P, lb.reshape(1, D), gn.reshape(1, HGRN_HEAD), states, dog)


def _lower_bounds(logits):
    p = jax.nn.softmax(logits, axis=0)
    rows, run = [], jnp.zeros_like(p[0:1])
    for i in range(logits.shape[0]):
        run = run + p[i:i + 1]
        rows.append(run - p[0:1])
    return jnp.concatenate(rows, axis=0) if len(rows) > 1 else rows[0]


def _lb_fwd(logits, *, name):
    def body(x_ref, o_ref):
        o_ref[...] = _lower_bounds(x_ref[...])
    return pl.pallas_call(body, name=name, out_shape=jax.ShapeDtypeStruct(logits.shape, F32))(logits)


def _lb_bwd(logits, dlb, *, name):
    def body(x_ref, g_ref, o_ref):
        _, vjp = jax.vjp(_lower_bounds, x_ref[...])
        o_ref[...] = vjp(g_ref[...])[0]
    return pl.pallas_call(body, name=name, out_shape=jax.ShapeDtypeStruct(logits.shape, F32))(logits, dlb)


def _ssd_pre(dtc_raw, dtr_raw, bias_c, bias_r, alog_c, alog_r):
    C = dtc_raw.shape[0]
    dt_c = jax.nn.softplus(dtc_raw + bias_c)
    dt_r = jax.nn.softplus(dtr_raw + bias_r)
    acs_c = jnp.dot(_tri(C), dt_c * -jnp.exp(alog_c), precision=HIGHEST, preferred_element_type=F32)
    acs_r = lax.dot_general(dt_r * -jnp.exp(alog_r), _tri(C), (((1,), (1,)), ((), ())), precision=HIGHEST, preferred_element_type=F32)
    return dt_c, acs_c, acs_r


def _ssd_head(h0, xs, G, Bm, Cm, dt, acs, acs_row, dskip):
    C = xs.shape[0]
    causal = lax.broadcasted_iota(jnp.int32, (C, C), 0) >= lax.broadcasted_iota(jnp.int32, (C, C), 1)
    Lm = jnp.where(causal, jnp.exp(jnp.minimum(acs - acs_row, 0.0)), 0.0)
    X = xs * dt
    y = _dot_nn(G * Lm, X) + _dot_nt(Cm * jnp.exp(acs), h0) + dskip * xs
    alast = acs[C - 1:C]
    h1 = jnp.exp(alast) * h0 + _dot_tn(X, Bm * jnp.exp(alast - acs))
    return y, h1


def _ssd_layout(Bl, L, d_inner, order):
    G = M_GROUPS
    hpg = d_inner // M_HEADDIM // G
    gw = hpg * M_HEADDIM
    ix = (lambda b, g: (b, g)) if order == "bg" else (lambda g, b: (b, g))
    def cols(width, off):
        if order == "bg":
            return pl.BlockSpec((L, width), lambda b, g: (b, g + off))
        return pl.BlockSpec((L, width), lambda g, b: (b, g + off))
    def per_group(shape):
        n = len(shape)
        if order == "bg":
            return pl.BlockSpec((None,) + shape, lambda b, g: (g,) + (0,) * n)
        return pl.BlockSpec((None,) + shape, lambda g, b: (g,) + (0,) * n)
    if order == "bg":
        dtc = pl.BlockSpec((None, L, LANES), lambda b, g: (g, b, 0))
        dtr = pl.BlockSpec((None, 8, L), lambda b, g: (g, 0, b))
    else:
        dtc = pl.BlockSpec((None, L, LANES), lambda g, b: (g, b, 0))
        dtr = pl.BlockSpec((None, 8, L), lambda g, b: (g, 0, b))
    nb = d_inner // M_D_STATE
    return hpg, gw, cols, per_group, dtc, dtr, nb


def _ssd_fwd(xa, dtc, dtr, prm, Bl, L, d_inner, *, name, jobs=()):
    T = xa.shape[0]
    C = M_CHUNK
    NC = L // C
    G = M_GROUPS
    hpg, gw, cols, per_group, dtc_spec, dtr_spec, nb = _ssd_layout(Bl, L, d_inner, "bg")

    def body(xs_ref, b_ref, c_ref, dtc_ref, dtr_ref, bc_ref, br_ref, ac_ref, ar_ref, dk_ref, y_ref, hs_ref,
             dt_s, acs_s, acr_s, h_s):
        h_s[...] = jnp.zeros_like(h_s)

        def step(c, carry):
            r = pl.ds(pl.multiple_of(c * C, C), C)
            dt_c, acs_c, acs_r = _ssd_pre(dtc_ref[r, :], dtr_ref[:, r], bc_ref[...], br_ref[...], ac_ref[...], ar_ref[...])
            dt_s[...] = dt_c
            acs_s[...] = acs_c
            acr_s[...] = acs_r
            Bm, Cm = b_ref[r, :], c_ref[r, :]
            Gm = _dot_nt(Cm, Bm)
            for j in range(hpg):
                pc = slice(j * M_HEADDIM, (j + 1) * M_HEADDIM)
                h0 = h_s[j]
                hs_ref[c, j] = h0
                y, h1 = _ssd_head(h0, xs_ref[r, pc], Gm, Bm, Cm, dt_s[:, j:j + 1], acs_s[:, j:j + 1], acr_s[j:j + 1, :],
                                  dk_ref[0:1, j:j + 1])
                y_ref[r, pc] = y
                h_s[j] = h1
            return carry

        lax.fori_loop(0, NC, step, 0)

    return _call(
        body, name=name,
        out_shape=(jax.ShapeDtypeStruct((T, d_inner), F32), jax.ShapeDtypeStruct((Bl, G, NC, hpg, M_HEADDIM, M_D_STATE), F32)),
        grid=(Bl, G),
        in_specs=[cols(gw, 0), cols(M_D_STATE, nb), cols(M_D_STATE, nb + G), dtc_spec, dtr_spec,
                  per_group((1, LANES)), per_group((8, 1)), per_group((1, LANES)), per_group((8, 1)), per_group((1, LANES))],
        out_specs=(cols(gw, 0), pl.BlockSpec((None, None, NC, hpg, M_HEADDIM, M_D_STATE), lambda b, g: (b, g, 0, 0, 0, 0))),
        scratch_shapes=[pltpu.VMEM((C, LANES), F32), pltpu.VMEM((C, LANES), F32), pltpu.VMEM((8, C), F32),
                        pltpu.VMEM((hpg, M_HEADDIM, M_D_STATE), F32)],
        dims=("parallel", "parallel"), jobs=jobs,
    )(xa, xa, xa, dtc, dtr, *prm)


def _ssd_bwd(xa, dtc, dtr, prm, states, dy, Bl, L, d_inner, *, name, jobs=()):
    T = xa.shape[0]
    C = M_CHUNK
    NC = L // C
    G = M_GROUPS
    hpg, gw, cols, per_group, dtc_spec, dtr_spec, nb = _ssd_layout(Bl, L, d_inner, "gb")

    def body(xs_ref, b_ref, c_ref, dtc_ref, dtr_ref, bc_ref, br_ref, ac_ref, ar_ref, dk_ref, hs_ref, dy_ref,
             dxs_ref, db_ref, dc_ref, ddtc_ref, ddtr_ref, gbc_ref, gbr_ref, gac_ref, gar_ref, gdk_ref,
             dt_s, acs_s, acr_s, ddt_s, dacs_s, dacr_s, dh_s):
        dh_s[...] = jnp.zeros_like(dh_s)

        @pl.when(pl.program_id(1) == 0)
        def _():
            for ref in (gbc_ref, gbr_ref, gac_ref, gar_ref, gdk_ref):
                ref[...] = jnp.zeros_like(ref)

        def step(i, carry):
            c = NC - 1 - i
            r = pl.ds(pl.multiple_of(c * C, C), C)
            pre_in = (dtc_ref[r, :], dtr_ref[:, r], bc_ref[...], br_ref[...], ac_ref[...], ar_ref[...])
            (dt_c, acs_c, acs_r), pre_vjp = jax.vjp(_ssd_pre, *pre_in)
            dt_s[...] = dt_c
            acs_s[...] = acs_c
            acr_s[...] = acs_r
            ddt_s[...] = jnp.zeros_like(ddt_s)
            dacs_s[...] = jnp.zeros_like(dacs_s)
            dacr_s[...] = jnp.zeros_like(dacr_s)
            Bm, Cm = b_ref[r, :], c_ref[r, :]
            Gm = _dot_nt(Cm, Bm)
            dG = jnp.zeros((C, C), F32)
            dB = jnp.zeros((C, M_D_STATE), F32)
            dC = jnp.zeros((C, M_D_STATE), F32)
            for j in range(hpg):
                pc = slice(j * M_HEADDIM, (j + 1) * M_HEADDIM)
                one = slice(j, j + 1)
                _, vjp = jax.vjp(_ssd_head, hs_ref[c, j], xs_ref[r, pc], Gm, Bm, Cm, dt_s[:, one], acs_s[:, one], acr_s[one, :],
                                 dk_ref[0:1, one])
                dh0, dxs, dGj, dBj, dCj, ddt, dacs, dacr, ddk = vjp((dy_ref[r, pc], dh_s[j]))
                dh_s[j] = dh0
                dxs_ref[r, pc] = dxs
                dG, dB, dC = dG + dGj, dB + dBj, dC + dCj
                ddt_s[:, one] = ddt
                dacs_s[:, one] = dacs
                dacr_s[one, :] = dacr
                gdk_ref[0:1, one] += ddk
            db_ref[r, :] = dB + _dot_tn(dG, Cm)
            dc_ref[r, :] = dC + _dot_nn(dG, Bm)
            ddtc, ddtr, gbc, gbr, gac, gar = pre_vjp((ddt_s[...], dacs_s[...], dacr_s[...]))
            ddtc_ref[r, :] = ddtc
            ddtr_ref[:, r] = ddtr
            gbc_ref[...] += gbc
            gbr_ref[...] += gbr
            gac_ref[...] += gac
            gar_ref[...] += gar
            return carry

        lax.fori_loop(0, NC, step, 0)

    GN = G * M_D_STATE
    bc_cols = lambda off: pl.BlockSpec((L, M_D_STATE), lambda g, b: (b, g + off))
    outs = _call(
        body, name=name,
        out_shape=(jax.ShapeDtypeStruct((T, d_inner), F32), jax.ShapeDtypeStruct((T, GN), F32), jax.ShapeDtypeStruct((T, GN), F32),
                   jax.ShapeDtypeStruct(dtc.shape, F32), jax.ShapeDtypeStruct(dtr.shape, F32),
                   jax.ShapeDtypeStruct((G, 1, LANES), F32), jax.ShapeDtypeStruct((G, 8, 1), F32),
                   jax.ShapeDtypeStruct((G, 1, LANES), F32), jax.ShapeDtypeStruct((G, 8, 1), F32),
                   jax.ShapeDtypeStruct((G, 1, LANES), F32)),
        grid=(G, Bl),
        in_specs=[cols(gw, 0), cols(M_D_STATE, nb), cols(M_D_STATE, nb + G), dtc_spec, dtr_spec,
                  per_group((1, LANES)), per_group((8, 1)), per_group((1, LANES)), per_group((8, 1)), per_group((1, LANES)),
                  pl.BlockSpec((None, None, NC, hpg, M_HEADDIM, M_D_STATE), lambda g, b: (b, g, 0, 0, 0, 0)), cols(gw, 0)],
        out_specs=(cols(gw, 0), bc_cols(0), bc_cols(0), dtc_spec, dtr_spec,
                   per_group((1, LANES)), per_group((8, 1)), per_group((1, LANES)), per_group((8, 1)), per_group((1, LANES))),
        scratch_shapes=[pltpu.VMEM((C, LANES), F32), pltpu.VMEM((C, LANES), F32), pltpu.VMEM((8, C), F32),
                        pltpu.VMEM((C, LANES), F32), pltpu.VMEM((C, LANES), F32), pltpu.VMEM((8, C), F32),
                        pltpu.VMEM((hpg, M_HEADDIM, M_D_STATE), F32)],
        dims=("arbitrary", "arbitrary"), jobs=jobs,
    )(xa, xa, xa, dtc, dtr, *prm, states, dy)
    return outs


def _mgate(y, z, w):
    t = y * _silu(z)
    return t * lax.rsqrt(jnp.mean(t * t, axis=-1, keepdims=True) + NORM_EPS) * w


def _mgate_fwd(y, z, w, *, name, jobs=()):
    T, d_inner = y.shape
    gw = d_inner // M_GROUPS
    tr = _tile(T, 512, 8)

    def body(y_ref, z_ref, w_ref, o_ref):
        o_ref[...] = _mgate(y_ref[...], z_ref[...], w_ref[...]).astype(o_ref.dtype)

    blk = pl.BlockSpec((tr, gw), lambda i, g: (i, g))
    return _call(
        body, name=name, out_shape=jax.ShapeDtypeStruct((T, d_inner), BF16), grid=(T // tr, M_GROUPS),
        in_specs=[blk, blk, pl.BlockSpec((1, gw), lambda i, g: (0, g))], out_specs=blk,
        dims=("parallel", "parallel"), jobs=jobs,
    )(y, z, w.reshape(1, d_inner))


def _mgate_bwd(y, z, w, dyg, *, name):
    T, d_inner = y.shape
    gw = d_inner // M_GROUPS
    tr = _tile(T, 512, 8)

    def body(y_ref, z_ref, w_ref, g_ref, dy_ref, dz_ref, dw_ref):
        _, vjp = jax.vjp(_mgate, y_ref[...], z_ref[...], w_ref[...])
        dy, dz, dw = vjp(g_ref[...].astype(F32))
        dy_ref[...] = dy
        dz_ref[...] = dz.astype(dz_ref.dtype)

        @pl.when(pl.program_id(1) == 0)
        def _():
            dw_ref[...] = jnp.zeros_like(dw_ref)

        dw_ref[...] += dw

    blk = pl.BlockSpec((tr, gw), lambda g, i: (i, g))
    vec = pl.BlockSpec((1, gw), lambda g, i: (0, g))
    return pl.pallas_call(
        body, name=name,
        out_shape=(jax.ShapeDtypeStruct((T, d_inner), F32), jax.ShapeDtypeStruct((T, d_inner), BF16), jax.ShapeDtypeStruct((1, d_inner), F32)),
        grid=(M_GROUPS, T // tr), in_specs=[blk, blk, vec, blk], out_specs=(blk, blk, vec),
        compiler_params=_params(("parallel", "arbitrary")),
    )(y, z, w.reshape(1, d_inner), dyg)


def _adamw_math(w, g, m, v):
    m1 = ADAM_B1 * m + (1.0 - ADAM_B1) * g
    v1 = ADAM_B2 * v + (1.0 - ADAM_B2) * (g * g)
    m_hat = m1 / (1.0 - ADAM_B1 ** ADAM_STEP)
    v_hat = v1 / (1.0 - ADAM_B2 ** ADAM_STEP)
    delta = -ADAM_LR * (m_hat / (jnp.sqrt(v_hat) + ADAM_EPS) + ADAM_WD * w)
    return delta, m1, v1


def _adamw_sharded(land, w, m, v, *, name, jobs=()):
    R, n = w.shape
    tr = _tile(R, 128, 8)

    def body(l_ref, w_ref, m_ref, v_ref, g_ref, d_ref, m1_ref, v1_ref):
        g = l_ref[0].astype(F32)
        for k in range(1, N_DEV):
            g = g + l_ref[k].astype(F32)
        d, m1, v1 = _adamw_math(w_ref[...], g, m_ref[...], v_ref[...])
        g_ref[...] = g
        d_ref[...] = d
        m1_ref[...] = m1
        v1_ref[...] = v1

    blk = pl.BlockSpec((tr, n), lambda i: (i, 0))
    out = jax.ShapeDtypeStruct((R, n), F32)
    return _call(
        body, name=name, out_shape=(out, out, out, out), grid=(R // tr,),
        in_specs=[pl.BlockSpec((N_DEV, tr, n), lambda i: (0, i, 0)), blk, blk, blk], out_specs=(blk, blk, blk, blk),
        dims=("parallel",), jobs=jobs,
    )(land, w, m, v)


def _sum_devices(parts, *, name):
    _, R, n = parts.shape
    tr = _tile(R, 512, 8)

    def body(p_ref, o_ref):
        g = p_ref[0]
        for k in range(1, N_DEV):
            g = g + p_ref[k]
        o_ref[...] = g

    return pl.pallas_call(
        body, name=name, out_shape=jax.ShapeDtypeStruct((R, n), F32), grid=(R // tr,),
        in_specs=[pl.BlockSpec((N_DEV, tr, n), lambda i: (0, i, 0))], out_specs=pl.BlockSpec((tr, n), lambda i: (i, 0)),
        compiler_params=_params(("parallel",)),
    )(parts)


def _adamw_packed(g, w, m, v, *, name):
    R, n = w.shape
    tr = _tile(R, 512, 8)

    def body(g_ref, w_ref, m_ref, v_ref, d_ref, m1_ref, v1_ref):
        d, m1, v1 = _adamw_math(w_ref[...], g_ref[...], m_ref[...], v_ref[...])
        d_ref[...] = d
        m1_ref[...] = m1
        v1_ref[...] = v1

    blk = pl.BlockSpec((tr, n), lambda i: (i, 0))
    out = jax.ShapeDtypeStruct((R, n), F32)
    return pl.pallas_call(
        body, name=name, out_shape=(out, out, out), grid=(R // tr,), in_specs=[blk] * 4, out_specs=(blk, blk, blk),
        compiler_params=_params(("parallel",)),
    )(g, w, m, v)


PACK_ALIGN = 8 * LANES


def _pack(arrs):
    parts = []
    for a in arrs:
        flat = a.reshape(-1).astype(F32)
        pad = (-flat.shape[0]) % PACK_ALIGN
        parts.append(jnp.pad(flat, (0, pad)).reshape(-1, LANES))
    return jnp.concatenate(parts, axis=0)


def _unpack(packed, shapes, lead=()):
    out, r = [], 0
    nl = len(lead)
    for s in shapes:
        n = math.prod(s)
        rows = (n + PACK_ALIGN - 1) // PACK_ALIGN * (PACK_ALIGN // LANES)
        part = lax.slice_in_dim(packed, r, r + rows, axis=nl)
        out.append(part.reshape(lead + (rows * LANES,))[..., :n].reshape(lead + tuple(s)))
        r += rows
    return out


EXCHANGE_BYTES_PER_US = 80e3
CHIP_GATHER_BYTES_PER_US = 68e3
SIBLING_BYTES_PER_US = 250e3
EXCHANGE_ROW_ALIGN = 64
CARRY_FILL = 1.0
CARRY_FILL_BACKWARD = 0.75

CALL_US = {
    "hgrn_in_fwd": 155, "hgrn_scan_fwd": 770, "hgrn_out_fwd": 40, "ffn_up_fwd": 220, "ffn_gate_fwd": 114, "ffn_down_fwd": 120,
    "mamba_z_fwd": 75, "mamba_xbc_fwd": 112, "mamba_scan_fwd": 400, "mamba_out_fwd": 86, "mamba_conv_fwd": 90, "mamba_gate_fwd": 75,
    "ffn_down_bwd_x": 113, "ffn_down_bwd_w": 125, "ffn_gate_bwd": 170, "ffn_up_bwd_x": 240, "ffn_up_bwd_w": 229,
    "hgrn_out_bwd_x": 40, "hgrn_out_bwd_w": 45, "hgrn_scan_bwd": 1800, "hgrn_in_bwd_x": 172, "hgrn_in_bwd_w": 158,
    "mamba_out_bwd_x": 78, "mamba_out_bwd_w": 88, "mamba_scan_bwd": 1200, "mamba_z_bwd_x": 88, "mamba_xbc_bwd_x": 130,
    "mamba_z_bwd_w": 82, "mamba_xbc_bwd_w": 120,
    "adamw_hgrn_w_in": 65, "adamw_hgrn_w_out": 16, "adamw_m_w_in": 87, "adamw_m_w_out": 32, "adamw_f_w_up": 168,
    "adamw_f_w_down": 109,
}


class _Plan:
    def __init__(self):
        self.queue = []
        self.handoff = []
        self.fill = CARRY_FILL

    def push(self, src, mode, buf, q0=0):
        rows, n = src.shape[-2], src.shape[-1]
        blocks = len(OTHER_CHIPS) if mode == "gather_chip" else N_DEV - 1
        rate = CHIP_GATHER_BYTES_PER_US if mode == "gather_chip" else EXCHANGE_BYTES_PER_US
        self.queue.append(dict(src=src, mode=mode, buf=buf, r0=0, r1=rows, q0=q0, us_per_row=blocks * n * src.dtype.itemsize / rate))

    def _hand_on(self, it, r0, r1):
        q0, q1 = it["q0"] + r0, it["q0"] + r1
        if self.handoff and self.handoff[-1]["buf"] is it["buf"] and self.handoff[-1]["r1"] == q0:
            self.handoff[-1]["r1"] = q1
            return
        us = len(OTHER_CHIPS) * it["src"].shape[-1] * it["src"].dtype.itemsize / SIBLING_BYTES_PER_US
        self.handoff.append(dict(src=jnp.zeros((8, LANES), it["src"].dtype), mode="forward", buf=it["buf"], r0=q0, r1=q1, q0=0,
                                 us_per_row=us))

    def take(self, name):
        budget, jobs, used, spawned = self.fill * CALL_US[name], [], set(), []
        for queue in (self.handoff, self.queue):
            i = 0
            while i < len(queue):
                it = queue[i]
                if id(it["buf"]) in used:
                    i += 1
                    continue
                left = it["r1"] - it["r0"]
                fit = int(budget / it["us_per_row"])
                rows = left if fit >= left else fit // EXCHANGE_ROW_ALIGN * EXCHANGE_ROW_ALIGN
                if rows <= 0:
                    break
                jobs.append(_Job(it["src"], it["mode"], it["buf"], it["r0"], it["r0"] + rows, it["q0"] + it["r0"]))
                used.add(id(it["buf"]))
                budget -= rows * it["us_per_row"]
                if it["mode"] == "gather_chip":
                    spawned.append((it, it["r0"], it["r0"] + rows))
                it["r0"] += rows
                if it["r0"] < it["r1"]:
                    break
                queue.pop(i)
        for it, r0, r1 in spawned:
            self._hand_on(it, r0, r1)
        return jobs

    def finish(self, buf, *, name):
        mine = [it for it in self.queue if it["buf"] is buf]
        self.queue = [it for it in self.queue if it["buf"] is not buf]
        for i, it in enumerate(mine):
            _exchange([_Job(it["src"], it["mode"], buf, it["r0"], it["r1"], it["q0"] + it["r0"])], name=f"{name}_{i}" if i else name)
            if it["mode"] == "gather_chip":
                self._hand_on(it, it["r0"], it["r1"])
        mine = [it for it in self.handoff if it["buf"] is buf]
        self.handoff = [it for it in self.handoff if it["buf"] is not buf]
        for i, it in enumerate(mine):
            _exchange([_Job(it["src"], "forward", buf, it["r0"], it["r1"], it["r0"])], name=f"{name}_handoff_{i}" if i else f"{name}_handoff")
        return buf.arr


def _pad_to(a, axis, size):
    pad = [(0, 0)] * a.ndim
    pad[axis] = (0, size - a.shape[axis])
    return jnp.pad(a, pad)


def _ssd_operands(dt_raw, dt_bias, a_log, dskip):
    T, heads = dt_raw.shape
    G = M_GROUPS
    hpg = heads // G
    t3 = dt_raw.reshape(T, G, hpg)
    dtc = _pad_to(t3.transpose(1, 0, 2), 2, LANES)
    dtr = _pad_to(t3.transpose(1, 2, 0), 1, 8)
    col = lambda p: _pad_to(p.reshape(G, 1, hpg), 2, LANES)
    row = lambda p: _pad_to(p.reshape(G, hpg, 1), 1, 8)
    return dtc, dtr, (col(dt_bias), row(dt_bias), col(a_log), row(a_log), col(dskip))


def _ssd_unpack(ddtc, ddtr, gbc, gbr, gac, gar, gdk, heads):
    G = M_GROUPS
    hpg = heads // G
    T = ddtc.shape[1]
    ddt = ddtc[:, :, :hpg].transpose(1, 0, 2).reshape(T, heads) + ddtr[:, :hpg, :].transpose(2, 0, 1).reshape(T, heads)
    both = lambda c, r: c[:, 0, :hpg].reshape(heads) + r[:, :hpg, 0].reshape(heads)
    return ddt, both(gbc, gbr), both(gac, gar), gdk[:, 0, :hpg].reshape(heads)


class _LazyWeights:
    def __init__(self, name, shape, fetch):
        self.name, self.shape, self.fetch, self.got = name, shape, fetch, {}

    def __getitem__(self, layer):
        if layer not in self.got:
            self.got[layer] = self.fetch(self.name, layer)
        return self.got[layer]


class _GradSink:
    def __init__(self, name, layers, push):
        self.name, self.shape, self.push = name, (layers,), push

    def __setitem__(self, layer, g):
        self.push(self.name, layer, g)


def _local_step(x, target, W, plan, push):
    Bl, L, D = x.shape
    T = Bl * L
    depth = W["mix_norm"].shape[0]
    d_inner = W["m_w_out"].shape[1]
    heads = d_inner // M_HEADDIM
    conv_dim = W["m_conv_w"].shape[2]

    def mm(a, b, mode, out_dtype, *, name, add=None, out_stack=None):
        return _matmul(a, b, mode, out_dtype, name=name, add=add, out_stack=out_stack,
                       jobs=plan.take(name) if name in CALL_US else ())

    lb = _lb_fwd(W["hgrn_lb_logits"], name="hgrn_lower_bounds")
    h = x.reshape(T, D)
    saved = []
    for i in range(depth):
        j = i // 2
        s = {"h": h}
        u = _rms_fwd(h, W["mix_norm"][i], name="mix_norm_fwd")
        s["u"] = u
        if i % 2 == 0:
            P = mm(u, W["hgrn_w_in"][j], "nn", F32, name="hgrn_in_fwd")
            og, st = _hgrn_fwd(P, lb[j], W["hgrn_gnorm"][j], Bl, L, name="hgrn_scan_fwd", jobs=plan.take("hgrn_scan_fwd"))
            h = mm(og, W["hgrn_w_out"][j], "nn", F32, add=h, name="hgrn_out_fwd")
            s.update(P=P, og=og, st=st)
        else:
            w_in = W["m_w_in"][j]
            z = mm(u, w_in[:, :d_inner], "nn", F32, name="mamba_z_fwd")
            xbc = mm(u, w_in[:, d_inner:d_inner + conv_dim], "nn", F32, name="mamba_xbc_fwd")
            dt_raw = mm(u, w_in[:, d_inner + conv_dim:], "nn", F32, name="mamba_dt_fwd")
            xa = _mconv_fwd(xbc, W["m_conv_w"][j], W["m_conv_b"][j], Bl, L, name="mamba_conv_fwd", jobs=plan.take("mamba_conv_fwd"))
            dtc, dtr, prm = _ssd_operands(dt_raw, W["m_dt_bias"][j], W["m_A_log"][j], W["m_D"][j])
            y, hs = _ssd_fwd(xa, dtc, dtr, prm, Bl, L, d_inner, name="mamba_scan_fwd", jobs=plan.take("mamba_scan_fwd"))
            yg = _mgate_fwd(y, z, W["m_norm"][j], name="mamba_gate_fwd", jobs=plan.take("mamba_gate_fwd"))
            h = mm(yg, W["m_w_out"][j], "nn", F32, add=h, name="mamba_out_fwd")
            s.update(z=z, xbc=xbc, xa=xa, dtc=dtc, dtr=dtr, prm=prm, y=y, hs=hs, yg=yg)
        s["h2"] = h
        u2 = _rms_fwd(h, W["ffn_norm"][i], name="ffn_norm_fwd")
        a = mm(u2, W["f_w_up"][i], "nn", BF16, name="ffn_up_fwd")
        yf, hf = _ffn_gate_fwd(a, W["f_conv_w"][i], W["f_conv_b"][i], Bl, L, name="ffn_gate_fwd", jobs=plan.take("ffn_gate_fwd"))
        h = mm(yf, W["f_w_down"][i], "nn", F32, add=h, name="ffn_down_fwd")
        s.update(u2=u2, a=a, yf=yf, hf=hf)
        saved.append(s)

    loss, dh, d_final = _loss_head(h, W["final_norm"], target.reshape(T, D), name="loss_head")
    plan.fill = CARRY_FILL_BACKWARD

    g = {k: (_GradSink(k, W[k].shape[0], push) if isinstance(W[k], _LazyWeights) else [None] * W[k].shape[0])
         for k in W if k != "final_norm"}
    g["final_norm"] = d_final.reshape(D)
    dlb = [None] * W["hgrn_lb_logits"].shape[0]
    for i in reversed(range(depth)):
        j = i // 2
        s = saved[i]
        g["f_w_down"][i] = mm(s["yf"], dh, "tn", BF16, name="ffn_down_bwd_w")
        dyf = mm(dh, W["f_w_down"][i], "nt", BF16, name="ffn_down_bwd_x")
        da, pf = _ffn_gate_bwd(s["a"], s["hf"], W["f_conv_w"][i], dyf, Bl, L, name="ffn_gate_bwd",
                               jobs=plan.take("ffn_gate_bwd"))
        g["f_w_up"][i] = mm(s["u2"], da, "tn", BF16, out_stack=N_DEV, name="ffn_up_bwd_w")
        du2 = mm(da, W["f_w_up"][i], "nt", F32, name="ffn_up_bwd_x")
        g["f_conv_w"][i], g["f_conv_b"][i] = pf[:FFN_CONV], pf[FFN_CONV]
        dh, dn = _rms_bwd(s["h2"], W["ffn_norm"][i], du2, dh, name="ffn_norm_bwd")
        g["ffn_norm"][i] = dn.reshape(D)
        if i % 2 == 0:
            g["hgrn_w_out"][j] = mm(s["og"], dh, "tn", BF16, name="hgrn_out_bwd_w")
            dog = mm(dh, W["hgrn_w_out"][j], "nt", BF16, name="hgrn_out_bwd_x")
            dP4, dlb_j, dgn = _hgrn_bwd(s["P"], lb[j], W["hgrn_gnorm"][j], s["st"], dog, Bl, L, name="hgrn_scan_bwd",
                                        jobs=plan.take("hgrn_scan_bwd"))
            g["hgrn_w_in"][j] = mm(s["u"], dP4, "tn", BF16, out_stack=N_DEV, name="hgrn_in_bwd_w")
            du = mm(dP4, W["hgrn_w_in"][j], "nt", F32, name="hgrn_in_bwd_x")
            dlb[j] = dlb_j
            g["hgrn_gnorm"][j] = dgn.reshape(HGRN_HEAD)
        else:
            w_in = W["m_w_in"][j]
            g["m_w_out"][j] = mm(s["yg"], dh, "tn", BF16, name="mamba_out_bwd_w")
            dyg = mm(dh, W["m_w_out"][j], "nt", BF16, name="mamba_out_bwd_x")
            dy, dz, dnw = _mgate_bwd(s["y"], s["z"], W["m_norm"][j], dyg, name="mamba_gate_bwd")
            dxs, dB, dC, ddtc, ddtr, gbc, gbr, gac, gar, gdk = _ssd_bwd(
                s["xa"], s["dtc"], s["dtr"], s["prm"], s["hs"], dy, Bl, L, d_inner, name="mamba_scan_bwd",
                jobs=plan.take("mamba_scan_bwd"))
            ddt, g["m_dt_bias"][j], g["m_A_log"][j], g["m_D"][j] = _ssd_unpack(ddtc, ddtr, gbc, gbr, gac, gar, gdk, heads)
            dxa = jnp.concatenate([dxs, dB, dC], axis=1)
            dxbc, pm = _mconv_bwd(s["xbc"], W["m_conv_w"][j], W["m_conv_b"][j], dxa, Bl, L, name="mamba_conv_bwd")
            g["m_conv_w"][j], g["m_conv_b"][j], g["m_norm"][j] = pm[:M_CONV], pm[M_CONV], dnw.reshape(d_inner)
            ddt = ddt.astype(BF16)
            g["m_w_in"][j] = jnp.concatenate([
                mm(s["u"], dz, "tn", BF16, name="mamba_z_bwd_w"), mm(s["u"], dxbc, "tn", BF16, name="mamba_xbc_bwd_w"),
                mm(s["u"], ddt, "tn", BF16, name="mamba_dt_bwd_w")], axis=1)
            du = mm(dz, w_in[:, :d_inner], "nt", F32, name="mamba_z_bwd_x")
            du = mm(dxbc, w_in[:, d_inner:d_inner + conv_dim], "nt", F32, add=du, name="mamba_xbc_bwd_x")
            du = mm(ddt, w_in[:, d_inner + conv_dim:], "nt", F32, add=du, name="mamba_dt_bwd_x")
        dh, dn = _rms_bwd(s["h"], W["mix_norm"][i], du, dh, name="mix_norm_bwd")
        g["mix_norm"][i] = dn.reshape(D)
    g["hgrn_lb_logits"] = _lb_bwd(W["hgrn_lb_logits"], jnp.concatenate(dlb, axis=0), name="hgrn_lower_bounds_bwd")
    grads = {k: (jnp.stack(v) if isinstance(v, list) else v) for k, v in g.items() if not isinstance(v, _GradSink)}
    return loss, dh.reshape(Bl, L, D), grads


WEIGHTS = ("mix_norm", "ffn_norm", "final_norm", "hgrn_w_in", "hgrn_lb_logits", "hgrn_gnorm", "hgrn_w_out", "m_w_in",
           "m_conv_w", "m_conv_b", "m_dt_bias", "m_A_log", "m_D", "m_norm", "m_w_out", "f_w_up", "f_conv_w", "f_conv_b",
           "f_w_down")
COL_SHARDED = ("hgrn_w_in", "m_w_in", "f_w_up")
STACKED = ("hgrn_w_in", "f_w_up")
ROW_SHARDED = ("hgrn_w_out", "m_w_out", "f_w_down")
SMALL_SHARDED = ("m_conv_w", "m_conv_b", "m_norm", "f_conv_w")
SMALL = tuple(k for k in WEIGHTS if k not in COL_SHARDED + ROW_SHARDED)


def _join_last(g):
    nd = g.ndim
    t = g.transpose(tuple(range(1, nd - 1)) + (0, nd - 1))
    return t.reshape(t.shape[:-2] + (N_DEV * g.shape[-1],))


def kernel(x, mix_norm, ffn_norm, final_norm, hgrn_w_in, hgrn_lb_logits, hgrn_gnorm, hgrn_w_out, m_w_in, m_conv_w, m_conv_b, m_dt_bias, m_A_log, m_D, m_norm, m_w_out, f_w_up, f_conv_w, f_conv_b, f_w_down, loss_target, m_mix_norm, m_ffn_norm, m_final_norm, m_hgrn_w_in, m_hgrn_lb_logits, m_hgrn_gnorm, m_hgrn_w_out, m_m_w_in, m_m_conv_w, m_m_conv_b, m_m_dt_bias, m_m_A_log, m_m_D, m_m_norm, m_m_w_out, m_f_w_up, m_f_conv_w, m_f_conv_b, m_f_w_down, v_mix_norm, v_ffn_norm, v_final_norm, v_hgrn_w_in, v_hgrn_lb_logits, v_hgrn_gnorm, v_hgrn_w_out, v_m_w_in, v_m_conv_w, v_m_conv_b, v_m_dt_bias, v_m_A_log, v_m_D, v_m_norm, v_m_w_out, v_f_w_up, v_f_conv_w, v_f_conv_b, v_f_w_down):
    w = dict(zip(WEIGHTS, (mix_norm, ffn_norm, final_norm, hgrn_w_in, hgrn_lb_logits, hgrn_gnorm, hgrn_w_out, m_w_in, m_conv_w, m_conv_b, m_dt_bias, m_A_log, m_D, m_norm, m_w_out, f_w_up, f_conv_w, f_conv_b, f_w_down)))
    m = dict(zip(WEIGHTS, (m_mix_norm, m_ffn_norm, m_final_norm, m_hgrn_w_in, m_hgrn_lb_logits, m_hgrn_gnorm, m_hgrn_w_out, m_m_w_in, m_m_conv_w, m_m_conv_b, m_m_dt_bias, m_m_A_log, m_m_D, m_m_norm, m_m_w_out, m_f_w_up, m_f_conv_w, m_f_conv_b, m_f_w_down)))
    v = dict(zip(WEIGHTS, (v_mix_norm, v_ffn_norm, v_final_norm, v_hgrn_w_in, v_hgrn_lb_logits, v_hgrn_gnorm, v_hgrn_w_out, v_m_w_in, v_m_conv_w, v_m_conv_b, v_m_dt_bias, v_m_A_log, v_m_D, v_m_norm, v_m_w_out, v_f_w_up, v_f_conv_w, v_f_conv_b, v_f_w_down)))
    me = 4 * lax.axis_index("x") + 2 * lax.axis_index("y") + lax.axis_index("c")

    def gather_now(a, *, name):
        buf = _Buf(a.shape[0], a.shape[1], a.dtype)
        _exchange([_Job(a, "gather", buf)], name=name)
        return buf.arr

    plan = _Plan()
    depth = mix_norm.shape[0]
    use_order = []
    for i in range(depth):
        use_order += [("hgrn_w_in", i // 2), ("hgrn_w_out", i // 2)] if i % 2 == 0 else [("m_w_in", i // 2), ("m_w_out", i // 2)]
        use_order += [("f_w_up", i), ("f_w_down", i)]
    gathers = {}
    for k, layer in use_order:
        shard = w[k][layer].astype(BF16)
        gathers[k, layer] = _Buf(shard.shape[0], shard.shape[1], BF16)
        plan.push(shard, "gather_chip", gathers[k, layer])

    def fetch(k, layer):
        g = plan.finish(gathers[k, layer], name=f"gather_{k}")
        if k in STACKED:
            return g
        if k in COL_SHARDED:
            return g.transpose(1, 0, 2).reshape(g.shape[1], N_DEV * g.shape[2])
        return g.reshape(N_DEV * g.shape[1], g.shape[2])

    full = {k: w[k] for k in SMALL if k not in SMALL_SHARDED}
    for k in COL_SHARDED:
        full[k] = _LazyWeights(k, (w[k].shape[0], w[k].shape[1], N_DEV * w[k].shape[2]), fetch)
    for k in ROW_SHARDED:
        full[k] = _LazyWeights(k, (w[k].shape[0], N_DEV * w[k].shape[1], w[k].shape[2]), fetch)
    shard_shapes = [w[k].shape for k in SMALL_SHARDED]
    gathered = gather_now(_pack([w[k] for k in SMALL_SHARDED]), name="gather_small_params")
    for k, g in zip(SMALL_SHARDED, _unpack(gathered, shard_shapes, lead=(N_DEV,))):
        full[k] = _join_last(g)

    lands = {k: _Buf(math.prod(w[k].shape[:-1]), w[k].shape[-1], BF16) for k in COL_SHARDED + ROW_SHARDED}

    def push(k, layer, g):
        if k in STACKED:
            parts = g
        elif k in COL_SHARDED:
            parts = g.reshape(g.shape[0], N_DEV, g.shape[1] // N_DEV).transpose(1, 0, 2)
        else:
            parts = g.reshape(N_DEV, g.shape[0] // N_DEV, g.shape[1])
        plan.push(parts, "scatter", lands[k], q0=layer * parts.shape[1])

    loss, grad_x, grads = _local_step(x, loss_target, full, plan, push)

    small_full_shapes = [(1, 1)] + [grads[k].shape for k in SMALL]
    packed = _pack([loss] + [grads[k] for k in SMALL])
    small_parts = _Buf(packed.shape[0], packed.shape[1], F32)
    plan.push(packed, "gather_chip", small_parts)
    out = {}
    for k in ("m_w_out", "m_w_in", "f_w_down", "f_w_up", "hgrn_w_out", "hgrn_w_in"):
        land = plan.finish(lands[k], name="scatter_" + k)
        rows, n = math.prod(w[k].shape[:-1]), w[k].shape[-1]
        res = _adamw_sharded(land, w[k].reshape(rows, n), m[k].reshape(rows, n), v[k].reshape(rows, n),
                             name="adamw_" + k, jobs=plan.take("adamw_" + k))
        out[k] = tuple(r.reshape(w[k].shape) for r in res)

    summed = _sum_devices(plan.finish(small_parts, name="gather_small_grads"), name="sum_small_grads")
    parts = _unpack(summed, small_full_shapes)
    loss_all, small_g = parts[0], dict(zip(SMALL, parts[1:]))
    for k in SMALL_SHARDED:
        n = w[k].shape[-1]
        small_g[k] = lax.dynamic_slice_in_dim(small_g[k], me * n, n, axis=small_g[k].ndim - 1)
    shapes = [w[k].shape for k in SMALL]
    res = _adamw_packed(_pack([small_g[k] for k in SMALL]), _pack([w[k] for k in SMALL]), _pack([m[k] for k in SMALL]),
                        _pack([v[k] for k in SMALL]), name="adamw_small")
    res = [_unpack(r, shapes) for r in res]
    for i, k in enumerate(SMALL):
        out[k] = (small_g[k], res[0][i], res[1][i], res[2][i])

    return (loss_all.reshape(()), grad_x, *[out[k][0] for k in WEIGHTS], *[out[k][1] for k in WEIGHTS],
            *[out[k][2] for k in WEIGHTS], *[out[k][3] for k in WEIGHTS])
```

```python
import functools
import math

import jax
import jax.numpy as jnp
from jax import lax
from jax.experimental import pallas as pl
from jax.experimental.pallas import tpu as pltpu

F32 = jnp.float32
BF16 = jnp.bfloat16
HIGHEST = lax.Precision.HIGHEST

NORM_EPS = 1e-5
ADAM_LR, ADAM_B1, ADAM_B2, ADAM_EPS, ADAM_WD, ADAM_STEP = 0.001, 0.9, 0.999, 1e-08, 0.01, 10

N_DEV = 8
LANES = 128
V7X_VMEM_BYTES = 64 * 1024 * 1024
VMEM_LIMIT = V7X_VMEM_BYTES * 3 // 4

HGRN_HEAD = 128
HGRN_CHUNK = 128
HGRN_SUB = 16
M_HEADDIM = 64
M_GROUPS = 8
M_D_STATE = 128
M_CONV = 4
M_CHUNK = 256
FFN_CONV = 3
CONV_COLS = 256
HGRN_PAR = 2


def _params(dims=None, **kw):
    return pltpu.CompilerParams(dimension_semantics=dims, vmem_limit_bytes=VMEM_LIMIT, **kw)


def _tile(dim, target, align=LANES, stretch=False):
    if stretch:
        t = _tile(dim, target, align)
        if 4 * t >= 3 * target:
            return t
        for up in range((target // align + 1) * align, 3 * target // 2 + 1, align):
            if dim % up == 0:
                return up
        return t
    t = (min(target, dim) // align) * align
    while t >= align:
        if dim % t == 0:
            return t
        if 2 * t < target and dim <= 2 * target:
            return dim
        t -= align
    return dim


class _Buf:
    def __init__(self, rows, n, dtype):
        self.arr = lax.empty((N_DEV, rows, n), dtype)


SIBLING = 1
OTHER_CHIPS = (2, 4, 6)


class _Job:
    def __init__(self, src, mode, buf, r0=0, r1=None, q0=0):
        rows = src.shape[-2]
        self.src, self.mode, self.buf, self.r0, self.r1, self.q0 = src, mode, buf, r0, rows if r1 is None else r1, q0


def _job_copies(job, src_ref, out_ref, send_sems, recv_sems, local_sem):
    x, y, c = lax.axis_index("x"), lax.axis_index("y"), lax.axis_index("c")
    me = 4 * x + 2 * y + c
    n = job.r1 - job.r0
    rows, land = pl.ds(job.r0, n), pl.ds(job.q0, n)

    def peer_of(k):
        kx, ky, kc = (k >> 2) & 1, (k >> 1) & 1, k & 1
        px, py, pc = (1 - x if kx else x), (1 - y if ky else y), (1 - c if kc else c)
        return 4 * px + 2 * py + pc, (px, py, pc)

    pairs = []
    if job.mode == "forward":
        sib, sib_id = peer_of(SIBLING)
        for i, k in enumerate(OTHER_CHIPS):
            mine, _ = peer_of(k)
            theirs, _ = peer_of(k ^ SIBLING)
            sems = dict(send_sem=send_sems.at[i], recv_sem=recv_sems.at[i], device_id=sib_id, device_id_type=pl.DeviceIdType.MESH)
            send = pltpu.make_async_remote_copy(src_ref=out_ref.at[mine, land], dst_ref=out_ref.at[mine, land], **sems)
            landing = pltpu.make_async_remote_copy(src_ref=out_ref.at[mine, land], dst_ref=out_ref.at[theirs, land], **sems)
            pairs.append((send, landing))
        return None, pairs
    part = (lambda d: src_ref.at[d, rows]) if job.mode == "scatter" else (lambda d: src_ref.at[rows])
    local = pltpu.make_async_copy(part(me), out_ref.at[me, land], local_sem)
    for k in ((SIBLING,) + OTHER_CHIPS if job.mode == "gather_chip" else range(1, N_DEV)):
        peer, peer_id = peer_of(k)
        sems = dict(send_sem=send_sems.at[k - 1], recv_sem=recv_sems.at[k - 1], device_id=peer_id,
                    device_id_type=pl.DeviceIdType.MESH)
        send = pltpu.make_async_remote_copy(src_ref=part(peer), dst_ref=out_ref.at[me, land], **sems)
        landing = pltpu.make_async_remote_copy(src_ref=part(me), dst_ref=out_ref.at[peer, land], **sems)
        pairs.append((send, landing))
    return local, pairs


def _jobs_start(jobs, src_refs, out_refs, send_sems, recv_sems, local_sems):
    for i, job in enumerate(jobs):
        local, pairs = _job_copies(job, src_refs[i], out_refs[i], send_sems.at[i], recv_sems.at[i], local_sems.at[i])
        if local is not None:
            local.start()
        for send, _ in pairs:
            send.start()


def _jobs_wait(jobs, src_refs, out_refs, send_sems, recv_sems, local_sems):
    for i, job in enumerate(jobs):
        local, pairs = _job_copies(job, src_refs[i], out_refs[i], send_sems.at[i], recv_sems.at[i], local_sems.at[i])
        for _, landing in pairs:
            landing.wait_recv()
        for send, _ in pairs:
            send.wait_send()
        if local is not None:
            local.wait()


def _job_scratch(n):
    return [pltpu.SemaphoreType.DMA((n, N_DEV - 1)), pltpu.SemaphoreType.DMA((n, N_DEV - 1)), pltpu.SemaphoreType.DMA((n,))]


def _buf_shapes(jobs):
    assert len({id(j.buf) for j in jobs}) == len(jobs), "one job per buffer and call"
    return tuple(jax.ShapeDtypeStruct(j.buf.arr.shape, j.buf.arr.dtype) for j in jobs)


def _exchange(jobs, *, name):
    n = len(jobs)

    def body(*refs):
        srcs, dsts, sems = refs[:n], refs[2 * n:3 * n], refs[3 * n:]
        _jobs_start(jobs, srcs, dsts, *sems)
        _jobs_wait(jobs, srcs, dsts, *sems)

    hbm = pl.BlockSpec(memory_space=pl.ANY)
    outs = pl.pallas_call(
        body, name=name, out_shape=_buf_shapes(jobs), in_specs=[hbm] * (2 * n), out_specs=(hbm,) * n,
        input_output_aliases={n + i: i for i in range(n)},
        scratch_shapes=_job_scratch(n), compiler_params=pltpu.CompilerParams(has_side_effects=True),
    )(*[j.src for j in jobs], *[j.buf.arr for j in jobs])
    for j, o in zip(jobs, outs):
        j.buf.arr = o


def _call(body, *, name, out_shape, grid, in_specs, out_specs, scratch_shapes=(), dims, jobs=()):
    single = not isinstance(out_shape, (tuple, list))
    out_shape = (out_shape,) if single else tuple(out_shape)
    out_specs = (out_specs,) if single else tuple(out_specs)
    n_in, n_out, n_scr, nj = len(in_specs), len(out_shape), len(scratch_shapes), len(jobs)
    if not jobs:
        plain = pl.pallas_call(body, name=name, out_shape=out_shape, grid=grid, in_specs=list(in_specs), out_specs=out_specs,
                               scratch_shapes=list(scratch_shapes), compiler_params=_params(dims))

        def run_plain(*args):
            r = plain(*args)
            return r[0] if single else tuple(r)

        return run_plain

    def carrying(*refs):
        ins, srcs = refs[:n_in], refs[n_in:n_in + nj]
        o0 = n_in + 2 * nj
        outs, dsts = refs[o0:o0 + n_out], refs[o0 + n_out:o0 + n_out + nj]
        scr, sems = refs[o0 + n_out + nj:o0 + n_out + nj + n_scr], refs[o0 + n_out + nj + n_scr:]
        ids = [pl.program_id(d) for d in range(len(grid))]
        first = functools.reduce(jnp.logical_and, [i == 0 for i in ids])
        last = functools.reduce(jnp.logical_and, [i == g - 1 for i, g in zip(ids, grid)])

        @pl.when(first)
        def _():
            _jobs_start(jobs, srcs, dsts, *sems)

        body(*ins, *outs, *scr)

        @pl.when(last)
        def _():
            _jobs_wait(jobs, srcs, dsts, *sems)

    hbm = pl.BlockSpec(memory_space=pl.ANY)
    call = pl.pallas_call(
        carrying, name=name, out_shape=out_shape + _buf_shapes(jobs), grid=grid,
        in_specs=list(in_specs) + [hbm] * (2 * nj), out_specs=out_specs + (hbm,) * nj,
        input_output_aliases={n_in + nj + i: n_out + i for i in range(nj)},
        scratch_shapes=list(scratch_shapes) + _job_scratch(nj),
        compiler_params=_params(("arbitrary",) * len(grid), has_side_effects=True))

    def run(*args):
        r = call(*args, *[j.src for j in jobs], *[j.buf.arr for j in jobs])
        for j, o in zip(jobs, r[n_out:]):
            j.buf.arr = o
        return r[0] if single else tuple(r[:n_out])

    return run


def _dg(a, b, ca, cb):
    return lax.dot_general(a.astype(BF16), b.astype(BF16), (((ca,), (cb,)), ((), ())), preferred_element_type=F32)


@jax.custom_vjp
def _dot_nn(a, b):
    return _dg(a, b, 1, 0)


def _dot_nn_f(a, b):
    return _dg(a, b, 1, 0), (a, b)


def _dot_nn_b(res, g):
    a, b = res
    return _dg(g, b, 1, 1), _dg(a, g, 0, 0)


_dot_nn.defvjp(_dot_nn_f, _dot_nn_b)


@jax.custom_vjp
def _dot_nt(a, b):
    return _dg(a, b, 1, 1)


def _dot_nt_f(a, b):
    return _dg(a, b, 1, 1), (a, b)


def _dot_nt_b(res, g):
    a, b = res
    return _dg(g, b, 1, 0), _dg(g, a, 0, 0)


_dot_nt.defvjp(_dot_nt_f, _dot_nt_b)


@jax.custom_vjp
def _dot_tn(a, b):
    return _dg(a, b, 0, 0)


def _dot_tn_f(a, b):
    return _dg(a, b, 0, 0), (a, b)


def _dot_tn_b(res, g):
    a, b = res
    return _dg(b, g, 1, 1), _dg(a, g, 1, 0)


_dot_tn.defvjp(_dot_tn_f, _dot_tn_b)


def _tri(n):
    return (lax.broadcasted_iota(jnp.int32, (n, n), 0) >= lax.broadcasted_iota(jnp.int32, (n, n), 1)).astype(F32)


def _silu(x):
    return x * jax.nn.sigmoid(x)


def _matmul(a, b, mode, out_dtype, *, name, add=None, out_stack=None, tm=1024, tn=1024, tk=2048, jobs=()):
    a_n = a.shape[2] if a.ndim == 3 else None
    b_n = b.shape[2] if b.ndim == 3 else None
    a_shape = (a.shape[1], a.shape[0] * a.shape[2]) if a_n else a.shape
    b_shape = (b.shape[1], b.shape[0] * b.shape[2]) if b_n else b.shape
    if mode == "nn":
        (M, K), (K2, N) = a_shape, b_shape
    elif mode == "nt":
        (M, K), (N, K2) = a_shape, b_shape
    else:
        (K, M), (K2, N) = a_shape, b_shape
    assert K == K2, (a.shape, b.shape, mode)
    n_lim = {"m": M, "n": N, "k": K}
    for dim, n in (("m" if mode == "tn" else "k", a_n), ("k" if mode == "nt" else "n", b_n), ("n", N // out_stack if out_stack else None)):
        if n:
            n_lim[dim] = math.gcd(n_lim[dim], n)
    tm, tn = _tile(n_lim["m"], tm, 8 if M % LANES else LANES, stretch=True), _tile(n_lim["n"], tn, stretch=True)
    tk = _tile(n_lim["k"], tk)
    nk = K // tk
    ca, cb = {"nn": (1, 0), "nt": (1, 1), "tn": (0, 0)}[mode]

    def body(*refs):
        a_ref, b_ref = refs[:2]
        add_ref = refs[2] if add is not None else None
        o_ref = refs[2 + (add is not None)]

        def finish(r):
            if add is not None:
                r = r + add_ref[...]
            o_ref[...] = r.astype(o_ref.dtype)

        if nk == 1:
            finish(_dg(a_ref[...], b_ref[...], ca, cb))
            return
        acc_ref = refs[-1]
        k = pl.program_id(2)

        @pl.when(k == 0)
        def _():
            acc_ref[...] = _dg(a_ref[...], b_ref[...], ca, cb)

        @pl.when((k > 0) & (k < nk - 1))
        def _():
            acc_ref[...] += _dg(a_ref[...], b_ref[...], ca, cb)

        @pl.when(k == nk - 1)
        def _():
            finish(acc_ref[...] + _dg(a_ref[...], b_ref[...], ca, cb))

    def spec(rows, cols, t_rows, t_cols, n):
        if not n:
            return pl.BlockSpec((t_rows, t_cols), lambda i, j, k: (rows(i, j, k), cols(i, j, k)))
        per = n // t_cols
        return pl.BlockSpec((None, t_rows, t_cols), lambda i, j, k: (cols(i, j, k) // per, rows(i, j, k), cols(i, j, k) % per))

    gi, gj, gk = (lambda i, j, k: i), (lambda i, j, k: j), (lambda i, j, k: k)
    a_spec = spec(gk, gi, tk, tm, a_n) if mode == "tn" else spec(gi, gk, tm, tk, a_n)
    b_spec = spec(gj, gk, tn, tk, b_n) if mode == "nt" else spec(gk, gj, tk, tn, b_n)
    o_spec = spec(gi, gj, tm, tn, N // out_stack if out_stack else None)
    out_shape = (out_stack, M, N // out_stack) if out_stack else (M, N)
    in_specs, args = [a_spec, b_spec], [a, b]
    if add is not None:
        assert not out_stack
        in_specs.append(o_spec)
        args.append(add)
    return _call(
        body, name=name, out_shape=jax.ShapeDtypeStruct(out_shape, out_dtype), grid=(M // tm, N // tn, nk),
        in_specs=in_specs, out_specs=o_spec, scratch_shapes=[pltpu.VMEM((tm, tn), F32)] if nk > 1 else [],
        dims=("parallel", "parallel", "arbitrary"), jobs=jobs,
    )(*args)


def _rms_fwd(h, w, *, name):
    T, D = h.shape
    tr = _tile(T, 256, 8)

    def body(h_ref, w_ref, u_ref):
        x = h_ref[...]
        u_ref[...] = (x * lax.rsqrt(jnp.mean(x * x, axis=-1, keepdims=True) + NORM_EPS) * w_ref[...]).astype(u_ref.dtype)

    return pl.pallas_call(
        body, name=name, out_shape=jax.ShapeDtypeStruct((T, D), BF16), grid=(T // tr,),
        in_specs=[pl.BlockSpec((tr, D), lambda i: (i, 0)), pl.BlockSpec((1, D), lambda i: (0, 0))],
        out_specs=pl.BlockSpec((tr, D), lambda i: (i, 0)), compiler_params=_params(("parallel",)),
    )(h, w.reshape(1, D))


def _rms_bwd(h, w, du, dh_in, *, name):
    T, D = h.shape
    tr = _tile(T, 256, 8)

    def body(h_ref, w_ref, du_ref, dhin_ref, dh_ref, dw_ref):
        x = h_ref[...]
        g = du_ref[...].astype(F32)
        rstd = lax.rsqrt(jnp.mean(x * x, axis=-1, keepdims=True) + NORM_EPS)
        xhat = x * rstd
        gx = g * w_ref[...]
        dh_ref[...] = dhin_ref[...] + rstd * (gx - xhat * jnp.mean(gx * xhat, axis=-1, keepdims=True))

        @pl.when(pl.program_id(0) == 0)
        def _():
            dw_ref[...] = jnp.zeros_like(dw_ref)

        dw_ref[...] += jnp.sum(g * xhat, axis=0, keepdims=True)

    row = pl.BlockSpec((tr, D), lambda i: (i, 0))
    vec = pl.BlockSpec((1, D), lambda i: (0, 0))
    return pl.pallas_call(
        body, name=name, out_shape=(jax.ShapeDtypeStruct((T, D), F32), jax.ShapeDtypeStruct((1, D), F32)), grid=(T // tr,),
        in_specs=[row, vec, row, row], out_specs=(row, vec), compiler_params=_params(("arbitrary",)),
    )(h, w.reshape(1, D), du, dh_in)


def _loss_head(h, w, target, *, name):
    T, D = h.shape
    tr = _tile(T, 256, 8)

    def body(h_ref, w_ref, t_ref, loss_ref, dh_ref, dw_ref):
        x = h_ref[...]
        rstd = lax.rsqrt(jnp.mean(x * x, axis=-1, keepdims=True) + NORM_EPS)
        xhat = x * rstd
        err = xhat * w_ref[...] - t_ref[...]
        g = err * (1.0 / D)
        gx = g * w_ref[...]
        dh_ref[...] = rstd * (gx - xhat * jnp.mean(gx * xhat, axis=-1, keepdims=True))

        @pl.when(pl.program_id(0) == 0)
        def _():
            dw_ref[...] = jnp.zeros_like(dw_ref)
            loss_ref[...] = jnp.zeros_like(loss_ref)

        dw_ref[...] += jnp.sum(g * xhat, axis=0, keepdims=True)
        loss_ref[...] += (0.5 / D) * jnp.sum(jnp.sum(err * err, axis=-1, keepdims=True), axis=0, keepdims=True)

    row = pl.BlockSpec((tr, D), lambda i: (i, 0))
    vec = pl.BlockSpec((1, D), lambda i: (0, 0))
    one = pl.BlockSpec((1, 1), lambda i: (0, 0))
    return pl.pallas_call(
        body, name=name,
        out_shape=(jax.ShapeDtypeStruct((1, 1), F32), jax.ShapeDtypeStruct((T, D), F32), jax.ShapeDtypeStruct((1, D), F32)),
        grid=(T // tr,), in_specs=[row, vec, row], out_specs=(one, row, vec), compiler_params=_params(("arbitrary",)),
    )(h, w.reshape(1, D), target)


def _shift_down(x, s):
    if s == 0:
        return x
    rows = lax.broadcasted_iota(jnp.int32, x.shape, 0)
    return jnp.where(rows >= s, pltpu.roll(x, s, 0), 0.0)


def _shift_up(x, s):
    if s == 0:
        return x
    n = x.shape[0]
    rows = lax.broadcasted_iota(jnp.int32, x.shape, 0)
    return jnp.where(rows < n - s, pltpu.roll(x, n - s, 0), 0.0)


def _conv_pre(a, w_ref, b_ref, taps):
    pre = b_ref[0:1, :] + w_ref[taps - 1:taps, :] * a
    for k in range(taps - 1):
        pre = pre + w_ref[k:k + 1, :] * _shift_down(a, taps - 1 - k)
    return pre


def _conv_bwd(a, dpre, w_ref, taps):
    da = w_ref[taps - 1:taps, :] * dpre
    rows = []
    for k in range(taps - 1):
        up = _shift_up(dpre, taps - 1 - k)
        da = da + w_ref[k:k + 1, :] * up
        rows.append(jnp.sum(up * a, axis=0, keepdims=True))
    rows.append(jnp.sum(dpre * a, axis=0, keepdims=True))
    rows.append(jnp.sum(dpre, axis=0, keepdims=True))
    return da, rows


def _ffn_gate_fwd(a, cw, cb, Bl, L, *, name, jobs=()):
    T, F2 = a.shape
    F = F2 // 2
    tc = _tile(F, CONV_COLS)
    nj = F // tc

    def body(ag_ref, au_ref, wg_ref, wu_ref, bg_ref, bu_ref, y_ref, h_ref):
        hg = _conv_pre(ag_ref[...].astype(F32), wg_ref, bg_ref, FFN_CONV)
        hu = _conv_pre(au_ref[...].astype(F32), wu_ref, bu_ref, FFN_CONV)
        y_ref[...] = (_silu(hg) * hu).astype(y_ref.dtype)
        h_ref[0] = hg.astype(h_ref.dtype)
        h_ref[1] = hu.astype(h_ref.dtype)

    blk = lambda off: pl.BlockSpec((L, tc), lambda b, j: (b, j + off))
    wblk = lambda off: pl.BlockSpec((FFN_CONV, tc), lambda b, j: (0, j + off))
    bblk = lambda off: pl.BlockSpec((1, tc), lambda b, j: (0, j + off))
    return _call(
        body, name=name, out_shape=(jax.ShapeDtypeStruct((T, F), BF16), jax.ShapeDtypeStruct((2, T, F), BF16)), grid=(Bl, nj),
        in_specs=[blk(0), blk(nj), wblk(0), wblk(nj), bblk(0), bblk(nj)],
        out_specs=(blk(0), pl.BlockSpec((2, L, tc), lambda b, j: (0, b, j))),
        dims=("parallel", "parallel"), jobs=jobs,
    )(a, a, cw, cw, cb.reshape(1, F2), cb.reshape(1, F2))


def _ffn_gate_bwd(a, h, cw, dy, Bl, L, *, name, jobs=()):
    T, F2 = a.shape
    F = F2 // 2
    tc = _tile(F, CONV_COLS)
    nj = F // tc

    def body(ag_ref, au_ref, h_ref, wg_ref, wu_ref, dy_ref, da_ref, pg_ref, pu_ref):
        ag, au = ag_ref[...].astype(F32), au_ref[...].astype(F32)
        hg, hu = h_ref[0].astype(F32), h_ref[1].astype(F32)
        g = dy_ref[...].astype(F32)
        s = jax.nn.sigmoid(hg)
        dhg = g * hu * (s * (1.0 + hg * (1.0 - s)))
        dhu = g * (hg * s)
        dag, rg = _conv_bwd(ag, dhg, wg_ref, FFN_CONV)
        dau, ru = _conv_bwd(au, dhu, wu_ref, FFN_CONV)
        da_ref[0] = dag.astype(da_ref.dtype)
        da_ref[1] = dau.astype(da_ref.dtype)

        @pl.when(pl.program_id(1) == 0)
        def _():
            pg_ref[...] = jnp.zeros_like(pg_ref)
            pu_ref[...] = jnp.zeros_like(pu_ref)

        for k in range(FFN_CONV + 1):
            pg_ref[k:k + 1, :] += rg[k]
            pu_ref[k:k + 1, :] += ru[k]

    blk = lambda off: pl.BlockSpec((L, tc), lambda j, b: (b, j + off))
    wblk = lambda off: pl.BlockSpec((FFN_CONV, tc), lambda j, b: (0, j + off))
    pblk = lambda off: pl.BlockSpec((8, tc), lambda j, b: (0, j + off))
    both = pl.BlockSpec((2, L, tc), lambda j, b: (0, b, j))
    da, pg, pu = _call(
        body, name=name,
        out_shape=(jax.ShapeDtypeStruct((2, T, F), BF16), jax.ShapeDtypeStruct((8, F), F32), jax.ShapeDtypeStruct((8, F), F32)),
        grid=(nj, Bl),
        in_specs=[blk(0), blk(nj), both, wblk(0), wblk(nj), blk(0)],
        out_specs=(both, pblk(0), pblk(0)),
        dims=("parallel", "arbitrary"), jobs=jobs,
    )(a, a, h, cw, cw, dy)
    return da, jnp.concatenate([pg, pu], axis=1)


def _mconv_fwd(xbc, cw, cb, Bl, L, *, name, jobs=()):
    T, W = xbc.shape
    tc = _tile(W, CONV_COLS)

    def body(a_ref, w_ref, b_ref, y_ref):
        y_ref[...] = _silu(_conv_pre(a_ref[...], w_ref, b_ref, M_CONV))

    blk = pl.BlockSpec((L, tc), lambda b, j: (b, j))
    return _call(
        body, name=name, out_shape=jax.ShapeDtypeStruct((T, W), F32), grid=(Bl, W // tc),
        in_specs=[blk, pl.BlockSpec((M_CONV, tc), lambda b, j: (0, j)), pl.BlockSpec((1, tc), lambda b, j: (0, j))],
        out_specs=blk, dims=("parallel", "parallel"), jobs=jobs,
    )(xbc, cw, cb.reshape(1, W))


def _mconv_bwd(xbc, cw, cb, dy, Bl, L, *, name):
    T, W = xbc.shape
    tc = _tile(W, CONV_COLS)

    def body(a_ref, w_ref, b_ref, dy_ref, da_ref, p_ref):
        a = a_ref[...]
        pre = _conv_pre(a, w_ref, b_ref, M_CONV)
        s = jax.nn.sigmoid(pre)
        dpre = dy_ref[...] * (s * (1.0 + pre * (1.0 - s)))
        da, rows = _conv_bwd(a, dpre, w_ref, M_CONV)
        da_ref[...] = da.astype(da_ref.dtype)

        @pl.when(pl.program_id(1) == 0)
        def _():
            p_ref[...] = jnp.zeros_like(p_ref)

        for k in range(M_CONV + 1):
            p_ref[k:k + 1, :] += rows[k]

    blk = pl.BlockSpec((L, tc), lambda j, b: (b, j))
    return pl.pallas_call(
        body, name=name, out_shape=(jax.ShapeDtypeStruct((T, W), BF16), jax.ShapeDtypeStruct((8, W), F32)),
        grid=(W // tc, Bl),
        in_specs=[blk, pl.BlockSpec((M_CONV, tc), lambda j, b: (0, j)), pl.BlockSpec((1, tc), lambda j, b: (0, j)), blk],
        out_specs=(blk, pl.BlockSpec((8, tc), lambda j, b: (0, j))),
        compiler_params=_params(("parallel", "arbitrary")),
    )(xbc, cw, cb.reshape(1, W), dy)


def _hgrn_chunk(st, qr, fr, v, gr, lb, gn):
    C = qr.shape[0]
    q = _silu(qr)
    logf = jnp.log(lb + (1.0 - lb) * jax.nn.sigmoid(fr))
    k = (1.0 - lb) * jax.nn.sigmoid(-fr)
    b = jnp.dot(_tri(C), logf, precision=HIGHEST, preferred_element_type=F32)
    o = _dot_nt(q * jnp.exp(b), st)
    keep = lax.broadcasted_iota(jnp.int32, (HGRN_SUB, HGRN_SUB), 0) >= lax.broadcasted_iota(jnp.int32, (HGRN_SUB, HGRN_SUB), 1)
    cols = lax.broadcasted_iota(jnp.int32, (HGRN_SUB, C), 1)
    parts = []
    for blk in range(C // HGRN_SUB):
        r0 = blk * HGRN_SUB
        bi, qi, ki, vi = b[r0:r0 + HGRN_SUB], q[r0:r0 + HGRN_SUB], k[r0:r0 + HGRN_SUB], v[r0:r0 + HGRN_SUB]
        decay = jnp.exp(jnp.minimum(bi[:, None, :] - bi[None, :, :], 0.0))
        a_diag = jnp.where(keep, jnp.sum(qi[:, None, :] * ki[None, :, :] * decay, axis=-1), 0.0)
        oi = _dot_nn(a_diag, vi)
        if blk > 0:
            bs = b[r0 - 1:r0]
            a_off = _dot_nt(qi * jnp.exp(bi - bs), k * jnp.exp(jnp.minimum(bs - b, 0.0)))
            oi = oi + _dot_nn(jnp.where(cols < r0, a_off, 0.0), v)
        parts.append(oi)
    o = o + jnp.concatenate(parts, axis=0)
    bl = b[C - 1:C]
    st1 = st * jnp.exp(bl) + _dot_tn(v, k * jnp.exp(bl - b))
    og = o * lax.rsqrt(jnp.mean(o * o, axis=-1, keepdims=True) + NORM_EPS) * gn * _silu(gr)
    return st1, og


def _hgrn_specs(Bl, L, H, order):
    W, HB = HGRN_PAR * HGRN_HEAD, H // HGRN_PAR

    def at(off):
        if order == "bh":
            return pl.BlockSpec((L, W), lambda b, h: (b, h + off))
        return pl.BlockSpec((L, W), lambda h, b: (b, h + off))
    return [at(0), at(HB), at(2 * HB), at(3 * HB)]


def _hgrn_fwd(P, lb, gn, Bl, L, *, name, jobs=()):
    T, D4 = P.shape
    D = D4 // 4
    H = D // HGRN_HEAD
    C = HGRN_CHUNK
    NC = L // C
    W, HB = HGRN_PAR * HGRN_HEAD, H // HGRN_PAR
    heads = [slice(p * HGRN_HEAD, (p + 1) * HGRN_HEAD) for p in range(HGRN_PAR)]

    def body(q_ref, f_ref, v_ref, g_ref, lb_ref, gn_ref, og_ref, st_ref):
        gnv = gn_ref[...]

        def step(c, sts):
            r = pl.ds(pl.multiple_of(c * C, C), C)
            new = []
            for p, hd in enumerate(heads):
                st_ref[p, c] = sts[p]
                st1, og = _hgrn_chunk(sts[p], q_ref[r, hd], f_ref[r, hd], v_ref[r, hd], g_ref[r, hd], lb_ref[:, hd], gnv)
                og_ref[r, hd] = og.astype(og_ref.dtype)
                new.append(st1)
            return tuple(new)

        lax.fori_loop(0, NC, step, tuple(jnp.zeros((HGRN_HEAD, HGRN_HEAD), F32) for _ in heads))

    return _call(
        body, name=name,
        out_shape=(jax.ShapeDtypeStruct((T, D), BF16), jax.ShapeDtypeStruct((Bl, H, NC, HGRN_HEAD, HGRN_HEAD), F32)),
        grid=(Bl, HB),
        in_specs=_hgrn_specs(Bl, L, H, "bh") + [pl.BlockSpec((1, W), lambda b, h: (0, h)), pl.BlockSpec((1, HGRN_HEAD), lambda b, h: (0, 0))],
        out_specs=(pl.BlockSpec((L, W), lambda b, h: (b, h)),
                   pl.BlockSpec((None, HGRN_PAR, NC, HGRN_HEAD, HGRN_HEAD), lambda b, h: (b, h, 0, 0, 0))),
        dims=("parallel", "parallel"), jobs=jobs,
    )(P, P, P, P, lb.reshape(1, D), gn.reshape(1, HGRN_HEAD))


def _hgrn_bwd(P, lb, gn, states, dog, Bl, L, *, name, jobs=()):
    T, D4 = P.shape
    D = D4 // 4
    H = D // HGRN_HEAD
    C = HGRN_CHUNK
    NC = L // C
    W, HB = HGRN_PAR * HGRN_HEAD, H // HGRN_PAR
    heads = [slice(p * HGRN_HEAD, (p + 1) * HGRN_HEAD) for p in range(HGRN_PAR)]

    def body(q_ref, f_ref, v_ref, g_ref, lb_ref, gn_ref, st_ref, dog_ref, dp_ref, dlb_ref, dgn_ref):
        gnv = gn_ref[...]

        @pl.when(pl.program_id(1) == 0)
        def _():
            dlb_ref[...] = jnp.zeros_like(dlb_ref)

        @pl.when((pl.program_id(0) == 0) & (pl.program_id(1) == 0))
        def _():
            dgn_ref[...] = jnp.zeros_like(dgn_ref)

        def step(i, dsts):
            c = NC - 1 - i
            r = pl.ds(pl.multiple_of(c * C, C), C)
            new, dgn_sum = [], jnp.zeros((1, HGRN_HEAD), F32)
            for p, hd in enumerate(heads):
                _, vjp = jax.vjp(_hgrn_chunk, st_ref[p, c], q_ref[r, hd], f_ref[r, hd], v_ref[r, hd], g_ref[r, hd],
                                 lb_ref[:, hd], gnv)
                dst0, dq, df, dv, dg, dlb, dgn = vjp((dsts[p], dog_ref[r, hd].astype(F32)))
                dp_ref[0, r, hd] = dq.astype(dp_ref.dtype)
                dp_ref[1, r, hd] = df.astype(dp_ref.dtype)
                dp_ref[2, r, hd] = dv.astype(dp_ref.dtype)
                dp_ref[3, r, hd] = dg.astype(dp_ref.dtype)
                dlb_ref[:, hd] += dlb
                dgn_sum = dgn_sum + dgn
                new.append(dst0)
            dgn_ref[...] += dgn_sum
            return tuple(new)

        lax.fori_loop(0, NC, step, tuple(jnp.zeros((HGRN_HEAD, HGRN_HEAD), F32) for _ in heads))

    return _call(
        body, name=name,
        out_shape=(jax.ShapeDtypeStruct((4, T, D), BF16), jax.ShapeDtypeStruct((1, D), F32), jax.ShapeDtypeStruct((1, HGRN_HEAD), F32)),
        grid=(HB, Bl),
        in_specs=_hgrn_specs(Bl, L, H, "hb") + [
            pl.BlockSpec((1, W), lambda h, b: (0, h)), pl.BlockSpec((1, HGRN_HEAD), lambda h, b: (0, 0)),
            pl.BlockSpec((None, HGRN_PAR, NC, HGRN_HEAD, HGRN_HEAD), lambda h, b: (b, h, 0, 0, 0)),
            pl.BlockSpec((L, W), lambda h, b: (b, h))],
        out_specs=(pl.BlockSpec((4, L, W), lambda h, b: (0, b, h)),
                   pl.BlockSpec((1, W), lambda h, b: (0, h)), pl.BlockSpec((1, HGRN_HEAD), lambda h, b: (0, 0))),
        dims=("arbitrary", "arbitrary"), jobs=jobs,
    )(P, P, P, P, lb.reshape(1, D), gn.reshape(1, HGRN_HEAD), states, dog)


def _lower_bounds(logits):
    p = jax.nn.softmax(logits, axis=0)
    rows, run = [], jnp.zeros_like(p[0:1])
    for i in range(logits.shape[0]):
        run = run + p[i:i + 1]
        rows.append(run - p[0:1])
    return jnp.concatenate(rows, axis=0) if len(rows) > 1 else rows[0]


def _lb_fwd(logits, *, name):
    def body(x_ref, o_ref):
        o_ref[...] = _lower_bounds(x_ref[...])
    return pl.pallas_call(body, name=name, out_shape=jax.ShapeDtypeStruct(logits.shape, F32))(logits)


def _lb_bwd(logits, dlb, *, name):
    def body(x_ref, g_ref, o_ref):
        _, vjp = jax.vjp(_lower_bounds, x_ref[...])
        o_ref[...] = vjp(g_ref[...])[0]
    return pl.pallas_call(body, name=name, out_shape=jax.ShapeDtypeStruct(logits.shape, F32))(logits, dlb)


def _ssd_pre(dtc_raw, dtr_raw, bias_c, bias_r, alog_c, alog_r):
    C = dtc_raw.shape[0]
    dt_c = jax.nn.softplus(dtc_raw + bias_c)
    dt_r = jax.nn.softplus(dtr_raw + bias_r)
    acs_c = jnp.dot(_tri(C), dt_c * -jnp.exp(alog_c), precision=HIGHEST, preferred_element_type=F32)
    acs_r = lax.dot_general(dt_r * -jnp.exp(alog_r), _tri(C), (((1,), (1,)), ((), ())), precision=HIGHEST, preferred_element_type=F32)
    return dt_c, acs_c, acs_r


def _ssd_head(h0, xs, G, Bm, Cm, dt, acs, acs_row, dskip):
    C = xs.shape[0]
    causal = lax.broadcasted_iota(jnp.int32, (C, C), 0) >= lax.broadcasted_iota(jnp.int32, (C, C), 1)
    Lm = jnp.where(causal, jnp.exp(jnp.minimum(acs - acs_row, 0.0)), 0.0)
    X = xs * dt
    y = _dot_nn(G * Lm, X) + _dot_nt(Cm * jnp.exp(acs), h0) + dskip * xs
    alast = acs[C - 1:C]
    h1 = jnp.exp(alast) * h0 + _dot_tn(X, Bm * jnp.exp(alast - acs))
    return y, h1


def _ssd_layout(Bl, L, d_inner, order):
    G = M_GROUPS
    hpg = d_inner // M_HEADDIM // G
    gw = hpg * M_HEADDIM
    ix = (lambda b, g: (b, g)) if order == "bg" else (lambda g, b: (b, g))
    def cols(width, off):
        if order == "bg":
            return pl.BlockSpec((L, width), lambda b, g: (b, g + off))
        return pl.BlockSpec((L, width), lambda g, b: (b, g + off))
    def per_group(shape):
        n = len(shape)
        if order == "bg":
            return pl.BlockSpec((None,) + shape, lambda b, g: (g,) + (0,) * n)
        return pl.BlockSpec((None,) + shape, lambda g, b: (g,) + (0,) * n)
    if order == "bg":
        dtc = pl.BlockSpec((None, L, LANES), lambda b, g: (g, b, 0))
        dtr = pl.BlockSpec((None, 8, L), lambda b, g: (g, 0, b))
    else:
        dtc = pl.BlockSpec((None, L, LANES), lambda g, b: (g, b, 0))
        dtr = pl.BlockSpec((None, 8, L), lambda g, b: (g, 0, b))
    nb = d_inner // M_D_STATE
    return hpg, gw, cols, per_group, dtc, dtr, nb


def _ssd_fwd(xa, dtc, dtr, prm, Bl, L, d_inner, *, name, jobs=()):
    T = xa.shape[0]
    C = M_CHUNK
    NC = L // C
    G = M_GROUPS
    hpg, gw, cols, per_group, dtc_spec, dtr_spec, nb = _ssd_layout(Bl, L, d_inner, "bg")

    def body(xs_ref, b_ref, c_ref, dtc_ref, dtr_ref, bc_ref, br_ref, ac_ref, ar_ref, dk_ref, y_ref, hs_ref,
             dt_s, acs_s, acr_s, h_s):
        h_s[...] = jnp.zeros_like(h_s)

        def step(c, carry):
            r = pl.ds(pl.multiple_of(c * C, C), C)
            dt_c, acs_c, acs_r = _ssd_pre(dtc_ref[r, :], dtr_ref[:, r], bc_ref[...], br_ref[...], ac_ref[...], ar_ref[...])
            dt_s[...] = dt_c
            acs_s[...] = acs_c
            acr_s[...] = acs_r
            Bm, Cm = b_ref[r, :], c_ref[r, :]
            Gm = _dot_nt(Cm, Bm)
            for j in range(hpg):
                pc = slice(j * M_HEADDIM, (j + 1) * M_HEADDIM)
                h0 = h_s[j]
                hs_ref[c, j] = h0
                y, h1 = _ssd_head(h0, xs_ref[r, pc], Gm, Bm, Cm, dt_s[:, j:j + 1], acs_s[:, j:j + 1], acr_s[j:j + 1, :],
                                  dk_ref[0:1, j:j + 1])
                y_ref[r, pc] = y
                h_s[j] = h1
            return carry

        lax.fori_loop(0, NC, step, 0)

    return _call(
        body, name=name,
        out_shape=(jax.ShapeDtypeStruct((T, d_inner), F32), jax.ShapeDtypeStruct((Bl, G, NC, hpg, M_HEADDIM, M_D_STATE), F32)),
        grid=(Bl, G),
        in_specs=[cols(gw, 0), cols(M_D_STATE, nb), cols(M_D_STATE, nb + G), dtc_spec, dtr_spec,
                  per_group((1, LANES)), per_group((8, 1)), per_group((1, LANES)), per_group((8, 1)), per_group((1, LANES))],
        out_specs=(cols(gw, 0), pl.BlockSpec((None, None, NC, hpg, M_HEADDIM, M_D_STATE), lambda b, g: (b, g, 0, 0, 0, 0))),
        scratch_shapes=[pltpu.VMEM((C, LANES), F32), pltpu.VMEM((C, LANES), F32), pltpu.VMEM((8, C), F32),
                        pltpu.VMEM((hpg, M_HEADDIM, M_D_STATE), F32)],
        dims=("parallel", "parallel"), jobs=jobs,
    )(xa, xa, xa, dtc, dtr, *prm)


def _ssd_bwd(xa, dtc, dtr, prm, states, dy, Bl, L, d_inner, *, name, jobs=()):
    T = xa.shape[0]
    C = M_CHUNK
    NC = L // C
    G = M_GROUPS
    hpg, gw, cols, per_group, dtc_spec, dtr_spec, nb = _ssd_layout(Bl, L, d_inner, "gb")

    def body(xs_ref, b_ref, c_ref, dtc_ref, dtr_ref, bc_ref, br_ref, ac_ref, ar_ref, dk_ref, hs_ref, dy_ref,
             dxs_ref, db_ref, dc_ref, ddtc_ref, ddtr_ref, gbc_ref, gbr_ref, gac_ref, gar_ref, gdk_ref,
             dt_s, acs_s, acr_s, ddt_s, dacs_s, dacr_s, dh_s):
        dh_s[...] = jnp.zeros_like(dh_s)

        @pl.when(pl.program_id(1) == 0)
        def _():
            for ref in (gbc_ref, gbr_ref, gac_ref, gar_ref, gdk_ref):
                ref[...] = jnp.zeros_like(ref)

        def step(i, carry):
            c = NC - 1 - i
            r = pl.ds(pl.multiple_of(c * C, C), C)
            pre_in = (dtc_ref[r, :], dtr_ref[:, r], bc_ref[...], br_ref[...], ac_ref[...], ar_ref[...])
            (dt_c, acs_c, acs_r), pre_vjp = jax.vjp(_ssd_pre, *pre_in)
            dt_s[...] = dt_c
            acs_s[...] = acs_c
            acr_s[...] = acs_r
            ddt_s[...] = jnp.zeros_like(ddt_s)
            dacs_s[...] = jnp.zeros_like(dacs_s)
            dacr_s[...] = jnp.zeros_like(dacr_s)
            Bm, Cm = b_ref[r, :], c_ref[r, :]
            Gm = _dot_nt(Cm, Bm)
            dG = jnp.zeros((C, C), F32)
            dB = jnp.zeros((C, M_D_STATE), F32)
            dC = jnp.zeros((C, M_D_STATE), F32)
            for j in range(hpg):
                pc = slice(j * M_HEADDIM, (j + 1) * M_HEADDIM)
                one = slice(j, j + 1)
                _, vjp = jax.vjp(_ssd_head, hs_ref[c, j], xs_ref[r, pc], Gm, Bm, Cm, dt_s[:, one], acs_s[:, one], acr_s[one, :],
                                 dk_ref[0:1, one])
                dh0, dxs, dGj, dBj, dCj, ddt, dacs, dacr, ddk = vjp((dy_ref[r, pc], dh_s[j]))
                dh_s[j] = dh0
                dxs_ref[r, pc] = dxs
                dG, dB, dC = dG + dGj, dB + dBj, dC + dCj
                ddt_s[:, one] = ddt
                dacs_s[:, one] = dacs
                dacr_s[one, :] = dacr
                gdk_ref[0:1, one] += ddk
            db_ref[r, :] = dB + _dot_tn(dG, Cm)
            dc_ref[r, :] = dC + _dot_nn(dG, Bm)
            ddtc, ddtr, gbc, gbr, gac, gar = pre_vjp((ddt_s[...], dacs_s[...], dacr_s[...]))
            ddtc_ref[r, :] = ddtc
            ddtr_ref[:, r] = ddtr
            gbc_ref[...] += gbc
            gbr_ref[...] += gbr
            gac_ref[...] += gac
            gar_ref[...] += gar
            return carry

        lax.fori_loop(0, NC, step, 0)

    GN = G * M_D_STATE
    bc_cols = lambda off: pl.BlockSpec((L, M_D_STATE), lambda g, b: (b, g + off))
    outs = _call(
        body, name=name,
        out_shape=(jax.ShapeDtypeStruct((T, d_inner), F32), jax.ShapeDtypeStruct((T, GN), F32), jax.ShapeDtypeStruct((T, GN), F32),
                   jax.ShapeDtypeStruct(dtc.shape, F32), jax.ShapeDtypeStruct(dtr.shape, F32),
                   jax.ShapeDtypeStruct((G, 1, LANES), F32), jax.ShapeDtypeStruct((G, 8, 1), F32),
                   jax.ShapeDtypeStruct((G, 1, LANES), F32), jax.ShapeDtypeStruct((G, 8, 1), F32),
                   jax.ShapeDtypeStruct((G, 1, LANES), F32)),
        grid=(G, Bl),
        in_specs=[cols(gw, 0), cols(M_D_STATE, nb), cols(M_D_STATE, nb + G), dtc_spec, dtr_spec,
                  per_group((1, LANES)), per_group((8, 1)), per_group((1, LANES)), per_group((8, 1)), per_group((1, LANES)),
                  pl.BlockSpec((None, None, NC, hpg, M_HEADDIM, M_D_STATE), lambda g, b: (b, g, 0, 0, 0, 0)), cols(gw, 0)],
        out_specs=(cols(gw, 0), bc_cols(0), bc_cols(0), dtc_spec, dtr_spec,
                   per_group((1, LANES)), per_group((8, 1)), per_group((1, LANES)), per_group((8, 1)), per_group((1, LANES))),
        scratch_shapes=[pltpu.VMEM((C, LANES), F32), pltpu.VMEM((C, LANES), F32), pltpu.VMEM((8, C), F32),
                        pltpu.VMEM((C, LANES), F32), pltpu.VMEM((C, LANES), F32), pltpu.VMEM((8, C), F32),
                        pltpu.VMEM((hpg, M_HEADDIM, M_D_STATE), F32)],
        dims=("arbitrary", "arbitrary"), jobs=jobs,
    )(xa, xa, xa, dtc, dtr, *prm, states, dy)
    return outs


def _mgate(y, z, w):
    t = y * _silu(z)
    return t * lax.rsqrt(jnp.mean(t * t, axis=-1, keepdims=True) + NORM_EPS) * w


def _mgate_fwd(y, z, w, *, name, jobs=()):
    T, d_inner = y.shape
    gw = d_inner // M_GROUPS
    tr = _tile(T, 512, 8)

    def body(y_ref, z_ref, w_ref, o_ref):
        o_ref[...] = _mgate(y_ref[...], z_ref[...], w_ref[...]).astype(o_ref.dtype)

    blk = pl.BlockSpec((tr, gw), lambda i, g: (i, g))
    return _call(
        body, name=name, out_shape=jax.ShapeDtypeStruct((T, d_inner), BF16), grid=(T // tr, M_GROUPS),
        in_specs=[blk, blk, pl.BlockSpec((1, gw), lambda i, g: (0, g))], out_specs=blk,
        dims=("parallel", "parallel"), jobs=jobs,
    )(y, z, w.reshape(1, d_inner))


def _mgate_bwd(y, z, w, dyg, *, name):
    T, d_inner = y.shape
    gw = d_inner // M_GROUPS
    tr = _tile(T, 512, 8)

    def body(y_ref, z_ref, w_ref, g_ref, dy_ref, dz_ref, dw_ref):
        _, vjp = jax.vjp(_mgate, y_ref[...], z_ref[...], w_ref[...])
        dy, dz, dw = vjp(g_ref[...].astype(F32))
        dy_ref[...] = dy
        dz_ref[...] = dz.astype(dz_ref.dtype)

        @pl.when(pl.program_id(1) == 0)
        def _():
            dw_ref[...] = jnp.zeros_like(dw_ref)

        dw_ref[...] += dw

    blk = pl.BlockSpec((tr, gw), lambda g, i: (i, g))
    vec = pl.BlockSpec((1, gw), lambda g, i: (0, g))
    return pl.pallas_call(
        body, name=name,
        out_shape=(jax.ShapeDtypeStruct((T, d_inner), F32), jax.ShapeDtypeStruct((T, d_inner), BF16), jax.ShapeDtypeStruct((1, d_inner), F32)),
        grid=(M_GROUPS, T // tr), in_specs=[blk, blk, vec, blk], out_specs=(blk, blk, vec),
        compiler_params=_params(("parallel", "arbitrary")),
    )(y, z, w.reshape(1, d_inner), dyg)


def _adamw_math(w, g, m, v):
    m1 = ADAM_B1 * m + (1.0 - ADAM_B1) * g
    v1 = ADAM_B2 * v + (1.0 - ADAM_B2) * (g * g)
    m_hat = m1 / (1.0 - ADAM_B1 ** ADAM_STEP)
    v_hat = v1 / (1.0 - ADAM_B2 ** ADAM_STEP)
    delta = -ADAM_LR * (m_hat / (jnp.sqrt(v_hat) + ADAM_EPS) + ADAM_WD * w)
    return delta, m1, v1


def _adamw_sharded(land, w, m, v, *, name, jobs=()):
    R, n = w.shape
    tr = _tile(R, 128, 8)

    def body(l_ref, w_ref, m_ref, v_ref, g_ref, d_ref, m1_ref, v1_ref):
        g = l_ref[0].astype(F32)
        for k in range(1, N_DEV):
            g = g + l_ref[k].astype(F32)
        d, m1, v1 = _adamw_math(w_ref[...], g, m_ref[...], v_ref[...])
        g_ref[...] = g
        d_ref[...] = d
        m1_ref[...] = m1
        v1_ref[...] = v1

    blk = pl.BlockSpec((tr, n), lambda i: (i, 0))
    out = jax.ShapeDtypeStruct((R, n), F32)
    return _call(
        body, name=name, out_shape=(out, out, out, out), grid=(R // tr,),
        in_specs=[pl.BlockSpec((N_DEV, tr, n), lambda i: (0, i, 0)), blk, blk, blk], out_specs=(blk, blk, blk, blk),
        dims=("parallel",), jobs=jobs,
    )(land, w, m, v)


def _sum_devices(parts, *, name):
    _, R, n = parts.shape
    tr = _tile(R, 512, 8)

    def body(p_ref, o_ref):
        g = p_ref[0]
        for k in range(1, N_DEV):
            g = g + p_ref[k]
        o_ref[...] = g

    return pl.pallas_call(
        body, name=name, out_shape=jax.ShapeDtypeStruct((R, n), F32), grid=(R // tr,),
        in_specs=[pl.BlockSpec((N_DEV, tr, n), lambda i: (0, i, 0))], out_specs=pl.BlockSpec((tr, n), lambda i: (i, 0)),
        compiler_params=_params(("parallel",)),
    )(parts)


def _adamw_packed(g, w, m, v, *, name):
    R, n = w.shape
    tr = _tile(R, 512, 8)

    def body(g_ref, w_ref, m_ref, v_ref, d_ref, m1_ref, v1_ref):
        d, m1, v1 = _adamw_math(w_ref[...], g_ref[...], m_ref[...], v_ref[...])
        d_ref[...] = d
        m1_ref[...] = m1
        v1_ref[...] = v1

    blk = pl.BlockSpec((tr, n), lambda i: (i, 0))
    out = jax.ShapeDtypeStruct((R, n), F32)
    return pl.pallas_call(
        body, name=name, out_shape=(out, out, out), grid=(R // tr,), in_specs=[blk] * 4, out_specs=(blk, blk, blk),
        compiler_params=_params(("parallel",)),
    )(g, w, m, v)


PACK_ALIGN = 8 * LANES


def _pack(arrs):
    parts = []
    for a in arrs:
        flat = a.reshape(-1).astype(F32)
        pad = (-flat.shape[0]) % PACK_ALIGN
        parts.append(jnp.pad(flat, (0, pad)).reshape(-1, LANES))
    return jnp.concatenate(parts, axis=0)


def _unpack(packed, shapes, lead=()):
    out, r = [], 0
    nl = len(lead)
    for s in shapes:
        n = math.prod(s)
        rows = (n + PACK_ALIGN - 1) // PACK_ALIGN * (PACK_ALIGN // LANES)
        part = lax.slice_in_dim(packed, r, r + rows, axis=nl)
        out.append(part.reshape(lead + (rows * LANES,))[..., :n].reshape(lead + tuple(s)))
        r += rows
    return out


EXCHANGE_BYTES_PER_US = 80e3
CHIP_GATHER_BYTES_PER_US = 68e3
SIBLING_BYTES_PER_US = 250e3
EXCHANGE_ROW_ALIGN = 64
CARRY_FILL = 1.0
CARRY_FILL_BACKWARD = 0.75

CALL_US = {
    "hgrn_in_fwd": 155, "hgrn_scan_fwd": 770, "hgrn_out_fwd": 40, "ffn_up_fwd": 220, "ffn_gate_fwd": 114, "ffn_down_fwd": 120,
    "mamba_z_fwd": 75, "mamba_xbc_fwd": 112, "mamba_scan_fwd": 400, "mamba_out_fwd": 86, "mamba_conv_fwd": 90, "mamba_gate_fwd": 75,
    "ffn_down_bwd_x": 113, "ffn_down_bwd_w": 125, "ffn_gate_bwd": 170, "ffn_up_bwd_x": 240, "ffn_up_bwd_w": 229,
    "hgrn_out_bwd_x": 40, "hgrn_out_bwd_w": 45, "hgrn_scan_bwd": 1800, "hgrn_in_bwd_x": 172, "hgrn_in_bwd_w": 158,
    "mamba_out_bwd_x": 78, "mamba_out_bwd_w": 88, "mamba_scan_bwd": 1200, "mamba_z_bwd_x": 88, "mamba_xbc_bwd_x": 130,
    "mamba_z_bwd_w": 82, "mamba_xbc_bwd_w": 120,
    "adamw_hgrn_w_in": 65, "adamw_hgrn_w_out": 16, "adamw_m_w_in": 87, "adamw_m_w_out": 32, "adamw_f_w_up": 168,
    "adamw_f_w_down": 109,
}


class _Plan:
    def __init__(self):
        self.queue = []
        self.handoff = []
        self.fill = CARRY_FILL

    def push(self, src, mode, buf, q0=0):
        rows, n = src.shape[-2], src.shape[-1]
        blocks = len(OTHER_CHIPS) if mode == "gather_chip" else N_DEV - 1
        rate = CHIP_GATHER_BYTES_PER_US if mode == "gather_chip" else EXCHANGE_BYTES_PER_US
        self.queue.append(dict(src=src, mode=mode, buf=buf, r0=0, r1=rows, q0=q0, us_per_row=blocks * n * src.dtype.itemsize / rate))

    def _hand_on(self, it, r0, r1):
        q0, q1 = it["q0"] + r0, it["q0"] + r1
        if self.handoff and self.handoff[-1]["buf"] is it["buf"] and self.handoff[-1]["r1"] == q0:
            self.handoff[-1]["r1"] = q1
            return
        us = len(OTHER_CHIPS) * it["src"].shape[-1] * it["src"].dtype.itemsize / SIBLING_BYTES_PER_US
        self.handoff.append(dict(src=jnp.zeros((8, LANES), it["src"].dtype), mode="forward", buf=it["buf"], r0=q0, r1=q1, q0=0,
                                 us_per_row=us))

    def take(self, name):
        budget, jobs, used, spawned = self.fill * CALL_US[name], [], set(), []
        for queue in (self.handoff, self.queue):
            i = 0
            while i < len(queue):
                it = queue[i]
                if id(it["buf"]) in used:
                    i += 1
                    continue
                left = it["r1"] - it["r0"]
                fit = int(budget / it["us_per_row"])
                rows = left if fit >= left else fit // EXCHANGE_ROW_ALIGN * EXCHANGE_ROW_ALIGN
                if rows <= 0:
                    break
                jobs.append(_Job(it["src"], it["mode"], it["buf"], it["r0"], it["r0"] + rows, it["q0"] + it["r0"]))
                used.add(id(it["buf"]))
                budget -= rows * it["us_per_row"]
                if it["mode"] == "gather_chip":
                    spawned.append((it, it["r0"], it["r0"] + rows))
                it["r0"] += rows
                if it["r0"] < it["r1"]:
                    break
                queue.pop(i)
        for it, r0, r1 in spawned:
            self._hand_on(it, r0, r1)
        return jobs

    def finish(self, buf, *, name):
        mine = [it for it in self.queue if it["buf"] is buf]
        self.queue = [it for it in self.queue if it["buf"] is not buf]
        for i, it in enumerate(mine):
            _exchange([_Job(it["src"], it["mode"], buf, it["r0"], it["r1"], it["q0"] + it["r0"])], name=f"{name}_{i}" if i else name)
            if it["mode"] == "gather_chip":
                self._hand_on(it, it["r0"], it["r1"])
        mine = [it for it in self.handoff if it["buf"] is buf]
        self.handoff = [it for it in self.handoff if it["buf"] is not buf]
        for i, it in enumerate(mine):
            _exchange([_Job(it["src"], "forward", buf, it["r0"], it["r1"], it["r0"])], name=f"{name}_handoff_{i}" if i else f"{name}_handoff")
        return buf.arr


def _pad_to(a, axis, size):
    pad = [(0, 0)] * a.ndim
    pad[axis] = (0, size - a.shape[axis])
    return jnp.pad(a, pad)


def _ssd_operands(dt_raw, dt_bias, a_log, dskip):
    T, heads = dt_raw.shape
    G = M_GROUPS
    hpg = heads // G
    t3 = dt_raw.reshape(T, G, hpg)
    dtc = _pad_to(t3.transpose(1, 0, 2), 2, LANES)
    dtr = _pad_to(t3.transpose(1, 2, 0), 1, 8)
    col = lambda p: _pad_to(p.reshape(G, 1, hpg), 2, LANES)
    row = lambda p: _pad_to(p.reshape(G, hpg, 1), 1, 8)
    return dtc, dtr, (col(dt_bias), row(dt_bias), col(a_log), row(a_log), col(dskip))


def _ssd_unpack(ddtc, ddtr, gbc, gbr, gac, gar, gdk, heads):
    G = M_GROUPS
    hpg = heads // G
    T = ddtc.shape[1]
    ddt = ddtc[:, :, :hpg].transpose(1, 0, 2).reshape(T, heads) + ddtr[:, :hpg, :].transpose(2, 0, 1).reshape(T, heads)
    both = lambda c, r: c[:, 0, :hpg].reshape(heads) + r[:, :hpg, 0].reshape(heads)
    return ddt, both(gbc, gbr), both(gac, gar), gdk[:, 0, :hpg].reshape(heads)


class _LazyWeights:
    def __init__(self, name, shape, fetch):
        self.name, self.shape, self.fetch, self.got = name, shape, fetch, {}

    def __getitem__(self, layer):
        if layer not in self.got:
            self.got[layer] = self.fetch(self.name, layer)
        return self.got[layer]


class _GradSink:
    def __init__(self, name, layers, push):
        self.name, self.shape, self.push = name, (layers,), push

    def __setitem__(self, layer, g):
        self.push(self.name, layer, g)


def _local_step(x, target, W, plan, push):
    Bl, L, D = x.shape
    T = Bl * L
    depth = W["mix_norm"].shape[0]
    d_inner = W["m_w_out"].shape[1]
    heads = d_inner // M_HEADDIM
    conv_dim = W["m_conv_w"].shape[2]

    def mm(a, b, mode, out_dtype, *, name, add=None, out_stack=None):
        return _matmul(a, b, mode, out_dtype, name=name, add=add, out_stack=out_stack,
                       jobs=plan.take(name) if name in CALL_US else ())

    lb = _lb_fwd(W["hgrn_lb_logits"], name="hgrn_lower_bounds")
    h = x.reshape(T, D)
    saved = []
    for i in range(depth):
        j = i // 2
        s = {"h": h}
        u = _rms_fwd(h, W["mix_norm"][i], name="mix_norm_fwd")
        s["u"] = u
        if i % 2 == 0:
            P = mm(u, W["hgrn_w_in"][j], "nn", F32, name="hgrn_in_fwd")
            og, st = _hgrn_fwd(P, lb[j], W["hgrn_gnorm"][j], Bl, L, name="hgrn_scan_fwd", jobs=plan.take("hgrn_scan_fwd"))
            h = mm(og, W["hgrn_w_out"][j], "nn", F32, add=h, name="hgrn_out_fwd")
            s.update(P=P, og=og, st=st)
        else:
            w_in = W["m_w_in"][j]
            z = mm(u, w_in[:, :d_inner], "nn", F32, name="mamba_z_fwd")
            xbc = mm(u, w_in[:, d_inner:d_inner + conv_dim], "nn", F32, name="mamba_xbc_fwd")
            dt_raw = mm(u, w_in[:, d_inner + conv_dim:], "nn", F32, name="mamba_dt_fwd")
            xa = _mconv_fwd(xbc, W["m_conv_w"][j], W["m_conv_b"][j], Bl, L, name="mamba_conv_fwd", jobs=plan.take("mamba_conv_fwd"))
            dtc, dtr, prm = _ssd_operands(dt_raw, W["m_dt_bias"][j], W["m_A_log"][j], W["m_D"][j])
            y, hs = _ssd_fwd(xa, dtc, dtr, prm, Bl, L, d_inner, name="mamba_scan_fwd", jobs=plan.take("mamba_scan_fwd"))
            yg = _mgate_fwd(y, z, W["m_norm"][j], name="mamba_gate_fwd", jobs=plan.take("mamba_gate_fwd"))
            h = mm(yg, W["m_w_out"][j], "nn", F32, add=h, name="mamba_out_fwd")
            s.update(z=z, xbc=xbc, xa=xa, dtc=dtc, dtr=dtr, prm=prm, y=y, hs=hs, yg=yg)
        s["h2"] = h
        u2 = _rms_fwd(h, W["ffn_norm"][i], name="ffn_norm_fwd")
        a = mm(u2, W["f_w_up"][i], "nn", BF16, name="ffn_up_fwd")
        yf, hf = _ffn_gate_fwd(a, W["f_conv_w"][i], W["f_conv_b"][i], Bl, L, name="ffn_gate_fwd", jobs=plan.take("ffn_gate_fwd"))
        h = mm(yf, W["f_w_down"][i], "nn", F32, add=h, name="ffn_down_fwd")
        s.update(u2=u2, a=a, yf=yf, hf=hf)
        saved.append(s)

    loss, dh, d_final = _loss_head(h, W["final_norm"], target.reshape(T, D), name="loss_head")
    plan.fill = CARRY_FILL_BACKWARD

    g = {k: (_GradSink(k, W[k].shape[0], push) if isinstance(W[k], _LazyWeights) else [None] * W[k].shape[0])
         for k in W if k != "final_norm"}
    g["final_norm"] = d_final.reshape(D)
    dlb = [None] * W["hgrn_lb_logits"].shape[0]
    for i in reversed(range(depth)):
        j = i // 2
        s = saved[i]
        g["f_w_down"][i] = mm(s["yf"], dh, "tn", BF16, name="ffn_down_bwd_w")
        dyf = mm(dh, W["f_w_down"][i], "nt", BF16, name="ffn_down_bwd_x")
        da, pf = _ffn_gate_bwd(s["a"], s["hf"], W["f_conv_w"][i], dyf, Bl, L, name="ffn_gate_bwd",
                               jobs=plan.take("ffn_gate_bwd"))
        g["f_w_up"][i] = mm(s["u2"], da, "tn", BF16, out_stack=N_DEV, name="ffn_up_bwd_w")
        du2 = mm(da, W["f_w_up"][i], "nt", F32, name="ffn_up_bwd_x")
        g["f_conv_w"][i], g["f_conv_b"][i] = pf[:FFN_CONV], pf[FFN_CONV]
        dh, dn = _rms_bwd(s["h2"], W["ffn_norm"][i], du2, dh, name="ffn_norm_bwd")
        g["ffn_norm"][i] = dn.reshape(D)
        if i == 0:
            plan.fill = CARRY_FILL
        if i % 2 == 0:
            g["hgrn_w_out"][j] = mm(s["og"], dh, "tn", BF16, name="hgrn_out_bwd_w")
            dog = mm(dh, W["hgrn_w_out"][j], "nt", BF16, name="hgrn_out_bwd_x")
            dP4, dlb_j, dgn = _hgrn_bwd(s["P"], lb[j], W["hgrn_gnorm"][j], s["st"], dog, Bl, L, name="hgrn_scan_bwd",
                                        jobs=plan.take("hgrn_scan_bwd"))
            g["hgrn_w_in"][j] = mm(s["u"], dP4, "tn", BF16, out_stack=N_DEV, name="hgrn_in_bwd_w")
            du = mm(dP4, W["hgrn_w_in"][j], "nt", F32, name="hgrn_in_bwd_x")
            dlb[j] = dlb_j
            g["hgrn_gnorm"][j] = dgn.reshape(HGRN_HEAD)
        else:
            w_in = W["m_w_in"][j]
            g["m_w_out"][j] = mm(s["yg"], dh, "tn", BF16, name="mamba_out_bwd_w")
            dyg = mm(dh, W["m_w_out"][j], "nt", BF16, name="mamba_out_bwd_x")
            dy, dz, dnw = _mgate_bwd(s["y"], s["z"], W["m_norm"][j], dyg, name="mamba_gate_bwd")
            dxs, dB, dC, ddtc, ddtr, gbc, gbr, gac, gar, gdk = _ssd_bwd(
                s["xa"], s["dtc"], s["dtr"], s["prm"], s["hs"], dy, Bl, L, d_inner, name="mamba_scan_bwd",
                jobs=plan.take("mamba_scan_bwd"))
            ddt, g["m_dt_bias"][j], g["m_A_log"][j], g["m_D"][j] = _ssd_unpack(ddtc, ddtr, gbc, gbr, gac, gar, gdk, heads)
            dxa = jnp.concatenate([dxs, dB, dC], axis=1)
            dxbc, pm = _mconv_bwd(s["xbc"], W["m_conv_w"][j], W["m_conv_b"][j], dxa, Bl, L, name="mamba_conv_bwd")
            g["m_conv_w"][j], g["m_conv_b"][j], g["m_norm"][j] = pm[:M_CONV], pm[M_CONV], dnw.reshape(d_inner)
            ddt = ddt.astype(BF16)
            g["m_w_in"][j] = jnp.concatenate([
                mm(s["u"], dz, "tn", BF16, name="mamba_z_bwd_w"), mm(s["u"], dxbc, "tn", BF16, name="mamba_xbc_bwd_w"),
                mm(s["u"], ddt, "tn", BF16, name="mamba_dt_bwd_w")], axis=1)
            du = mm(dz, w_in[:, :d_inner], "nt", F32, name="mamba_z_bwd_x")
            du = mm(dxbc, w_in[:, d_inner:d_inner + conv_dim], "nt", F32, add=du, name="mamba_xbc_bwd_x")
            du = mm(ddt, w_in[:, d_inner + conv_dim:], "nt", F32, add=du, name="mamba_dt_bwd_x")
        dh, dn = _rms_bwd(s["h"], W["mix_norm"][i], du, dh, name="mix_norm_bwd")
        g["mix_norm"][i] = dn.reshape(D)
    g["hgrn_lb_logits"] = _lb_bwd(W["hgrn_lb_logits"], jnp.concatenate(dlb, axis=0), name="hgrn_lower_bounds_bwd")
    grads = {k: (jnp.stack(v) if isinstance(v, list) else v) for k, v in g.items() if not isinstance(v, _GradSink)}
    return loss, dh.reshape(Bl, L, D), grads


WEIGHTS = ("mix_norm", "ffn_norm", "final_norm", "hgrn_w_in", "hgrn_lb_logits", "hgrn_gnorm", "hgrn_w_out", "m_w_in",
           "m_conv_w", "m_conv_b", "m_dt_bias", "m_A_log", "m_D", "m_norm", "m_w_out", "f_w_up", "f_conv_w", "f_conv_b",
           "f_w_down")
COL_SHARDED = ("hgrn_w_in", "m_w_in", "f_w_up")
STACKED = ("hgrn_w_in", "f_w_up")
ROW_SHARDED = ("hgrn_w_out", "m_w_out", "f_w_down")
SMALL_SHARDED = ("m_conv_w", "m_conv_b", "m_norm", "f_conv_w")
SMALL = tuple(k for k in WEIGHTS if k not in COL_SHARDED + ROW_SHARDED)


def _join_last(g):
    nd = g.ndim
    t = g.transpose(tuple(range(1, nd - 1)) + (0, nd - 1))
    return t.reshape(t.shape[:-2] + (N_DEV * g.shape[-1],))


def kernel(x, mix_norm, ffn_norm, final_norm, hgrn_w_in, hgrn_lb_logits, hgrn_gnorm, hgrn_w_out, m_w_in, m_conv_w, m_conv_b, m_dt_bias, m_A_log, m_D, m_norm, m_w_out, f_w_up, f_conv_w, f_conv_b, f_w_down, loss_target, m_mix_norm, m_ffn_norm, m_final_norm, m_hgrn_w_in, m_hgrn_lb_logits, m_hgrn_gnorm, m_hgrn_w_out, m_m_w_in, m_m_conv_w, m_m_conv_b, m_m_dt_bias, m_m_A_log, m_m_D, m_m_norm, m_m_w_out, m_f_w_up, m_f_conv_w, m_f_conv_b, m_f_w_down, v_mix_norm, v_ffn_norm, v_final_norm, v_hgrn_w_in, v_hgrn_lb_logits, v_hgrn_gnorm, v_hgrn_w_out, v_m_w_in, v_m_conv_w, v_m_conv_b, v_m_dt_bias, v_m_A_log, v_m_D, v_m_norm, v_m_w_out, v_f_w_up, v_f_conv_w, v_f_conv_b, v_f_w_down):
    w = dict(zip(WEIGHTS, (mix_norm, ffn_norm, final_norm, hgrn_w_in, hgrn_lb_logits, hgrn_gnorm, hgrn_w_out, m_w_in, m_conv_w, m_conv_b, m_dt_bias, m_A_log, m_D, m_norm, m_w_out, f_w_up, f_conv_w, f_conv_b, f_w_down)))
    m = dict(zip(WEIGHTS, (m_mix_norm, m_ffn_norm, m_final_norm, m_hgrn_w_in, m_hgrn_lb_logits, m_hgrn_gnorm, m_hgrn_w_out, m_m_w_in, m_m_conv_w, m_m_conv_b, m_m_dt_bias, m_m_A_log, m_m_D, m_m_norm, m_m_w_out, m_f_w_up, m_f_conv_w, m_f_conv_b, m_f_w_down)))
    v = dict(zip(WEIGHTS, (v_mix_norm, v_ffn_norm, v_final_norm, v_hgrn_w_in, v_hgrn_lb_logits, v_hgrn_gnorm, v_hgrn_w_out, v_m_w_in, v_m_conv_w, v_m_conv_b, v_m_dt_bias, v_m_A_log, v_m_D, v_m_norm, v_m_w_out, v_f_w_up, v_f_conv_w, v_f_conv_b, v_f_w_down)))
    me = 4 * lax.axis_index("x") + 2 * lax.axis_index("y") + lax.axis_index("c")

    def gather_now(a, *, name):
        buf = _Buf(a.shape[0], a.shape[1], a.dtype)
        _exchange([_Job(a, "gather", buf)], name=name)
        return buf.arr

    plan = _Plan()
    depth = mix_norm.shape[0]
    use_order = []
    for i in range(depth):
        use_order += [("hgrn_w_in", i // 2), ("hgrn_w_out", i // 2)] if i % 2 == 0 else [("m_w_in", i // 2), ("m_w_out", i // 2)]
        use_order += [("f_w_up", i), ("f_w_down", i)]
    gathers = {}
    for k, layer in use_order:
        shard = w[k][layer].astype(BF16)
        gathers[k, layer] = _Buf(shard.shape[0], shard.shape[1], BF16)
        plan.push(shard, "gather_chip", gathers[k, layer])

    def fetch(k, layer):
        g = plan.finish(gathers[k, layer], name=f"gather_{k}")
        if k in STACKED:
            return g
        if k in COL_SHARDED:
            return g.transpose(1, 0, 2).reshape(g.shape[1], N_DEV * g.shape[2])
        return g.reshape(N_DEV * g.shape[1], g.shape[2])

    full = {k: w[k] for k in SMALL if k not in SMALL_SHARDED}
    for k in COL_SHARDED:
        full[k] = _LazyWeights(k, (w[k].shape[0], w[k].shape[1], N_DEV * w[k].shape[2]), fetch)
    for k in ROW_SHARDED:
        full[k] = _LazyWeights(k, (w[k].shape[0], N_DEV * w[k].shape[1], w[k].shape[2]), fetch)
    shard_shapes = [w[k].shape for k in SMALL_SHARDED]
    gathered = gather_now(_pack([w[k] for k in SMALL_SHARDED]), name="gather_small_params")
    for k, g in zip(SMALL_SHARDED, _unpack(gathered, shard_shapes, lead=(N_DEV,))):
        full[k] = _join_last(g)

    lands = {k: _Buf(math.prod(w[k].shape[:-1]), w[k].shape[-1], BF16) for k in COL_SHARDED + ROW_SHARDED}

    def push(k, layer, g):
        if k in STACKED:
            parts = g
        elif k in COL_SHARDED:
            parts = g.reshape(g.shape[0], N_DEV, g.shape[1] // N_DEV).transpose(1, 0, 2)
        else:
            parts = g.reshape(N_DEV, g.shape[0] // N_DEV, g.shape[1])
        plan.push(parts, "scatter", lands[k], q0=layer * parts.shape[1])

    loss, grad_x, grads = _local_step(x, loss_target, full, plan, push)

    small_full_shapes = [(1, 1)] + [grads[k].shape for k in SMALL]
    packed = _pack([loss] + [grads[k] for k in SMALL])
    small_parts = _Buf(packed.shape[0], packed.shape[1], F32)
    plan.push(packed, "gather_chip", small_parts)
    out = {}
    for k in ("m_w_out", "m_w_in", "f_w_down", "f_w_up", "hgrn_w_out", "hgrn_w_in"):
        land = plan.finish(lands[k], name="scatter_" + k)
        rows, n = math.prod(w[k].shape[:-1]), w[k].shape[-1]
        res = _adamw_sharded(land, w[k].reshape(rows, n), m[k].reshape(rows, n), v[k].reshape(rows, n),
                             name="adamw_" + k, jobs=plan.take("adamw_" + k))
        out[k] = tuple(r.reshape(w[k].shape) for r in res)

    summed = _sum_devices(plan.finish(small_parts, name="gather_small_grads"), name="sum_small_grads")
    parts = _unpack(summed, small_full_shapes)
    loss_all, small_g = parts[0], dict(zip(SMALL, parts[1:]))
    for k in SMALL_SHARDED:
        n = w[k].shape[-1]
        small_g[k] = lax.dynamic_slice_in_dim(small_g[k], me * n, n, axis=small_g[k].ndim - 1)
    shapes = [w[k].shape for k in SMALL]
    res = _adamw_packed(_pack([small_g[k] for k in SMALL]), _pack([w[k] for k in SMALL]), _pack([m[k] for k in SMALL]),
                        _pack([v[k] for k in SMALL]), name="adamw_small")
    res = [_unpack(r, shapes) for r in res]
    for i, k in enumerate(SMALL):
        out[k] = (small_g[k], res[0][i], res[1][i], res[2][i])

    return (loss_all.reshape(()), grad_x, *[out[k][0] for k in WEIGHTS], *[out[k][1] for k in WEIGHTS],
            *[out[k][2] for k in WEIGHTS], *[out[k][3] for k in WEIGHTS])
```

```python
import functools
import math

import jax
import jax.numpy as jnp
from jax import lax
from jax.experimental import pallas as pl
from jax.experimental.pallas import tpu as pltpu

F32 = jnp.float32
BF16 = jnp.bfloat16
HIGHEST = lax.Precision.HIGHEST

NORM_EPS = 1e-5
ADAM_LR, ADAM_B1, ADAM_B2, ADAM_EPS, ADAM_WD, ADAM_STEP = 0.001, 0.9, 0.999, 1e-08, 0.01, 10

N_DEV = 8
LANES = 128
V7X_VMEM_BYTES = 64 * 1024 * 1024
VMEM_LIMIT = V7X_VMEM_BYTES * 3 // 4

HGRN_HEAD = 128
HGRN_CHUNK = 128
HGRN_SUB = 32
M_HEADDIM = 64
M_GROUPS = 8
M_D_STATE = 128
M_CONV = 4
M_CHUNK = 256
FFN_CONV = 3
CONV_COLS = 256
HGRN_PAR = 2


def _params(dims=None, **kw):
    return pltpu.CompilerParams(dimension_semantics=dims, vmem_limit_bytes=VMEM_LIMIT, **kw)


def _tile(dim, target, align=LANES, stretch=False):
    if stretch:
        t = _tile(dim, target, align)
        if 4 * t >= 3 * target:
            return t
        for up in range((target // align + 1) * align, 3 * target // 2 + 1, align):
            if dim % up == 0:
                return up
        return t
    t = (min(target, dim) // align) * align
    while t >= align:
        if dim % t == 0:
            return t
        if 2 * t < target and dim <= 2 * target:
            return dim
        t -= align
    return dim


class _Buf:
    def __init__(self, rows, n, dtype):
        self.arr = lax.empty((N_DEV, rows, n), dtype)


SIBLING = 1
OTHER_CHIPS = (2, 4, 6)


class _Job:
    def __init__(self, src, mode, buf, r0=0, r1=None, q0=0):
        rows = src.shape[-2]
        self.src, self.mode, self.buf, self.r0, self.r1, self.q0 = src, mode, buf, r0, rows if r1 is None else r1, q0


def _job_copies(job, src_ref, out_ref, send_sems, recv_sems, local_sem):
    x, y, c = lax.axis_index("x"), lax.axis_index("y"), lax.axis_index("c")
    me = 4 * x + 2 * y + c
    n = job.r1 - job.r0
    rows, land = pl.ds(job.r0, n), pl.ds(job.q0, n)

    def peer_of(k):
        kx, ky, kc = (k >> 2) & 1, (k >> 1) & 1, k & 1
        px, py, pc = (1 - x if kx else x), (1 - y if ky else y), (1 - c if kc else c)
        return 4 * px + 2 * py + pc, (px, py, pc)

    pairs = []
    if job.mode == "forward":
        sib, sib_id = peer_of(SIBLING)
        for i, k in enumerate(OTHER_CHIPS):
            mine, _ = peer_of(k)
            theirs, _ = peer_of(k ^ SIBLING)
            sems = dict(send_sem=send_sems.at[i], recv_sem=recv_sems.at[i], device_id=sib_id, device_id_type=pl.DeviceIdType.MESH)
            send = pltpu.make_async_remote_copy(src_ref=out_ref.at[mine, land], dst_ref=out_ref.at[mine, land], **sems)
            landing = pltpu.make_async_remote_copy(src_ref=out_ref.at[mine, land], dst_ref=out_ref.at[theirs, land], **sems)
            pairs.append((send, landing))
        return None, pairs
    part = (lambda d: src_ref.at[d, rows]) if job.mode == "scatter" else (lambda d: src_ref.at[rows])
    local = pltpu.make_async_copy(part(me), out_ref.at[me, land], local_sem)
    for k in ((SIBLING,) + OTHER_CHIPS if job.mode == "gather_chip" else range(1, N_DEV)):
        peer, peer_id = peer_of(k)
        sems = dict(send_sem=send_sems.at[k - 1], recv_sem=recv_sems.at[k - 1], device_id=peer_id,
                    device_id_type=pl.DeviceIdType.MESH)
        send = pltpu.make_async_remote_copy(src_ref=part(peer), dst_ref=out_ref.at[me, land], **sems)
        landing = pltpu.make_async_remote_copy(src_ref=part(me), dst_ref=out_ref.at[peer, land], **sems)
        pairs.append((send, landing))
    return local, pairs


def _jobs_start(jobs, src_refs, out_refs, send_sems, recv_sems, local_sems):
    for i, job in enumerate(jobs):
        local, pairs = _job_copies(job, src_refs[i], out_refs[i], send_sems.at[i], recv_sems.at[i], local_sems.at[i])
        if local is not None:
            local.start()
        for send, _ in pairs:
            send.start()


def _jobs_wait(jobs, src_refs, out_refs, send_sems, recv_sems, local_sems):
    for i, job in enumerate(jobs):
        local, pairs = _job_copies(job, src_refs[i], out_refs[i], send_sems.at[i], recv_sems.at[i], local_sems.at[i])
        for _, landing in pairs:
            landing.wait_recv()
        for send, _ in pairs:
            send.wait_send()
        if local is not None:
            local.wait()


def _job_scratch(n):
    return [pltpu.SemaphoreType.DMA((n, N_DEV - 1)), pltpu.SemaphoreType.DMA((n, N_DEV - 1)), pltpu.SemaphoreType.DMA((n,))]


def _buf_shapes(jobs):
    assert len({id(j.buf) for j in jobs}) == len(jobs), "one job per buffer and call"
    return tuple(jax.ShapeDtypeStruct(j.buf.arr.shape, j.buf.arr.dtype) for j in jobs)


def _exchange(jobs, *, name):
    n = len(jobs)

    def body(*refs):
        srcs, dsts, sems = refs[:n], refs[2 * n:3 * n], refs[3 * n:]
        _jobs_start(jobs, srcs, dsts, *sems)
        _jobs_wait(jobs, srcs, dsts, *sems)

    hbm = pl.BlockSpec(memory_space=pl.ANY)
    outs = pl.pallas_call(
        body, name=name, out_shape=_buf_shapes(jobs), in_specs=[hbm] * (2 * n), out_specs=(hbm,) * n,
        input_output_aliases={n + i: i for i in range(n)},
        scratch_shapes=_job_scratch(n), compiler_params=pltpu.CompilerParams(has_side_effects=True),
    )(*[j.src for j in jobs], *[j.buf.arr for j in jobs])
    for j, o in zip(jobs, outs):
        j.buf.arr = o


def _call(body, *, name, out_shape, grid, in_specs, out_specs, scratch_shapes=(), dims, jobs=()):
    single = not isinstance(out_shape, (tuple, list))
    out_shape = (out_shape,) if single else tuple(out_shape)
    out_specs = (out_specs,) if single else tuple(out_specs)
    n_in, n_out, n_scr, nj = len(in_specs), len(out_shape), len(scratch_shapes), len(jobs)
    if not jobs:
        plain = pl.pallas_call(body, name=name, out_shape=out_shape, grid=grid, in_specs=list(in_specs), out_specs=out_specs,
                               scratch_shapes=list(scratch_shapes), compiler_params=_params(dims))

        def run_plain(*args):
            r = plain(*args)
            return r[0] if single else tuple(r)

        return run_plain

    def carrying(*refs):
        ins, srcs = refs[:n_in], refs[n_in:n_in + nj]
        o0 = n_in + 2 * nj
        outs, dsts = refs[o0:o0 + n_out], refs[o0 + n_out:o0 + n_out + nj]
        scr, sems = refs[o0 + n_out + nj:o0 + n_out + nj + n_scr], refs[o0 + n_out + nj + n_scr:]
        ids = [pl.program_id(d) for d in range(len(grid))]
        first = functools.reduce(jnp.logical_and, [i == 0 for i in ids])
        last = functools.reduce(jnp.logical_and, [i == g - 1 for i, g in zip(ids, grid)])

        @pl.when(first)
        def _():
            _jobs_start(jobs, srcs, dsts, *sems)

        body(*ins, *outs, *scr)

        @pl.when(last)
        def _():
            _jobs_wait(jobs, srcs, dsts, *sems)

    hbm = pl.BlockSpec(memory_space=pl.ANY)
    call = pl.pallas_call(
        carrying, name=name, out_shape=out_shape + _buf_shapes(jobs), grid=grid,
        in_specs=list(in_specs) + [hbm] * (2 * nj), out_specs=out_specs + (hbm,) * nj,
        input_output_aliases={n_in + nj + i: n_out + i for i in range(nj)},
        scratch_shapes=list(scratch_shapes) + _job_scratch(nj),
        compiler_params=_params(("arbitrary",) * len(grid), has_side_effects=True))

    def run(*args):
        r = call(*args, *[j.src for j in jobs], *[j.buf.arr for j in jobs])
        for j, o in zip(jobs, r[n_out:]):
            j.buf.arr = o
        return r[0] if single else tuple(r[:n_out])

    return run


def _dg(a, b, ca, cb):
    return lax.dot_general(a.astype(BF16), b.astype(BF16), (((ca,), (cb,)), ((), ())), preferred_element_type=F32)


@jax.custom_vjp
def _dot_nn(a, b):
    return _dg(a, b, 1, 0)


def _dot_nn_f(a, b):
    return _dg(a, b, 1, 0), (a, b)


def _dot_nn_b(res, g):
    a, b = res
    return _dg(g, b, 1, 1), _dg(a, g, 0, 0)


_dot_nn.defvjp(_dot_nn_f, _dot_nn_b)


@jax.custom_vjp
def _dot_nt(a, b):
    return _dg(a, b, 1, 1)


def _dot_nt_f(a, b):
    return _dg(a, b, 1, 1), (a, b)


def _dot_nt_b(res, g):
    a, b = res
    return _dg(g, b, 1, 0), _dg(g, a, 0, 0)


_dot_nt.defvjp(_dot_nt_f, _dot_nt_b)


@jax.custom_vjp
def _dot_tn(a, b):
    return _dg(a, b, 0, 0)


def _dot_tn_f(a, b):
    return _dg(a, b, 0, 0), (a, b)


def _dot_tn_b(res, g):
    a, b = res
    return _dg(b, g, 1, 1), _dg(a, g, 1, 0)


_dot_tn.defvjp(_dot_tn_f, _dot_tn_b)


def _tri(n):
    return (lax.broadcasted_iota(jnp.int32, (n, n), 0) >= lax.broadcasted_iota(jnp.int32, (n, n), 1)).astype(F32)


def _silu(x):
    return x * jax.nn.sigmoid(x)


def _matmul(a, b, mode, out_dtype, *, name, add=None, out_stack=None, tm=1024, tn=1024, tk=2048, jobs=()):
    a_n = a.shape[2] if a.ndim == 3 else None
    b_n = b.shape[2] if b.ndim == 3 else None
    a_shape = (a.shape[1], a.shape[0] * a.shape[2]) if a_n else a.shape
    b_shape = (b.shape[1], b.shape[0] * b.shape[2]) if b_n else b.shape
    if mode == "nn":
        (M, K), (K2, N) = a_shape, b_shape
    elif mode == "nt":
        (M, K), (N, K2) = a_shape, b_shape
    else:
        (K, M), (K2, N) = a_shape, b_shape
    assert K == K2, (a.shape, b.shape, mode)
    n_lim = {"m": M, "n": N, "k": K}
    for dim, n in (("m" if mode == "tn" else "k", a_n), ("k" if mode == "nt" else "n", b_n), ("n", N // out_stack if out_stack else None)):
        if n:
            n_lim[dim] = math.gcd(n_lim[dim], n)
    tm, tn = _tile(n_lim["m"], tm, 8 if M % LANES else LANES, stretch=True), _tile(n_lim["n"], tn, stretch=True)
    tk = _tile(n_lim["k"], tk)
    nk = K // tk
    ca, cb = {"nn": (1, 0), "nt": (1, 1), "tn": (0, 0)}[mode]

    def body(*refs):
        a_ref, b_ref = refs[:2]
        add_ref = refs[2] if add is not None else None
        o_ref = refs[2 + (add is not None)]

        def finish(r):
            if add is not None:
                r = r + add_ref[...]
            o_ref[...] = r.astype(o_ref.dtype)

        if nk == 1:
            finish(_dg(a_ref[...], b_ref[...], ca, cb))
            return
        acc_ref = refs[-1]
        k = pl.program_id(2)

        @pl.when(k == 0)
        def _():
            acc_ref[...] = _dg(a_ref[...], b_ref[...], ca, cb)

        @pl.when((k > 0) & (k < nk - 1))
        def _():
            acc_ref[...] += _dg(a_ref[...], b_ref[...], ca, cb)

        @pl.when(k == nk - 1)
        def _():
            finish(acc_ref[...] + _dg(a_ref[...], b_ref[...], ca, cb))

    def spec(rows, cols, t_rows, t_cols, n):
        if not n:
            return pl.BlockSpec((t_rows, t_cols), lambda i, j, k: (rows(i, j, k), cols(i, j, k)))
        per = n // t_cols
        return pl.BlockSpec((None, t_rows, t_cols), lambda i, j, k: (cols(i, j, k) // per, rows(i, j, k), cols(i, j, k) % per))

    gi, gj, gk = (lambda i, j, k: i), (lambda i, j, k: j), (lambda i, j, k: k)
    a_spec = spec(gk, gi, tk, tm, a_n) if mode == "tn" else spec(gi, gk, tm, tk, a_n)
    b_spec = spec(gj, gk, tn, tk, b_n) if mode == "nt" else spec(gk, gj, tk, tn, b_n)
    o_spec = spec(gi, gj, tm, tn, N // out_stack if out_stack else None)
    out_shape = (out_stack, M, N // out_stack) if out_stack else (M, N)
    in_specs, args = [a_spec, b_spec], [a, b]
    if add is not None:
        assert not out_stack
        in_specs.append(o_spec)
        args.append(add)
    return _call(
        body, name=name, out_shape=jax.ShapeDtypeStruct(out_shape, out_dtype), grid=(M // tm, N // tn, nk),
        in_specs=in_specs, out_specs=o_spec, scratch_shapes=[pltpu.VMEM((tm, tn), F32)] if nk > 1 else [],
        dims=("parallel", "parallel", "arbitrary"), jobs=jobs,
    )(*args)


def _rms_fwd(h, w, *, name):
    T, D = h.shape
    tr = _tile(T, 256, 8)

    def body(h_ref, w_ref, u_ref):
        x = h_ref[...]
        u_ref[...] = (x * lax.rsqrt(jnp.mean(x * x, axis=-1, keepdims=True) + NORM_EPS) * w_ref[...]).astype(u_ref.dtype)

    return pl.pallas_call(
        body, name=name, out_shape=jax.ShapeDtypeStruct((T, D), BF16), grid=(T // tr,),
        in_specs=[pl.BlockSpec((tr, D), lambda i: (i, 0)), pl.BlockSpec((1, D), lambda i: (0, 0))],
        out_specs=pl.BlockSpec((tr, D), lambda i: (i, 0)), compiler_params=_params(("parallel",)),
    )(h, w.reshape(1, D))


def _rms_bwd(h, w, du, dh_in, *, name):
    T, D = h.shape
    tr = _tile(T, 256, 8)

    def body(h_ref, w_ref, du_ref, dhin_ref, dh_ref, dw_ref):
        x = h_ref[...]
        g = du_ref[...].astype(F32)
        rstd = lax.rsqrt(jnp.mean(x * x, axis=-1, keepdims=True) + NORM_EPS)
        xhat = x * rstd
        gx = g * w_ref[...]
        dh_ref[...] = dhin_ref[...] + rstd * (gx - xhat * jnp.mean(gx * xhat, axis=-1, keepdims=True))

        @pl.when(pl.program_id(0) == 0)
        def _():
            dw_ref[...] = jnp.zeros_like(dw_ref)

        dw_ref[...] += jnp.sum(g * xhat, axis=0, keepdims=True)

    row = pl.BlockSpec((tr, D), lambda i: (i, 0))
    vec = pl.BlockSpec((1, D), lambda i: (0, 0))
    return pl.pallas_call(
        body, name=name, out_shape=(jax.ShapeDtypeStruct((T, D), F32), jax.ShapeDtypeStruct((1, D), F32)), grid=(T // tr,),
        in_specs=[row, vec, row, row], out_specs=(row, vec), compiler_params=_params(("arbitrary",)),
    )(h, w.reshape(1, D), du, dh_in)


def _loss_head(h, w, target, *, name):
    T, D = h.shape
    tr = _tile(T, 256, 8)

    def body(h_ref, w_ref, t_ref, loss_ref, dh_ref, dw_ref):
        x = h_ref[...]
        rstd = lax.rsqrt(jnp.mean(x * x, axis=-1, keepdims=True) + NORM_EPS)
        xhat = x * rstd
        err = xhat * w_ref[...] - t_ref[...]
        g = err * (1.0 / D)
        gx = g * w_ref[...]
        dh_ref[...] = rstd * (gx - xhat * jnp.mean(gx * xhat, axis=-1, keepdims=True))

        @pl.when(pl.program_id(0) == 0)
        def _():
            dw_ref[...] = jnp.zeros_like(dw_ref)
            loss_ref[...] = jnp.zeros_like(loss_ref)

        dw_ref[...] += jnp.sum(g * xhat, axis=0, keepdims=True)
        loss_ref[...] += (0.5 / D) * jnp.sum(jnp.sum(err * err, axis=-1, keepdims=True), axis=0, keepdims=True)

    row = pl.BlockSpec((tr, D), lambda i: (i, 0))
    vec = pl.BlockSpec((1, D), lambda i: (0, 0))
    one = pl.BlockSpec((1, 1), lambda i: (0, 0))
    return pl.pallas_call(
        body, name=name,
        out_shape=(jax.ShapeDtypeStruct((1, 1), F32), jax.ShapeDtypeStruct((T, D), F32), jax.ShapeDtypeStruct((1, D), F32)),
        grid=(T // tr,), in_specs=[row, vec, row], out_specs=(one, row, vec), compiler_params=_params(("arbitrary",)),
    )(h, w.reshape(1, D), target)


def _shift_down(x, s):
    if s == 0:
        return x
    rows = lax.broadcasted_iota(jnp.int32, x.shape, 0)
    return jnp.where(rows >= s, pltpu.roll(x, s, 0), 0.0)


def _shift_up(x, s):
    if s == 0:
        return x
    n = x.shape[0]
    rows = lax.broadcasted_iota(jnp.int32, x.shape, 0)
    return jnp.where(rows < n - s, pltpu.roll(x, n - s, 0), 0.0)


def _conv_pre(a, w_ref, b_ref, taps):
    pre = b_ref[0:1, :] + w_ref[taps - 1:taps, :] * a
    for k in range(taps - 1):
        pre = pre + w_ref[k:k + 1, :] * _shift_down(a, taps - 1 - k)
    return pre


def _conv_bwd(a, dpre, w_ref, taps):
    da = w_ref[taps - 1:taps, :] * dpre
    rows = []
    for k in range(taps - 1):
        up = _shift_up(dpre, taps - 1 - k)
        da = da + w_ref[k:k + 1, :] * up
        rows.append(jnp.sum(up * a, axis=0, keepdims=True))
    rows.append(jnp.sum(dpre * a, axis=0, keepdims=True))
    rows.append(jnp.sum(dpre, axis=0, keepdims=True))
    return da, rows


def _ffn_gate_fwd(a, cw, cb, Bl, L, *, name, jobs=()):
    T, F2 = a.shape
    F = F2 // 2
    tc = _tile(F, CONV_COLS)
    nj = F // tc

    def body(ag_ref, au_ref, wg_ref, wu_ref, bg_ref, bu_ref, y_ref, h_ref):
        hg = _conv_pre(ag_ref[...].astype(F32), wg_ref, bg_ref, FFN_CONV)
        hu = _conv_pre(au_ref[...].astype(F32), wu_ref, bu_ref, FFN_CONV)
        y_ref[...] = (_silu(hg) * hu).astype(y_ref.dtype)
        h_ref[0] = hg.astype(h_ref.dtype)
        h_ref[1] = hu.astype(h_ref.dtype)

    blk = lambda off: pl.BlockSpec((L, tc), lambda b, j: (b, j + off))
    wblk = lambda off: pl.BlockSpec((FFN_CONV, tc), lambda b, j: (0, j + off))
    bblk = lambda off: pl.BlockSpec((1, tc), lambda b, j: (0, j + off))
    return _call(
        body, name=name, out_shape=(jax.ShapeDtypeStruct((T, F), BF16), jax.ShapeDtypeStruct((2, T, F), BF16)), grid=(Bl, nj),
        in_specs=[blk(0), blk(nj), wblk(0), wblk(nj), bblk(0), bblk(nj)],
        out_specs=(blk(0), pl.BlockSpec((2, L, tc), lambda b, j: (0, b, j))),
        dims=("parallel", "parallel"), jobs=jobs,
    )(a, a, cw, cw, cb.reshape(1, F2), cb.reshape(1, F2))


def _ffn_gate_bwd(a, h, cw, dy, Bl, L, *, name, jobs=()):
    T, F2 = a.shape
    F = F2 // 2
    tc = _tile(F, CONV_COLS)
    nj = F // tc

    def body(ag_ref, au_ref, h_ref, wg_ref, wu_ref, dy_ref, da_ref, pg_ref, pu_ref):
        ag, au = ag_ref[...].astype(F32), au_ref[...].astype(F32)
        hg, hu = h_ref[0].astype(F32), h_ref[1].astype(F32)
        g = dy_ref[...].astype(F32)
        s = jax.nn.sigmoid(hg)
        dhg = g * hu * (s * (1.0 + hg * (1.0 - s)))
        dhu = g * (hg * s)
        dag, rg = _conv_bwd(ag, dhg, wg_ref, FFN_CONV)
        dau, ru = _conv_bwd(au, dhu, wu_ref, FFN_CONV)
        da_ref[0] = dag.astype(da_ref.dtype)
        da_ref[1] = dau.astype(da_ref.dtype)

        @pl.when(pl.program_id(1) == 0)
        def _():
            pg_ref[...] = jnp.zeros_like(pg_ref)
            pu_ref[...] = jnp.zeros_like(pu_ref)

        for k in range(FFN_CONV + 1):
            pg_ref[k:k + 1, :] += rg[k]
            pu_ref[k:k + 1, :] += ru[k]

    blk = lambda off: pl.BlockSpec((L, tc), lambda j, b: (b, j + off))
    wblk = lambda off: pl.BlockSpec((FFN_CONV, tc), lambda j, b: (0, j + off))
    pblk = lambda off: pl.BlockSpec((8, tc), lambda j, b: (0, j + off))
    both = pl.BlockSpec((2, L, tc), lambda j, b: (0, b, j))
    da, pg, pu = _call(
        body, name=name,
        out_shape=(jax.ShapeDtypeStruct((2, T, F), BF16), jax.ShapeDtypeStruct((8, F), F32), jax.ShapeDtypeStruct((8, F), F32)),
        grid=(nj, Bl),
        in_specs=[blk(0), blk(nj), both, wblk(0), wblk(nj), blk(0)],
        out_specs=(both, pblk(0), pblk(0)),
        dims=("parallel", "arbitrary"), jobs=jobs,
    )(a, a, h, cw, cw, dy)
    return da, jnp.concatenate([pg, pu], axis=1)


def _mconv_fwd(xbc, cw, cb, Bl, L, *, name, jobs=()):
    T, W = xbc.shape
    tc = _tile(W, CONV_COLS)

    def body(a_ref, w_ref, b_ref, y_ref):
        y_ref[...] = _silu(_conv_pre(a_ref[...], w_ref, b_ref, M_CONV))

    blk = pl.BlockSpec((L, tc), lambda b, j: (b, j))
    return _call(
        body, name=name, out_shape=jax.ShapeDtypeStruct((T, W), F32), grid=(Bl, W // tc),
        in_specs=[blk, pl.BlockSpec((M_CONV, tc), lambda b, j: (0, j)), pl.BlockSpec((1, tc), lambda b, j: (0, j))],
        out_specs=blk, dims=("parallel", "parallel"), jobs=jobs,
    )(xbc, cw, cb.reshape(1, W))


def _mconv_bwd(xbc, cw, cb, dy, Bl, L, *, name):
    T, W = xbc.shape
    tc = _tile(W, CONV_COLS)

    def body(a_ref, w_ref, b_ref, dy_ref, da_ref, p_ref):
        a = a_ref[...]
        pre = _conv_pre(a, w_ref, b_ref, M_CONV)
        s = jax.nn.sigmoid(pre)
        dpre = dy_ref[...] * (s * (1.0 + pre * (1.0 - s)))
        da, rows = _conv_bwd(a, dpre, w_ref, M_CONV)
        da_ref[...] = da.astype(da_ref.dtype)

        @pl.when(pl.program_id(1) == 0)
        def _():
            p_ref[...] = jnp.zeros_like(p_ref)

        for k in range(M_CONV + 1):
            p_ref[k:k + 1, :] += rows[k]

    blk = pl.BlockSpec((L, tc), lambda j, b: (b, j))
    return pl.pallas_call(
        body, name=name, out_shape=(jax.ShapeDtypeStruct((T, W), BF16), jax.ShapeDtypeStruct((8, W), F32)),
        grid=(W // tc, Bl),
        in_specs=[blk, pl.BlockSpec((M_CONV, tc), lambda j, b: (0, j)), pl.BlockSpec((1, tc), lambda j, b: (0, j)), blk],
        out_specs=(blk, pl.BlockSpec((8, tc), lambda j, b: (0, j))),
        compiler_params=_params(("parallel", "arbitrary")),
    )(xbc, cw, cb.reshape(1, W), dy)


def _hgrn_chunk(st, qr, fr, v, gr, lb, gn):
    C = qr.shape[0]
    q = _silu(qr)
    logf = jnp.log(lb + (1.0 - lb) * jax.nn.sigmoid(fr))
    k = (1.0 - lb) * jax.nn.sigmoid(-fr)
    b = jnp.dot(_tri(C), logf, precision=HIGHEST, preferred_element_type=F32)
    o = _dot_nt(q * jnp.exp(b), st)
    keep = lax.broadcasted_iota(jnp.int32, (HGRN_SUB, HGRN_SUB), 0) >= lax.broadcasted_iota(jnp.int32, (HGRN_SUB, HGRN_SUB), 1)
    cols = lax.broadcasted_iota(jnp.int32, (HGRN_SUB, C), 1)
    parts = []
    for blk in range(C // HGRN_SUB):
        r0 = blk * HGRN_SUB
        bi, qi, ki, vi = b[r0:r0 + HGRN_SUB], q[r0:r0 + HGRN_SUB], k[r0:r0 + HGRN_SUB], v[r0:r0 + HGRN_SUB]
        decay = jnp.exp(jnp.minimum(bi[:, None, :] - bi[None, :, :], 0.0))
        a_diag = jnp.where(keep, jnp.sum(qi[:, None, :] * ki[None, :, :] * decay, axis=-1), 0.0)
        oi = _dot_nn(a_diag, vi)
        if blk > 0:
            bs = b[r0 - 1:r0]
            a_off = _dot_nt(qi * jnp.exp(bi - bs), k * jnp.exp(jnp.minimum(bs - b, 0.0)))
            oi = oi + _dot_nn(jnp.where(cols < r0, a_off, 0.0), v)
        parts.append(oi)
    o = o + jnp.concatenate(parts, axis=0)
    bl = b[C - 1:C]
    st1 = st * jnp.exp(bl) + _dot_tn(v, k * jnp.exp(bl - b))
    og = o * lax.rsqrt(jnp.mean(o * o, axis=-1, keepdims=True) + NORM_EPS) * gn * _silu(gr)
    return st1, og


def _hgrn_specs(Bl, L, H, order):
    W, HB = HGRN_PAR * HGRN_HEAD, H // HGRN_PAR

    def at(off):
        if order == "bh":
            return pl.BlockSpec((L, W), lambda b, h: (b, h + off))
        return pl.BlockSpec((L, W), lambda h, b: (b, h + off))
    return [at(0), at(HB), at(2 * HB), at(3 * HB)]


def _hgrn_fwd(P, lb, gn, Bl, L, *, name, jobs=()):
    T, D4 = P.shape
    D = D4 // 4
    H = D // HGRN_HEAD
    C = HGRN_CHUNK
    NC = L // C
    W, HB = HGRN_PAR * HGRN_HEAD, H // HGRN_PAR
    heads = [slice(p * HGRN_HEAD, (p + 1) * HGRN_HEAD) for p in range(HGRN_PAR)]

    def body(q_ref, f_ref, v_ref, g_ref, lb_ref, gn_ref, og_ref, st_ref):
        gnv = gn_ref[...]

        def step(c, sts):
            r = pl.ds(pl.multiple_of(c * C, C), C)
            new = []
            for p, hd in enumerate(heads):
                st_ref[p, c] = sts[p]
                st1, og = _hgrn_chunk(sts[p], q_ref[r, hd], f_ref[r, hd], v_ref[r, hd], g_ref[r, hd], lb_ref[:, hd], gnv)
                og_ref[r, hd] = og.astype(og_ref.dtype)
                new.append(st1)
            return tuple(new)

        lax.fori_loop(0, NC, step, tuple(jnp.zeros((HGRN_HEAD, HGRN_HEAD), F32) for _ in heads))

    return _call(
        body, name=name,
        out_shape=(jax.ShapeDtypeStruct((T, D), BF16), jax.ShapeDtypeStruct((Bl, H, NC, HGRN_HEAD, HGRN_HEAD), F32)),
        grid=(Bl, HB),
        in_specs=_hgrn_specs(Bl, L, H, "bh") + [pl.BlockSpec((1, W), lambda b, h: (0, h)), pl.BlockSpec((1, HGRN_HEAD), lambda b, h: (0, 0))],
        out_specs=(pl.BlockSpec((L, W), lambda b, h: (b, h)),
                   pl.BlockSpec((None, HGRN_PAR, NC, HGRN_HEAD, HGRN_HEAD), lambda b, h: (b, h, 0, 0, 0))),
        dims=("parallel", "parallel"), jobs=jobs,
    )(P, P, P, P, lb.reshape(1, D), gn.reshape(1, HGRN_HEAD))


def _hgrn_bwd(P, lb, gn, states, dog, Bl, L, *, name, jobs=()):
    T, D4 = P.shape
    D = D4 // 4
    H = D // HGRN_HEAD
    C = HGRN_CHUNK
    NC = L // C
    W, HB = HGRN_PAR * HGRN_HEAD, H // HGRN_PAR
    heads = [slice(p * HGRN_HEAD, (p + 1) * HGRN_HEAD) for p in range(HGRN_PAR)]

    def body(q_ref, f_ref, v_ref, g_ref, lb_ref, gn_ref, st_ref, dog_ref, dp_ref, dlb_ref, dgn_ref):
        gnv = gn_ref[...]

        @pl.when(pl.program_id(1) == 0)
        def _():
            dlb_ref[...] = jnp.zeros_like(dlb_ref)

        @pl.when((pl.program_id(0) == 0) & (pl.program_id(1) == 0))
        def _():
            dgn_ref[...] = jnp.zeros_like(dgn_ref)

        def step(i, dsts):
            c = NC - 1 - i
            r = pl.ds(pl.multiple_of(c * C, C), C)
            new, dgn_sum = [], jnp.zeros((1, HGRN_HEAD), F32)
            for p, hd in enumerate(heads):
                _, vjp = jax.vjp(_hgrn_chunk, st_ref[p, c], q_ref[r, hd], f_ref[r, hd], v_ref[r, hd], g_ref[r, hd],
                                 lb_ref[:, hd], gnv)
                dst0, dq, df, dv, dg, dlb, dgn = vjp((dsts[p], dog_ref[r, hd].astype(F32)))
                dp_ref[0, r, hd] = dq.astype(dp_ref.dtype)
                dp_ref[1, r, hd] = df.astype(dp_ref.dtype)
                dp_ref[2, r, hd] = dv.astype(dp_ref.dtype)
                dp_ref[3, r, hd] = dg.astype(dp_ref.dtype)
                dlb_ref[:, hd] += dlb
                dgn_sum = dgn_sum + dgn
                new.append(dst0)
            dgn_ref[...] += dgn_sum
            return tuple(new)

        lax.fori_loop(0, NC, step, tuple(jnp.zeros((HGRN_HEAD, HGRN_HEAD), F32) for _ in heads))

    return _call(
        body, name=name,
        out_shape=(jax.ShapeDtypeStruct((4, T, D), BF16), jax.ShapeDtypeStruct((1, D), F32), jax.ShapeDtypeStruct((1, HGRN_HEAD), F32)),
        grid=(HB, Bl),
        in_specs=_hgrn_specs(Bl, L, H, "hb") + [
            pl.BlockSpec((1, W), lambda h, b: (0, h)), pl.BlockSpec((1, HGRN_HEAD), lambda h, b: (0, 0)),
            pl.BlockSpec((None, HGRN_PAR, NC, HGRN_HEAD, HGRN_HEAD), lambda h, b: (b, h, 0, 0, 0)),
            pl.BlockSpec((L, W), lambda h, b: (b, h))],
        out_specs=(pl.BlockSpec((4, L, W), lambda h, b: (0, b, h)),
                   pl.BlockSpec((1, W), lambda h, b: (0, h)), pl.BlockSpec((1, HGRN_HEAD), lambda h, b: (0, 0))),
        dims=("arbitrary", "arbitrary"), jobs=jobs,
    )(P, P, P, P, lb.reshape(1, D), gn.reshape(1, HGRN_HEAD), states, dog)


def _lower_bounds(logits):
    p = jax.nn.softmax(logits, axis=0)
    rows, run = [], jnp.zeros_like(p[0:1])
    for i in range(logits.shape[0]):
        run = run + p[i:i + 1]
        rows.append(run - p[0:1])
    return jnp.concatenate(rows, axis=0) if len(rows) > 1 else rows[0]


def _lb_fwd(logits, *, name):
    def body(x_ref, o_ref):
        o_ref[...] = _lower_bounds(x_ref[...])
    return pl.pallas_call(body, name=name, out_shape=jax.ShapeDtypeStruct(logits.shape, F32))(logits)


def _lb_bwd(logits, dlb, *, name):
    def body(x_ref, g_ref, o_ref):
        _, vjp = jax.vjp(_lower_bounds, x_ref[...])
        o_ref[...] = vjp(g_ref[...])[0]
    return pl.pallas_call(body, name=name, out_shape=jax.ShapeDtypeStruct(logits.shape, F32))(logits, dlb)


def _ssd_pre(dtc_raw, dtr_raw, bias_c, bias_r, alog_c, alog_r):
    C = dtc_raw.shape[0]
    dt_c = jax.nn.softplus(dtc_raw + bias_c)
    dt_r = jax.nn.softplus(dtr_raw + bias_r)
    acs_c = jnp.dot(_tri(C), dt_c * -jnp.exp(alog_c), precision=HIGHEST, preferred_element_type=F32)
    acs_r = lax.dot_general(dt_r * -jnp.exp(alog_r), _tri(C), (((1,), (1,)), ((), ())), precision=HIGHEST, preferred_element_type=F32)
    return dt_c, acs_c, acs_r


def _ssd_head(h0, xs, G, Bm, Cm, dt, acs, acs_row, dskip):
    C = xs.shape[0]
    causal = lax.broadcasted_iota(jnp.int32, (C, C), 0) >= lax.broadcasted_iota(jnp.int32, (C, C), 1)
    Lm = jnp.where(causal, jnp.exp(jnp.minimum(acs - acs_row, 0.0)), 0.0)
    X = xs * dt
    y = _dot_nn(G * Lm, X) + _dot_nt(Cm * jnp.exp(acs), h0) + dskip * xs
    alast = acs[C - 1:C]
    h1 = jnp.exp(alast) * h0 + _dot_tn(X, Bm * jnp.exp(alast - acs))
    return y, h1


def _ssd_layout(Bl, L, d_inner, order):
    G = M_GROUPS
    hpg = d_inner // M_HEADDIM // G
    gw = hpg * M_HEADDIM
    ix = (lambda b, g: (b, g)) if order == "bg" else (lambda g, b: (b, g))
    def cols(width, off):
        if order == "bg":
            return pl.BlockSpec((L, width), lambda b, g: (b, g + off))
        return pl.BlockSpec((L, width), lambda g, b: (b, g + off))
    def per_group(shape):
        n = len(shape)
        if order == "bg":
            return pl.BlockSpec((None,) + shape, lambda b, g: (g,) + (0,) * n)
        return pl.BlockSpec((None,) + shape, lambda g, b: (g,) + (0,) * n)
    if order == "bg":
        dtc = pl.BlockSpec((None, L, LANES), lambda b, g: (g, b, 0))
        dtr = pl.BlockSpec((None, 8, L), lambda b, g: (g, 0, b))
    else:
        dtc = pl.BlockSpec((None, L, LANES), lambda g, b: (g, b, 0))
        dtr = pl.BlockSpec((None, 8, L), lambda g, b: (g, 0, b))
    nb = d_inner // M_D_STATE
    return hpg, gw, cols, per_group, dtc, dtr, nb


def _ssd_fwd(xa, dtc, dtr, prm, Bl, L, d_inner, *, name, jobs=()):
    T = xa.shape[0]
    C = M_CHUNK
    NC = L // C
    G = M_GROUPS
    hpg, gw, cols, per_group, dtc_spec, dtr_spec, nb = _ssd_layout(Bl, L, d_inner, "bg")

    def body(xs_ref, b_ref, c_ref, dtc_ref, dtr_ref, bc_ref, br_ref, ac_ref, ar_ref, dk_ref, y_ref, hs_ref,
             dt_s, acs_s, acr_s, h_s):
        h_s[...] = jnp.zeros_like(h_s)

        def step(c, carry):
            r = pl.ds(pl.multiple_of(c * C, C), C)
            dt_c, acs_c, acs_r = _ssd_pre(dtc_ref[r, :], dtr_ref[:, r], bc_ref[...], br_ref[...], ac_ref[...], ar_ref[...])
            dt_s[...] = dt_c
            acs_s[...] = acs_c
            acr_s[...] = acs_r
            Bm, Cm = b_ref[r, :], c_ref[r, :]
            Gm = _dot_nt(Cm, Bm)
            for j in range(hpg):
                pc = slice(j * M_HEADDIM, (j + 1) * M_HEADDIM)
                h0 = h_s[j]
                hs_ref[c, j] = h0
                y, h1 = _ssd_head(h0, xs_ref[r, pc], Gm, Bm, Cm, dt_s[:, j:j + 1], acs_s[:, j:j + 1], acr_s[j:j + 1, :],
                                  dk_ref[0:1, j:j + 1])
                y_ref[r, pc] = y
                h_s[j] = h1
            return carry

        lax.fori_loop(0, NC, step, 0)

    return _call(
        body, name=name,
        out_shape=(jax.ShapeDtypeStruct((T, d_inner), F32), jax.ShapeDtypeStruct((Bl, G, NC, hpg, M_HEADDIM, M_D_STATE), F32)),
        grid=(Bl, G),
        in_specs=[cols(gw, 0), cols(M_D_STATE, nb), cols(M_D_STATE, nb + G), dtc_spec, dtr_spec,
                  per_group((1, LANES)), per_group((8, 1)), per_group((1, LANES)), per_group((8, 1)), per_group((1, LANES))],
        out_specs=(cols(gw, 0), pl.BlockSpec((None, None, NC, hpg, M_HEADDIM, M_D_STATE), lambda b, g: (b, g, 0, 0, 0, 0))),
        scratch_shapes=[pltpu.VMEM((C, LANES), F32), pltpu.VMEM((C, LANES), F32), pltpu.VMEM((8, C), F32),
                        pltpu.VMEM((hpg, M_HEADDIM, M_D_STATE), F32)],
        dims=("parallel", "parallel"), jobs=jobs,
    )(xa, xa, xa, dtc, dtr, *prm)


def _ssd_bwd(xa, dtc, dtr, prm, states, dy, Bl, L, d_inner, *, name, jobs=()):
    T = xa.shape[0]
    C = M_CHUNK
    NC = L // C
    G = M_GROUPS
    hpg, gw, cols, per_group, dtc_spec, dtr_spec, nb = _ssd_layout(Bl, L, d_inner, "gb")

    def body(xs_ref, b_ref, c_ref, dtc_ref, dtr_ref, bc_ref, br_ref, ac_ref, ar_ref, dk_ref, hs_ref, dy_ref,
             dxs_ref, db_ref, dc_ref, ddtc_ref, ddtr_ref, gbc_ref, gbr_ref, gac_ref, gar_ref, gdk_ref,
             dt_s, acs_s, acr_s, ddt_s, dacs_s, dacr_s, dh_s):
        dh_s[...] = jnp.zeros_like(dh_s)

        @pl.when(pl.program_id(1) == 0)
        def _():
            for ref in (gbc_ref, gbr_ref, gac_ref, gar_ref, gdk_ref):
                ref[...] = jnp.zeros_like(ref)

        def step(i, carry):
            c = NC - 1 - i
            r = pl.ds(pl.multiple_of(c * C, C), C)
            pre_in = (dtc_ref[r, :], dtr_ref[:, r], bc_ref[...], br_ref[...], ac_ref[...], ar_ref[...])
            (dt_c, acs_c, acs_r), pre_vjp = jax.vjp(_ssd_pre, *pre_in)
            dt_s[...] = dt_c
            acs_s[...] = acs_c
            acr_s[...] = acs_r
            ddt_s[...] = jnp.zeros_like(ddt_s)
            dacs_s[...] = jnp.zeros_like(dacs_s)
            dacr_s[...] = jnp.zeros_like(dacr_s)
            Bm, Cm = b_ref[r, :], c_ref[r, :]
            Gm = _dot_nt(Cm, Bm)
            dG = jnp.zeros((C, C), F32)
            dB = jnp.zeros((C, M_D_STATE), F32)
            dC = jnp.zeros((C, M_D_STATE), F32)
            for j in range(hpg):
                pc = slice(j * M_HEADDIM, (j + 1) * M_HEADDIM)
                one = slice(j, j + 1)
                _, vjp = jax.vjp(_ssd_head, hs_ref[c, j], xs_ref[r, pc], Gm, Bm, Cm, dt_s[:, one], acs_s[:, one], acr_s[one, :],
                                 dk_ref[0:1, one])
                dh0, dxs, dGj, dBj, dCj, ddt, dacs, dacr, ddk = vjp((dy_ref[r, pc], dh_s[j]))
                dh_s[j] = dh0
                dxs_ref[r, pc] = dxs
                dG, dB, dC = dG + dGj, dB + dBj, dC + dCj
                ddt_s[:, one] = ddt
                dacs_s[:, one] = dacs
                dacr_s[one, :] = dacr
                gdk_ref[0:1, one] += ddk
            db_ref[r, :] = dB + _dot_tn(dG, Cm)
            dc_ref[r, :] = dC + _dot_nn(dG, Bm)
            ddtc, ddtr, gbc, gbr, gac, gar = pre_vjp((ddt_s[...], dacs_s[...], dacr_s[...]))
            ddtc_ref[r, :] = ddtc
            ddtr_ref[:, r] = ddtr
            gbc_ref[...] += gbc
            gbr_ref[...] += gbr
            gac_ref[...] += gac
            gar_ref[...] += gar
            return carry

        lax.fori_loop(0, NC, step, 0)

    GN = G * M_D_STATE
    bc_cols = lambda off: pl.BlockSpec((L, M_D_STATE), lambda g, b: (b, g + off))
    outs = _call(
        body, name=name,
        out_shape=(jax.ShapeDtypeStruct((T, d_inner), F32), jax.ShapeDtypeStruct((T, GN), F32), jax.ShapeDtypeStruct((T, GN), F32),
                   jax.ShapeDtypeStruct(dtc.shape, F32), jax.ShapeDtypeStruct(dtr.shape, F32),
                   jax.ShapeDtypeStruct((G, 1, LANES), F32), jax.ShapeDtypeStruct((G, 8, 1), F32),
                   jax.ShapeDtypeStruct((G, 1, LANES), F32), jax.ShapeDtypeStruct((G, 8, 1), F32),
                   jax.ShapeDtypeStruct((G, 1, LANES), F32)),
        grid=(G, Bl),
        in_specs=[cols(gw, 0), cols(M_D_STATE, nb), cols(M_D_STATE, nb + G), dtc_spec, dtr_spec,
                  per_group((1, LANES)), per_group((8, 1)), per_group((1, LANES)), per_group((8, 1)), per_group((1, LANES)),
                  pl.BlockSpec((None, None, NC, hpg, M_HEADDIM, M_D_STATE), lambda g, b: (b, g, 0, 0, 0, 0)), cols(gw, 0)],
        out_specs=(cols(gw, 0), bc_cols(0), bc_cols(0), dtc_spec, dtr_spec,
                   per_group((1, LANES)), per_group((8, 1)), per_group((1, LANES)), per_group((8, 1)), per_group((1, LANES))),
        scratch_shapes=[pltpu.VMEM((C, LANES), F32), pltpu.VMEM((C, LANES), F32), pltpu.VMEM((8, C), F32),
                        pltpu.VMEM((C, LANES), F32), pltpu.VMEM((C, LANES), F32), pltpu.VMEM((8, C), F32),
                        pltpu.VMEM((hpg, M_HEADDIM, M_D_STATE), F32)],
        dims=("arbitrary", "arbitrary"), jobs=jobs,
    )(xa, xa, xa, dtc, dtr, *prm, states, dy)
    return outs


def _mgate(y, z, w):
    t = y * _silu(z)
    return t * lax.rsqrt(jnp.mean(t * t, axis=-1, keepdims=True) + NORM_EPS) * w


def _mgate_fwd(y, z, w, *, name, jobs=()):
    T, d_inner = y.shape
    gw = d_inner // M_GROUPS
    tr = _tile(T, 512, 8)

    def body(y_ref, z_ref, w_ref, o_ref):
        o_ref[...] = _mgate(y_ref[...], z_ref[...], w_ref[...]).astype(o_ref.dtype)

    blk = pl.BlockSpec((tr, gw), lambda i, g: (i, g))
    return _call(
        body, name=name, out_shape=jax.ShapeDtypeStruct((T, d_inner), BF16), grid=(T // tr, M_GROUPS),
        in_specs=[blk, blk, pl.BlockSpec((1, gw), lambda i, g: (0, g))], out_specs=blk,
        dims=("parallel", "parallel"), jobs=jobs,
    )(y, z, w.reshape(1, d_inner))


def _mgate_bwd(y, z, w, dyg, *, name):
    T, d_inner = y.shape
    gw = d_inner // M_GROUPS
    tr = _tile(T, 512, 8)

    def body(y_ref, z_ref, w_ref, g_ref, dy_ref, dz_ref, dw_ref):
        _, vjp = jax.vjp(_mgate, y_ref[...], z_ref[...], w_ref[...])
        dy, dz, dw = vjp(g_ref[...].astype(F32))
        dy_ref[...] = dy
        dz_ref[...] = dz.astype(dz_ref.dtype)

        @pl.when(pl.program_id(1) == 0)
        def _():
            dw_ref[...] = jnp.zeros_like(dw_ref)

        dw_ref[...] += dw

    blk = pl.BlockSpec((tr, gw), lambda g, i: (i, g))
    vec = pl.BlockSpec((1, gw), lambda g, i: (0, g))
    return pl.pallas_call(
        body, name=name,
        out_shape=(jax.ShapeDtypeStruct((T, d_inner), F32), jax.ShapeDtypeStruct((T, d_inner), BF16), jax.ShapeDtypeStruct((1, d_inner), F32)),
        grid=(M_GROUPS, T // tr), in_specs=[blk, blk, vec, blk], out_specs=(blk, blk, vec),
        compiler_params=_params(("parallel", "arbitrary")),
    )(y, z, w.reshape(1, d_inner), dyg)


def _adamw_math(w, g, m, v):
    m1 = ADAM_B1 * m + (1.0 - ADAM_B1) * g
    v1 = ADAM_B2 * v + (1.0 - ADAM_B2) * (g * g)
    m_hat = m1 / (1.0 - ADAM_B1 ** ADAM_STEP)
    v_hat = v1 / (1.0 - ADAM_B2 ** ADAM_STEP)
    delta = -ADAM_LR * (m_hat / (jnp.sqrt(v_hat) + ADAM_EPS) + ADAM_WD * w)
    return delta, m1, v1


def _adamw_sharded(land, w, m, v, *, name, jobs=()):
    R, n = w.shape
    tr = _tile(R, 128, 8)

    def body(l_ref, w_ref, m_ref, v_ref, g_ref, d_ref, m1_ref, v1_ref):
        g = l_ref[0].astype(F32)
        for k in range(1, N_DEV):
            g = g + l_ref[k].astype(F32)
        d, m1, v1 = _adamw_math(w_ref[...], g, m_ref[...], v_ref[...])
        g_ref[...] = g
        d_ref[...] = d
        m1_ref[...] = m1
        v1_ref[...] = v1

    blk = pl.BlockSpec((tr, n), lambda i: (i, 0))
    out = jax.ShapeDtypeStruct((R, n), F32)
    return _call(
        body, name=name, out_shape=(out, out, out, out), grid=(R // tr,),
        in_specs=[pl.BlockSpec((N_DEV, tr, n), lambda i: (0, i, 0)), blk, blk, blk], out_specs=(blk, blk, blk, blk),
        dims=("parallel",), jobs=jobs,
    )(land, w, m, v)


def _sum_devices(parts, *, name):
    _, R, n = parts.shape
    tr = _tile(R, 512, 8)

    def body(p_ref, o_ref):
        g = p_ref[0]
        for k in range(1, N_DEV):
            g = g + p_ref[k]
        o_ref[...] = g

    return pl.pallas_call(
        body, name=name, out_shape=jax.ShapeDtypeStruct((R, n), F32), grid=(R // tr,),
        in_specs=[pl.BlockSpec((N_DEV, tr, n), lambda i: (0, i, 0))], out_specs=pl.BlockSpec((tr, n), lambda i: (i, 0)),
        compiler_params=_params(("parallel",)),
    )(parts)


def _adamw_packed(g, w, m, v, *, name):
    R, n = w.shape
    tr = _tile(R, 512, 8)

    def body(g_ref, w_ref, m_ref, v_ref, d_ref, m1_ref, v1_ref):
        d, m1, v1 = _adamw_math(w_ref[...], g_ref[...], m_ref[...], v_ref[...])
        d_ref[...] = d
        m1_ref[...] = m1
        v1_ref[...] = v1

    blk = pl.BlockSpec((tr, n), lambda i: (i, 0))
    out = jax.ShapeDtypeStruct((R, n), F32)
    return pl.pallas_call(
        body, name=name, out_shape=(out, out, out), grid=(R // tr,), in_specs=[blk] * 4, out_specs=(blk, blk, blk),
        compiler_params=_params(("parallel",)),
    )(g, w, m, v)


PACK_ALIGN = 8 * LANES


def _pack(arrs):
    parts = []
    for a in arrs:
        flat = a.reshape(-1).astype(F32)
        pad = (-flat.shape[0]) % PACK_ALIGN
        parts.append(jnp.pad(flat, (0, pad)).reshape(-1, LANES))
    return jnp.concatenate(parts, axis=0)


def _unpack(packed, shapes, lead=()):
    out, r = [], 0
    nl = len(lead)
    for s in shapes:
        n = math.prod(s)
        rows = (n + PACK_ALIGN - 1) // PACK_ALIGN * (PACK_ALIGN // LANES)
        part = lax.slice_in_dim(packed, r, r + rows, axis=nl)
        out.append(part.reshape(lead + (rows * LANES,))[..., :n].reshape(lead + tuple(s)))
        r += rows
    return out


EXCHANGE_BYTES_PER_US = 80e3
CHIP_GATHER_BYTES_PER_US = 68e3
SIBLING_BYTES_PER_US = 250e3
EXCHANGE_ROW_ALIGN = 64
CARRY_FILL = 1.0
CARRY_FILL_BACKWARD = 0.75

CALL_US = {
    "hgrn_in_fwd": 155, "hgrn_scan_fwd": 770, "hgrn_out_fwd": 40, "ffn_up_fwd": 220, "ffn_gate_fwd": 114, "ffn_down_fwd": 120,
    "mamba_z_fwd": 75, "mamba_xbc_fwd": 112, "mamba_scan_fwd": 400, "mamba_out_fwd": 86, "mamba_conv_fwd": 90, "mamba_gate_fwd": 75,
    "ffn_down_bwd_x": 113, "ffn_down_bwd_w": 125, "ffn_gate_bwd": 170, "ffn_up_bwd_x": 240, "ffn_up_bwd_w": 229,
    "hgrn_out_bwd_x": 40, "hgrn_out_bwd_w": 45, "hgrn_scan_bwd": 1800, "hgrn_in_bwd_x": 172, "hgrn_in_bwd_w": 158,
    "mamba_out_bwd_x": 78, "mamba_out_bwd_w": 88, "mamba_scan_bwd": 1200, "mamba_z_bwd_x": 88, "mamba_xbc_bwd_x": 130,
    "mamba_z_bwd_w": 82, "mamba_xbc_bwd_w": 120,
    "adamw_hgrn_w_in": 65, "adamw_hgrn_w_out": 16, "adamw_m_w_in": 87, "adamw_m_w_out": 32, "adamw_f_w_up": 168,
    "adamw_f_w_down": 109,
}


class _Plan:
    def __init__(self):
        self.queue = []
        self.handoff = []
        self.fill = CARRY_FILL

    def push(self, src, mode, buf, q0=0):
        rows, n = src.shape[-2], src.shape[-1]
        blocks = len(OTHER_CHIPS) if mode == "gather_chip" else N_DEV - 1
        rate = CHIP_GATHER_BYTES_PER_US if mode == "gather_chip" else EXCHANGE_BYTES_PER_US
        self.queue.append(dict(src=src, mode=mode, buf=buf, r0=0, r1=rows, q0=q0, us_per_row=blocks * n * src.dtype.itemsize / rate))

    def _hand_on(self, it, r0, r1):
        q0, q1 = it["q0"] + r0, it["q0"] + r1
        if self.handoff and self.handoff[-1]["buf"] is it["buf"] and self.handoff[-1]["r1"] == q0:
            self.handoff[-1]["r1"] = q1
            return
        us = len(OTHER_CHIPS) * it["src"].shape[-1] * it["src"].dtype.itemsize / SIBLING_BYTES_PER_US
        self.handoff.append(dict(src=jnp.zeros((8, LANES), it["src"].dtype), mode="forward", buf=it["buf"], r0=q0, r1=q1, q0=0,
                                 us_per_row=us))

    def take(self, name):
        budget, jobs, used, spawned = self.fill * CALL_US[name], [], set(), []
        for queue in (self.handoff, self.queue):
            i = 0
            while i < len(queue):
                it = queue[i]
                if id(it["buf"]) in used:
                    i += 1
                    continue
                left = it["r1"] - it["r0"]
                fit = int(budget / it["us_per_row"])
                rows = left if fit >= left else fit // EXCHANGE_ROW_ALIGN * EXCHANGE_ROW_ALIGN
                if rows <= 0:
                    break
                jobs.append(_Job(it["src"], it["mode"], it["buf"], it["r0"], it["r0"] + rows, it["q0"] + it["r0"]))
                used.add(id(it["buf"]))
                budget -= rows * it["us_per_row"]
                if it["mode"] == "gather_chip":
                    spawned.append((it, it["r0"], it["r0"] + rows))
                it["r0"] += rows
                if it["r0"] < it["r1"]:
                    break
                queue.pop(i)
        for it, r0, r1 in spawned:
            self._hand_on(it, r0, r1)
        return jobs

    def finish(self, buf, *, name):
        mine = [it for it in self.queue if it["buf"] is buf]
        self.queue = [it for it in self.queue if it["buf"] is not buf]
        for i, it in enumerate(mine):
            _exchange([_Job(it["src"], it["mode"], buf, it["r0"], it["r1"], it["q0"] + it["r0"])], name=f"{name}_{i}" if i else name)
            if it["mode"] == "gather_chip":
                self._hand_on(it, it["r0"], it["r1"])
        mine = [it for it in self.handoff if it["buf"] is buf]
        self.handoff = [it for it in self.handoff if it["buf"] is not buf]
        for i, it in enumerate(mine):
            _exchange([_Job(it["src"], "forward", buf, it["r0"], it["r1"], it["r0"])], name=f"{name}_handoff_{i}" if i else f"{name}_handoff")
        return buf.arr


def _pad_to(a, axis, size):
    pad = [(0, 0)] * a.ndim
    pad[axis] = (0, size - a.shape[axis])
    return jnp.pad(a, pad)


def _ssd_operands(dt_raw, dt_bias, a_log, dskip):
    T, heads = dt_raw.shape
    G = M_GROUPS
    hpg = heads // G
    t3 = dt_raw.reshape(T, G, hpg)
    dtc = _pad_to(t3.transpose(1, 0, 2), 2, LANES)
    dtr = _pad_to(t3.transpose(1, 2, 0), 1, 8)
    col = lambda p: _pad_to(p.reshape(G, 1, hpg), 2, LANES)
    row = lambda p: _pad_to(p.reshape(G, hpg, 1), 1, 8)
    return dtc, dtr, (col(dt_bias), row(dt_bias), col(a_log), row(a_log), col(dskip))


def _ssd_unpack(ddtc, ddtr, gbc, gbr, gac, gar, gdk, heads):
    G = M_GROUPS
    hpg = heads // G
    T = ddtc.shape[1]
    ddt = ddtc[:, :, :hpg].transpose(1, 0, 2).reshape(T, heads) + ddtr[:, :hpg, :].transpose(2, 0, 1).reshape(T, heads)
    both = lambda c, r: c[:, 0, :hpg].reshape(heads) + r[:, :hpg, 0].reshape(heads)
    return ddt, both(gbc, gbr), both(gac, gar), gdk[:, 0, :hpg].reshape(heads)


class _LazyWeights:
    def __init__(self, name, shape, fetch):
        self.name, self.shape, self.fetch, self.got = name, shape, fetch, {}

    def __getitem__(self, layer):
        if layer not in self.got:
            self.got[layer] = self.fetch(self.name, layer)
        return self.got[layer]


class _GradSink:
    def __init__(self, name, layers, push):
        self.name, self.shape, self.push = name, (layers,), push

    def __setitem__(self, layer, g):
        self.push(self.name, layer, g)


def _local_step(x, target, W, plan, push):
    Bl, L, D = x.shape
    T = Bl * L
    depth = W["mix_norm"].shape[0]
    d_inner = W["m_w_out"].shape[1]
    heads = d_inner // M_HEADDIM
    conv_dim = W["m_conv_w"].shape[2]

    def mm(a, b, mode, out_dtype, *, name, add=None, out_stack=None):
        return _matmul(a, b, mode, out_dtype, name=name, add=add, out_stack=out_stack,
                       jobs=plan.take(name) if name in CALL_US else ())

    lb = _lb_fwd(W["hgrn_lb_logits"], name="hgrn_lower_bounds")
    h = x.reshape(T, D)
    saved = []
    for i in range(depth):
        j = i // 2
        s = {"h": h}
        u = _rms_fwd(h, W["mix_norm"][i], name="mix_norm_fwd")
        s["u"] = u
        if i % 2 == 0:
            P = mm(u, W["hgrn_w_in"][j], "nn", F32, name="hgrn_in_fwd")
            og, st = _hgrn_fwd(P, lb[j], W["hgrn_gnorm"][j], Bl, L, name="hgrn_scan_fwd", jobs=plan.take("hgrn_scan_fwd"))
            h = mm(og, W["hgrn_w_out"][j], "nn", F32, add=h, name="hgrn_out_fwd")
            s.update(P=P, og=og, st=st)
        else:
            w_in = W["m_w_in"][j]
            z = mm(u, w_in[:, :d_inner], "nn", F32, name="mamba_z_fwd")
            xbc = mm(u, w_in[:, d_inner:d_inner + conv_dim], "nn", F32, name="mamba_xbc_fwd")
            dt_raw = mm(u, w_in[:, d_inner + conv_dim:], "nn", F32, name="mamba_dt_fwd")
            xa = _mconv_fwd(xbc, W["m_conv_w"][j], W["m_conv_b"][j], Bl, L, name="mamba_conv_fwd", jobs=plan.take("mamba_conv_fwd"))
            dtc, dtr, prm = _ssd_operands(dt_raw, W["m_dt_bias"][j], W["m_A_log"][j], W["m_D"][j])
            y, hs = _ssd_fwd(xa, dtc, dtr, prm, Bl, L, d_inner, name="mamba_scan_fwd", jobs=plan.take("mamba_scan_fwd"))
            yg = _mgate_fwd(y, z, W["m_norm"][j], name="mamba_gate_fwd", jobs=plan.take("mamba_gate_fwd"))
            h = mm(yg, W["m_w_out"][j], "nn", F32, add=h, name="mamba_out_fwd")
            s.update(z=z, xbc=xbc, xa=xa, dtc=dtc, dtr=dtr, prm=prm, y=y, hs=hs, yg=yg)
        s["h2"] = h
        u2 = _rms_fwd(h, W["ffn_norm"][i], name="ffn_norm_fwd")
        a = mm(u2, W["f_w_up"][i], "nn", BF16, name="ffn_up_fwd")
        yf, hf = _ffn_gate_fwd(a, W["f_conv_w"][i], W["f_conv_b"][i], Bl, L, name="ffn_gate_fwd", jobs=plan.take("ffn_gate_fwd"))
        h = mm(yf, W["f_w_down"][i], "nn", F32, add=h, name="ffn_down_fwd")
        s.update(u2=u2, a=a, yf=yf, hf=hf)
        saved.append(s)

    loss, dh, d_final = _loss_head(h, W["final_norm"], target.reshape(T, D), name="loss_head")
    plan.fill = CARRY_FILL_BACKWARD

    g = {k: (_GradSink(k, W[k].shape[0], push) if isinstance(W[k], _LazyWeights) else [None] * W[k].shape[0])
         for k in W if k != "final_norm"}
    g["final_norm"] = d_final.reshape(D)
    dlb = [None] * W["hgrn_lb_logits"].shape[0]
    for i in reversed(range(depth)):
        j = i // 2
        s = saved[i]
        g["f_w_down"][i] = mm(s["yf"], dh, "tn", BF16, name="ffn_down_bwd_w")
        dyf = mm(dh, W["f_w_down"][i], "nt", BF16, name="ffn_down_bwd_x")
        da, pf = _ffn_gate_bwd(s["a"], s["hf"], W["f_conv_w"][i], dyf, Bl, L, name="ffn_gate_bwd",
                               jobs=plan.take("ffn_gate_bwd"))
        g["f_w_up"][i] = mm(s["u2"], da, "tn", BF16, out_stack=N_DEV, name="ffn_up_bwd_w")
        du2 = mm(da, W["f_w_up"][i], "nt", F32, name="ffn_up_bwd_x")
        g["f_conv_w"][i], g["f_conv_b"][i] = pf[:FFN_CONV], pf[FFN_CONV]
        dh, dn = _rms_bwd(s["h2"], W["ffn_norm"][i], du2, dh, name="ffn_norm_bwd")
        g["ffn_norm"][i] = dn.reshape(D)
        if i % 2 == 0:
            g["hgrn_w_out"][j] = mm(s["og"], dh, "tn", BF16, name="hgrn_out_bwd_w")
            dog = mm(dh, W["hgrn_w_out"][j], "nt", BF16, name="hgrn_out_bwd_x")
            dP4, dlb_j, dgn = _hgrn_bwd(s["P"], lb[j], W["hgrn_gnorm"][j], s["st"], dog, Bl, L, name="hgrn_scan_bwd",
                                        jobs=plan.take("hgrn_scan_bwd"))
            g["hgrn_w_in"][j] = mm(s["u"], dP4, "tn", BF16, out_stack=N_DEV, name="hgrn_in_bwd_w")
            du = mm(dP4, W["hgrn_w_in"][j], "nt", F32, name="hgrn_in_bwd_x")
            dlb[j] = dlb_j
            g["hgrn_gnorm"][j] = dgn.reshape(HGRN_HEAD)
        else:
            w_in = W["m_w_in"][j]
            g["m_w_out"][j] = mm(s["yg"], dh, "tn", BF16, name="mamba_out_bwd_w")
            dyg = mm(dh, W["m_w_out"][j], "nt", BF16, name="mamba_out_bwd_x")
            dy, dz, dnw = _mgate_bwd(s["y"], s["z"], W["m_norm"][j], dyg, name="mamba_gate_bwd")
            dxs, dB, dC, ddtc, ddtr, gbc, gbr, gac, gar, gdk = _ssd_bwd(
                s["xa"], s["dtc"], s["dtr"], s["prm"], s["hs"], dy, Bl, L, d_inner, name="mamba_scan_bwd",
                jobs=plan.take("mamba_scan_bwd"))
            ddt, g["m_dt_bias"][j], g["m_A_log"][j], g["m_D"][j] = _ssd_unpack(ddtc, ddtr, gbc, gbr, gac, gar, gdk, heads)
            dxa = jnp.concatenate([dxs, dB, dC], axis=1)
            dxbc, pm = _mconv_bwd(s["xbc"], W["m_conv_w"][j], W["m_conv_b"][j], dxa, Bl, L, name="mamba_conv_bwd")
            g["m_conv_w"][j], g["m_conv_b"][j], g["m_norm"][j] = pm[:M_CONV], pm[M_CONV], dnw.reshape(d_inner)
            ddt = ddt.astype(BF16)
            g["m_w_in"][j] = jnp.concatenate([
                mm(s["u"], dz, "tn", BF16, name="mamba_z_bwd_w"), mm(s["u"], dxbc, "tn", BF16, name="mamba_xbc_bwd_w"),
                mm(s["u"], ddt, "tn", BF16, name="mamba_dt_bwd_w")], axis=1)
            du = mm(dz, w_in[:, :d_inner], "nt", F32, name="mamba_z_bwd_x")
            du = mm(dxbc, w_in[:, d_inner:d_inner + conv_dim], "nt", F32, add=du, name="mamba_xbc_bwd_x")
            du = mm(ddt, w_in[:, d_inner + conv_dim:], "nt", F32, add=du, name="mamba_dt_bwd_x")
        dh, dn = _rms_bwd(s["h"], W["mix_norm"][i], du, dh, name="mix_norm_bwd")
        g["mix_norm"][i] = dn.reshape(D)
    g["hgrn_lb_logits"] = _lb_bwd(W["hgrn_lb_logits"], jnp.concatenate(dlb, axis=0), name="hgrn_lower_bounds_bwd")
    grads = {k: (jnp.stack(v) if isinstance(v, list) else v) for k, v in g.items() if not isinstance(v, _GradSink)}
    return loss, dh.reshape(Bl, L, D), grads


WEIGHTS = ("mix_norm", "ffn_norm", "final_norm", "hgrn_w_in", "hgrn_lb_logits", "hgrn_gnorm", "hgrn_w_out", "m_w_in",
           "m_conv_w", "m_conv_b", "m_dt_bias", "m_A_log", "m_D", "m_norm", "m_w_out", "f_w_up", "f_conv_w", "f_conv_b",
           "f_w_down")
COL_SHARDED = ("hgrn_w_in", "m_w_in", "f_w_up")
STACKED = ("hgrn_w_in", "f_w_up")
ROW_SHARDED = ("hgrn_w_out", "m_w_out", "f_w_down")
SMALL_SHARDED = ("m_conv_w", "m_conv_b", "m_norm", "f_conv_w")
SMALL = tuple(k for k in WEIGHTS if k not in COL_SHARDED + ROW_SHARDED)


def _join_last(g):
    nd = g.ndim
    t = g.transpose(tuple(range(1, nd - 1)) + (0, nd - 1))
    return t.reshape(t.shape[:-2] + (N_DEV * g.shape[-1],))


def kernel(x, mix_norm, ffn_norm, final_norm, hgrn_w_in, hgrn_lb_logits, hgrn_gnorm, hgrn_w_out, m_w_in, m_conv_w, m_conv_b, m_dt_bias, m_A_log, m_D, m_norm, m_w_out, f_w_up, f_conv_w, f_conv_b, f_w_down, loss_target, m_mix_norm, m_ffn_norm, m_final_norm, m_hgrn_w_in, m_hgrn_lb_logits, m_hgrn_gnorm, m_hgrn_w_out, m_m_w_in, m_m_conv_w, m_m_conv_b, m_m_dt_bias, m_m_A_log, m_m_D, m_m_norm, m_m_w_out, m_f_w_up, m_f_conv_w, m_f_conv_b, m_f_w_down, v_mix_norm, v_ffn_norm, v_final_norm, v_hgrn_w_in, v_hgrn_lb_logits, v_hgrn_gnorm, v_hgrn_w_out, v_m_w_in, v_m_conv_w, v_m_conv_b, v_m_dt_bias, v_m_A_log, v_m_D, v_m_norm, v_m_w_out, v_f_w_up, v_f_conv_w, v_f_conv_b, v_f_w_down):
    w = dict(zip(WEIGHTS, (mix_norm, ffn_norm, final_norm, hgrn_w_in, hgrn_lb_logits, hgrn_gnorm, hgrn_w_out, m_w_in, m_conv_w, m_conv_b, m_dt_bias, m_A_log, m_D, m_norm, m_w_out, f_w_up, f_conv_w, f_conv_b, f_w_down)))
    m = dict(zip(WEIGHTS, (m_mix_norm, m_ffn_norm, m_final_norm, m_hgrn_w_in, m_hgrn_lb_logits, m_hgrn_gnorm, m_hgrn_w_out, m_m_w_in, m_m_conv_w, m_m_conv_b, m_m_dt_bias, m_m_A_log, m_m_D, m_m_norm, m_m_w_out, m_f_w_up, m_f_conv_w, m_f_conv_b, m_f_w_down)))
    v = dict(zip(WEIGHTS, (v_mix_norm, v_ffn_norm, v_final_norm, v_hgrn_w_in, v_hgrn_lb_logits, v_hgrn_gnorm, v_hgrn_w_out, v_m_w_in, v_m_conv_w, v_m_conv_b, v_m_dt_bias, v_m_A_log, v_m_D, v_m_norm, v_m_w_out, v_f_w_up, v_f_conv_w, v_f_conv_b, v_f_w_down)))
    me = 4 * lax.axis_index("x") + 2 * lax.axis_index("y") + lax.axis_index("c")

    def gather_now(a, *, name):
        buf = _Buf(a.shape[0], a.shape[1], a.dtype)
        _exchange([_Job(a, "gather", buf)], name=name)
        return buf.arr

    plan = _Plan()
    depth = mix_norm.shape[0]
    use_order = []
    for i in range(depth):
        use_order += [("hgrn_w_in", i // 2), ("hgrn_w_out", i // 2)] if i % 2 == 0 else [("m_w_in", i // 2), ("m_w_out", i // 2)]
        use_order += [("f_w_up", i), ("f_w_down", i)]
    gathers = {}
    for k, layer in use_order:
        shard = w[k][layer].astype(BF16)
        gathers[k, layer] = _Buf(shard.shape[0], shard.shape[1], BF16)
        plan.push(shard, "gather_chip", gathers[k, layer])

    def fetch(k, layer):
        g = plan.finish(gathers[k, layer], name=f"gather_{k}")
        if k in STACKED:
            return g
        if k in COL_SHARDED:
            return g.transpose(1, 0, 2).reshape(g.shape[1], N_DEV * g.shape[2])
        return g.reshape(N_DEV * g.shape[1], g.shape[2])

    full = {k: w[k] for k in SMALL if k not in SMALL_SHARDED}
    for k in COL_SHARDED:
        full[k] = _LazyWeights(k, (w[k].shape[0], w[k].shape[1], N_DEV * w[k].shape[2]), fetch)
    for k in ROW_SHARDED:
        full[k] = _LazyWeights(k, (w[k].shape[0], N_DEV * w[k].shape[1], w[k].shape[2]), fetch)
    shard_shapes = [w[k].shape for k in SMALL_SHARDED]
    gathered = gather_now(_pack([w[k] for k in SMALL_SHARDED]), name="gather_small_params")
    for k, g in zip(SMALL_SHARDED, _unpack(gathered, shard_shapes, lead=(N_DEV,))):
        full[k] = _join_last(g)

    lands = {k: _Buf(math.prod(w[k].shape[:-1]), w[k].shape[-1], BF16) for k in COL_SHARDED + ROW_SHARDED}

    def push(k, layer, g):
        if k in STACKED:
            parts = g
        elif k in COL_SHARDED:
            parts = g.reshape(g.shape[0], N_DEV, g.shape[1] // N_DEV).transpose(1, 0, 2)
        else:
            parts = g.reshape(N_DEV, g.shape[0] // N_DEV, g.shape[1])
        plan.push(parts, "scatter", lands[k], q0=layer * parts.shape[1])

    loss, grad_x, grads = _local_step(x, loss_target, full, plan, push)

    small_full_shapes = [(1, 1)] + [grads[k].shape for k in SMALL]
    packed = _pack([loss] + [grads[k] for k in SMALL])
    small_parts = _Buf(packed.shape[0], packed.shape[1], F32)
    plan.push(packed, "gather_chip", small_parts)
    out = {}
    for k in ("m_w_out", "m_w_in", "f_w_down", "f_w_up", "hgrn_w_out", "hgrn_w_in"):
        land = plan.finish(lands[k], name="scatter_" + k)
        rows, n = math.prod(w[k].shape[:-1]), w[k].shape[-1]
        res = _adamw_sharded(land, w[k].reshape(rows, n), m[k].reshape(rows, n), v[k].reshape(rows, n),
                             name="adamw_" + k, jobs=plan.take("adamw_" + k))
        out[k] = tuple(r.reshape(w[k].shape) for r in res)

    summed = _sum_devices(plan.finish(small_parts, name="gather_small_grads"), name="sum_small_grads")
    parts = _unpack(summed, small_full_shapes)
    loss_all, small_g = parts[0], dict(zip(SMALL, parts[1:]))
    for k in SMALL_SHARDED:
        n = w[k].shape[-1]
        small_g[k] = lax.dynamic_slice_in_dim(small_g[k], me * n, n, axis=small_g[k].ndim - 1)
    shapes = [w[k].shape for k in SMALL]
    res = _adamw_packed(_pack([small_g[k] for k in SMALL]), _pack([w[k] for k in SMALL]), _pack([m[k] for k in SMALL]),
                        _pack([v[k] for k in SMALL]), name="adamw_small")
    res = [_unpack(r, shapes) for r in res]
    for i, k in enumerate(SMALL):
        out[k] = (small_g[k], res[0][i], res[1][i], res[2][i])

    return (loss_all.reshape(()), grad_x, *[out[k][0] for k in WEIGHTS], *[out[k][1] for k in WEIGHTS],
            *[out[k][2] for k in WEIGHTS], *[out[k][3] for k in WEIGHTS])
```
